```python
import math
import jax
import jax.numpy as jnp
from jax import lax
import numpy as np

D_MODEL = 1024
BATCH = 8
SEQ = 2048
DEPTH = 2
DEC_BATCH = 128
DEC_SEQ = 8
PAST_LEN = 16384
PAGE_SIZE = 128

N_EVEN = (DEPTH + 1) // 2
N_ODD = DEPTH // 2
RET_HEADS = 4
RET_DK = D_MODEL // 8
RET_DV = D_MODEL // 8
ML_HEADS = 4
ML_DK = D_MODEL // 8
ML_DV = D_MODEL // 8
ML_CONV = 4
HG_HEADS = 8
HG_DK = D_MODEL // 8
HG_DV = D_MODEL // 8
D_FF = 2816
N_EXPERTS = 8
TOP_K = 2
D_FF_EXPERT = 2816
RET_CHUNK = 128
ML_CHUNK = 128
HG_CHUNK = 16
ROPE_BASE = 10000.0
EPS = 1e-6
FORGET_BIAS_LO = 3.0
FORGET_BIAS_HI = 6.0

RET_W = RET_HEADS * RET_DV
ML_W = ML_HEADS * ML_DV
HG_W = HG_HEADS * HG_DV
AB_SPLITS = (RET_HEADS * RET_DK, RET_HEADS * RET_DK, RET_W, RET_W,
             ML_HEADS * ML_DK, ML_HEADS * ML_DK, ML_W, ML_W, ML_HEADS, ML_HEADS)
AB_IN = sum(AB_SPLITS)
C_SPLITS = (HG_HEADS * HG_DK, HG_HEADS * HG_DK, HG_W, HG_W)
C_IN = sum(C_SPLITS)

kernel_name = 'hybrid_retnet_mlstm_hgrn2_moe_step'


def _rmsnorm(x, g):
    xf = x.astype(jnp.float32)
    y = xf * lax.rsqrt(jnp.mean(jnp.square(xf), axis=-1, keepdims=True) + EPS)
    return (y * g.astype(jnp.float32)).astype(x.dtype)


def _head_norm(o, g, center):
    if center:
        o = o - jnp.mean(o, axis=-1, keepdims=True)
    o = o * lax.rsqrt(jnp.mean(jnp.square(o), axis=-1, keepdims=True) + EPS)
    return o.reshape(o.shape[0], o.shape[1], -1) * g.astype(jnp.float32)


def _split(a, sizes):
    return jnp.split(a, np.cumsum(sizes)[:-1].tolist(), axis=-1)


def _rope(x, positions):
    half = x.shape[-1] // 2
    inv = ROPE_BASE ** (-jnp.arange(half, dtype=jnp.float32) / half)
    ang = positions.astype(jnp.float32)[:, None] * inv[None, :]
    cos = jnp.cos(ang)[None, :, None, :]
    sin = jnp.sin(ang)[None, :, None, :]
    x1, x2 = x[..., :half], x[..., half:]
    return jnp.concatenate([x1 * cos - x2 * sin, x1 * sin + x2 * cos], axis=-1)


def _to_chunks(a, c):
    B, L = a.shape[:2]
    a = a.reshape((B, L // c, c) + a.shape[2:])
    return jnp.moveaxis(jnp.moveaxis(a, 1, 0), 3, 2)


def _from_chunks(o):
    n, B, H, c = o.shape[:4]
    o = jnp.moveaxis(jnp.moveaxis(o, 2, 3), 0, 1)
    return o.reshape((B, n * c, H) + o.shape[4:])


def _causal_conv(x, buf, w, b):
    L = x.shape[1]
    xc = jnp.concatenate([buf.astype(x.dtype), x], axis=1)
    y = b
    for j in range(ML_CONV):
        y = y + xc[:, j:j + L] * w[j]
    return y, xc[:, L:]


def _retention(q, k, v, state):
    B, L, H, _ = q.shape
    c = math.gcd(L, RET_CHUNK)
    log_gamma = jnp.log1p(-jnp.exp2(-5.0 - jnp.arange(H, dtype=jnp.float32)))
    idx = jnp.arange(c, dtype=jnp.float32)
    diff = idx[:, None] - idx[None, :]
    causal = diff >= 0
    decay = jnp.where(causal[None], jnp.exp(jnp.where(causal, diff, 0.0)[None] * log_gamma[:, None, None]), 0.0)
    q_dec = jnp.exp((idx + 1.0)[None, :] * log_gamma[:, None])[..., None]
    k_dec = jnp.exp((c - 1.0 - idx)[None, :] * log_gamma[:, None])[..., None]
    chunk_dec = jnp.exp(c * log_gamma)[:, None, None]

    def step(S, inp):
        qc, kc, vc = inp
        intra = jnp.einsum('bhts,bhsv->bhtv', jnp.einsum('bhtk,bhsk->bhts', qc, kc) * decay, vc)
        inter = jnp.einsum('bhtk,bhkv->bhtv', qc, S) * q_dec
        S_new = S * chunk_dec + jnp.einsum('bhsk,bhsv->bhkv', kc * k_dec, vc)
        return S_new, intra + inter

    S, out = lax.scan(step, state, (_to_chunks(q, c), _to_chunks(k, c), _to_chunks(v, c)))
    return _from_chunks(out), S


def _mlstm(q, k, v, i_pre, log_f, C, n, m):
    B, L, H, _ = q.shape
    c = math.gcd(L, ML_CHUNK)
    causal = jnp.tril(jnp.ones((c, c), dtype=bool))

    def step(carry, inp):
        C, n, m = carry
        qc, kc, vc, ic, fc = inp
        b = jnp.cumsum(fc, axis=-1)
        src = ic - b
        m_t = b + jnp.maximum(m[..., None], lax.cummax(src, axis=2))
        w = jnp.exp(jnp.where(causal, b[..., :, None] + src[..., None, :] - m_t[..., :, None], -jnp.inf))
        w_prev = jnp.exp(b + m[..., None] - m_t)
        s = jnp.einsum('bhtk,bhsk->bhts', qc, kc) * w
        num = jnp.einsum('bhts,bhsv->bhtv', s, vc) + w_prev[..., None] * jnp.einsum('bhtk,bhkv->bhtv', qc, C)
        den = jnp.sum(s, axis=-1) + w_prev * jnp.einsum('bhtk,bhk->bht', qc, n)
        h = num / jnp.maximum(jnp.abs(den), jnp.exp(-m_t))[..., None]
        m_new = m_t[..., -1]
        w_end = jnp.exp(b[..., -1:] + src - m_new[..., None])
        dec = jnp.exp(b[..., -1] + m - m_new)
        C_new = dec[..., None, None] * C + jnp.einsum('bhsk,bhsv->bhkv', kc * w_end[..., None], vc)
        n_new = dec[..., None] * n + jnp.einsum('bhsk,bhs->bhk', kc, w_end)
        return (C_new, n_new, m_new), h

    (C, n, m), out = lax.scan(step, (C, n, m), (_to_chunks(q, c), _to_chunks(k, c), _to_chunks(v, c),
                                                 _to_chunks(i_pre, c), _to_chunks(log_f, c)))
    return _from_chunks(out), C, n, m


def _hgrn2(q, k, log_f, v, S):
    B, L, H, _ = q.shape
    c = math.gcd(L, HG_CHUNK)
    causal = jnp.tril(jnp.ones((c, c), dtype=bool))[..., None]

    def step(S, inp):
        qc, kc, fc, vc = inp
        A = jnp.cumsum(fc, axis=2)
        inter = jnp.einsum('bhtk,bhkv->bhtv', qc * jnp.exp(A), S)
        dec = jnp.exp(jnp.where(causal, A[:, :, :, None, :] - A[:, :, None, :, :], -jnp.inf))
        s = jnp.einsum('bhtk,bhtsk->bhts', qc, dec * kc[:, :, None, :, :])
        intra = jnp.einsum('bhts,bhsv->bhtv', s, vc)
        A_end = A[:, :, -1:, :]
        S_new = jnp.exp(A_end[:, :, 0, :])[..., None] * S + jnp.einsum('bhsk,bhsv->bhkv', kc * jnp.exp(A_end - A), vc)
        return S_new, inter + intra

    S, out = lax.scan(step, S, (_to_chunks(q, c), _to_chunks(k, c), _to_chunks(log_f, c), _to_chunks(v, c)))
    return _from_chunks(out), S


def _mixer_ab(h, positions, ret_s, mc_s, mn_s, mm_s, conv_s, w_in, b_gates, conv_w, conv_b, ret_gn, ml_gn, w_out):
    f32 = jnp.float32
    B, L, _ = h.shape
    proj = jnp.einsum('bld,de->ble', h, w_in)
    rq, rk, rv, rg, mq, mk, mv, mo, mi, mf = _split(proj, AB_SPLITS)
    q = _rope(rq.reshape(B, L, RET_HEADS, RET_DK).astype(f32), positions) * (RET_DK ** -0.5)
    k = _rope(rk.reshape(B, L, RET_HEADS, RET_DK).astype(f32), positions)
    v = rv.reshape(B, L, RET_HEADS, RET_DV).astype(f32)
    o_ret, ret_new = _retention(q, k, v, ret_s.astype(f32))
    o_ret = jax.nn.silu(rg.astype(f32)) * _head_norm(o_ret, ret_gn, True)
    qk, conv_new = _causal_conv(jnp.concatenate([mq, mk], axis=-1), conv_s, conv_w, conv_b)
    qk = jax.nn.silu(qk.astype(f32))
    q_m = qk[..., :ML_HEADS * ML_DK].reshape(B, L, ML_HEADS, ML_DK)
    k_m = qk[..., ML_HEADS * ML_DK:].reshape(B, L, ML_HEADS, ML_DK) * (ML_DK ** -0.5)
    v_m = mv.astype(f32).reshape(B, L, ML_HEADS, ML_DV)
    i_pre = (mi + b_gates[:ML_HEADS]).astype(f32)
    log_f = jax.nn.log_sigmoid((mf + b_gates[ML_HEADS:]).astype(f32))
    h_m, c_new, n_new, m_new = _mlstm(q_m, k_m, v_m, i_pre, log_f,
                                      mc_s.astype(f32), mn_s.astype(f32), mm_s.astype(f32))
    h_m = jax.nn.sigmoid(mo.astype(f32)).reshape(B, L, ML_HEADS, ML_DV) * h_m
    h_m = _head_norm(h_m, ml_gn, True)
    out = jnp.einsum('ble,ed->bld', jnp.concatenate([o_ret, h_m], axis=-1).astype(h.dtype), w_out)
    return out, ret_new, c_new, n_new, m_new, conv_new


def _mixer_c(h, hg_s, lower_bound, w_in, hg_gn, w_out):
    f32 = jnp.float32
    B, L, _ = h.shape
    qp, fp, ip, gp = _split(jnp.einsum('bld,de->ble', h, w_in), C_SPLITS)
    f = lower_bound + (1.0 - lower_bound) * jax.nn.sigmoid(fp.astype(f32))
    shp = (B, L, HG_HEADS, HG_DK)
    q = jax.nn.silu(qp.astype(f32)).reshape(shp)
    k = (1.0 - f).reshape(shp)
    log_f = jnp.log(f).reshape(shp)
    v = ip.astype(f32).reshape(B, L, HG_HEADS, HG_DV)
    o, s_new = _hgrn2(q, k, log_f, v, hg_s.astype(f32))
    o = _head_norm(o, hg_gn, False) * jax.nn.silu(gp.astype(f32))
    return jnp.einsum('ble,ed->bld', o.astype(h.dtype), w_out), s_new


def _swiglu(h, wg, wu, wd):
    a = jnp.einsum('bld,df->blf', h, wg)
    u = jnp.einsum('bld,df->blf', h, wu)
    return jnp.einsum('blf,fd->bld', jax.nn.silu(a) * u, wd)


def _moe(h, w_router, b_router, wg, wu, wd):
    f32 = jnp.float32
    logits = jnp.einsum('bld,de->ble', h, w_router).astype(f32) + b_router.astype(f32)
    top_val, top_idx = lax.top_k(logits, TOP_K)
    top_w = jax.nn.softmax(top_val, axis=-1)
    gates = jnp.sum(jax.nn.one_hot(top_idx, N_EXPERTS, dtype=f32) * top_w[..., None], axis=-2)
    out = jnp.zeros(h.shape, f32)
    for e in range(N_EXPERTS):
        out = out + gates[..., e:e + 1] * _swiglu(h, wg[e], wu[e], wd[e]).astype(f32)
    return out.astype(h.dtype)


def _trunk(x, positions, ret_s, mc_s, mn_s, mm_s, conv_s, hg_s, p):
    P = jax.nn.softmax(p['hg_lower'].astype(jnp.float32), axis=0)
    lower = jnp.cumsum(P, axis=0) - P[0:1]
    ret_l, mc_l, mn_l, mm_l, cv_l, hg_l = [], [], [], [], [], []
    for l in range(DEPTH):
        j = l // 2
        h = _rmsnorm(x, p['ln_mix'][l])
        if l % 2 == 0:
            mix, r_new, c_new, n_new, m_new, cv_new = _mixer_ab(
                h, positions, ret_s[j], mc_s[j], mn_s[j], mm_s[j], conv_s[j],
                p['w_in_ab'][j], p['b_gates_ab'][j], p['conv_w_ab'][j], p['conv_b_ab'][j],
                p['ret_gn'][j], p['ml_gn'][j], p['w_out_ab'][j])
            ret_l.append(r_new.astype(x.dtype))
            mc_l.append(c_new.astype(x.dtype))
            mn_l.append(n_new.astype(x.dtype))
            mm_l.append(m_new.astype(x.dtype))
            cv_l.append(cv_new.astype(x.dtype))
        else:
            mix, s_new = _mixer_c(h, hg_s[j], lower[l], p['w_in_c'][j], p['hg_gn'][j], p['w_out_c'][j])
            hg_l.append(s_new.astype(x.dtype))
        x = x + mix
        h = _rmsnorm(x, p['ln_ffn'][l])
        if l % 2 == 0:
            ffn = _swiglu(h, p['w_ffn_gate'][j], p['w_ffn_up'][j], p['w_ffn_down'][j])
        else:
            ffn = _moe(h, p['w_router'][j], p['b_router'][j], p['w_moe_gate'][j], p['w_moe_up'][j], p['w_moe_down'][j])
        x = x + ffn
    y = _rmsnorm(x, p['ln_final'])
    return (y, jnp.stack(ret_l), jnp.stack(mc_l), jnp.stack(mn_l), jnp.stack(mm_l), jnp.stack(cv_l), jnp.stack(hg_l))


def setup_inputs(seed: int = 0) -> dict:
    key = jax.random.key(seed)
    keys = jax.random.split(key, 32)
    f32 = jnp.float32

    def nrm(i, shape, scale):
        return scale * jax.random.normal(keys[i], shape, f32)

    d = D_MODEL
    qk_w = 2 * ML_HEADS * ML_DK
    return {
        'x_prompt': nrm(0, (BATCH, SEQ, d), 1.0),
        'x_sample': nrm(1, (DEC_BATCH, DEC_SEQ, d), 1.0),
        'state_ret': nrm(2, (N_EVEN, DEC_BATCH, RET_HEADS, RET_DK, RET_DV), 0.5),
        'state_mlstm_c': nrm(3, (N_EVEN, DEC_BATCH, ML_HEADS, ML_DK, ML_DV), 1.0),
        'state_mlstm_n': nrm(4, (N_EVEN, DEC_BATCH, ML_HEADS, ML_DK), 1.0),
        'state_mlstm_m': 2.0 + nrm(5, (N_EVEN, DEC_BATCH, ML_HEADS), 1.0),
        'state_conv': nrm(6, (N_EVEN, DEC_BATCH, ML_CONV - 1, qk_w), 1.0),
        'state_hgrn': nrm(7, (N_ODD, DEC_BATCH, HG_HEADS, HG_DK, HG_DV), 0.5),
        'ln_mix': 1.0 + nrm(8, (DEPTH, d), 0.05),
        'ln_ffn': 1.0 + nrm(9, (DEPTH, d), 0.05),
        'ln_final': 1.0 + nrm(10, (d,), 0.05),
        'w_in_ab': nrm(11, (N_EVEN, d, AB_IN), d ** -0.5),
        'b_gates_ab': jnp.concatenate([nrm(12, (N_EVEN, ML_HEADS), 0.1),
                                       jnp.linspace(FORGET_BIAS_LO, FORGET_BIAS_HI, ML_HEADS, dtype=f32)[None, :]
                                       + nrm(13, (N_EVEN, ML_HEADS), 0.1)], axis=-1),
        'conv_w_ab': nrm(14, (N_EVEN, ML_CONV, qk_w), ML_CONV ** -0.5),
        'conv_b_ab': nrm(15, (N_EVEN, qk_w), 0.02),
        'ret_gn': 1.0 + nrm(16, (N_EVEN, RET_W), 0.05),
        'ml_gn': 1.0 + nrm(17, (N_EVEN, ML_W), 0.05),
        'w_out_ab': nrm(18, (N_EVEN, RET_W + ML_W, d), (RET_W + ML_W) ** -0.5),
        'w_in_c': nrm(19, (N_ODD, d, C_IN), d ** -0.5),
        'hg_lower': 1.0 + nrm(20, (DEPTH, HG_HEADS * HG_DK), 0.1),
        'hg_gn': 1.0 + nrm(21, (N_ODD, HG_W), 0.05),
        'w_out_c': nrm(22, (N_ODD, HG_W, d), HG_W ** -0.5),
        'w_ffn_gate': nrm(23, (N_EVEN, d, D_FF), d ** -0.5),
        'w_ffn_up': nrm(24, (N_EVEN, d, D_FF), d ** -0.5),
        'w_ffn_down': nrm(25, (N_EVEN, D_FF, d), D_FF ** -0.5),
        'w_router': nrm(26, (N_ODD, d, N_EXPERTS), d ** -0.5),
        'b_router': nrm(27, (N_ODD, N_EXPERTS), 0.01),
        'w_moe_gate': nrm(28, (N_ODD, N_EXPERTS, d, D_FF_EXPERT), d ** -0.5),
        'w_moe_up': nrm(29, (N_ODD, N_EXPERTS, d, D_FF_EXPERT), d ** -0.5),
        'w_moe_down': nrm(30, (N_ODD, N_EXPERTS, D_FF_EXPERT, d), D_FF_EXPERT ** -0.5),
    }


def reference(x_prompt, x_sample, state_ret, state_mlstm_c, state_mlstm_n, state_mlstm_m, state_conv, state_hgrn,
              ln_mix, ln_ffn, ln_final, w_in_ab, b_gates_ab, conv_w_ab, conv_b_ab, ret_gn, ml_gn, w_out_ab,
              w_in_c, hg_lower, hg_gn, w_out_c, w_ffn_gate, w_ffn_up, w_ffn_down,
              w_router, b_router, w_moe_gate, w_moe_up, w_moe_down):
    p = {'ln_mix': ln_mix, 'ln_ffn': ln_ffn, 'ln_final': ln_final,
         'w_in_ab': w_in_ab, 'b_gates_ab': b_gates_ab, 'conv_w_ab': conv_w_ab, 'conv_b_ab': conv_b_ab,
         'ret_gn': ret_gn, 'ml_gn': ml_gn, 'w_out_ab': w_out_ab,
         'w_in_c': w_in_c, 'hg_lower': hg_lower, 'hg_gn': hg_gn, 'w_out_c': w_out_c,
         'w_ffn_gate': w_ffn_gate, 'w_ffn_up': w_ffn_up, 'w_ffn_down': w_ffn_down,
         'w_router': w_router, 'b_router': b_router,
         'w_moe_gate': w_moe_gate, 'w_moe_up': w_moe_up, 'w_moe_down': w_moe_down}
    bp, lp = x_prompt.shape[0], x_prompt.shape[1]
    dt = x_prompt.dtype
    (y_prompt, ret_p, mc_p, mn_p, mm_p, conv_p, hg_p) = _trunk(
        x_prompt, jnp.arange(lp, dtype=jnp.int32),
        jnp.zeros((N_EVEN, bp, RET_HEADS, RET_DK, RET_DV), dt),
        jnp.zeros((N_EVEN, bp, ML_HEADS, ML_DK, ML_DV), dt),
        jnp.zeros((N_EVEN, bp, ML_HEADS, ML_DK), dt),
        jnp.full((N_EVEN, bp, ML_HEADS), -jnp.inf, dt),
        jnp.zeros((N_EVEN, bp, ML_CONV - 1, 2 * ML_HEADS * ML_DK), dt),
        jnp.zeros((N_ODD, bp, HG_HEADS, HG_DK, HG_DV), dt),
        p)
    (y_sample, ret_s, mc_s, mn_s, mm_s, conv_s, hg_s) = _trunk(
        x_sample, PAST_LEN + jnp.arange(x_sample.shape[1], dtype=jnp.int32),
        state_ret, state_mlstm_c, state_mlstm_n, state_mlstm_m, state_conv, state_hgrn, p)
    return (y_prompt, y_sample, ret_p, ret_s, mc_p, mc_s, mn_p, mn_s, mm_p, mm_s, conv_p, conv_s, hg_p, hg_s)
```

```python
import functools
import math

import numpy as np
import jax
import jax.numpy as jnp
from jax import lax
from jax.experimental import pallas as pl
from jax.experimental.pallas import tpu as pltpu

F32 = jnp.float32
BF16 = jnp.bfloat16

D_MODEL = 1024
N_HEAD_BLOCKS = 8
HEAD_DIM = 128
RET_HEADS = 4
ML_HEADS = 4
ML_CONV = 4
HG_HEADS = 8
D_FF = 2816
N_EXPERTS = 8
ROPE_BASE = 10000.0
EPS = 1e-6
RET_CHUNK = 128
ML_CHUNK = 128
HG_BLOCK = 128
HG_SUB = 16

LANES = 128
SUBLANES = 8
AB_MAIN = 4096
AB_PAD = 4224
VMEM_LIMIT = 56 * 1024 * 1024

NEG_INF = float("-inf")


def _cparams(sem):
    return pltpu.CompilerParams(dimension_semantics=sem, vmem_limit_bytes=VMEM_LIMIT)


def _sigmoid(x):
    return 1.0 / (1.0 + jnp.exp(-x))


def _silu(x):
    return x * _sigmoid(x)


def _rmsnorm_rows(x, g):
    ms = jnp.mean(x * x, axis=-1, keepdims=True)
    return x * lax.rsqrt(ms + EPS) * g


def _dot(a, b):
    return jnp.dot(a, b, preferred_element_type=F32)


def _dot_nt(a, b):
    return lax.dot_general(a, b, (((1,), (1,)), ((), ())), preferred_element_type=F32)


def _dot_tn(a, b):
    return lax.dot_general(a, b, (((0,), (0,)), ((), ())), preferred_element_type=F32)


def _pad_rows(x, rows):
    if x.shape[0] == rows:
        return x
    return jnp.concatenate([x, jnp.zeros((rows - x.shape[0], x.shape[1]), x.dtype)], axis=0)


def _norm_matmul_body(x_ref, g_ref, w_ref, o_ref, h_ref):
    @pl.when(pl.program_id(1) == 0)
    def _():
        h_ref[...] = _rmsnorm_rows(x_ref[...], g_ref[...]).astype(BF16)

    o_ref[...] = _dot(h_ref[...], w_ref[...])


def _norm_matmul(x, g, w, tm, tn):
    t, d = x.shape
    n = w.shape[1]
    return pl.pallas_call(
        _norm_matmul_body,
        grid=(t // tm, n // tn),
        in_specs=[
            pl.BlockSpec((tm, d), lambda i, j: (i, 0)),
            pl.BlockSpec((1, d), lambda i, j: (0, 0)),
            pl.BlockSpec((d, tn), lambda i, j: (0, j)),
        ],
        out_specs=pl.BlockSpec((tm, tn), lambda i, j: (i, j)),
        out_shape=jax.ShapeDtypeStruct((t, n), F32),
        scratch_shapes=[pltpu.VMEM((tm, d), BF16)],
        compiler_params=_cparams(("parallel", "arbitrary")),
        name="norm_matmul",
    )(x, g, w)


def _proj_residual_body(*refs, n_in):
    a_refs = refs[:n_in]
    w_refs = refs[n_in:2 * n_in]
    x_ref = refs[2 * n_in]
    o_ref = refs[2 * n_in + 1]
    acc = x_ref[...]
    for a_ref, w_ref in zip(a_refs, w_refs):
        acc = acc + _dot(a_ref[...].astype(BF16), w_ref[...])
    o_ref[...] = acc


def _proj_residual(acts, weights, x, tm):
    t, d = x.shape
    n_in = len(acts)
    in_specs = [pl.BlockSpec((tm, a.shape[1]), lambda i: (i, 0)) for a in acts]
    in_specs += [pl.BlockSpec(w.shape, lambda i: (0, 0)) for w in weights]
    in_specs += [pl.BlockSpec((tm, d), lambda i: (i, 0))]
    return pl.pallas_call(
        functools.partial(_proj_residual_body, n_in=n_in),
        grid=(t // tm,),
        in_specs=in_specs,
        out_specs=pl.BlockSpec((tm, d), lambda i: (i, 0)),
        out_shape=jax.ShapeDtypeStruct((t, d), F32),
        compiler_params=_cparams(("parallel",)),
        name="proj_residual",
    )(*acts, *weights, x)


def _ffn_body(x_ref, g_ref, wg_ref, wu_ref, wd_ref, o_ref, h_ref, acc_ref):
    f = pl.program_id(1)

    @pl.when(f == 0)
    def _():
        h_ref[...] = _rmsnorm_rows(x_ref[...], g_ref[...]).astype(BF16)
        acc_ref[...] = jnp.zeros_like(acc_ref)

    h = h_ref[...]
    a = _dot(h, wg_ref[...])
    u = _dot(h, wu_ref[...])
    act = (_silu(a) * u).astype(BF16)
    acc_ref[...] += _dot(act, wd_ref[...])

    @pl.when(f == pl.num_programs(1) - 1)
    def _():
        o_ref[...] = x_ref[...] + acc_ref[...]


def _ffn(x, g, wg, wu, wd, tm, tf):
    t, d = x.shape
    dff = wg.shape[1]
    return pl.pallas_call(
        _ffn_body,
        grid=(t // tm, dff // tf),
        in_specs=[
            pl.BlockSpec((tm, d), lambda i, f: (i, 0)),
            pl.BlockSpec((1, d), lambda i, f: (0, 0)),
            pl.BlockSpec((d, tf), lambda i, f: (0, f)),
            pl.BlockSpec((d, tf), lambda i, f: (0, f)),
            pl.BlockSpec((tf, d), lambda i, f: (f, 0)),
        ],
        out_specs=pl.BlockSpec((tm, d), lambda i, f: (i, 0)),
        out_shape=jax.ShapeDtypeStruct((t, d), F32),
        scratch_shapes=[pltpu.VMEM((tm, d), BF16), pltpu.VMEM((tm, d), F32)],
        compiler_params=_cparams(("parallel", "arbitrary")),
        name="ffn",
    )(x, g, wg, wu, wd)


def _router_body(x_ref, g_ref, whi_ref, wlo_ref, b_ref, o_ref):
    h = _rmsnorm_rows(x_ref[...], g_ref[...])
    h_hi = h.astype(BF16)
    h_lo = (h - h_hi.astype(F32)).astype(BF16)
    logits = (_dot(h_hi, whi_ref[...]) + _dot(h_hi, wlo_ref[...]) + _dot(h_lo, whi_ref[...])
              + b_ref[...])
    lane = lax.broadcasted_iota(jnp.int32, logits.shape, 1)
    lg = jnp.where(lane < N_EXPERTS, logits, NEG_INF)
    m1 = jnp.max(lg, axis=-1, keepdims=True)
    i1 = jnp.min(jnp.where(lg == m1, lane, LANES), axis=-1, keepdims=True)
    lg2 = jnp.where(lane == i1, NEG_INF, lg)
    m2 = jnp.max(lg2, axis=-1, keepdims=True)
    i2 = jnp.min(jnp.where(lg2 == m2, lane, LANES), axis=-1, keepdims=True)
    e = jnp.exp(m2 - m1)
    w1 = 1.0 / (1.0 + e)
    w2 = e / (1.0 + e)
    o_ref[...] = jnp.where(lane == i1, w1, 0.0) + jnp.where(lane == i2, w2, 0.0)


def _router(x, g, w_hi, w_lo, b, tm):
    t, d = x.shape
    return pl.pallas_call(
        _router_body,
        grid=(t // tm,),
        in_specs=[
            pl.BlockSpec((tm, d), lambda i: (i, 0)),
            pl.BlockSpec((1, d), lambda i: (0, 0)),
            pl.BlockSpec((d, LANES), lambda i: (0, 0)),
            pl.BlockSpec((d, LANES), lambda i: (0, 0)),
            pl.BlockSpec((1, LANES), lambda i: (0, 0)),
        ],
        out_specs=pl.BlockSpec((tm, LANES), lambda i: (i, 0)),
        out_shape=jax.ShapeDtypeStruct((t, LANES), F32),
        compiler_params=_cparams(("parallel",)),
        name="router",
    )(x, g, w_hi, w_lo, b)


def _moe_body(x_ref, g_ref, gates_ref, wg_ref, wu_ref, wd_ref, gf_ref, o_ref, h_ref, acc_ref, y_ref):
    e = pl.program_id(1)
    f = pl.program_id(2)
    last_f = pl.num_programs(2) - 1

    @pl.when((e == 0) & (f == 0))
    def _():
        h_ref[...] = _rmsnorm_rows(x_ref[...], g_ref[...]).astype(BF16)
        acc_ref[...] = jnp.zeros_like(acc_ref)

    h = h_ref[...]
    a = _dot(h, wg_ref[0])
    u = _dot(h, wu_ref[0])
    act = (_silu(a) * u).astype(BF16)
    y = _dot(act, wd_ref[0])

    @pl.when(f == 0)
    def _():
        y_ref[...] = y

    @pl.when(f != 0)
    def _():
        y_ref[...] += y

    @pl.when(f == last_f)
    def _():
        gates = gates_ref[...]
        lane = lax.broadcasted_iota(jnp.int32, gates.shape, 1)
        ge = jnp.sum(jnp.where(lane == e, gates, 0.0), axis=-1, keepdims=True)
        acc_ref[...] += ge * y_ref[...]

    @pl.when((e == pl.num_programs(1) - 1) & (f == last_f))
    def _():
        o_ref[...] = _rmsnorm_rows(x_ref[...] + acc_ref[...], gf_ref[...])


def _moe(x, g, gates, wg, wu, wd, g_final, tm, tf):
    t, d = x.shape
    n_e, _, dff = wg.shape
    return pl.pallas_call(
        _moe_body,
        grid=(t // tm, n_e, dff // tf),
        in_specs=[
            pl.BlockSpec((tm, d), lambda i, e, f: (i, 0)),
            pl.BlockSpec((1, d), lambda i, e, f: (0, 0)),
            pl.BlockSpec((tm, LANES), lambda i, e, f: (i, 0)),
            pl.BlockSpec((1, d, tf), lambda i, e, f: (e, 0, f)),
            pl.BlockSpec((1, d, tf), lambda i, e, f: (e, 0, f)),
            pl.BlockSpec((1, tf, d), lambda i, e, f: (e, f, 0)),
            pl.BlockSpec((1, d), lambda i, e, f: (0, 0)),
        ],
        out_specs=pl.BlockSpec((tm, d), lambda i, e, f: (i, 0)),
        out_shape=jax.ShapeDtypeStruct((t, d), F32),
        scratch_shapes=[pltpu.VMEM((tm, d), BF16), pltpu.VMEM((tm, d), F32), pltpu.VMEM((tm, d), F32)],
        compiler_params=_cparams(("parallel", "arbitrary", "arbitrary")),
        name="moe",
    )(x, g, gates, wg, wu, wd, g_final)


def _retention_tables(c_real, ct):
    h = np.arange(RET_HEADS, dtype=np.float64)
    log_gamma = np.log1p(-np.exp2(-5.0 - h))
    idx = np.arange(ct, dtype=np.float64)
    live = idx < c_real
    diff = idx[:, None] - idx[None, :]
    causal = (diff >= 0) & live[:, None] & live[None, :]
    decay = np.where(causal[None], np.exp(np.where(causal, diff, 0.0)[None] * log_gamma[:, None, None]), 0.0)
    q_dec = np.where(live[None], np.exp((idx + 1.0)[None] * log_gamma[:, None]), 0.0)
    k_dec = np.where(live[None], np.exp((c_real - 1.0 - idx)[None] * log_gamma[:, None]), 0.0)
    q_dec = np.broadcast_to(q_dec[..., None], (RET_HEADS, ct, LANES))
    k_dec = np.broadcast_to(k_dec[..., None], (RET_HEADS, ct, LANES))
    return (jnp.asarray(decay, F32), jnp.asarray(q_dec, F32), jnp.asarray(k_dec, F32))


def _rope_tables(pos0, length):
    half = HEAD_DIM // 2
    inv = ROPE_BASE ** (-np.arange(half, dtype=np.float64) / half)
    ang = (pos0 + np.arange(length, dtype=np.float64))[:, None] * inv[None, :]
    cos = np.concatenate([np.cos(ang), np.cos(ang)], axis=-1)
    sin = np.concatenate([-np.sin(ang), np.sin(ang)], axis=-1)
    return jnp.asarray(cos, F32), jnp.asarray(sin, F32)


def _retention_body(q_ref, k_ref, v_ref, g_ref, cos_ref, sin_ref, dec_ref, qd_ref, kd_ref, s0_ref, gn_ref,
                    o_ref, s_ref, *, c_real, ct):
    @pl.when(pl.program_id(2) == 0)
    def _():
        s_ref[...] = s0_ref[...]

    cos = cos_ref[...]
    sin = sin_ref[...]
    q = q_ref[0]
    k = k_ref[0]
    q = (q * cos + pltpu.roll(q, HEAD_DIM // 2, 1) * sin) * (HEAD_DIM ** -0.5)
    k = k * cos + pltpu.roll(k, HEAD_DIM // 2, 1) * sin
    q = _pad_rows(q, ct)
    k = _pad_rows(k, ct)
    vb = _pad_rows(v_ref[0], ct).astype(BF16)
    qb = q.astype(BF16)
    q_dec = qd_ref[0]
    state = s_ref[0, 0]

    s = _dot_nt(qb, k.astype(BF16)) * dec_ref[0]
    o = _dot(s.astype(BF16), vb) + _dot(qb, state.astype(BF16)) * q_dec
    chunk_dec = q_dec[c_real - 1:c_real, :]
    s_ref[0, 0] = state * chunk_dec + _dot_tn((k * kd_ref[0]).astype(BF16), vb)

    o = o[:c_real]
    o = o - jnp.mean(o, axis=-1, keepdims=True)
    o = o * lax.rsqrt(jnp.mean(o * o, axis=-1, keepdims=True) + EPS)
    o_ref[0] = _silu(g_ref[0]) * (o * gn_ref[...])


def _retention(proj, pos0, state, gn):
    b, length, _ = proj.shape
    c_real = math.gcd(length, RET_CHUNK)
    ct = RET_CHUNK
    cos, sin = _rope_tables(pos0, length)
    decay, q_dec, k_dec = _retention_tables(c_real, ct)

    def col(off):
        return pl.BlockSpec((1, c_real, HEAD_DIM), lambda bi, h, c: (bi, c, off + h))

    table = lambda last: pl.BlockSpec((1, ct, last), lambda bi, h, c: (h, 0, 0))
    state_spec = pl.BlockSpec((1, 1, HEAD_DIM, HEAD_DIM), lambda bi, h, c: (bi, h, 0, 0))
    return pl.pallas_call(
        functools.partial(_retention_body, c_real=c_real, ct=ct),
        grid=(b, RET_HEADS, length // c_real),
        in_specs=[
            col(0), col(RET_HEADS), col(2 * RET_HEADS), col(3 * RET_HEADS),
            pl.BlockSpec((c_real, HEAD_DIM), lambda bi, h, c: (c, 0)),
            pl.BlockSpec((c_real, HEAD_DIM), lambda bi, h, c: (c, 0)),
            table(ct), table(LANES), table(LANES),
            state_spec,
            pl.BlockSpec((1, HEAD_DIM), lambda bi, h, c: (0, h)),
        ],
        out_specs=[
            pl.BlockSpec((1, c_real, HEAD_DIM), lambda bi, h, c: (bi, c, h)),
            state_spec,
        ],
        out_shape=[
            jax.ShapeDtypeStruct((b, length, RET_HEADS * HEAD_DIM), F32),
            jax.ShapeDtypeStruct(state.shape, F32),
        ],
        compiler_params=_cparams(("parallel", "parallel", "arbitrary")),
        name="retention",
    )(proj, proj, proj, proj, cos, sin, decay, q_dec, k_dec, state, gn)


def _scan_rows(x, op, fill):
    rows = x.shape[0]
    row = lax.broadcasted_iota(jnp.int32, x.shape, 0)
    sh = 1
    while sh < rows:
        x = op(x, jnp.where(row >= sh, pltpu.roll(x, sh, 0), fill))
        sh *= 2
    return x


def _mlstm_body(q_ref, k_ref, v_ref, og_ref, gate_ref, bg_ref, cw_ref, cb_ref, cs_ref, c0_ref, n0_ref, m0_ref,
                gn_ref, o_ref, c_ref, n_ref, m_ref, conv_ref, xc_ref, *, c_real, ct):
    width = ML_HEADS * HEAD_DIM
    tail = ML_CONV - 1

    @pl.when(pl.program_id(1) == 0)
    def _():
        c_ref[...] = c0_ref[...]
        n_ref[...] = n0_ref[...]
        m_ref[...] = m0_ref[...]
        xc_ref[SUBLANES - tail:SUBLANES, :] = cs_ref[0]

    xc_ref[SUBLANES:SUBLANES + c_real, :width] = q_ref[0]
    xc_ref[SUBLANES:SUBLANES + c_real, width:] = k_ref[0]
    y = cb_ref[...] + cw_ref[tail:tail + 1, :] * xc_ref[SUBLANES:SUBLANES + c_real, :]
    for j in range(tail):
        y = y + cw_ref[j:j + 1, :] * xc_ref[SUBLANES - tail + j:SUBLANES - tail + j + c_real, :]
    new_tail = xc_ref[c_real:c_real + SUBLANES, :]
    conv_ref[0] = new_tail[SUBLANES - tail:, :]
    xc_ref[0:SUBLANES, :] = new_tail
    qk = _silu(y)
    q_all = _pad_rows(qk[:, :width], ct)
    k_all = _pad_rows(qk[:, width:] * (HEAD_DIM ** -0.5), ct)
    v_all = _pad_rows(v_ref[0], ct)

    gates = _pad_rows(gate_ref[0] + bg_ref[...], ct)
    f_pre = pltpu.roll(gates, LANES - ML_HEADS, 1)
    log_f = jnp.minimum(f_pre, 0.0) - jnp.log(1.0 + jnp.exp(-jnp.abs(f_pre)))
    b_cum = _scan_rows(log_f, jnp.add, 0.0)
    src = gates - b_cum
    cmx = _scan_rows(src, jnp.maximum, NEG_INF)
    m_prev = m_ref[0]
    mx = jnp.maximum(m_prev, cmx)
    m_t = b_cum + mx
    w_prev = jnp.exp(m_prev - mx)
    e_neg_m = jnp.exp(-m_t)
    last = c_real - 1
    b_last = b_cum[last:last + 1, :]
    m_new = m_t[last:last + 1, :]
    row = lax.broadcasted_iota(jnp.int32, (ct, LANES), 0)
    w_end = jnp.where(row < c_real, jnp.exp(b_last + src - m_new), 0.0)
    dec = jnp.exp(b_last + m_prev - m_new)
    m_ref[0] = m_new
    src_t = jnp.transpose(src)

    t_idx = lax.broadcasted_iota(jnp.int32, (ct, ct), 0)
    s_idx = lax.broadcasted_iota(jnp.int32, (ct, ct), 1)
    live = (s_idx <= t_idx) & (s_idx < c_real)

    for hd in range(ML_HEADS):
        lo, hi = hd * HEAD_DIM, (hd + 1) * HEAD_DIM
        q = q_all[:, lo:hi]
        k = k_all[:, lo:hi]
        qb = q.astype(BF16)
        vb = v_all[:, lo:hi].astype(BF16)
        c_state = c_ref[0, hd]
        n_state = n_ref[0, hd:hd + 1, :]
        wp = w_prev[:, hd:hd + 1]
        w = jnp.where(live, jnp.exp(src_t[hd:hd + 1, :] - mx[:, hd:hd + 1]), 0.0)
        s = _dot_nt(qb, k.astype(BF16)) * w
        num = _dot(s.astype(BF16), vb) + wp * _dot(qb, c_state.astype(BF16))
        den = jnp.sum(s, axis=-1, keepdims=True) + wp * jnp.sum(q * n_state, axis=-1, keepdims=True)
        hh = num / jnp.maximum(jnp.abs(den), e_neg_m[:, hd:hd + 1])
        kw = k * w_end[:, hd:hd + 1]
        dec_h = dec[:, hd:hd + 1]
        c_ref[0, hd] = dec_h * c_state + _dot_tn(kw.astype(BF16), vb)
        n_ref[0, hd:hd + 1, :] = dec_h * n_state + jnp.sum(kw, axis=0, keepdims=True)

        hh = hh[:c_real]
        hh = _sigmoid(og_ref[0, :, lo:hi]) * hh
        hh = hh - jnp.mean(hh, axis=-1, keepdims=True)
        hh = hh * lax.rsqrt(jnp.mean(hh * hh, axis=-1, keepdims=True) + EPS)
        o_ref[0, :, lo:hi] = hh * gn_ref[:, lo:hi]


def _mlstm(proj, b_gates, conv_w, conv_b, conv_s, c0, n0, m0, gn):
    b, length, _ = proj.shape
    c_real = math.gcd(length, ML_CHUNK)
    ct = ML_CHUNK
    width = ML_HEADS * HEAD_DIM
    base = (2 * RET_HEADS * HEAD_DIM + 2 * RET_HEADS * HEAD_DIM) // width

    def col(i):
        return pl.BlockSpec((1, c_real, width), lambda bi, c: (bi, c, base + i))

    full = lambda shape: pl.BlockSpec(shape, lambda bi, c: (0,) * len(shape))
    per_b = lambda shape: pl.BlockSpec((1,) + shape, lambda bi, c: (bi,) + (0,) * len(shape))
    bg = jnp.zeros((1, LANES), F32).at[0, :2 * ML_HEADS].set(b_gates)
    m0p = jnp.zeros((b, 1, LANES), F32).at[:, 0, :ML_HEADS].set(m0)
    outs = pl.pallas_call(
        functools.partial(_mlstm_body, c_real=c_real, ct=ct),
        grid=(b, length // c_real),
        in_specs=[
            col(0), col(1), col(2), col(3),
            pl.BlockSpec((1, c_real, LANES), lambda bi, c: (bi, c, AB_MAIN // LANES)),
            full((1, LANES)),
            full((ML_CONV, 2 * width)),
            full((1, 2 * width)),
            per_b((ML_CONV - 1, 2 * width)),
            per_b((ML_HEADS, HEAD_DIM, HEAD_DIM)),
            per_b((ML_HEADS, HEAD_DIM)),
            per_b((1, LANES)),
            full((1, width)),
        ],
        out_specs=[
            pl.BlockSpec((1, c_real, width), lambda bi, c: (bi, c, 0)),
            per_b((ML_HEADS, HEAD_DIM, HEAD_DIM)),
            per_b((ML_HEADS, HEAD_DIM)),
            per_b((1, LANES)),
            per_b((ML_CONV - 1, 2 * width)),
        ],
        out_shape=[
            jax.ShapeDtypeStruct((b, length, width), F32),
            jax.ShapeDtypeStruct(c0.shape, F32),
            jax.ShapeDtypeStruct(n0.shape, F32),
            jax.ShapeDtypeStruct((b, 1, LANES), F32),
            jax.ShapeDtypeStruct(conv_s.shape, F32),
        ],
        scratch_shapes=[pltpu.VMEM((ct + SUBLANES, 2 * width), F32)],
        compiler_params=_cparams(("parallel", "arbitrary")),
        name="mlstm",
    )(proj, proj, proj, proj, proj, bg, conv_w, conv_b, conv_s, c0, n0, m0p, gn)
    h_m, c_new, n_new, m_new, conv_new = outs
    return h_m, c_new, n_new, m_new[:, 0, :ML_HEADS], conv_new


def _hgrn_body(q_ref, f_ref, i_ref, g_ref, hl_ref, s0_ref, gn_ref, o_ref, s_ref, st_ref, *, blk, sub):
    @pl.when(pl.program_id(2) == 0)
    def _():
        st_ref[...] = jnp.transpose(s0_ref[0, 0])

    hl = hl_ref[...]
    hmax = jnp.max(hl, axis=0, keepdims=True)
    ex = jnp.exp(hl - hmax)
    p = ex / jnp.sum(ex, axis=0, keepdims=True)
    lower = (p[0:1] + p[1:2]) - p[0:1]

    f = lower + (1.0 - lower) * _sigmoid(f_ref[0])
    log_f = jnp.log(f)
    kk = 1.0 - f
    q = _silu(q_ref[0])
    v = i_ref[0]

    row = lax.broadcasted_iota(jnp.int32, (blk, LANES), 0)
    rsub = row & (sub - 1)
    a = log_f
    sh = 1
    while sh < sub:
        a = a + jnp.where(rsub >= sh, pltpu.roll(a, sh, 0), 0.0)
        sh *= 2
    q_in = q * jnp.exp(a)

    t_idx = lax.broadcasted_iota(jnp.int32, (sub, LANES), 0)
    outs = []
    for j in range(blk // sub):
        r0 = j * sub
        a_j = a[r0:r0 + sub]
        q_j = q[r0:r0 + sub]
        k_j = kk[r0:r0 + sub]
        v_j = v[r0:r0 + sub]
        a_end = a_j[sub - 1:sub, :]
        st = st_ref[...]
        o_j = _dot_nt(q_in[r0:r0 + sub].astype(BF16), st.astype(BF16))
        for s in range(sub):
            z = q_j * jnp.exp(jnp.where(t_idx >= s, a_j - a_j[s:s + 1, :], NEG_INF)) * k_j[s:s + 1, :]
            o_j = o_j + jnp.sum(z, axis=-1, keepdims=True) * v_j[s:s + 1, :]
        k_hat = k_j * jnp.exp(a_end - a_j)
        st_ref[...] = st * jnp.exp(a_end) + _dot_tn(v_j.astype(BF16), k_hat.astype(BF16))
        outs.append(o_j)
    o = outs[0] if len(outs) == 1 else jnp.concatenate(outs, axis=0)
    o = o * lax.rsqrt(jnp.mean(o * o, axis=-1, keepdims=True) + EPS)
    o_ref[0] = o * gn_ref[...] * _silu(g_ref[0])

    @pl.when(pl.program_id(2) == pl.num_programs(2) - 1)
    def _():
        s_ref[0, 0] = jnp.transpose(st_ref[...])


def _hgrn(proj, hg_lower, state, gn):
    b, length, _ = proj.shape
    blk = min(length, HG_BLOCK)
    sub = math.gcd(length, HG_SUB)

    def col(off):
        return pl.BlockSpec((1, blk, HEAD_DIM), lambda bi, h, c: (bi, c, off + h))

    state_spec = pl.BlockSpec((1, 1, HEAD_DIM, HEAD_DIM), lambda bi, h, c: (bi, h, 0, 0))
    return pl.pallas_call(
        functools.partial(_hgrn_body, blk=blk, sub=sub),
        grid=(b, HG_HEADS, length // blk),
        in_specs=[
            col(0), col(HG_HEADS), col(2 * HG_HEADS), col(3 * HG_HEADS),
            pl.BlockSpec((hg_lower.shape[0], HEAD_DIM), lambda bi, h, c: (0, h)),
            state_spec,
            pl.BlockSpec((1, HEAD_DIM), lambda bi, h, c: (0, h)),
        ],
        out_specs=[
            pl.BlockSpec((1, blk, HEAD_DIM), lambda bi, h, c: (bi, c, h)),
            state_spec,
        ],
        out_shape=[
            jax.ShapeDtypeStruct((b, length, HG_HEADS * HEAD_DIM), F32),
            jax.ShapeDtypeStruct(state.shape, F32),
        ],
        scratch_shapes=[pltpu.VMEM((HEAD_DIM, HEAD_DIM), F32)],
        compiler_params=_cparams(("parallel", "parallel", "arbitrary")),
        name="hgrn2",
    )(proj, proj, proj, proj, hg_lower, state, gn)


def _prep_weights(p):
    w_in_ab = p["w_in_ab"][0]
    w_ab = jnp.zeros((D_MODEL, AB_PAD), BF16)
    w_ab = w_ab.at[:, :AB_MAIN].set(w_in_ab[:, :AB_MAIN].astype(BF16))
    w_ab = w_ab.at[:, AB_MAIN:AB_MAIN + 2 * ML_HEADS].set(w_in_ab[:, AB_MAIN:].astype(BF16))
    w_router = jnp.zeros((D_MODEL, LANES), F32).at[:, :N_EXPERTS].set(p["w_router"][0])
    w_router_hi = w_router.astype(BF16)
    w_router_lo = (w_router - w_router_hi.astype(F32)).astype(BF16)
    b_router = jnp.zeros((1, LANES), F32).at[0, :N_EXPERTS].set(p["b_router"][0])
    return {
        "w_ab": w_ab,
        "w_out_ab": p["w_out_ab"][0].astype(BF16),
        "w_in_c": p["w_in_c"][0].astype(BF16),
        "w_out_c": p["w_out_c"][0].astype(BF16),
        "w_ffn_gate": p["w_ffn_gate"][0].astype(BF16),
        "w_ffn_up": p["w_ffn_up"][0].astype(BF16),
        "w_ffn_down": p["w_ffn_down"][0].astype(BF16),
        "w_router_hi": w_router_hi,
        "w_router_lo": w_router_lo,
        "b_router": b_router,
        "w_moe_gate": p["w_moe_gate"][0].astype(BF16),
        "w_moe_up": p["w_moe_up"][0].astype(BF16),
        "w_moe_down": p["w_moe_down"][0].astype(BF16),
    }


def _trunk(x, pos0, ret_s, mc_s, mn_s, mm_s, conv_s, hg_s, p, w):
    b, length, d = x.shape
    t = b * length
    tm = min(512, t)
    row = lambda v: v.reshape(1, -1)
    x0 = x.reshape(t, d)

    proj = _norm_matmul(x0, row(p["ln_mix"][0]), w["w_ab"], tm, AB_PAD // 3).reshape(b, length, AB_PAD)
    o_ret, ret_new = _retention(proj, pos0, ret_s, row(p["ret_gn"][0]))
    h_m, c_new, n_new, m_new, conv_new = _mlstm(
        proj, p["b_gates_ab"][0], p["conv_w_ab"][0], row(p["conv_b_ab"][0]), conv_s, mc_s, mn_s, mm_s,
        row(p["ml_gn"][0]))
    half = RET_HEADS * HEAD_DIM
    x1 = _proj_residual([o_ret.reshape(t, half), h_m.reshape(t, half)],
                        [w["w_out_ab"][:half], w["w_out_ab"][half:]], x0, tm)
    x2 = _ffn(x1, row(p["ln_ffn"][0]), w["w_ffn_gate"], w["w_ffn_up"], w["w_ffn_down"], tm, D_FF // 2)

    proj_c = _norm_matmul(x2, row(p["ln_mix"][1]), w["w_in_c"], tm, 1024).reshape(b, length, 4 * D_MODEL)
    o_hg, hg_new = _hgrn(proj_c, p["hg_lower"], hg_s, row(p["hg_gn"][0]))
    x3 = _proj_residual([o_hg.reshape(t, d)], [w["w_out_c"]], x2, tm)
    gates = _router(x3, row(p["ln_ffn"][1]), w["w_router_hi"], w["w_router_lo"], w["b_router"], tm)
    y = _moe(x3, row(p["ln_ffn"][1]), gates, w["w_moe_gate"], w["w_moe_up"], w["w_moe_down"],
             row(p["ln_final"]), tm, D_FF // 2)
    return (y.reshape(b, length, d), ret_new[None], c_new[None], n_new[None], m_new[None], conv_new[None],
            hg_new[None])


def kernel(x_prompt, x_sample, state_ret, state_mlstm_c, state_mlstm_n, state_mlstm_m, state_conv, state_hgrn,
           ln_mix, ln_ffn, ln_final, w_in_ab, b_gates_ab, conv_w_ab, conv_b_ab, ret_gn, ml_gn, w_out_ab,
           w_in_c, hg_lower, hg_gn, w_out_c, w_ffn_gate, w_ffn_up, w_ffn_down,
           w_router, b_router, w_moe_gate, w_moe_up, w_moe_down):
    p = {"ln_mix": ln_mix, "ln_ffn": ln_ffn, "ln_final": ln_final,
         "w_in_ab": w_in_ab, "b_gates_ab": b_gates_ab, "conv_w_ab": conv_w_ab, "conv_b_ab": conv_b_ab,
         "ret_gn": ret_gn, "ml_gn": ml_gn, "w_out_ab": w_out_ab,
         "w_in_c": w_in_c, "hg_lower": hg_lower, "hg_gn": hg_gn, "w_out_c": w_out_c,
         "w_ffn_gate": w_ffn_gate, "w_ffn_up": w_ffn_up, "w_ffn_down": w_ffn_down,
         "w_router": w_router, "b_router": b_router,
         "w_moe_gate": w_moe_gate, "w_moe_up": w_moe_up, "w_moe_down": w_moe_down}
    w = _prep_weights(p)
    bp = x_prompt.shape[0]
    zeros = lambda *shape: jnp.zeros(shape, F32)
    prompt = _trunk(
        x_prompt, 0,
        zeros(bp, RET_HEADS, HEAD_DIM, HEAD_DIM), zeros(bp, ML_HEADS, HEAD_DIM, HEAD_DIM),
        zeros(bp, ML_HEADS, HEAD_DIM), jnp.full((bp, ML_HEADS), NEG_INF, F32),
        zeros(bp, ML_CONV - 1, 2 * ML_HEADS * HEAD_DIM), zeros(bp, HG_HEADS, HEAD_DIM, HEAD_DIM), p, w)
    sample = _trunk(
        x_sample, 16384,
        state_ret[0], state_mlstm_c[0], state_mlstm_n[0], state_mlstm_m[0], state_conv[0], state_hgrn[0], p, w)
    out = []
    for a, s in zip(prompt, sample):
        out += [a, s]
    return tuple(out)
```

```python
import functools
import math

import numpy as np
import jax
import jax.numpy as jnp
from jax import lax
from jax.experimental import pallas as pl
from jax.experimental.pallas import tpu as pltpu

F32 = jnp.float32
BF16 = jnp.bfloat16

D_MODEL = 1024
HEAD_DIM = 128
RET_HEADS = 4
ML_HEADS = 4
ML_CONV = 4
HG_HEADS = 8
D_FF = 2816
N_EXPERTS = 8
ROPE_BASE = 10000.0
EPS = 1e-6
RET_CHUNK = 128
ML_CHUNK = 128
HG_BLOCK = 128
HG_SUB = 16

LANES = 128
SUBLANES = 8
AB_MAIN = 4096
AB_PAD = 4224
VMEM_LIMIT = 56 * 1024 * 1024

NEG_INF = float("-inf")


def _cparams(sem):
    return pltpu.CompilerParams(dimension_semantics=sem, vmem_limit_bytes=VMEM_LIMIT)


def _sigmoid(x):
    return 1.0 / (1.0 + jnp.exp(-x))


def _silu(x):
    return x * _sigmoid(x)


def _rmsnorm_rows(x, g):
    ms = jnp.mean(x * x, axis=-1, keepdims=True)
    return x * lax.rsqrt(ms + EPS) * g


def _dot(a, b):
    return jnp.dot(a, b, preferred_element_type=F32)


def _dot_nt(a, b):
    return lax.dot_general(a, b, (((1,), (1,)), ((), ())), preferred_element_type=F32)


def _dot_tn(a, b):
    return lax.dot_general(a, b, (((0,), (0,)), ((), ())), preferred_element_type=F32)


def _pad_rows(x, rows):
    if x.shape[0] == rows:
        return x
    return jnp.concatenate([x, jnp.zeros((rows - x.shape[0], x.shape[1]), x.dtype)], axis=0)


def _norm_matmul_body(x_ref, g_ref, w_ref, o_ref, h_ref):
    @pl.when(pl.program_id(1) == 0)
    def _():
        h_ref[...] = _rmsnorm_rows(x_ref[...], g_ref[...]).astype(BF16)

    o_ref[...] = _dot(h_ref[...], w_ref[...])


def _norm_matmul(x, g, w, tm, tn):
    t, d = x.shape
    n = w.shape[1]
    return pl.pallas_call(
        _norm_matmul_body,
        grid=(t // tm, n // tn),
        in_specs=[
            pl.BlockSpec((tm, d), lambda i, j: (i, 0)),
            pl.BlockSpec((1, d), lambda i, j: (0, 0)),
            pl.BlockSpec((d, tn), lambda i, j: (0, j)),
        ],
        out_specs=pl.BlockSpec((tm, tn), lambda i, j: (i, j)),
        out_shape=jax.ShapeDtypeStruct((t, n), F32),
        scratch_shapes=[pltpu.VMEM((tm, d), BF16)],
        compiler_params=_cparams(("parallel", "arbitrary")),
        name="norm_matmul",
    )(x, g, w)


def _proj_residual_body(*refs, n_in):
    a_refs = refs[:n_in]
    w_refs = refs[n_in:2 * n_in]
    x_ref = refs[2 * n_in]
    o_ref = refs[2 * n_in + 1]
    acc = x_ref[...]
    for a_ref, w_ref in zip(a_refs, w_refs):
        acc = acc + _dot(a_ref[...].astype(BF16), w_ref[...])
    o_ref[...] = acc


def _proj_residual(acts, weights, x, tm):
    t, d = x.shape
    n_in = len(acts)
    in_specs = [pl.BlockSpec((tm, a.shape[1]), lambda i: (i, 0)) for a in acts]
    in_specs += [pl.BlockSpec(w.shape, lambda i: (0, 0)) for w in weights]
    in_specs += [pl.BlockSpec((tm, d), lambda i: (i, 0))]
    return pl.pallas_call(
        functools.partial(_proj_residual_body, n_in=n_in),
        grid=(t // tm,),
        in_specs=in_specs,
        out_specs=pl.BlockSpec((tm, d), lambda i: (i, 0)),
        out_shape=jax.ShapeDtypeStruct((t, d), F32),
        compiler_params=_cparams(("parallel",)),
        name="proj_residual",
    )(*acts, *weights, x)


def _ffn_body(x_ref, g_ref, wg_ref, wu_ref, wd_ref, o_ref, h_ref, acc_ref):
    f = pl.program_id(1)

    @pl.when(f == 0)
    def _():
        h_ref[...] = _rmsnorm_rows(x_ref[...], g_ref[...]).astype(BF16)
        acc_ref[...] = jnp.zeros_like(acc_ref)

    h = h_ref[...]
    a = _dot(h, wg_ref[...])
    u = _dot(h, wu_ref[...])
    act = (_silu(a) * u).astype(BF16)
    acc_ref[...] += _dot(act, wd_ref[...])

    @pl.when(f == pl.num_programs(1) - 1)
    def _():
        o_ref[...] = x_ref[...] + acc_ref[...]


def _ffn(x, g, wg, wu, wd, tm, tf):
    t, d = x.shape
    dff = wg.shape[1]
    return pl.pallas_call(
        _ffn_body,
        grid=(t // tm, dff // tf),
        in_specs=[
            pl.BlockSpec((tm, d), lambda i, f: (i, 0)),
            pl.BlockSpec((1, d), lambda i, f: (0, 0)),
            pl.BlockSpec((d, tf), lambda i, f: (0, f)),
            pl.BlockSpec((d, tf), lambda i, f: (0, f)),
            pl.BlockSpec((tf, d), lambda i, f: (f, 0)),
        ],
        out_specs=pl.BlockSpec((tm, d), lambda i, f: (i, 0)),
        out_shape=jax.ShapeDtypeStruct((t, d), F32),
        scratch_shapes=[pltpu.VMEM((tm, d), BF16), pltpu.VMEM((tm, d), F32)],
        compiler_params=_cparams(("parallel", "arbitrary")),
        name="ffn",
    )(x, g, wg, wu, wd)


MOE_TB = 512
MOE_ALIGN = 16
MOE_TILE = 512
MOE_CHUNKS_PER_TILE = MOE_TILE // MOE_ALIGN


def _moe_block_rows(tb):
    return 2 * tb + N_EXPERTS * MOE_ALIGN


def _route_body(x_ref, g_ref, whi_ref, wlo_ref, b_ref, tri_ref, h_ref, col_ref, rowf_ref, cnt_ref):
    h = _rmsnorm_rows(x_ref[...], g_ref[...])
    h_hi = h.astype(BF16)
    h_ref[...] = h_hi
    h_lo = (h - h_hi.astype(F32)).astype(BF16)
    logits = (_dot(h_hi, whi_ref[...]) + _dot(h_hi, wlo_ref[...]) + _dot(h_lo, whi_ref[...])
              + b_ref[...])
    lane = lax.broadcasted_iota(jnp.int32, logits.shape, 1)
    lg = jnp.where(lane < N_EXPERTS, logits, NEG_INF)
    m1 = jnp.max(lg, axis=-1, keepdims=True)
    i1 = jnp.min(jnp.where(lg == m1, lane, LANES), axis=-1, keepdims=True)
    lg2 = jnp.where(lane == i1, NEG_INF, lg)
    m2 = jnp.max(lg2, axis=-1, keepdims=True)
    i2 = jnp.min(jnp.where(lg2 == m2, lane, LANES), axis=-1, keepdims=True)
    e = jnp.exp(m2 - m1)
    w1 = 1.0 / (1.0 + e)
    w2 = e / (1.0 + e)

    sel = jnp.where(lane == i1, 1.0, jnp.where(lane == i2, 1.0, 0.0))
    before = _dot(tri_ref[...], sel.astype(BF16))
    n = jnp.broadcast_to(jnp.sum(sel, axis=0, keepdims=True), (SUBLANES, LANES))
    padded = jnp.floor((n + (MOE_ALIGN - 1.0)) * (1.0 / MOE_ALIGN)) * MOE_ALIGN
    lane8 = lax.broadcasted_iota(jnp.int32, padded.shape, 1)
    incl = padded
    sh = 1
    while sh < N_EXPERTS:
        incl = incl + jnp.where(lane8 >= sh, pltpu.roll(incl, sh, 1), 0.0)
        sh *= 2
    pos = (incl - padded)[0:1, :] + before
    loc1 = jnp.sum(jnp.where(lane == i1, pos, 0.0), axis=-1, keepdims=True)
    loc2 = jnp.sum(jnp.where(lane == i2, pos, 0.0), axis=-1, keepdims=True)
    col = jnp.where(lane == 0, loc1, jnp.where(lane == 1, loc2, jnp.where(lane == 2, w1,
                                                                          jnp.where(lane == 3, w2, 0.0))))
    col_ref[...] = col
    for c in range(col.shape[0] // LANES):
        rowf_ref[0, :, c * LANES:(c + 1) * LANES] = jnp.transpose(col[c * LANES:(c + 1) * LANES, :])[:SUBLANES, :]
    cnt_ref[0] = n


def _route(x, g, w_hi, w_lo, b, tb):
    t, d = x.shape
    nb = t // tb
    tri = jnp.asarray(np.tril(np.ones((tb, tb), np.float32), -1), BF16)
    full = lambda shape: pl.BlockSpec(shape, lambda i: (0,) * len(shape))
    return pl.pallas_call(
        _route_body,
        grid=(nb,),
        in_specs=[
            pl.BlockSpec((tb, d), lambda i: (i, 0)),
            full((1, d)), full((d, LANES)), full((d, LANES)), full((1, LANES)), full((tb, tb)),
        ],
        out_specs=[
            pl.BlockSpec((tb, d), lambda i: (i, 0)),
            pl.BlockSpec((tb, LANES), lambda i: (i, 0)),
            pl.BlockSpec((1, SUBLANES, tb), lambda i: (i, 0, 0)),
            pl.BlockSpec((1, SUBLANES, LANES), lambda i: (i, 0, 0)),
        ],
        out_shape=[
            jax.ShapeDtypeStruct((t, d), BF16),
            jax.ShapeDtypeStruct((t, LANES), F32),
            jax.ShapeDtypeStruct((nb, SUBLANES, tb), F32),
            jax.ShapeDtypeStruct((nb, SUBLANES, LANES), F32),
        ],
        compiler_params=_cparams(("parallel",)),
        name="route",
    )(x, g, w_hi, w_lo, b, tri)


def _moe_plan(cnt, n_tiles):
    n = cnt[:, 0, :N_EXPERTS].astype(jnp.int32)
    chunks = (n + (MOE_ALIGN - 1)) // MOE_ALIGN
    src = jnp.cumsum(chunks, axis=1) - chunks
    seg_tiles = (jnp.sum(chunks, axis=0) + (MOE_CHUNKS_PER_TILE - 1)) // MOE_CHUNKS_PER_TILE
    seg_end = jnp.cumsum(seg_tiles)
    dst = ((seg_end - seg_tiles) * MOE_CHUNKS_PER_TILE)[None, :] + jnp.cumsum(chunks, axis=0) - chunks
    tile = jnp.arange(n_tiles, dtype=jnp.int32)
    tile_expert = jnp.minimum(jnp.sum((tile[:, None] >= seg_end[None, :]).astype(jnp.int32), axis=1),
                              N_EXPERTS - 1)
    tile_valid = (tile < seg_end[-1]).astype(jnp.int32)
    flat = lambda a: a.reshape(-1).astype(jnp.int32)
    return flat(src), flat(dst), flat(chunks), tile_expert.astype(jnp.int32), tile_valid


def _chunk_rows(chunk):
    return pl.ds(pl.multiple_of(chunk * MOE_ALIGN, MOE_ALIGN), MOE_ALIGN)


def _for_block_chunks(nch_ref, blk, fn):
    for e in range(N_EXPERTS):
        def body(c, carry, e=e):
            fn(e, c)
            return carry
        lax.fori_loop(0, nch_ref[blk * N_EXPERTS + e], body, 0)


def _block_chunk_count(nch_ref, blk):
    total = nch_ref[blk * N_EXPERTS]
    for e in range(1, N_EXPERTS):
        total = total + nch_ref[blk * N_EXPERTS + e]
    return total


def _gather_body(src_ref, dst_ref, nch_ref, h_ref, rowf_ref, zero_ref, hs_ref, z_ref, sem):
    del zero_ref
    b = pl.program_id(0)
    slot = lax.rem(b, 2)
    tb = h_ref.shape[0]
    loc = rowf_ref[0].astype(jnp.int32)
    r = lax.broadcasted_iota(jnp.int32, (z_ref.shape[1], tb), 0)
    onehot = jnp.where(r == loc[0:1, :], 1.0, jnp.where(r == loc[1:2, :], 1.0, 0.0)).astype(BF16)
    z_ref[slot] = _dot(onehot, h_ref[...]).astype(BF16)

    def copy(buf, src_chunk, dst_chunk):
        return pltpu.make_async_copy(z_ref.at[buf, _chunk_rows(src_chunk)], hs_ref.at[_chunk_rows(dst_chunk)],
                                     sem.at[buf])

    def drain(blk, buf):
        lax.fori_loop(0, _block_chunk_count(nch_ref, blk), lambda c, carry: (copy(buf, 0, 0).wait(), carry)[1], 0)

    @pl.when(b > 0)
    def _():
        drain(b - 1, 1 - slot)

    _for_block_chunks(nch_ref, b, lambda e, c: copy(slot, src_ref[b * N_EXPERTS + e] + c,
                                                    dst_ref[b * N_EXPERTS + e] + c).start())

    @pl.when(b == pl.num_programs(0) - 1)
    def _():
        drain(b, slot)


def _moe_gather(src, dst, nch, h, rowf, n_tiles, tb):
    t, d = h.shape
    rows = _moe_block_rows(tb)
    zeros = jnp.zeros((n_tiles * MOE_TILE, d), BF16)
    return pl.pallas_call(
        _gather_body,
        grid_spec=pltpu.PrefetchScalarGridSpec(
            num_scalar_prefetch=3,
            grid=(t // tb,),
            in_specs=[
                pl.BlockSpec((tb, d), lambda i, *_: (i, 0)),
                pl.BlockSpec((1, SUBLANES, tb), lambda i, *_: (i, 0, 0)),
                pl.BlockSpec(memory_space=pl.ANY),
            ],
            out_specs=pl.BlockSpec(memory_space=pl.ANY),
            scratch_shapes=[pltpu.VMEM((2, rows, d), BF16), pltpu.SemaphoreType.DMA((2,))],
        ),
        out_shape=jax.ShapeDtypeStruct(zeros.shape, BF16),
        input_output_aliases={5: 0},
        compiler_params=_cparams(("arbitrary",)),
        name="moe_gather",
    )(src, dst, nch, h, rowf, zeros)


def _expert_ffn_body(te_ref, tv_ref, xs_ref, wg_ref, wu_ref, wd_ref, o_ref, acc_ref):
    del te_ref
    f = pl.program_id(1)
    last_f = pl.num_programs(1) - 1
    valid = tv_ref[pl.program_id(0)] > 0

    @pl.when(valid)
    def _():
        xs = xs_ref[...]
        a = _dot(xs, wg_ref[0])
        u = _dot(xs, wu_ref[0])
        y = _dot((_silu(a) * u).astype(BF16), wd_ref[0])

        @pl.when(f == 0)
        def _():
            acc_ref[...] = y

        @pl.when(f != 0)
        def _():
            acc_ref[...] += y

        @pl.when(f == last_f)
        def _():
            o_ref[...] = acc_ref[...].astype(BF16)

    @pl.when(jnp.logical_not(valid) & (f == last_f))
    def _():
        o_ref[...] = jnp.zeros_like(o_ref)


def _expert_ffn(tile_expert, tile_valid, xs, wg, wu, wd, tf):
    rows, d = xs.shape
    dff = wg.shape[2]
    return pl.pallas_call(
        _expert_ffn_body,
        grid_spec=pltpu.PrefetchScalarGridSpec(
            num_scalar_prefetch=2,
            grid=(rows // MOE_TILE, dff // tf),
            in_specs=[
                pl.BlockSpec((MOE_TILE, d), lambda i, f, te, tv: (i, 0)),
                pl.BlockSpec((1, d, tf), lambda i, f, te, tv: (te[i], 0, f)),
                pl.BlockSpec((1, d, tf), lambda i, f, te, tv: (te[i], 0, f)),
                pl.BlockSpec((1, tf, d), lambda i, f, te, tv: (te[i], f, 0)),
            ],
            out_specs=pl.BlockSpec((MOE_TILE, d), lambda i, f, te, tv: (i, 0)),
            scratch_shapes=[pltpu.VMEM((MOE_TILE, d), F32)],
        ),
        out_shape=jax.ShapeDtypeStruct((rows, d), BF16),
        compiler_params=_cparams(("arbitrary", "arbitrary")),
        name="expert_ffn",
    )(tile_expert, tile_valid, xs, wg, wu, wd)


def _combine_body(src_ref, dst_ref, nch_ref, x_ref, col_ref, gf_ref, ys_ref, o_ref, y_ref, sem):
    b = pl.program_id(0)
    nb = pl.num_programs(0)
    slot = lax.rem(b, 2)

    def copy(buf, src_chunk, dst_chunk):
        return pltpu.make_async_copy(ys_ref.at[_chunk_rows(dst_chunk)], y_ref.at[buf, _chunk_rows(src_chunk)],
                                     sem.at[buf])

    def fetch(blk, buf):
        _for_block_chunks(nch_ref, blk, lambda e, c: copy(buf, src_ref[blk * N_EXPERTS + e] + c,
                                                          dst_ref[blk * N_EXPERTS + e] + c).start())

    @pl.when(b == 0)
    def _():
        y_ref[...] = jnp.zeros_like(y_ref)
        fetch(0, 0)

    @pl.when(b + 1 < nb)
    def _():
        fetch(b + 1, 1 - slot)

    lax.fori_loop(0, _block_chunk_count(nch_ref, b), lambda c, carry: (copy(slot, 0, 0).wait(), carry)[1], 0)

    col = col_ref[...]
    loc1 = col[:, 0:1].astype(jnp.int32)
    loc2 = col[:, 1:2].astype(jnp.int32)
    r = lax.broadcasted_iota(jnp.int32, (col.shape[0], y_ref.shape[1]), 1)
    weights = jnp.where(r == loc1, col[:, 2:3], jnp.where(r == loc2, col[:, 3:4], 0.0)).astype(BF16)
    o_ref[...] = _rmsnorm_rows(x_ref[...] + _dot(weights, y_ref[slot]), gf_ref[...])


def _moe_combine(src, dst, nch, x, col, g_final, ys, tb):
    t, d = x.shape
    rows = _moe_block_rows(tb)
    return pl.pallas_call(
        _combine_body,
        grid_spec=pltpu.PrefetchScalarGridSpec(
            num_scalar_prefetch=3,
            grid=(t // tb,),
            in_specs=[
                pl.BlockSpec((tb, d), lambda i, *_: (i, 0)),
                pl.BlockSpec((tb, LANES), lambda i, *_: (i, 0)),
                pl.BlockSpec((1, d), lambda i, *_: (0, 0)),
                pl.BlockSpec(memory_space=pl.ANY),
            ],
            out_specs=pl.BlockSpec((tb, d), lambda i, *_: (i, 0)),
            scratch_shapes=[pltpu.VMEM((2, rows, d), BF16), pltpu.SemaphoreType.DMA((2,))],
        ),
        out_shape=jax.ShapeDtypeStruct((t, d), F32),
        compiler_params=_cparams(("arbitrary",)),
        name="moe_combine",
    )(src, dst, nch, x, col, g_final, ys)


def _moe(x, g, w, g_final):
    t, _ = x.shape
    tb = min(MOE_TB, t)
    nb = t // tb
    h, col, rowf, cnt = _route(x, g, w["w_router_hi"], w["w_router_lo"], w["b_router"], tb)
    n_tiles = -(-(2 * t + nb * N_EXPERTS * (MOE_ALIGN - 1)) // MOE_TILE) + N_EXPERTS
    src, dst, nch, tile_expert, tile_valid = _moe_plan(cnt, n_tiles)
    xs = _moe_gather(src, dst, nch, h, rowf, n_tiles, tb)
    ys = _expert_ffn(tile_expert, tile_valid, xs, w["w_moe_gate"], w["w_moe_up"], w["w_moe_down"], D_FF // 2)
    return _moe_combine(src, dst, nch, x, col, g_final, ys, tb)


def _retention_tables(c_real, ct):
    h = np.arange(RET_HEADS, dtype=np.float64)
    log_gamma = np.log1p(-np.exp2(-5.0 - h))
    idx = np.arange(ct, dtype=np.float64)
    live = idx < c_real
    diff = idx[:, None] - idx[None, :]
    causal = (diff >= 0) & live[:, None] & live[None, :]
    decay = np.where(causal[None], np.exp(np.where(causal, diff, 0.0)[None] * log_gamma[:, None, None]), 0.0)
    q_dec = np.where(live[None], np.exp((idx + 1.0)[None] * log_gamma[:, None]), 0.0)
    k_dec = np.where(live[None], np.exp((c_real - 1.0 - idx)[None] * log_gamma[:, None]), 0.0)
    q_dec = np.broadcast_to(q_dec[..., None], (RET_HEADS, ct, LANES))
    k_dec = np.broadcast_to(k_dec[..., None], (RET_HEADS, ct, LANES))
    return (jnp.asarray(decay, F32), jnp.asarray(q_dec, F32), jnp.asarray(k_dec, F32))


def _rope_tables(pos0, length):
    half = HEAD_DIM // 2
    inv = ROPE_BASE ** (-np.arange(half, dtype=np.float64) / half)
    ang = (pos0 + np.arange(length, dtype=np.float64))[:, None] * inv[None, :]
    cos = np.concatenate([np.cos(ang), np.cos(ang)], axis=-1)
    sin = np.concatenate([-np.sin(ang), np.sin(ang)], axis=-1)
    return jnp.asarray(cos, F32), jnp.asarray(sin, F32)


def _retention_body(q_ref, k_ref, v_ref, g_ref, cos_ref, sin_ref, dec_ref, qd_ref, kd_ref, s0_ref, gn_ref,
                    o_ref, s_ref, *, c_real, ct):
    @pl.when(pl.program_id(2) == 0)
    def _():
        s_ref[...] = s0_ref[...]

    cos = cos_ref[...]
    sin = sin_ref[...]
    q = q_ref[0]
    k = k_ref[0]
    q = (q * cos + pltpu.roll(q, HEAD_DIM // 2, 1) * sin) * (HEAD_DIM ** -0.5)
    k = k * cos + pltpu.roll(k, HEAD_DIM // 2, 1) * sin
    q = _pad_rows(q, ct)
    k = _pad_rows(k, ct)
    vb = _pad_rows(v_ref[0], ct).astype(BF16)
    qb = q.astype(BF16)
    q_dec = qd_ref[0]
    state = s_ref[0, 0]

    s = _dot_nt(qb, k.astype(BF16)) * dec_ref[0]
    o = _dot(s.astype(BF16), vb) + _dot(qb, state.astype(BF16)) * q_dec
    chunk_dec = q_dec[c_real - 1:c_real, :]
    s_ref[0, 0] = state * chunk_dec + _dot_tn((k * kd_ref[0]).astype(BF16), vb)

    o = o[:c_real]
    o = o - jnp.mean(o, axis=-1, keepdims=True)
    o = o * lax.rsqrt(jnp.mean(o * o, axis=-1, keepdims=True) + EPS)
    o_ref[0] = _silu(g_ref[0]) * (o * gn_ref[...])


def _retention(proj, pos0, state, gn):
    b, length, _ = proj.shape
    c_real = math.gcd(length, RET_CHUNK)
    ct = RET_CHUNK
    cos, sin = _rope_tables(pos0, length)
    decay, q_dec, k_dec = _retention_tables(c_real, ct)

    def col(off):
        return pl.BlockSpec((1, c_real, HEAD_DIM), lambda bi, h, c: (bi, c, off + h))

    table = lambda last: pl.BlockSpec((1, ct, last), lambda bi, h, c: (h, 0, 0))
    state_spec = pl.BlockSpec((1, 1, HEAD_DIM, HEAD_DIM), lambda bi, h, c: (bi, h, 0, 0))
    return pl.pallas_call(
        functools.partial(_retention_body, c_real=c_real, ct=ct),
        grid=(b, RET_HEADS, length // c_real),
        in_specs=[
            col(0), col(RET_HEADS), col(2 * RET_HEADS), col(3 * RET_HEADS),
            pl.BlockSpec((c_real, HEAD_DIM), lambda bi, h, c: (c, 0)),
            pl.BlockSpec((c_real, HEAD_DIM), lambda bi, h, c: (c, 0)),
            table(ct), table(LANES), table(LANES),
            state_spec,
            pl.BlockSpec((1, HEAD_DIM), lambda bi, h, c: (0, h)),
        ],
        out_specs=[
            pl.BlockSpec((1, c_real, HEAD_DIM), lambda bi, h, c: (bi, c, h)),
            state_spec,
        ],
        out_shape=[
            jax.ShapeDtypeStruct((b, length, RET_HEADS * HEAD_DIM), F32),
            jax.ShapeDtypeStruct(state.shape, F32),
        ],
        compiler_params=_cparams(("parallel", "parallel", "arbitrary")),
        name="retention",
    )(proj, proj, proj, proj, cos, sin, decay, q_dec, k_dec, state, gn)


def _scan_rows(x, op, fill):
    rows = x.shape[0]
    row = lax.broadcasted_iota(jnp.int32, x.shape, 0)
    sh = 1
    while sh < rows:
        x = op(x, jnp.where(row >= sh, pltpu.roll(x, sh, 0), fill))
        sh *= 2
    return x


def _mlstm_body(q_ref, k_ref, v_ref, og_ref, gate_ref, bg_ref, cw_ref, cb_ref, cs_ref, c0_ref, n0_ref, m0_ref,
                gn_ref, o_ref, c_ref, n_ref, m_ref, conv_ref, xc_ref, *, c_real, ct):
    width = ML_HEADS * HEAD_DIM
    tail = ML_CONV - 1

    @pl.when(pl.program_id(1) == 0)
    def _():
        c_ref[...] = c0_ref[...]
        n_ref[...] = n0_ref[...]
        m_ref[...] = m0_ref[...]
        xc_ref[SUBLANES - tail:SUBLANES, :] = cs_ref[0]

    xc_ref[SUBLANES:SUBLANES + c_real, :width] = q_ref[0]
    xc_ref[SUBLANES:SUBLANES + c_real, width:] = k_ref[0]
    y = cb_ref[...] + cw_ref[tail:tail + 1, :] * xc_ref[SUBLANES:SUBLANES + c_real, :]
    for j in range(tail):
        y = y + cw_ref[j:j + 1, :] * xc_ref[SUBLANES - tail + j:SUBLANES - tail + j + c_real, :]
    new_tail = xc_ref[c_real:c_real + SUBLANES, :]
    conv_ref[0] = new_tail[SUBLANES - tail:, :]
    xc_ref[0:SUBLANES, :] = new_tail
    qk = _silu(y)
    q_all = _pad_rows(qk[:, :width], ct)
    k_all = _pad_rows(qk[:, width:] * (HEAD_DIM ** -0.5), ct)
    v_all = _pad_rows(v_ref[0], ct)

    gates = _pad_rows(gate_ref[0] + bg_ref[...], ct)
    f_pre = pltpu.roll(gates, LANES - ML_HEADS, 1)
    log_f = jnp.minimum(f_pre, 0.0) - jnp.log(1.0 + jnp.exp(-jnp.abs(f_pre)))
    b_cum = _scan_rows(log_f, jnp.add, 0.0)
    src = gates - b_cum
    cmx = _scan_rows(src, jnp.maximum, NEG_INF)
    m_prev = m_ref[0]
    mx = jnp.maximum(m_prev, cmx)
    m_t = b_cum + mx
    w_prev = jnp.exp(m_prev - mx)
    e_neg_m = jnp.exp(-m_t)
    last = c_real - 1
    b_last = b_cum[last:last + 1, :]
    m_new = m_t[last:last + 1, :]
    row = lax.broadcasted_iota(jnp.int32, (ct, LANES), 0)
    w_end = jnp.where(row < c_real, jnp.exp(b_last + src - m_new), 0.0)
    dec = jnp.exp(b_last + m_prev - m_new)
    m_ref[0] = m_new
    src_t = jnp.transpose(src)

    t_idx = lax.broadcasted_iota(jnp.int32, (ct, ct), 0)
    s_idx = lax.broadcasted_iota(jnp.int32, (ct, ct), 1)
    live = (s_idx <= t_idx) & (s_idx < c_real)

    for hd in range(ML_HEADS):
        lo, hi = hd * HEAD_DIM, (hd + 1) * HEAD_DIM
        q = q_all[:, lo:hi]
        k = k_all[:, lo:hi]
        qb = q.astype(BF16)
        vb = v_all[:, lo:hi].astype(BF16)
        c_state = c_ref[0, hd]
        n_state = n_ref[0, hd:hd + 1, :]
        wp = w_prev[:, hd:hd + 1]
        w = jnp.where(live, jnp.exp(src_t[hd:hd + 1, :] - mx[:, hd:hd + 1]), 0.0)
        s = _dot_nt(qb, k.astype(BF16)) * w
        num = _dot(s.astype(BF16), vb) + wp * _dot(qb, c_state.astype(BF16))
        den = jnp.sum(s, axis=-1, keepdims=True) + wp * jnp.sum(q * n_state, axis=-1, keepdims=True)
        hh = num / jnp.maximum(jnp.abs(den), e_neg_m[:, hd:hd + 1])
        kw = k * w_end[:, hd:hd + 1]
        dec_h = dec[:, hd:hd + 1]
        c_ref[0, hd] = dec_h * c_state + _dot_tn(kw.astype(BF16), vb)
        n_ref[0, hd:hd + 1, :] = dec_h * n_state + jnp.sum(kw, axis=0, keepdims=True)

        hh = hh[:c_real]
        hh = _sigmoid(og_ref[0, :, lo:hi]) * hh
        hh = hh - jnp.mean(hh, axis=-1, keepdims=True)
        hh = hh * lax.rsqrt(jnp.mean(hh * hh, axis=-1, keepdims=True) + EPS)
        o_ref[0, :, lo:hi] = hh * gn_ref[:, lo:hi]


def _mlstm(proj, b_gates, conv_w, conv_b, conv_s, c0, n0, m0, gn):
    b, length, _ = proj.shape
    c_real = math.gcd(length, ML_CHUNK)
    ct = ML_CHUNK
    width = ML_HEADS * HEAD_DIM
    base = (2 * RET_HEADS * HEAD_DIM + 2 * RET_HEADS * HEAD_DIM) // width

    def col(i):
        return pl.BlockSpec((1, c_real, width), lambda bi, c: (bi, c, base + i))

    full = lambda shape: pl.BlockSpec(shape, lambda bi, c: (0,) * len(shape))
    per_b = lambda shape: pl.BlockSpec((1,) + shape, lambda bi, c: (bi,) + (0,) * len(shape))
    bg = jnp.zeros((1, LANES), F32).at[0, :2 * ML_HEADS].set(b_gates)
    m0p = jnp.zeros((b, 1, LANES), F32).at[:, 0, :ML_HEADS].set(m0)
    outs = pl.pallas_call(
        functools.partial(_mlstm_body, c_real=c_real, ct=ct),
        grid=(b, length // c_real),
        in_specs=[
            col(0), col(1), col(2), col(3),
            pl.BlockSpec((1, c_real, LANES), lambda bi, c: (bi, c, AB_MAIN // LANES)),
            full((1, LANES)),
            full((ML_CONV, 2 * width)),
            full((1, 2 * width)),
            per_b((ML_CONV - 1, 2 * width)),
            per_b((ML_HEADS, HEAD_DIM, HEAD_DIM)),
            per_b((ML_HEADS, HEAD_DIM)),
            per_b((1, LANES)),
            full((1, width)),
        ],
        out_specs=[
            pl.BlockSpec((1, c_real, width), lambda bi, c: (bi, c, 0)),
            per_b((ML_HEADS, HEAD_DIM, HEAD_DIM)),
            per_b((ML_HEADS, HEAD_DIM)),
            per_b((1, LANES)),
            per_b((ML_CONV - 1, 2 * width)),
        ],
        out_shape=[
            jax.ShapeDtypeStruct((b, length, width), F32),
            jax.ShapeDtypeStruct(c0.shape, F32),
            jax.ShapeDtypeStruct(n0.shape, F32),
            jax.ShapeDtypeStruct((b, 1, LANES), F32),
            jax.ShapeDtypeStruct(conv_s.shape, F32),
        ],
        scratch_shapes=[pltpu.VMEM((ct + SUBLANES, 2 * width), F32)],
        compiler_params=_cparams(("parallel", "arbitrary")),
        name="mlstm",
    )(proj, proj, proj, proj, proj, bg, conv_w, conv_b, conv_s, c0, n0, m0p, gn)
    h_m, c_new, n_new, m_new, conv_new = outs
    return h_m, c_new, n_new, m_new[:, 0, :ML_HEADS], conv_new


def _hgrn_body(q_ref, f_ref, i_ref, g_ref, hl_ref, s0_ref, gn_ref, o_ref, s_ref, st_ref, *, blk, sub):
    @pl.when(pl.program_id(2) == 0)
    def _():
        st_ref[...] = jnp.transpose(s0_ref[0, 0])

    hl = hl_ref[...]
    hmax = jnp.max(hl, axis=0, keepdims=True)
    ex = jnp.exp(hl - hmax)
    p = ex / jnp.sum(ex, axis=0, keepdims=True)
    lower = (p[0:1] + p[1:2]) - p[0:1]

    f = lower + (1.0 - lower) * _sigmoid(f_ref[0])
    log_f = jnp.log(f)
    kk = 1.0 - f
    q = _silu(q_ref[0])
    v = i_ref[0]

    row = lax.broadcasted_iota(jnp.int32, (blk, LANES), 0)
    rsub = row & (sub - 1)
    a = log_f
    sh = 1
    while sh < sub:
        a = a + jnp.where(rsub >= sh, pltpu.roll(a, sh, 0), 0.0)
        sh *= 2
    q_in = q * jnp.exp(a)

    t_idx = lax.broadcasted_iota(jnp.int32, (sub, LANES), 0)
    outs = []
    for j in range(blk // sub):
        r0 = j * sub
        a_j = a[r0:r0 + sub]
        q_j = q[r0:r0 + sub]
        k_j = kk[r0:r0 + sub]
        v_j = v[r0:r0 + sub]
        a_end = a_j[sub - 1:sub, :]
        st = st_ref[...]
        o_j = _dot_nt(q_in[r0:r0 + sub].astype(BF16), st.astype(BF16))
        for s in range(sub):
            z = q_j * jnp.exp(jnp.where(t_idx >= s, a_j - a_j[s:s + 1, :], NEG_INF)) * k_j[s:s + 1, :]
            o_j = o_j + jnp.sum(z, axis=-1, keepdims=True) * v_j[s:s + 1, :]
        k_hat = k_j * jnp.exp(a_end - a_j)
        st_ref[...] = st * jnp.exp(a_end) + _dot_tn(v_j.astype(BF16), k_hat.astype(BF16))
        outs.append(o_j)
    o = outs[0] if len(outs) == 1 else jnp.concatenate(outs, axis=0)
    o = o * lax.rsqrt(jnp.mean(o * o, axis=-1, keepdims=True) + EPS)
    o_ref[0] = o * gn_ref[...] * _silu(g_ref[0])

    @pl.when(pl.program_id(2) == pl.num_programs(2) - 1)
    def _():
        s_ref[0, 0] = jnp.transpose(st_ref[...])


def _hgrn(proj, hg_lower, state, gn):
    b, length, _ = proj.shape
    blk = min(length, HG_BLOCK)
    sub = math.gcd(length, HG_SUB)

    def col(off):
        return pl.BlockSpec((1, blk, HEAD_DIM), lambda bi, h, c: (bi, c, off + h))

    state_spec = pl.BlockSpec((1, 1, HEAD_DIM, HEAD_DIM), lambda bi, h, c: (bi, h, 0, 0))
    return pl.pallas_call(
        functools.partial(_hgrn_body, blk=blk, sub=sub),
        grid=(b, HG_HEADS, length // blk),
        in_specs=[
            col(0), col(HG_HEADS), col(2 * HG_HEADS), col(3 * HG_HEADS),
            pl.BlockSpec((hg_lower.shape[0], HEAD_DIM), lambda bi, h, c: (0, h)),
            state_spec,
            pl.BlockSpec((1, HEAD_DIM), lambda bi, h, c: (0, h)),
        ],
        out_specs=[
            pl.BlockSpec((1, blk, HEAD_DIM), lambda bi, h, c: (bi, c, h)),
            state_spec,
        ],
        out_shape=[
            jax.ShapeDtypeStruct((b, length, HG_HEADS * HEAD_DIM), F32),
            jax.ShapeDtypeStruct(state.shape, F32),
        ],
        scratch_shapes=[pltpu.VMEM((HEAD_DIM, HEAD_DIM), F32)],
        compiler_params=_cparams(("parallel", "parallel", "arbitrary")),
        name="hgrn2",
    )(proj, proj, proj, proj, hg_lower, state, gn)


def _prep_weights(p):
    w_in_ab = p["w_in_ab"][0]
    w_ab = jnp.zeros((D_MODEL, AB_PAD), BF16)
    w_ab = w_ab.at[:, :AB_MAIN].set(w_in_ab[:, :AB_MAIN].astype(BF16))
    w_ab = w_ab.at[:, AB_MAIN:AB_MAIN + 2 * ML_HEADS].set(w_in_ab[:, AB_MAIN:].astype(BF16))
    w_router = jnp.zeros((D_MODEL, LANES), F32).at[:, :N_EXPERTS].set(p["w_router"][0])
    w_router_hi = w_router.astype(BF16)
    w_router_lo = (w_router - w_router_hi.astype(F32)).astype(BF16)
    b_router = jnp.zeros((1, LANES), F32).at[0, :N_EXPERTS].set(p["b_router"][0])
    return {
        "w_ab": w_ab,
        "w_out_ab": p["w_out_ab"][0].astype(BF16),
        "w_in_c": p["w_in_c"][0].astype(BF16),
        "w_out_c": p["w_out_c"][0].astype(BF16),
        "w_ffn_gate": p["w_ffn_gate"][0].astype(BF16),
        "w_ffn_up": p["w_ffn_up"][0].astype(BF16),
        "w_ffn_down": p["w_ffn_down"][0].astype(BF16),
        "w_router_hi": w_router_hi,
        "w_router_lo": w_router_lo,
        "b_router": b_router,
        "w_moe_gate": p["w_moe_gate"][0].astype(BF16),
        "w_moe_up": p["w_moe_up"][0].astype(BF16),
        "w_moe_down": p["w_moe_down"][0].astype(BF16),
    }


def _trunk(x, pos0, ret_s, mc_s, mn_s, mm_s, conv_s, hg_s, p, w):
    b, length, d = x.shape
    t = b * length
    tm = min(512, t)
    row = lambda v: v.reshape(1, -1)
    x0 = x.reshape(t, d)

    proj = _norm_matmul(x0, row(p["ln_mix"][0]), w["w_ab"], tm, AB_PAD // 3).reshape(b, length, AB_PAD)
    o_ret, ret_new = _retention(proj, pos0, ret_s, row(p["ret_gn"][0]))
    h_m, c_new, n_new, m_new, conv_new = _mlstm(
        proj, p["b_gates_ab"][0], p["conv_w_ab"][0], row(p["conv_b_ab"][0]), conv_s, mc_s, mn_s, mm_s,
        row(p["ml_gn"][0]))
    half = RET_HEADS * HEAD_DIM
    x1 = _proj_residual([o_ret.reshape(t, half), h_m.reshape(t, half)],
                        [w["w_out_ab"][:half], w["w_out_ab"][half:]], x0, tm)
    x2 = _ffn(x1, row(p["ln_ffn"][0]), w["w_ffn_gate"], w["w_ffn_up"], w["w_ffn_down"], tm, D_FF // 2)

    proj_c = _norm_matmul(x2, row(p["ln_mix"][1]), w["w_in_c"], tm, 1024).reshape(b, length, 4 * D_MODEL)
    o_hg, hg_new = _hgrn(proj_c, p["hg_lower"], hg_s, row(p["hg_gn"][0]))
    x3 = _proj_residual([o_hg.reshape(t, d)], [w["w_out_c"]], x2, tm)
    y = _moe(x3, row(p["ln_ffn"][1]), w, row(p["ln_final"]))
    return (y.reshape(b, length, d), ret_new[None], c_new[None], n_new[None], m_new[None], conv_new[None],
            hg_new[None])


def kernel(x_prompt, x_sample, state_ret, state_mlstm_c, state_mlstm_n, state_mlstm_m, state_conv, state_hgrn,
           ln_mix, ln_ffn, ln_final, w_in_ab, b_gates_ab, conv_w_ab, conv_b_ab, ret_gn, ml_gn, w_out_ab,
           w_in_c, hg_lower, hg_gn, w_out_c, w_ffn_gate, w_ffn_up, w_ffn_down,
           w_router, b_router, w_moe_gate, w_moe_up, w_moe_down):
    p = {"ln_mix": ln_mix, "ln_ffn": ln_ffn, "ln_final": ln_final,
         "w_in_ab": w_in_ab, "b_gates_ab": b_gates_ab, "conv_w_ab": conv_w_ab, "conv_b_ab": conv_b_ab,
         "ret_gn": ret_gn, "ml_gn": ml_gn, "w_out_ab": w_out_ab,
         "w_in_c": w_in_c, "hg_lower": hg_lower, "hg_gn": hg_gn, "w_out_c": w_out_c,
         "w_ffn_gate": w_ffn_gate, "w_ffn_up": w_ffn_up, "w_ffn_down": w_ffn_down,
         "w_router": w_router, "b_router": b_router,
         "w_moe_gate": w_moe_gate, "w_moe_up": w_moe_up, "w_moe_down": w_moe_down}
    w = _prep_weights(p)
    bp = x_prompt.shape[0]
    zeros = lambda *shape: jnp.zeros(shape, F32)
    prompt = _trunk(
        x_prompt, 0,
        zeros(bp, RET_HEADS, HEAD_DIM, HEAD_DIM), zeros(bp, ML_HEADS, HEAD_DIM, HEAD_DIM),
        zeros(bp, ML_HEADS, HEAD_DIM), jnp.full((bp, ML_HEADS), NEG_INF, F32),
        zeros(bp, ML_CONV - 1, 2 * ML_HEADS * HEAD_DIM), zeros(bp, HG_HEADS, HEAD_DIM, HEAD_DIM), p, w)
    sample = _trunk(
        x_sample, 16384,
        state_ret[0], state_mlstm_c[0], state_mlstm_n[0], state_mlstm_m[0], state_conv[0], state_hgrn[0], p, w)
    out = []
    for a, s in zip(prompt, sample):
        out += [a, s]
    return tuple(out)
```

```python
import functools
import math

import numpy as np
import jax
import jax.numpy as jnp
from jax import lax
from jax.experimental import pallas as pl
from jax.experimental.pallas import tpu as pltpu

F32 = jnp.float32
BF16 = jnp.bfloat16

D_MODEL = 1024
HEAD_DIM = 128
RET_HEADS = 4
ML_HEADS = 4
ML_CONV = 4
HG_HEADS = 8
D_FF = 2816
N_EXPERTS = 8
ROPE_BASE = 10000.0
EPS = 1e-6
RET_CHUNK = 128
ML_CHUNK = 128
HG_BLOCK = 128
HG_SUB = 16

LANES = 128
SUBLANES = 8
AB_MAIN = 4096
AB_PAD = 4224
VMEM_LIMIT = 56 * 1024 * 1024

NEG_INF = float("-inf")


def _cparams(sem):
    return pltpu.CompilerParams(dimension_semantics=sem, vmem_limit_bytes=VMEM_LIMIT)


def _sigmoid(x):
    return 1.0 / (1.0 + jnp.exp(-x))


def _silu(x):
    return x * _sigmoid(x)


def _rmsnorm_rows(x, g):
    ms = jnp.mean(x * x, axis=-1, keepdims=True)
    return x * lax.rsqrt(ms + EPS) * g


def _dot(a, b):
    return jnp.dot(a, b, preferred_element_type=F32)


def _dot_nt(a, b):
    return lax.dot_general(a, b, (((1,), (1,)), ((), ())), preferred_element_type=F32)


def _dot_tn(a, b):
    return lax.dot_general(a, b, (((0,), (0,)), ((), ())), preferred_element_type=F32)


def _pad_rows(x, rows):
    if x.shape[0] == rows:
        return x
    return jnp.concatenate([x, jnp.zeros((rows - x.shape[0], x.shape[1]), x.dtype)], axis=0)


def _norm_matmul_body(x_ref, g_ref, w_ref, o_ref, h_ref):
    @pl.when(pl.program_id(1) == 0)
    def _():
        h_ref[...] = _rmsnorm_rows(x_ref[...], g_ref[...]).astype(BF16)

    o_ref[...] = _dot(h_ref[...], w_ref[...])


def _norm_matmul(x, g, w, tm, tn):
    t, d = x.shape
    n = w.shape[1]
    return pl.pallas_call(
        _norm_matmul_body,
        grid=(t // tm, n // tn),
        in_specs=[
            pl.BlockSpec((tm, d), lambda i, j: (i, 0)),
            pl.BlockSpec((1, d), lambda i, j: (0, 0)),
            pl.BlockSpec((d, tn), lambda i, j: (0, j)),
        ],
        out_specs=pl.BlockSpec((tm, tn), lambda i, j: (i, j)),
        out_shape=jax.ShapeDtypeStruct((t, n), F32),
        scratch_shapes=[pltpu.VMEM((tm, d), BF16)],
        compiler_params=_cparams(("parallel", "arbitrary")),
        name="norm_matmul",
    )(x, g, w)


def _proj_residual_body(*refs, n_in):
    a_refs = refs[:n_in]
    w_refs = refs[n_in:2 * n_in]
    x_ref = refs[2 * n_in]
    o_ref = refs[2 * n_in + 1]
    acc = x_ref[...]
    for a_ref, w_ref in zip(a_refs, w_refs):
        acc = acc + _dot(a_ref[...].astype(BF16), w_ref[...])
    o_ref[...] = acc


def _proj_residual(acts, weights, x, tm):
    t, d = x.shape
    n_in = len(acts)
    in_specs = [pl.BlockSpec((tm, a.shape[1]), lambda i: (i, 0)) for a in acts]
    in_specs += [pl.BlockSpec(w.shape, lambda i: (0, 0)) for w in weights]
    in_specs += [pl.BlockSpec((tm, d), lambda i: (i, 0))]
    return pl.pallas_call(
        functools.partial(_proj_residual_body, n_in=n_in),
        grid=(t // tm,),
        in_specs=in_specs,
        out_specs=pl.BlockSpec((tm, d), lambda i: (i, 0)),
        out_shape=jax.ShapeDtypeStruct((t, d), F32),
        compiler_params=_cparams(("parallel",)),
        name="proj_residual",
    )(*acts, *weights, x)


def _ffn_body(x_ref, g_ref, wg_ref, wu_ref, wd_ref, o_ref, h_ref, acc_ref):
    f = pl.program_id(1)

    @pl.when(f == 0)
    def _():
        h_ref[...] = _rmsnorm_rows(x_ref[...], g_ref[...]).astype(BF16)
        acc_ref[...] = jnp.zeros_like(acc_ref)

    h = h_ref[...]
    a = _dot(h, wg_ref[...])
    u = _dot(h, wu_ref[...])
    act = (_silu(a) * u).astype(BF16)
    acc_ref[...] += _dot(act, wd_ref[...])

    @pl.when(f == pl.num_programs(1) - 1)
    def _():
        o_ref[...] = x_ref[...] + acc_ref[...]


def _ffn(x, g, wg, wu, wd, tm, tf):
    t, d = x.shape
    dff = wg.shape[1]
    return pl.pallas_call(
        _ffn_body,
        grid=(t // tm, dff // tf),
        in_specs=[
            pl.BlockSpec((tm, d), lambda i, f: (i, 0)),
            pl.BlockSpec((1, d), lambda i, f: (0, 0)),
            pl.BlockSpec((d, tf), lambda i, f: (0, f)),
            pl.BlockSpec((d, tf), lambda i, f: (0, f)),
            pl.BlockSpec((tf, d), lambda i, f: (f, 0)),
        ],
        out_specs=pl.BlockSpec((tm, d), lambda i, f: (i, 0)),
        out_shape=jax.ShapeDtypeStruct((t, d), F32),
        scratch_shapes=[pltpu.VMEM((tm, d), BF16), pltpu.VMEM((tm, d), F32)],
        compiler_params=_cparams(("parallel", "arbitrary")),
        name="ffn",
    )(x, g, wg, wu, wd)


MOE_TB = 512
MOE_ALIGN = 16
MOE_TILE = 512
MOE_CHUNKS_PER_TILE = MOE_TILE // MOE_ALIGN


def _moe_block_rows(tb):
    return 2 * tb + N_EXPERTS * MOE_ALIGN


def _route_body(x_ref, g_ref, whi_ref, wlo_ref, b_ref, tri_ref, h_ref, col_ref, rowf_ref, cnt_ref):
    h = _rmsnorm_rows(x_ref[...], g_ref[...])
    h_hi = h.astype(BF16)
    h_ref[...] = h_hi
    h_lo = (h - h_hi.astype(F32)).astype(BF16)
    logits = (_dot(h_hi, whi_ref[...]) + _dot(h_hi, wlo_ref[...]) + _dot(h_lo, whi_ref[...])
              + b_ref[...])
    lane = lax.broadcasted_iota(jnp.int32, logits.shape, 1)
    lg = jnp.where(lane < N_EXPERTS, logits, NEG_INF)
    m1 = jnp.max(lg, axis=-1, keepdims=True)
    i1 = jnp.min(jnp.where(lg == m1, lane, LANES), axis=-1, keepdims=True)
    lg2 = jnp.where(lane == i1, NEG_INF, lg)
    m2 = jnp.max(lg2, axis=-1, keepdims=True)
    i2 = jnp.min(jnp.where(lg2 == m2, lane, LANES), axis=-1, keepdims=True)
    e = jnp.exp(m2 - m1)
    w1 = 1.0 / (1.0 + e)
    w2 = e / (1.0 + e)

    sel = jnp.where(lane == i1, 1.0, jnp.where(lane == i2, 1.0, 0.0))
    before = _dot(tri_ref[...], sel.astype(BF16))
    n = jnp.broadcast_to(jnp.sum(sel, axis=0, keepdims=True), (SUBLANES, LANES))
    padded = jnp.floor((n + (MOE_ALIGN - 1.0)) * (1.0 / MOE_ALIGN)) * MOE_ALIGN
    lane8 = lax.broadcasted_iota(jnp.int32, padded.shape, 1)
    incl = padded
    sh = 1
    while sh < N_EXPERTS:
        incl = incl + jnp.where(lane8 >= sh, pltpu.roll(incl, sh, 1), 0.0)
        sh *= 2
    pos = (incl - padded)[0:1, :] + before
    loc1 = jnp.sum(jnp.where(lane == i1, pos, 0.0), axis=-1, keepdims=True)
    loc2 = jnp.sum(jnp.where(lane == i2, pos, 0.0), axis=-1, keepdims=True)
    col = jnp.where(lane == 0, loc1, jnp.where(lane == 1, loc2, jnp.where(lane == 2, w1,
                                                                          jnp.where(lane == 3, w2, 0.0))))
    col_ref[...] = col
    for c in range(col.shape[0] // LANES):
        rowf_ref[0, :, c * LANES:(c + 1) * LANES] = jnp.transpose(col[c * LANES:(c + 1) * LANES, :])[:SUBLANES, :]
    cnt_ref[0] = n


def _route(x, g, w_hi, w_lo, b, tb):
    t, d = x.shape
    nb = t // tb
    tri = jnp.asarray(np.tril(np.ones((tb, tb), np.float32), -1), BF16)
    full = lambda shape: pl.BlockSpec(shape, lambda i: (0,) * len(shape))
    return pl.pallas_call(
        _route_body,
        grid=(nb,),
        in_specs=[
            pl.BlockSpec((tb, d), lambda i: (i, 0)),
            full((1, d)), full((d, LANES)), full((d, LANES)), full((1, LANES)), full((tb, tb)),
        ],
        out_specs=[
            pl.BlockSpec((tb, d), lambda i: (i, 0)),
            pl.BlockSpec((tb, LANES), lambda i: (i, 0)),
            pl.BlockSpec((1, SUBLANES, tb), lambda i: (i, 0, 0)),
            pl.BlockSpec((1, SUBLANES, LANES), lambda i: (i, 0, 0)),
        ],
        out_shape=[
            jax.ShapeDtypeStruct((t, d), BF16),
            jax.ShapeDtypeStruct((t, LANES), F32),
            jax.ShapeDtypeStruct((nb, SUBLANES, tb), F32),
            jax.ShapeDtypeStruct((nb, SUBLANES, LANES), F32),
        ],
        compiler_params=_cparams(("parallel",)),
        name="route",
    )(x, g, w_hi, w_lo, b, tri)


def _moe_plan(cnt, n_tiles):
    n = cnt[:, 0, :N_EXPERTS].astype(jnp.int32)
    chunks = (n + (MOE_ALIGN - 1)) // MOE_ALIGN
    src = jnp.cumsum(chunks, axis=1) - chunks
    seg_tiles = (jnp.sum(chunks, axis=0) + (MOE_CHUNKS_PER_TILE - 1)) // MOE_CHUNKS_PER_TILE
    seg_end = jnp.cumsum(seg_tiles)
    dst = ((seg_end - seg_tiles) * MOE_CHUNKS_PER_TILE)[None, :] + jnp.cumsum(chunks, axis=0) - chunks
    tile = jnp.arange(n_tiles, dtype=jnp.int32)
    tile_expert = jnp.minimum(jnp.sum((tile[:, None] >= seg_end[None, :]).astype(jnp.int32), axis=1),
                              N_EXPERTS - 1)
    tile_valid = (tile < seg_end[-1]).astype(jnp.int32)
    flat = lambda a: a.reshape(-1).astype(jnp.int32)
    return flat(src), flat(dst), flat(chunks), tile_expert.astype(jnp.int32), tile_valid


def _chunk_rows(chunk):
    return pl.ds(pl.multiple_of(chunk * MOE_ALIGN, MOE_ALIGN), MOE_ALIGN)


def _for_block_chunks(nch_ref, blk, fn):
    for e in range(N_EXPERTS):
        def body(c, carry, e=e):
            fn(e, c)
            return carry
        lax.fori_loop(0, nch_ref[blk * N_EXPERTS + e], body, 0)


def _block_chunk_count(nch_ref, blk):
    total = nch_ref[blk * N_EXPERTS]
    for e in range(1, N_EXPERTS):
        total = total + nch_ref[blk * N_EXPERTS + e]
    return total


def _gather_body(src_ref, dst_ref, nch_ref, h_ref, rowf_ref, zero_ref, hs_ref, z_ref, sem):
    del zero_ref
    b = pl.program_id(0)
    slot = lax.rem(b, 2)
    tb = h_ref.shape[0]
    loc = rowf_ref[0].astype(jnp.int32)
    r = lax.broadcasted_iota(jnp.int32, (z_ref.shape[1], tb), 0)
    onehot = jnp.where(r == loc[0:1, :], 1.0, jnp.where(r == loc[1:2, :], 1.0, 0.0)).astype(BF16)
    z_ref[slot] = _dot(onehot, h_ref[...]).astype(BF16)

    def copy(buf, src_chunk, dst_chunk):
        return pltpu.make_async_copy(z_ref.at[buf, _chunk_rows(src_chunk)], hs_ref.at[_chunk_rows(dst_chunk)],
                                     sem.at[buf])

    def drain(blk, buf):
        lax.fori_loop(0, _block_chunk_count(nch_ref, blk), lambda c, carry: (copy(buf, 0, 0).wait(), carry)[1], 0)

    @pl.when(b > 0)
    def _():
        drain(b - 1, 1 - slot)

    _for_block_chunks(nch_ref, b, lambda e, c: copy(slot, src_ref[b * N_EXPERTS + e] + c,
                                                    dst_ref[b * N_EXPERTS + e] + c).start())

    @pl.when(b == pl.num_programs(0) - 1)
    def _():
        drain(b, slot)


def _moe_gather(src, dst, nch, h, rowf, n_tiles, tb):
    t, d = h.shape
    rows = _moe_block_rows(tb)
    zeros = jnp.zeros((n_tiles * MOE_TILE, d), BF16)
    return pl.pallas_call(
        _gather_body,
        grid_spec=pltpu.PrefetchScalarGridSpec(
            num_scalar_prefetch=3,
            grid=(t // tb,),
            in_specs=[
                pl.BlockSpec((tb, d), lambda i, *_: (i, 0)),
                pl.BlockSpec((1, SUBLANES, tb), lambda i, *_: (i, 0, 0)),
                pl.BlockSpec(memory_space=pl.ANY),
            ],
            out_specs=pl.BlockSpec(memory_space=pl.ANY),
            scratch_shapes=[pltpu.VMEM((2, rows, d), BF16), pltpu.SemaphoreType.DMA((2,))],
        ),
        out_shape=jax.ShapeDtypeStruct(zeros.shape, BF16),
        input_output_aliases={5: 0},
        compiler_params=_cparams(("arbitrary",)),
        name="moe_gather",
    )(src, dst, nch, h, rowf, zeros)


def _expert_ffn_body(te_ref, tv_ref, xs_ref, wg_ref, wu_ref, wd_ref, o_ref, acc_ref):
    del te_ref
    f = pl.program_id(1)
    last_f = pl.num_programs(1) - 1
    valid = tv_ref[pl.program_id(0)] > 0

    @pl.when(valid)
    def _():
        xs = xs_ref[...]
        a = _dot(xs, wg_ref[0])
        u = _dot(xs, wu_ref[0])
        y = _dot((_silu(a) * u).astype(BF16), wd_ref[0])

        @pl.when(f == 0)
        def _():
            acc_ref[...] = y

        @pl.when(f != 0)
        def _():
            acc_ref[...] += y

        @pl.when(f == last_f)
        def _():
            o_ref[...] = acc_ref[...].astype(BF16)

    @pl.when(jnp.logical_not(valid) & (f == last_f))
    def _():
        o_ref[...] = jnp.zeros_like(o_ref)


def _expert_ffn(tile_expert, tile_valid, xs, wg, wu, wd, tf):
    rows, d = xs.shape
    dff = wg.shape[2]
    return pl.pallas_call(
        _expert_ffn_body,
        grid_spec=pltpu.PrefetchScalarGridSpec(
            num_scalar_prefetch=2,
            grid=(rows // MOE_TILE, dff // tf),
            in_specs=[
                pl.BlockSpec((MOE_TILE, d), lambda i, f, te, tv: (i, 0)),
                pl.BlockSpec((1, d, tf), lambda i, f, te, tv: (te[i], 0, f)),
                pl.BlockSpec((1, d, tf), lambda i, f, te, tv: (te[i], 0, f)),
                pl.BlockSpec((1, tf, d), lambda i, f, te, tv: (te[i], f, 0)),
            ],
            out_specs=pl.BlockSpec((MOE_TILE, d), lambda i, f, te, tv: (i, 0)),
            scratch_shapes=[pltpu.VMEM((MOE_TILE, d), F32)],
        ),
        out_shape=jax.ShapeDtypeStruct((rows, d), BF16),
        compiler_params=_cparams(("arbitrary", "arbitrary")),
        name="expert_ffn",
    )(tile_expert, tile_valid, xs, wg, wu, wd)


def _combine_body(src_ref, dst_ref, nch_ref, x_ref, col_ref, gf_ref, ys_ref, o_ref, y_ref, sem):
    b = pl.program_id(0)
    nb = pl.num_programs(0)
    slot = lax.rem(b, 2)

    def copy(buf, src_chunk, dst_chunk):
        return pltpu.make_async_copy(ys_ref.at[_chunk_rows(dst_chunk)], y_ref.at[buf, _chunk_rows(src_chunk)],
                                     sem.at[buf])

    def fetch(blk, buf):
        _for_block_chunks(nch_ref, blk, lambda e, c: copy(buf, src_ref[blk * N_EXPERTS + e] + c,
                                                          dst_ref[blk * N_EXPERTS + e] + c).start())

    @pl.when(b == 0)
    def _():
        y_ref[...] = jnp.zeros_like(y_ref)
        fetch(0, 0)

    @pl.when(b + 1 < nb)
    def _():
        fetch(b + 1, 1 - slot)

    lax.fori_loop(0, _block_chunk_count(nch_ref, b), lambda c, carry: (copy(slot, 0, 0).wait(), carry)[1], 0)

    col = col_ref[...]
    loc1 = col[:, 0:1].astype(jnp.int32)
    loc2 = col[:, 1:2].astype(jnp.int32)
    r = lax.broadcasted_iota(jnp.int32, (col.shape[0], y_ref.shape[1]), 1)
    weights = jnp.where(r == loc1, col[:, 2:3], jnp.where(r == loc2, col[:, 3:4], 0.0)).astype(BF16)
    o_ref[...] = _rmsnorm_rows(x_ref[...] + _dot(weights, y_ref[slot]), gf_ref[...])


def _moe_combine(src, dst, nch, x, col, g_final, ys, tb):
    t, d = x.shape
    rows = _moe_block_rows(tb)
    return pl.pallas_call(
        _combine_body,
        grid_spec=pltpu.PrefetchScalarGridSpec(
            num_scalar_prefetch=3,
            grid=(t // tb,),
            in_specs=[
                pl.BlockSpec((tb, d), lambda i, *_: (i, 0)),
                pl.BlockSpec((tb, LANES), lambda i, *_: (i, 0)),
                pl.BlockSpec((1, d), lambda i, *_: (0, 0)),
                pl.BlockSpec(memory_space=pl.ANY),
            ],
            out_specs=pl.BlockSpec((tb, d), lambda i, *_: (i, 0)),
            scratch_shapes=[pltpu.VMEM((2, rows, d), BF16), pltpu.SemaphoreType.DMA((2,))],
        ),
        out_shape=jax.ShapeDtypeStruct((t, d), F32),
        compiler_params=_cparams(("arbitrary",)),
        name="moe_combine",
    )(src, dst, nch, x, col, g_final, ys)


def _moe(x, g, w, g_final):
    t, _ = x.shape
    tb = min(MOE_TB, t)
    nb = t // tb
    h, col, rowf, cnt = _route(x, g, w["w_router_hi"], w["w_router_lo"], w["b_router"], tb)
    n_tiles = -(-(2 * t + nb * N_EXPERTS * (MOE_ALIGN - 1)) // MOE_TILE) + N_EXPERTS
    src, dst, nch, tile_expert, tile_valid = _moe_plan(cnt, n_tiles)
    xs = _moe_gather(src, dst, nch, h, rowf, n_tiles, tb)
    ys = _expert_ffn(tile_expert, tile_valid, xs, w["w_moe_gate"], w["w_moe_up"], w["w_moe_down"], D_FF // 2)
    return _moe_combine(src, dst, nch, x, col, g_final, ys, tb)


def _retention_tables(c_real, ct):
    h = np.arange(RET_HEADS, dtype=np.float64)
    log_gamma = np.log1p(-np.exp2(-5.0 - h))
    idx = np.arange(ct, dtype=np.float64)
    live = idx < c_real
    diff = idx[:, None] - idx[None, :]
    causal = (diff >= 0) & live[:, None] & live[None, :]
    decay = np.where(causal[None], np.exp(np.where(causal, diff, 0.0)[None] * log_gamma[:, None, None]), 0.0)
    q_dec = np.where(live[None], np.exp((idx + 1.0)[None] * log_gamma[:, None]), 0.0)
    k_dec = np.where(live[None], np.exp((c_real - 1.0 - idx)[None] * log_gamma[:, None]), 0.0)
    q_dec = np.broadcast_to(q_dec[..., None], (RET_HEADS, ct, LANES))
    k_dec = np.broadcast_to(k_dec[..., None], (RET_HEADS, ct, LANES))
    return (jnp.asarray(decay, F32), jnp.asarray(q_dec, F32), jnp.asarray(k_dec, F32))


def _rope_tables(pos0, length):
    half = HEAD_DIM // 2
    inv = ROPE_BASE ** (-np.arange(half, dtype=np.float64) / half)
    ang = (pos0 + np.arange(length, dtype=np.float64))[:, None] * inv[None, :]
    cos = np.concatenate([np.cos(ang), np.cos(ang)], axis=-1)
    sin = np.concatenate([-np.sin(ang), np.sin(ang)], axis=-1)
    return jnp.asarray(cos, F32), jnp.asarray(sin, F32)


def _retention_body(q_ref, k_ref, v_ref, g_ref, cos_ref, sin_ref, dec_ref, qd_ref, kd_ref, s0_ref, gn_ref,
                    o_ref, s_ref, *, c_real, ct):
    @pl.when(pl.program_id(1) == 0)
    def _():
        s_ref[...] = s0_ref[...]

    cos = cos_ref[...]
    sin = sin_ref[...]
    for hd in range(RET_HEADS):
        lo, hi = hd * HEAD_DIM, (hd + 1) * HEAD_DIM
        q = q_ref[0, :, lo:hi]
        k = k_ref[0, :, lo:hi]
        q = (q * cos + pltpu.roll(q, HEAD_DIM // 2, 1) * sin) * (HEAD_DIM ** -0.5)
        k = k * cos + pltpu.roll(k, HEAD_DIM // 2, 1) * sin
        q = _pad_rows(q, ct)
        k = _pad_rows(k, ct)
        vb = _pad_rows(v_ref[0, :, lo:hi], ct).astype(BF16)
        qb = q.astype(BF16)
        q_dec = qd_ref[hd]
        state = s_ref[0, hd]

        s = _dot_nt(qb, k.astype(BF16)) * dec_ref[hd]
        o = _dot(s.astype(BF16), vb) + _dot(qb, state.astype(BF16)) * q_dec
        chunk_dec = q_dec[c_real - 1:c_real, :]
        s_ref[0, hd] = state * chunk_dec + _dot_tn((k * kd_ref[hd]).astype(BF16), vb)

        o = o[:c_real]
        o = o - jnp.mean(o, axis=-1, keepdims=True)
        o = o * lax.rsqrt(jnp.mean(o * o, axis=-1, keepdims=True) + EPS)
        o_ref[0, :, lo:hi] = _silu(g_ref[0, :, lo:hi]) * (o * gn_ref[:, lo:hi])


def _retention(proj, pos0, state, gn):
    b, length, _ = proj.shape
    c_real = math.gcd(length, RET_CHUNK)
    ct = RET_CHUNK
    width = RET_HEADS * HEAD_DIM
    cos, sin = _rope_tables(pos0, length)
    decay, q_dec, k_dec = _retention_tables(c_real, ct)

    def col(i):
        return pl.BlockSpec((1, c_real, width), lambda bi, c: (bi, c, i))

    full = lambda shape: pl.BlockSpec(shape, lambda bi, c: (0,) * len(shape))
    state_spec = pl.BlockSpec((1, RET_HEADS, HEAD_DIM, HEAD_DIM), lambda bi, c: (bi, 0, 0, 0))
    return pl.pallas_call(
        functools.partial(_retention_body, c_real=c_real, ct=ct),
        grid=(b, length // c_real),
        in_specs=[
            col(0), col(1), col(2), col(3),
            pl.BlockSpec((c_real, HEAD_DIM), lambda bi, c: (c, 0)),
            pl.BlockSpec((c_real, HEAD_DIM), lambda bi, c: (c, 0)),
            full((RET_HEADS, ct, ct)), full((RET_HEADS, ct, LANES)), full((RET_HEADS, ct, LANES)),
            state_spec,
            full((1, width)),
        ],
        out_specs=[
            pl.BlockSpec((1, c_real, width), lambda bi, c: (bi, c, 0)),
            state_spec,
        ],
        out_shape=[
            jax.ShapeDtypeStruct((b, length, width), F32),
            jax.ShapeDtypeStruct(state.shape, F32),
        ],
        compiler_params=_cparams(("parallel", "arbitrary")),
        name="retention",
    )(proj, proj, proj, proj, cos, sin, decay, q_dec, k_dec, state, gn)


def _scan_rows(x, op, fill):
    rows = x.shape[0]
    row = lax.broadcasted_iota(jnp.int32, x.shape, 0)
    sh = 1
    while sh < rows:
        x = op(x, jnp.where(row >= sh, pltpu.roll(x, sh, 0), fill))
        sh *= 2
    return x


def _mlstm_body(q_ref, k_ref, v_ref, og_ref, gate_ref, bg_ref, cw_ref, cb_ref, cs_ref, c0_ref, n0_ref, m0_ref,
                gn_ref, o_ref, c_ref, n_ref, m_ref, conv_ref, xc_ref, *, c_real, ct):
    width = ML_HEADS * HEAD_DIM
    tail = ML_CONV - 1

    @pl.when(pl.program_id(1) == 0)
    def _():
        c_ref[...] = c0_ref[...]
        n_ref[...] = n0_ref[...]
        m_ref[...] = m0_ref[...]
        xc_ref[SUBLANES - tail:SUBLANES, :] = cs_ref[0]

    xc_ref[SUBLANES:SUBLANES + c_real, :width] = q_ref[0]
    xc_ref[SUBLANES:SUBLANES + c_real, width:] = k_ref[0]
    y = cb_ref[...] + cw_ref[tail:tail + 1, :] * xc_ref[SUBLANES:SUBLANES + c_real, :]
    for j in range(tail):
        y = y + cw_ref[j:j + 1, :] * xc_ref[SUBLANES - tail + j:SUBLANES - tail + j + c_real, :]
    new_tail = xc_ref[c_real:c_real + SUBLANES, :]
    conv_ref[0] = new_tail[SUBLANES - tail:, :]
    xc_ref[0:SUBLANES, :] = new_tail
    qk = _silu(y)
    q_all = _pad_rows(qk[:, :width], ct)
    k_all = _pad_rows(qk[:, width:] * (HEAD_DIM ** -0.5), ct)
    v_all = _pad_rows(v_ref[0], ct)

    gates = _pad_rows(gate_ref[0] + bg_ref[...], ct)
    f_pre = pltpu.roll(gates, LANES - ML_HEADS, 1)
    log_f = jnp.minimum(f_pre, 0.0) - jnp.log(1.0 + jnp.exp(-jnp.abs(f_pre)))
    b_cum = _scan_rows(log_f, jnp.add, 0.0)
    src = gates - b_cum
    cmx = _scan_rows(src, jnp.maximum, NEG_INF)
    m_prev = m_ref[0]
    mx = jnp.maximum(m_prev, cmx)
    m_t = b_cum + mx
    w_prev = jnp.exp(m_prev - mx)
    e_neg_m = jnp.exp(-m_t)
    last = c_real - 1
    b_last = b_cum[last:last + 1, :]
    m_new = m_t[last:last + 1, :]
    row = lax.broadcasted_iota(jnp.int32, (ct, LANES), 0)
    w_end = jnp.where(row < c_real, jnp.exp(b_last + src - m_new), 0.0)
    dec = jnp.exp(b_last + m_prev - m_new)
    m_ref[0] = m_new
    src_t = jnp.transpose(src)

    t_idx = lax.broadcasted_iota(jnp.int32, (ct, ct), 0)
    s_idx = lax.broadcasted_iota(jnp.int32, (ct, ct), 1)
    live = (s_idx <= t_idx) & (s_idx < c_real)

    for hd in range(ML_HEADS):
        lo, hi = hd * HEAD_DIM, (hd + 1) * HEAD_DIM
        q = q_all[:, lo:hi]
        k = k_all[:, lo:hi]
        qb = q.astype(BF16)
        vb = v_all[:, lo:hi].astype(BF16)
        c_state = c_ref[0, hd]
        n_state = n_ref[0, hd:hd + 1, :]
        wp = w_prev[:, hd:hd + 1]
        w = jnp.where(live, jnp.exp(src_t[hd:hd + 1, :] - mx[:, hd:hd + 1]), 0.0)
        s = _dot_nt(qb, k.astype(BF16)) * w
        num = _dot(s.astype(BF16), vb) + wp * _dot(qb, c_state.astype(BF16))
        den = jnp.sum(s, axis=-1, keepdims=True) + wp * jnp.sum(q * n_state, axis=-1, keepdims=True)
        hh = num / jnp.maximum(jnp.abs(den), e_neg_m[:, hd:hd + 1])
        kw = k * w_end[:, hd:hd + 1]
        dec_h = dec[:, hd:hd + 1]
        c_ref[0, hd] = dec_h * c_state + _dot_tn(kw.astype(BF16), vb)
        n_ref[0, hd:hd + 1, :] = dec_h * n_state + jnp.sum(kw, axis=0, keepdims=True)

        hh = hh[:c_real]
        hh = _sigmoid(og_ref[0, :, lo:hi]) * hh
        hh = hh - jnp.mean(hh, axis=-1, keepdims=True)
        hh = hh * lax.rsqrt(jnp.mean(hh * hh, axis=-1, keepdims=True) + EPS)
        o_ref[0, :, lo:hi] = hh * gn_ref[:, lo:hi]


def _mlstm(proj, b_gates, conv_w, conv_b, conv_s, c0, n0, m0, gn):
    b, length, _ = proj.shape
    c_real = math.gcd(length, ML_CHUNK)
    ct = ML_CHUNK
    width = ML_HEADS * HEAD_DIM
    base = (2 * RET_HEADS * HEAD_DIM + 2 * RET_HEADS * HEAD_DIM) // width

    def col(i):
        return pl.BlockSpec((1, c_real, width), lambda bi, c: (bi, c, base + i))

    full = lambda shape: pl.BlockSpec(shape, lambda bi, c: (0,) * len(shape))
    per_b = lambda shape: pl.BlockSpec((1,) + shape, lambda bi, c: (bi,) + (0,) * len(shape))
    bg = jnp.zeros((1, LANES), F32).at[0, :2 * ML_HEADS].set(b_gates)
    m0p = jnp.zeros((b, 1, LANES), F32).at[:, 0, :ML_HEADS].set(m0)
    outs = pl.pallas_call(
        functools.partial(_mlstm_body, c_real=c_real, ct=ct),
        grid=(b, length // c_real),
        in_specs=[
            col(0), col(1), col(2), col(3),
            pl.BlockSpec((1, c_real, LANES), lambda bi, c: (bi, c, AB_MAIN // LANES)),
            full((1, LANES)),
            full((ML_CONV, 2 * width)),
            full((1, 2 * width)),
            per_b((ML_CONV - 1, 2 * width)),
            per_b((ML_HEADS, HEAD_DIM, HEAD_DIM)),
            per_b((ML_HEADS, HEAD_DIM)),
            per_b((1, LANES)),
            full((1, width)),
        ],
        out_specs=[
            pl.BlockSpec((1, c_real, width), lambda bi, c: (bi, c, 0)),
            per_b((ML_HEADS, HEAD_DIM, HEAD_DIM)),
            per_b((ML_HEADS, HEAD_DIM)),
            per_b((1, LANES)),
            per_b((ML_CONV - 1, 2 * width)),
        ],
        out_shape=[
            jax.ShapeDtypeStruct((b, length, width), F32),
            jax.ShapeDtypeStruct(c0.shape, F32),
            jax.ShapeDtypeStruct(n0.shape, F32),
            jax.ShapeDtypeStruct((b, 1, LANES), F32),
            jax.ShapeDtypeStruct(conv_s.shape, F32),
        ],
        scratch_shapes=[pltpu.VMEM((ct + SUBLANES, 2 * width), F32)],
        compiler_params=_cparams(("parallel", "arbitrary")),
        name="mlstm",
    )(proj, proj, proj, proj, proj, bg, conv_w, conv_b, conv_s, c0, n0, m0p, gn)
    h_m, c_new, n_new, m_new, conv_new = outs
    return h_m, c_new, n_new, m_new[:, 0, :ML_HEADS], conv_new


HG_HEADS_PER_ITER = 2


def _hgrn_head(cols, q_ref, f_ref, i_ref, g_ref, hl_ref, gn_ref, o_ref, st, *, blk, sub):
    hl = hl_ref[:, cols]
    hmax = jnp.max(hl, axis=0, keepdims=True)
    ex = jnp.exp(hl - hmax)
    p = ex / jnp.sum(ex, axis=0, keepdims=True)
    lower = (p[0:1] + p[1:2]) - p[0:1]
    gn = gn_ref[:, cols]

    f = lower + (1.0 - lower) * _sigmoid(f_ref[0, :, cols])
    log_k = jnp.log2(jnp.maximum(1.0 - f, 0.0))
    q = _silu(q_ref[0, :, cols])
    v = i_ref[0, :, cols]
    gate = _silu(g_ref[0, :, cols])

    row = lax.broadcasted_iota(jnp.int32, (blk, LANES), 0)
    rsub = row & (sub - 1)
    a = jnp.log2(f)
    sh = 1
    while sh < sub:
        a = a + jnp.where(rsub >= sh, pltpu.roll(a, sh, 0), 0.0)
        sh *= 2
    c = a - log_k
    q_in = q * jnp.exp2(a)

    t_idx = lax.broadcasted_iota(jnp.int32, (SUBLANES, LANES), 0)
    for j in range(blk // sub):
        r0 = j * sub
        a_end = a[r0 + sub - 1:r0 + sub, :]
        inter = _dot_nt(q_in[r0:r0 + sub].astype(BF16), st.astype(BF16))
        k_hat = jnp.exp2(a_end - c[r0:r0 + sub])
        v_j = v[r0:r0 + sub]
        st = st * jnp.exp2(a_end) + _dot_tn(v_j.astype(BF16), k_hat.astype(BF16))
        for part in range(sub // SUBLANES):
            t0 = r0 + part * SUBLANES
            a_t = a[t0:t0 + SUBLANES]
            q_t = q[t0:t0 + SUBLANES]
            o_t = inter[part * SUBLANES:(part + 1) * SUBLANES]
            for s in range((part + 1) * SUBLANES):
                arg = a_t - c[r0 + s:r0 + s + 1, :]
                if s >= part * SUBLANES:
                    arg = jnp.where(t_idx >= s - part * SUBLANES, arg, NEG_INF)
                z = q_t * jnp.exp2(arg)
                o_t = o_t + jnp.sum(z, axis=-1, keepdims=True) * v[r0 + s:r0 + s + 1, :]
            o_t = o_t * lax.rsqrt(jnp.mean(o_t * o_t, axis=-1, keepdims=True) + EPS)
            o_ref[0, t0:t0 + SUBLANES, cols] = o_t * gn * gate[t0:t0 + SUBLANES]
    return st


def _block_ref_rows(a, level, row8):
    rows = a.shape[0]
    half = level // 2
    if half >= SUBLANES:
        return jnp.concatenate([jnp.broadcast_to(a[b0 + half - 1:b0 + half, :], (level, LANES))
                                for b0 in range(0, rows, level)], axis=0)
    if level == 2:
        return jnp.where((row8 & 1) == 1, pltpu.roll(a, 1, 0), a)
    pieces = []
    for v0 in range(0, rows, SUBLANES):
        picks = [jnp.broadcast_to(a[v0 + b0 + half - 1:v0 + b0 + half, :], (SUBLANES, LANES))
                 for b0 in range(0, SUBLANES, level)]
        piece = picks[-1]
        for i in range(len(picks) - 2, -1, -1):
            piece = jnp.where(row8[:SUBLANES] < (i + 1) * level, picks[i], piece)
        pieces.append(piece)
    return jnp.concatenate(pieces, axis=0)


def _hgrn_head_block(cols, q_ref, f_ref, i_ref, g_ref, hl_ref, gn_ref, o_ref, st, *, blk):
    hl = hl_ref[:, cols]
    hmax = jnp.max(hl, axis=0, keepdims=True)
    ex = jnp.exp(hl - hmax)
    p = ex / jnp.sum(ex, axis=0, keepdims=True)
    lower = (p[0:1] + p[1:2]) - p[0:1]

    f = lower + (1.0 - lower) * _sigmoid(f_ref[0, :, cols])
    kk = 1.0 - f
    q = _silu(q_ref[0, :, cols])
    vb = i_ref[0, :, cols].astype(BF16)
    a = _scan_rows(jnp.log2(f), jnp.add, 0.0)
    c = a - jnp.log2(jnp.maximum(kk, 0.0))
    row8 = lax.broadcasted_iota(jnp.int32, (blk, LANES), 0) & (SUBLANES - 1)
    group = (lax.broadcasted_iota(jnp.int32, (blk, blk), 0)
             ^ lax.broadcasted_iota(jnp.int32, (blk, blk), 1))

    scores = None
    level = blk
    while level >= 2:
        half = level // 2
        if half >= SUBLANES:
            zero = jnp.zeros((half, LANES), F32)
            qs, ks = [], []
            for b0 in range(0, blk, level):
                ref = a[b0 + half - 1:b0 + half, :]
                qs += [zero, q[b0 + half:b0 + level] * jnp.exp2(a[b0 + half:b0 + level] - ref)]
                ks += [jnp.exp2(ref - c[b0:b0 + half]), zero]
            q_l = jnp.concatenate(qs, axis=0)
            k_l = jnp.concatenate(ks, axis=0)
        else:
            ref = _block_ref_rows(a, level, row8)
            upper = (row8 & (level - 1)) >= half
            q_l = jnp.where(upper, q * jnp.exp2(a - ref), 0.0)
            k_l = jnp.where(upper, 0.0, jnp.exp2(ref - c))
        r = _dot_nt(q_l.astype(BF16), k_l.astype(BF16))
        scores = r if scores is None else jnp.where(group < level, r, scores)
        level = half
    scores = jnp.where(group < 1, _dot_nt(q.astype(BF16), kk.astype(BF16)), scores)

    a_end = a[blk - 1:blk, :]
    o = _dot_nt((q * jnp.exp2(a)).astype(BF16), st.astype(BF16)) + _dot(scores.astype(BF16), vb)
    o = o * lax.rsqrt(jnp.mean(o * o, axis=-1, keepdims=True) + EPS)
    o_ref[0, :, cols] = o * gn_ref[:, cols] * _silu(g_ref[0, :, cols])
    return st * jnp.exp2(a_end) + _dot_tn(vb, jnp.exp2(a_end - c).astype(BF16))


def _hgrn_body(q_ref, f_ref, i_ref, g_ref, hl_ref, s0_ref, gn_ref, o_ref, s_ref, st_ref, *, blk, sub):
    @pl.when(pl.program_id(1) == 0)
    def _():
        for h in range(HG_HEADS):
            st_ref[h] = jnp.transpose(s0_ref[0, h])

    def heads(it, carry):
        for u in range(HG_HEADS_PER_ITER):
            h = it * HG_HEADS_PER_ITER + u
            cols = pl.ds(pl.multiple_of(h * HEAD_DIM, HEAD_DIM), HEAD_DIM)
            if blk == HG_BLOCK:
                st_ref[h] = _hgrn_head_block(cols, q_ref, f_ref, i_ref, g_ref, hl_ref, gn_ref, o_ref,
                                             st_ref[h], blk=blk)
            else:
                st_ref[h] = _hgrn_head(cols, q_ref, f_ref, i_ref, g_ref, hl_ref, gn_ref, o_ref, st_ref[h],
                                       blk=blk, sub=sub)
        return carry

    lax.fori_loop(0, HG_HEADS // HG_HEADS_PER_ITER, heads, 0)

    @pl.when(pl.program_id(1) == pl.num_programs(1) - 1)
    def _():
        for h in range(HG_HEADS):
            s_ref[0, h] = jnp.transpose(st_ref[h])


def _hgrn(proj, hg_lower, state, gn):
    b, length, _ = proj.shape
    blk = min(length, HG_BLOCK)
    sub = math.gcd(length, HG_SUB)
    width = HG_HEADS * HEAD_DIM

    def col(i):
        return pl.BlockSpec((1, blk, width), lambda bi, c: (bi, c, i))

    full = lambda shape: pl.BlockSpec(shape, lambda bi, c: (0,) * len(shape))
    state_spec = pl.BlockSpec((1, HG_HEADS, HEAD_DIM, HEAD_DIM), lambda bi, c: (bi, 0, 0, 0))
    return pl.pallas_call(
        functools.partial(_hgrn_body, blk=blk, sub=sub),
        grid=(b, length // blk),
        in_specs=[col(0), col(1), col(2), col(3), full(hg_lower.shape), state_spec, full((1, width))],
        out_specs=[pl.BlockSpec((1, blk, width), lambda bi, c: (bi, c, 0)), state_spec],
        out_shape=[
            jax.ShapeDtypeStruct((b, length, width), F32),
            jax.ShapeDtypeStruct(state.shape, F32),
        ],
        scratch_shapes=[pltpu.VMEM((HG_HEADS, HEAD_DIM, HEAD_DIM), F32)],
        compiler_params=_cparams(("parallel", "arbitrary")),
        name="hgrn2",
    )(proj, proj, proj, proj, hg_lower, state, gn)


def _prep_weights(p):
    w_in_ab = p["w_in_ab"][0]
    w_ab = jnp.zeros((D_MODEL, AB_PAD), BF16)
    w_ab = w_ab.at[:, :AB_MAIN].set(w_in_ab[:, :AB_MAIN].astype(BF16))
    w_ab = w_ab.at[:, AB_MAIN:AB_MAIN + 2 * ML_HEADS].set(w_in_ab[:, AB_MAIN:].astype(BF16))
    w_router = jnp.zeros((D_MODEL, LANES), F32).at[:, :N_EXPERTS].set(p["w_router"][0])
    w_router_hi = w_router.astype(BF16)
    w_router_lo = (w_router - w_router_hi.astype(F32)).astype(BF16)
    b_router = jnp.zeros((1, LANES), F32).at[0, :N_EXPERTS].set(p["b_router"][0])
    return {
        "w_ab": w_ab,
        "w_out_ab": p["w_out_ab"][0].astype(BF16),
        "w_in_c": p["w_in_c"][0].astype(BF16),
        "w_out_c": p["w_out_c"][0].astype(BF16),
        "w_ffn_gate": p["w_ffn_gate"][0].astype(BF16),
        "w_ffn_up": p["w_ffn_up"][0].astype(BF16),
        "w_ffn_down": p["w_ffn_down"][0].astype(BF16),
        "w_router_hi": w_router_hi,
        "w_router_lo": w_router_lo,
        "b_router": b_router,
        "w_moe_gate": p["w_moe_gate"][0].astype(BF16),
        "w_moe_up": p["w_moe_up"][0].astype(BF16),
        "w_moe_down": p["w_moe_down"][0].astype(BF16),
    }


def _trunk(x, pos0, ret_s, mc_s, mn_s, mm_s, conv_s, hg_s, p, w):
    b, length, d = x.shape
    t = b * length
    tm = min(512, t)
    row = lambda v: v.reshape(1, -1)
    x0 = x.reshape(t, d)

    proj = _norm_matmul(x0, row(p["ln_mix"][0]), w["w_ab"], tm, AB_PAD // 3).reshape(b, length, AB_PAD)
    o_ret, ret_new = _retention(proj, pos0, ret_s, row(p["ret_gn"][0]))
    h_m, c_new, n_new, m_new, conv_new = _mlstm(
        proj, p["b_gates_ab"][0], p["conv_w_ab"][0], row(p["conv_b_ab"][0]), conv_s, mc_s, mn_s, mm_s,
        row(p["ml_gn"][0]))
    half = RET_HEADS * HEAD_DIM
    x1 = _proj_residual([o_ret.reshape(t, half), h_m.reshape(t, half)],
                        [w["w_out_ab"][:half], w["w_out_ab"][half:]], x0, tm)
    x2 = _ffn(x1, row(p["ln_ffn"][0]), w["w_ffn_gate"], w["w_ffn_up"], w["w_ffn_down"], tm, D_FF // 2)

    proj_c = _norm_matmul(x2, row(p["ln_mix"][1]), w["w_in_c"], tm, 1024).reshape(b, length, 4 * D_MODEL)
    o_hg, hg_new = _hgrn(proj_c, p["hg_lower"], hg_s, row(p["hg_gn"][0]))
    x3 = _proj_residual([o_hg.reshape(t, d)], [w["w_out_c"]], x2, tm)
    y = _moe(x3, row(p["ln_ffn"][1]), w, row(p["ln_final"]))
    return (y.reshape(b, length, d), ret_new[None], c_new[None], n_new[None], m_new[None], conv_new[None],
            hg_new[None])


def kernel(x_prompt, x_sample, state_ret, state_mlstm_c, state_mlstm_n, state_mlstm_m, state_conv, state_hgrn,
           ln_mix, ln_ffn, ln_final, w_in_ab, b_gates_ab, conv_w_ab, conv_b_ab, ret_gn, ml_gn, w_out_ab,
           w_in_c, hg_lower, hg_gn, w_out_c, w_ffn_gate, w_ffn_up, w_ffn_down,
           w_router, b_router, w_moe_gate, w_moe_up, w_moe_down):
    p = {"ln_mix": ln_mix, "ln_ffn": ln_ffn, "ln_final": ln_final,
         "w_in_ab": w_in_ab, "b_gates_ab": b_gates_ab, "conv_w_ab": conv_w_ab, "conv_b_ab": conv_b_ab,
         "ret_gn": ret_gn, "ml_gn": ml_gn, "w_out_ab": w_out_ab,
         "w_in_c": w_in_c, "hg_lower": hg_lower, "hg_gn": hg_gn, "w_out_c": w_out_c,
         "w_ffn_gate": w_ffn_gate, "w_ffn_up": w_ffn_up, "w_ffn_down": w_ffn_down,
         "w_router": w_router, "b_router": b_router,
         "w_moe_gate": w_moe_gate, "w_moe_up": w_moe_up, "w_moe_down": w_moe_down}
    w = _prep_weights(p)
    bp = x_prompt.shape[0]
    zeros = lambda *shape: jnp.zeros(shape, F32)
    prompt = _trunk(
        x_prompt, 0,
        zeros(bp, RET_HEADS, HEAD_DIM, HEAD_DIM), zeros(bp, ML_HEADS, HEAD_DIM, HEAD_DIM),
        zeros(bp, ML_HEADS, HEAD_DIM), jnp.full((bp, ML_HEADS), NEG_INF, F32),
        zeros(bp, ML_CONV - 1, 2 * ML_HEADS * HEAD_DIM), zeros(bp, HG_HEADS, HEAD_DIM, HEAD_DIM), p, w)
    sample = _trunk(
        x_sample, 16384,
        state_ret[0], state_mlstm_c[0], state_mlstm_n[0], state_mlstm_m[0], state_conv[0], state_hgrn[0], p, w)
    out = []
    for a, s in zip(prompt, sample):
        out += [a, s]
    return tuple(out)
```

```python
import functools
import math

import numpy as np
import jax
import jax.numpy as jnp
from jax import lax
from jax.experimental import pallas as pl
from jax.experimental.pallas import tpu as pltpu

F32 = jnp.float32
BF16 = jnp.bfloat16

D_MODEL = 1024
HEAD_DIM = 128
RET_HEADS = 4
ML_HEADS = 4
ML_CONV = 4
HG_HEADS = 8
D_FF = 2816
N_EXPERTS = 8
ROPE_BASE = 10000.0
EPS = 1e-6
RET_CHUNK = 128
ML_CHUNK = 128
HG_BLOCK = 128
HG_SUB = 16

LANES = 128
SUBLANES = 8
AB_MAIN = 4096
AB_PAD = 4224
VMEM_LIMIT = 56 * 1024 * 1024

NEG_INF = float("-inf")


def _cparams(sem):
    return pltpu.CompilerParams(dimension_semantics=sem, vmem_limit_bytes=VMEM_LIMIT)


def _sigmoid(x):
    return 1.0 / (1.0 + jnp.exp(-x))


def _silu(x):
    return x * _sigmoid(x)


def _rmsnorm_rows(x, g):
    ms = jnp.mean(x * x, axis=-1, keepdims=True)
    return x * lax.rsqrt(ms + EPS) * g


def _dot(a, b):
    return jnp.dot(a, b, preferred_element_type=F32)


def _dot_nt(a, b):
    return lax.dot_general(a, b, (((1,), (1,)), ((), ())), preferred_element_type=F32)


def _dot_tn(a, b):
    return lax.dot_general(a, b, (((0,), (0,)), ((), ())), preferred_element_type=F32)


def _pad_rows(x, rows):
    if x.shape[0] == rows:
        return x
    return jnp.concatenate([x, jnp.zeros((rows - x.shape[0], x.shape[1]), x.dtype)], axis=0)


def _norm_matmul_body(x_ref, g_ref, w_ref, o_ref, h_ref):
    @pl.when(pl.program_id(1) == 0)
    def _():
        h_ref[...] = _rmsnorm_rows(x_ref[...], g_ref[...]).astype(BF16)

    o_ref[...] = _dot(h_ref[...], w_ref[...])


def _norm_matmul(x, g, w, tm, tn):
    t, d = x.shape
    n = w.shape[1]
    return pl.pallas_call(
        _norm_matmul_body,
        grid=(t // tm, n // tn),
        in_specs=[
            pl.BlockSpec((tm, d), lambda i, j: (i, 0)),
            pl.BlockSpec((1, d), lambda i, j: (0, 0)),
            pl.BlockSpec((d, tn), lambda i, j: (0, j)),
        ],
        out_specs=pl.BlockSpec((tm, tn), lambda i, j: (i, j)),
        out_shape=jax.ShapeDtypeStruct((t, n), F32),
        scratch_shapes=[pltpu.VMEM((tm, d), BF16)],
        compiler_params=_cparams(("parallel", "arbitrary")),
        name="norm_matmul",
    )(x, g, w)


def _mixer_residual(x_ref, a_refs, w_refs):
    x = x_ref[...]
    for a_ref, w_ref in zip(a_refs, w_refs):
        x = x + _dot(a_ref[...].astype(BF16), w_ref[...])
    return x


def _ffn_body(*refs, n_in):
    a_refs, w_refs = refs[:n_in], refs[n_in:2 * n_in]
    x_ref, g_ref, wg_ref, wu_ref, wd_ref, o_ref, h_ref, x1_ref = refs[2 * n_in:]
    f = pl.program_id(1)

    @pl.when(f == 0)
    def _():
        x1 = _mixer_residual(x_ref, a_refs, w_refs)
        x1_ref[...] = x1
        h_ref[...] = _rmsnorm_rows(x1, g_ref[...]).astype(BF16)

    h = h_ref[...]
    a = _dot(h, wg_ref[...])
    u = _dot(h, wu_ref[...])
    act = (_silu(a) * u).astype(BF16)
    x1_ref[...] += _dot(act, wd_ref[...])

    @pl.when(f == pl.num_programs(1) - 1)
    def _():
        o_ref[...] = x1_ref[...]


def _ffn(acts, weights, x, g, wg, wu, wd, tm, tf):
    t, d = x.shape
    dff = wg.shape[1]
    n_in = len(acts)
    in_specs = [pl.BlockSpec((tm, a.shape[1]), lambda i, f: (i, 0)) for a in acts]
    in_specs += [pl.BlockSpec(w.shape, lambda i, f: (0, 0)) for w in weights]
    in_specs += [
        pl.BlockSpec((tm, d), lambda i, f: (i, 0)),
        pl.BlockSpec((1, d), lambda i, f: (0, 0)),
        pl.BlockSpec((d, tf), lambda i, f: (0, f)),
        pl.BlockSpec((d, tf), lambda i, f: (0, f)),
        pl.BlockSpec((tf, d), lambda i, f: (f, 0)),
    ]
    return pl.pallas_call(
        functools.partial(_ffn_body, n_in=n_in),
        grid=(t // tm, dff // tf),
        in_specs=in_specs,
        out_specs=pl.BlockSpec((tm, d), lambda i, f: (i, 0)),
        out_shape=jax.ShapeDtypeStruct((t, d), F32),
        scratch_shapes=[pltpu.VMEM((tm, d), BF16), pltpu.VMEM((tm, d), F32)],
        compiler_params=_cparams(("parallel", "arbitrary")),
        name="ffn",
    )(*acts, *weights, x, g, wg, wu, wd)


MOE_TB = 512
MOE_ALIGN = 16
MOE_TILE = 512
MOE_CHUNKS_PER_TILE = MOE_TILE // MOE_ALIGN


def _moe_block_rows(tb):
    return 2 * tb + N_EXPERTS * MOE_ALIGN


def _route_body(a_ref, wo_ref, x_ref, g_ref, whi_ref, wlo_ref, b_ref, tri_ref,
                x3_ref, h_ref, col_ref, rowf_ref, cnt_ref):
    x3 = _mixer_residual(x_ref, [a_ref], [wo_ref])
    x3_ref[...] = x3
    h = _rmsnorm_rows(x3, g_ref[...])
    h_hi = h.astype(BF16)
    h_ref[...] = h_hi
    h_lo = (h - h_hi.astype(F32)).astype(BF16)
    logits = (_dot(h_hi, whi_ref[...]) + _dot(h_hi, wlo_ref[...]) + _dot(h_lo, whi_ref[...])
              + b_ref[...])
    lane = lax.broadcasted_iota(jnp.int32, logits.shape, 1)
    lg = jnp.where(lane < N_EXPERTS, logits, NEG_INF)
    m1 = jnp.max(lg, axis=-1, keepdims=True)
    i1 = jnp.min(jnp.where(lg == m1, lane, LANES), axis=-1, keepdims=True)
    lg2 = jnp.where(lane == i1, NEG_INF, lg)
    m2 = jnp.max(lg2, axis=-1, keepdims=True)
    i2 = jnp.min(jnp.where(lg2 == m2, lane, LANES), axis=-1, keepdims=True)
    e = jnp.exp(m2 - m1)
    w1 = 1.0 / (1.0 + e)
    w2 = e / (1.0 + e)

    sel = jnp.where(lane == i1, 1.0, jnp.where(lane == i2, 1.0, 0.0))
    before = _dot(tri_ref[...], sel.astype(BF16))
    n = jnp.broadcast_to(jnp.sum(sel, axis=0, keepdims=True), (SUBLANES, LANES))
    padded = jnp.floor((n + (MOE_ALIGN - 1.0)) * (1.0 / MOE_ALIGN)) * MOE_ALIGN
    lane8 = lax.broadcasted_iota(jnp.int32, padded.shape, 1)
    incl = padded
    sh = 1
    while sh < N_EXPERTS:
        incl = incl + jnp.where(lane8 >= sh, pltpu.roll(incl, sh, 1), 0.0)
        sh *= 2
    pos = (incl - padded)[0:1, :] + before
    loc1 = jnp.sum(jnp.where(lane == i1, pos, 0.0), axis=-1, keepdims=True)
    loc2 = jnp.sum(jnp.where(lane == i2, pos, 0.0), axis=-1, keepdims=True)
    col = jnp.where(lane == 0, loc1, jnp.where(lane == 1, loc2, jnp.where(lane == 2, w1,
                                                                          jnp.where(lane == 3, w2, 0.0))))
    col_ref[...] = col
    for c in range(col.shape[0] // LANES):
        rowf_ref[0, :, c * LANES:(c + 1) * LANES] = jnp.transpose(col[c * LANES:(c + 1) * LANES, :])[:SUBLANES, :]
    cnt_ref[0] = n


def _route(act, w_out, x, g, w_hi, w_lo, b, tb):
    t, d = x.shape
    nb = t // tb
    tri = jnp.asarray(np.tril(np.ones((tb, tb), np.float32), -1), BF16)
    full = lambda shape: pl.BlockSpec(shape, lambda i: (0,) * len(shape))
    return pl.pallas_call(
        _route_body,
        grid=(nb,),
        in_specs=[
            pl.BlockSpec((tb, act.shape[1]), lambda i: (i, 0)),
            full(w_out.shape),
            pl.BlockSpec((tb, d), lambda i: (i, 0)),
            full((1, d)), full((d, LANES)), full((d, LANES)), full((1, LANES)), full((tb, tb)),
        ],
        out_specs=[
            pl.BlockSpec((tb, d), lambda i: (i, 0)),
            pl.BlockSpec((tb, d), lambda i: (i, 0)),
            pl.BlockSpec((tb, LANES), lambda i: (i, 0)),
            pl.BlockSpec((1, SUBLANES, tb), lambda i: (i, 0, 0)),
            pl.BlockSpec((1, SUBLANES, LANES), lambda i: (i, 0, 0)),
        ],
        out_shape=[
            jax.ShapeDtypeStruct((t, d), F32),
            jax.ShapeDtypeStruct((t, d), BF16),
            jax.ShapeDtypeStruct((t, LANES), F32),
            jax.ShapeDtypeStruct((nb, SUBLANES, tb), F32),
            jax.ShapeDtypeStruct((nb, SUBLANES, LANES), F32),
        ],
        compiler_params=_cparams(("parallel",)),
        name="route",
    )(act, w_out, x, g, w_hi, w_lo, b, tri)


def _moe_plan(cnt, n_tiles):
    n = cnt[:, 0, :N_EXPERTS].astype(jnp.int32)
    chunks = (n + (MOE_ALIGN - 1)) // MOE_ALIGN
    src = jnp.cumsum(chunks, axis=1) - chunks
    seg_tiles = (jnp.sum(chunks, axis=0) + (MOE_CHUNKS_PER_TILE - 1)) // MOE_CHUNKS_PER_TILE
    seg_end = jnp.cumsum(seg_tiles)
    dst = ((seg_end - seg_tiles) * MOE_CHUNKS_PER_TILE)[None, :] + jnp.cumsum(chunks, axis=0) - chunks
    tile = jnp.arange(n_tiles, dtype=jnp.int32)
    tile_expert = jnp.minimum(jnp.sum((tile[:, None] >= seg_end[None, :]).astype(jnp.int32), axis=1),
                              N_EXPERTS - 1)
    tile_valid = (tile < seg_end[-1]).astype(jnp.int32)
    flat = lambda a: a.reshape(-1).astype(jnp.int32)
    return flat(src), flat(dst), flat(chunks), tile_expert.astype(jnp.int32), tile_valid


def _chunk_rows(chunk):
    return pl.ds(pl.multiple_of(chunk * MOE_ALIGN, MOE_ALIGN), MOE_ALIGN)


def _for_block_chunks(nch_ref, blk, fn):
    for e in range(N_EXPERTS):
        def body(c, carry, e=e):
            fn(e, c)
            return carry
        lax.fori_loop(0, nch_ref[blk * N_EXPERTS + e], body, 0)


def _block_chunk_count(nch_ref, blk):
    total = nch_ref[blk * N_EXPERTS]
    for e in range(1, N_EXPERTS):
        total = total + nch_ref[blk * N_EXPERTS + e]
    return total


def _gather_body(src_ref, dst_ref, nch_ref, h_ref, rowf_ref, zero_ref, hs_ref, z_ref, sem):
    del zero_ref
    b = pl.program_id(0)
    slot = lax.rem(b, 2)
    tb = h_ref.shape[0]
    loc = rowf_ref[0].astype(jnp.int32)
    r = lax.broadcasted_iota(jnp.int32, (z_ref.shape[1], tb), 0)
    onehot = jnp.where(r == loc[0:1, :], 1.0, jnp.where(r == loc[1:2, :], 1.0, 0.0)).astype(BF16)
    z_ref[slot] = _dot(onehot, h_ref[...]).astype(BF16)

    def copy(buf, src_chunk, dst_chunk):
        return pltpu.make_async_copy(z_ref.at[buf, _chunk_rows(src_chunk)], hs_ref.at[_chunk_rows(dst_chunk)],
                                     sem.at[buf])

    def drain(blk, buf):
        lax.fori_loop(0, _block_chunk_count(nch_ref, blk), lambda c, carry: (copy(buf, 0, 0).wait(), carry)[1], 0)

    @pl.when(b > 0)
    def _():
        drain(b - 1, 1 - slot)

    _for_block_chunks(nch_ref, b, lambda e, c: copy(slot, src_ref[b * N_EXPERTS + e] + c,
                                                    dst_ref[b * N_EXPERTS + e] + c).start())

    @pl.when(b == pl.num_programs(0) - 1)
    def _():
        drain(b, slot)


def _moe_gather(src, dst, nch, h, rowf, n_tiles, tb):
    t, d = h.shape
    rows = _moe_block_rows(tb)
    zeros = jnp.zeros((n_tiles * MOE_TILE, d), BF16)
    return pl.pallas_call(
        _gather_body,
        grid_spec=pltpu.PrefetchScalarGridSpec(
            num_scalar_prefetch=3,
            grid=(t // tb,),
            in_specs=[
                pl.BlockSpec((tb, d), lambda i, *_: (i, 0)),
                pl.BlockSpec((1, SUBLANES, tb), lambda i, *_: (i, 0, 0)),
                pl.BlockSpec(memory_space=pl.ANY),
            ],
            out_specs=pl.BlockSpec(memory_space=pl.ANY),
            scratch_shapes=[pltpu.VMEM((2, rows, d), BF16), pltpu.SemaphoreType.DMA((2,))],
        ),
        out_shape=jax.ShapeDtypeStruct(zeros.shape, BF16),
        input_output_aliases={5: 0},
        compiler_params=_cparams(("arbitrary",)),
        name="moe_gather",
    )(src, dst, nch, h, rowf, zeros)


def _expert_ffn_body(te_ref, tv_ref, xs_ref, wg_ref, wu_ref, wd_ref, o_ref, acc_ref):
    del te_ref
    f = pl.program_id(1)
    last_f = pl.num_programs(1) - 1
    valid = tv_ref[pl.program_id(0)] > 0

    @pl.when(valid)
    def _():
        xs = xs_ref[...]
        a = _dot(xs, wg_ref[0])
        u = _dot(xs, wu_ref[0])
        y = _dot((_silu(a) * u).astype(BF16), wd_ref[0])

        @pl.when(f == 0)
        def _():
            acc_ref[...] = y

        @pl.when(f != 0)
        def _():
            acc_ref[...] += y

        @pl.when(f == last_f)
        def _():
            o_ref[...] = acc_ref[...].astype(BF16)

    @pl.when(jnp.logical_not(valid) & (f == last_f))
    def _():
        o_ref[...] = jnp.zeros_like(o_ref)


def _expert_ffn(tile_expert, tile_valid, xs, wg, wu, wd, tf):
    rows, d = xs.shape
    dff = wg.shape[2]
    return pl.pallas_call(
        _expert_ffn_body,
        grid_spec=pltpu.PrefetchScalarGridSpec(
            num_scalar_prefetch=2,
            grid=(rows // MOE_TILE, dff // tf),
            in_specs=[
                pl.BlockSpec((MOE_TILE, d), lambda i, f, te, tv: (i, 0)),
                pl.BlockSpec((1, d, tf), lambda i, f, te, tv: (te[i], 0, f)),
                pl.BlockSpec((1, d, tf), lambda i, f, te, tv: (te[i], 0, f)),
                pl.BlockSpec((1, tf, d), lambda i, f, te, tv: (te[i], f, 0)),
            ],
            out_specs=pl.BlockSpec((MOE_TILE, d), lambda i, f, te, tv: (i, 0)),
            scratch_shapes=[pltpu.VMEM((MOE_TILE, d), F32)],
        ),
        out_shape=jax.ShapeDtypeStruct((rows, d), BF16),
        compiler_params=_cparams(("arbitrary", "arbitrary")),
        name="expert_ffn",
    )(tile_expert, tile_valid, xs, wg, wu, wd)


def _combine_body(src_ref, dst_ref, nch_ref, x_ref, col_ref, gf_ref, ys_ref, o_ref, y_ref, sem):
    b = pl.program_id(0)
    nb = pl.num_programs(0)
    slot = lax.rem(b, 2)

    def copy(buf, src_chunk, dst_chunk):
        return pltpu.make_async_copy(ys_ref.at[_chunk_rows(dst_chunk)], y_ref.at[buf, _chunk_rows(src_chunk)],
                                     sem.at[buf])

    def fetch(blk, buf):
        _for_block_chunks(nch_ref, blk, lambda e, c: copy(buf, src_ref[blk * N_EXPERTS + e] + c,
                                                          dst_ref[blk * N_EXPERTS + e] + c).start())

    @pl.when(b == 0)
    def _():
        y_ref[...] = jnp.zeros_like(y_ref)
        fetch(0, 0)

    @pl.when(b + 1 < nb)
    def _():
        fetch(b + 1, 1 - slot)

    lax.fori_loop(0, _block_chunk_count(nch_ref, b), lambda c, carry: (copy(slot, 0, 0).wait(), carry)[1], 0)

    col = col_ref[...]
    loc1 = col[:, 0:1].astype(jnp.int32)
    loc2 = col[:, 1:2].astype(jnp.int32)
    r = lax.broadcasted_iota(jnp.int32, (col.shape[0], y_ref.shape[1]), 1)
    weights = jnp.where(r == loc1, col[:, 2:3], jnp.where(r == loc2, col[:, 3:4], 0.0)).astype(BF16)
    o_ref[...] = _rmsnorm_rows(x_ref[...] + _dot(weights, y_ref[slot]), gf_ref[...])


def _moe_combine(src, dst, nch, x, col, g_final, ys, tb):
    t, d = x.shape
    rows = _moe_block_rows(tb)
    return pl.pallas_call(
        _combine_body,
        grid_spec=pltpu.PrefetchScalarGridSpec(
            num_scalar_prefetch=3,
            grid=(t // tb,),
            in_specs=[
                pl.BlockSpec((tb, d), lambda i, *_: (i, 0)),
                pl.BlockSpec((tb, LANES), lambda i, *_: (i, 0)),
                pl.BlockSpec((1, d), lambda i, *_: (0, 0)),
                pl.BlockSpec(memory_space=pl.ANY),
            ],
            out_specs=pl.BlockSpec((tb, d), lambda i, *_: (i, 0)),
            scratch_shapes=[pltpu.VMEM((2, rows, d), BF16), pltpu.SemaphoreType.DMA((2,))],
        ),
        out_shape=jax.ShapeDtypeStruct((t, d), F32),
        compiler_params=_cparams(("arbitrary",)),
        name="moe_combine",
    )(src, dst, nch, x, col, g_final, ys)


def _moe(act, w_out, x_in, g, w, g_final):
    t, _ = x_in.shape
    tb = min(MOE_TB, t)
    nb = t // tb
    x, h, col, rowf, cnt = _route(act, w_out, x_in, g, w["w_router_hi"], w["w_router_lo"], w["b_router"], tb)
    n_tiles = -(-(2 * t + nb * N_EXPERTS * (MOE_ALIGN - 1)) // MOE_TILE) + N_EXPERTS
    src, dst, nch, tile_expert, tile_valid = _moe_plan(cnt, n_tiles)
    xs = _moe_gather(src, dst, nch, h, rowf, n_tiles, tb)
    ys = _expert_ffn(tile_expert, tile_valid, xs, w["w_moe_gate"], w["w_moe_up"], w["w_moe_down"], D_FF // 2)
    return _moe_combine(src, dst, nch, x, col, g_final, ys, tb)


def _retention_tables(c_real, ct):
    h = np.arange(RET_HEADS, dtype=np.float64)
    log_gamma = np.log1p(-np.exp2(-5.0 - h))
    idx = np.arange(ct, dtype=np.float64)
    live = idx < c_real
    diff = idx[:, None] - idx[None, :]
    causal = (diff >= 0) & live[:, None] & live[None, :]
    decay = np.where(causal[None], np.exp(np.where(causal, diff, 0.0)[None] * log_gamma[:, None, None]), 0.0)
    q_dec = np.where(live[None], np.exp((idx + 1.0)[None] * log_gamma[:, None]), 0.0)
    k_dec = np.where(live[None], np.exp((c_real - 1.0 - idx)[None] * log_gamma[:, None]), 0.0)
    q_dec = np.broadcast_to(q_dec[..., None], (RET_HEADS, ct, LANES))
    k_dec = np.broadcast_to(k_dec[..., None], (RET_HEADS, ct, LANES))
    return (jnp.asarray(decay, F32), jnp.asarray(q_dec, F32), jnp.asarray(k_dec, F32))


def _rope_tables(pos0, length):
    half = HEAD_DIM // 2
    inv = ROPE_BASE ** (-np.arange(half, dtype=np.float64) / half)
    ang = (pos0 + np.arange(length, dtype=np.float64))[:, None] * inv[None, :]
    cos = np.concatenate([np.cos(ang), np.cos(ang)], axis=-1)
    sin = np.concatenate([-np.sin(ang), np.sin(ang)], axis=-1)
    return jnp.asarray(cos, F32), jnp.asarray(sin, F32)


def _retention_body(q_ref, k_ref, v_ref, g_ref, cos_ref, sin_ref, dec_ref, qd_ref, kd_ref, s0_ref, gn_ref,
                    o_ref, s_ref, *, c_real, ct, bb):
    @pl.when(pl.program_id(1) == 0)
    def _():
        s_ref[...] = s0_ref[...]

    cos = cos_ref[...]
    sin = sin_ref[...]
    for g, hd in [(g, hd) for g in range(bb) for hd in range(RET_HEADS)]:
        lo, hi = hd * HEAD_DIM, (hd + 1) * HEAD_DIM
        q = q_ref[g, :, lo:hi]
        k = k_ref[g, :, lo:hi]
        q = (q * cos + pltpu.roll(q, HEAD_DIM // 2, 1) * sin) * (HEAD_DIM ** -0.5)
        k = k * cos + pltpu.roll(k, HEAD_DIM // 2, 1) * sin
        q = _pad_rows(q, ct)
        k = _pad_rows(k, ct)
        vb = _pad_rows(v_ref[g, :, lo:hi], ct).astype(BF16)
        qb = q.astype(BF16)
        q_dec = qd_ref[hd]
        state = s_ref[g, hd]

        s = _dot_nt(qb, k.astype(BF16)) * dec_ref[hd]
        o = _dot(s.astype(BF16), vb) + _dot(qb, state.astype(BF16)) * q_dec
        chunk_dec = q_dec[c_real - 1:c_real, :]
        s_ref[g, hd] = state * chunk_dec + _dot_tn((k * kd_ref[hd]).astype(BF16), vb)

        o = o[:c_real]
        o = o - jnp.mean(o, axis=-1, keepdims=True)
        o = o * lax.rsqrt(jnp.mean(o * o, axis=-1, keepdims=True) + EPS)
        o_ref[g, :, lo:hi] = _silu(g_ref[g, :, lo:hi]) * (o * gn_ref[:, lo:hi])


def _retention(proj, pos0, state, gn, bb):
    b, length, _ = proj.shape
    c_real = math.gcd(length, RET_CHUNK)
    ct = RET_CHUNK
    width = RET_HEADS * HEAD_DIM
    cos, sin = _rope_tables(pos0, length)
    decay, q_dec, k_dec = _retention_tables(c_real, ct)

    def col(i):
        return pl.BlockSpec((bb, c_real, width), lambda bi, c: (bi, c, i))

    full = lambda shape: pl.BlockSpec(shape, lambda bi, c: (0,) * len(shape))
    state_spec = pl.BlockSpec((bb, RET_HEADS, HEAD_DIM, HEAD_DIM), lambda bi, c: (bi, 0, 0, 0))
    return pl.pallas_call(
        functools.partial(_retention_body, c_real=c_real, ct=ct, bb=bb),
        grid=(b // bb, length // c_real),
        in_specs=[
            col(0), col(1), col(2), col(3),
            pl.BlockSpec((c_real, HEAD_DIM), lambda bi, c: (c, 0)),
            pl.BlockSpec((c_real, HEAD_DIM), lambda bi, c: (c, 0)),
            full((RET_HEADS, ct, ct)), full((RET_HEADS, ct, LANES)), full((RET_HEADS, ct, LANES)),
            state_spec,
            full((1, width)),
        ],
        out_specs=[
            pl.BlockSpec((bb, c_real, width), lambda bi, c: (bi, c, 0)),
            state_spec,
        ],
        out_shape=[
            jax.ShapeDtypeStruct((b, length, width), F32),
            jax.ShapeDtypeStruct(state.shape, F32),
        ],
        compiler_params=_cparams(("parallel", "arbitrary")),
        name="retention",
    )(proj, proj, proj, proj, cos, sin, decay, q_dec, k_dec, state, gn)


def _scan_rows(x, op, fill):
    rows = x.shape[0]
    row = lax.broadcasted_iota(jnp.int32, x.shape, 0)
    sh = 1
    while sh < rows:
        x = op(x, jnp.where(row >= sh, pltpu.roll(x, sh, 0), fill))
        sh *= 2
    return x


def _mlstm_body(q_ref, k_ref, v_ref, og_ref, gate_ref, bg_ref, cw_ref, cb_ref, cs_ref, c0_ref, n0_ref, m0_ref,
                gn_ref, o_ref, c_ref, n_ref, m_ref, conv_ref, xc_ref, *, c_real, ct, bb):
    tail = ML_CONV - 1

    @pl.when(pl.program_id(1) == 0)
    def _():
        c_ref[...] = c0_ref[...]
        n_ref[...] = n0_ref[...]
        m_ref[...] = m0_ref[...]
        xc_ref[:, SUBLANES - tail:SUBLANES, :] = cs_ref[...]

    for g in range(bb):
        _mlstm_sequence(q_ref.at[g], k_ref.at[g], v_ref.at[g], og_ref.at[g], gate_ref.at[g], bg_ref, cw_ref, cb_ref,
                        gn_ref, o_ref.at[g], c_ref.at[g], n_ref.at[g], m_ref.at[g], conv_ref.at[g], xc_ref.at[g],
                        c_real=c_real, ct=ct)


def _mlstm_sequence(q_ref, k_ref, v_ref, og_ref, gate_ref, bg_ref, cw_ref, cb_ref, gn_ref, o_ref, c_ref, n_ref, m_ref,
                    conv_ref, xc_ref, *, c_real, ct):
    width = ML_HEADS * HEAD_DIM
    tail = ML_CONV - 1

    xc_ref[SUBLANES:SUBLANES + c_real, :width] = q_ref[...]
    xc_ref[SUBLANES:SUBLANES + c_real, width:] = k_ref[...]
    y = cb_ref[...] + cw_ref[tail:tail + 1, :] * xc_ref[SUBLANES:SUBLANES + c_real, :]
    for j in range(tail):
        y = y + cw_ref[j:j + 1, :] * xc_ref[SUBLANES - tail + j:SUBLANES - tail + j + c_real, :]
    new_tail = xc_ref[c_real:c_real + SUBLANES, :]
    conv_ref[...] = new_tail[SUBLANES - tail:, :]
    xc_ref[0:SUBLANES, :] = new_tail
    qk = _silu(y)
    q_all = _pad_rows(qk[:, :width], ct)
    k_all = _pad_rows(qk[:, width:] * (HEAD_DIM ** -0.5), ct)
    v_all = _pad_rows(v_ref[...], ct)

    gates = _pad_rows(gate_ref[...] + bg_ref[...], ct)
    f_pre = pltpu.roll(gates, LANES - ML_HEADS, 1)
    log_f = jnp.minimum(f_pre, 0.0) - jnp.log(1.0 + jnp.exp(-jnp.abs(f_pre)))
    b_cum = _scan_rows(log_f, jnp.add, 0.0)
    src = gates - b_cum
    cmx = _scan_rows(src, jnp.maximum, NEG_INF)
    m_prev = m_ref[...]
    mx = jnp.maximum(m_prev, cmx)
    m_t = b_cum + mx
    w_prev = jnp.exp(m_prev - mx)
    e_neg_m = jnp.exp(-m_t)
    last = c_real - 1
    b_last = b_cum[last:last + 1, :]
    m_new = m_t[last:last + 1, :]
    row = lax.broadcasted_iota(jnp.int32, (ct, LANES), 0)
    w_end = jnp.where(row < c_real, jnp.exp(b_last + src - m_new), 0.0)
    dec = jnp.exp(b_last + m_prev - m_new)
    m_ref[...] = m_new
    src_t = jnp.transpose(src)

    t_idx = lax.broadcasted_iota(jnp.int32, (ct, ct), 0)
    s_idx = lax.broadcasted_iota(jnp.int32, (ct, ct), 1)
    live = (s_idx <= t_idx) & (s_idx < c_real)

    for hd in range(ML_HEADS):
        lo, hi = hd * HEAD_DIM, (hd + 1) * HEAD_DIM
        q = q_all[:, lo:hi]
        k = k_all[:, lo:hi]
        qb = q.astype(BF16)
        vb = v_all[:, lo:hi].astype(BF16)
        c_state = c_ref[hd]
        n_state = n_ref[hd:hd + 1, :]
        wp = w_prev[:, hd:hd + 1]
        w = jnp.where(live, jnp.exp(src_t[hd:hd + 1, :] - mx[:, hd:hd + 1]), 0.0)
        s = _dot_nt(qb, k.astype(BF16)) * w
        num = _dot(s.astype(BF16), vb) + wp * _dot(qb, c_state.astype(BF16))
        den = jnp.sum(s, axis=-1, keepdims=True) + wp * jnp.sum(q * n_state, axis=-1, keepdims=True)
        hh = num / jnp.maximum(jnp.abs(den), e_neg_m[:, hd:hd + 1])
        kw = k * w_end[:, hd:hd + 1]
        dec_h = dec[:, hd:hd + 1]
        c_ref[hd] = dec_h * c_state + _dot_tn(kw.astype(BF16), vb)
        n_ref[hd:hd + 1, :] = dec_h * n_state + jnp.sum(kw, axis=0, keepdims=True)

        hh = hh[:c_real]
        hh = _sigmoid(og_ref[:, lo:hi]) * hh
        hh = hh - jnp.mean(hh, axis=-1, keepdims=True)
        hh = hh * lax.rsqrt(jnp.mean(hh * hh, axis=-1, keepdims=True) + EPS)
        o_ref[:, lo:hi] = hh * gn_ref[:, lo:hi]


def _mlstm(proj, b_gates, conv_w, conv_b, conv_s, c0, n0, m0, gn, bb):
    b, length, _ = proj.shape
    c_real = math.gcd(length, ML_CHUNK)
    ct = ML_CHUNK
    width = ML_HEADS * HEAD_DIM
    base = (2 * RET_HEADS * HEAD_DIM + 2 * RET_HEADS * HEAD_DIM) // width

    def col(i):
        return pl.BlockSpec((bb, c_real, width), lambda bi, c: (bi, c, base + i))

    full = lambda shape: pl.BlockSpec(shape, lambda bi, c: (0,) * len(shape))
    per_b = lambda shape: pl.BlockSpec((bb,) + shape, lambda bi, c: (bi,) + (0,) * len(shape))
    bg = jnp.zeros((1, LANES), F32).at[0, :2 * ML_HEADS].set(b_gates)
    m0p = jnp.zeros((b, 1, LANES), F32).at[:, 0, :ML_HEADS].set(m0)
    outs = pl.pallas_call(
        functools.partial(_mlstm_body, c_real=c_real, ct=ct, bb=bb),
        grid=(b // bb, length // c_real),
        in_specs=[
            col(0), col(1), col(2), col(3),
            pl.BlockSpec((bb, c_real, LANES), lambda bi, c: (bi, c, AB_MAIN // LANES)),
            full((1, LANES)),
            full((ML_CONV, 2 * width)),
            full((1, 2 * width)),
            per_b((ML_CONV - 1, 2 * width)),
            per_b((ML_HEADS, HEAD_DIM, HEAD_DIM)),
            per_b((ML_HEADS, HEAD_DIM)),
            per_b((1, LANES)),
            full((1, width)),
        ],
        out_specs=[
            pl.BlockSpec((bb, c_real, width), lambda bi, c: (bi, c, 0)),
            per_b((ML_HEADS, HEAD_DIM, HEAD_DIM)),
            per_b((ML_HEADS, HEAD_DIM)),
            per_b((1, LANES)),
            per_b((ML_CONV - 1, 2 * width)),
        ],
        out_shape=[
            jax.ShapeDtypeStruct((b, length, width), F32),
            jax.ShapeDtypeStruct(c0.shape, F32),
            jax.ShapeDtypeStruct(n0.shape, F32),
            jax.ShapeDtypeStruct((b, 1, LANES), F32),
            jax.ShapeDtypeStruct(conv_s.shape, F32),
        ],
        scratch_shapes=[pltpu.VMEM((bb, ct + SUBLANES, 2 * width), F32)],
        compiler_params=_cparams(("parallel", "arbitrary")),
        name="mlstm",
    )(proj, proj, proj, proj, proj, bg, conv_w, conv_b, conv_s, c0, n0, m0p, gn)
    h_m, c_new, n_new, m_new, conv_new = outs
    return h_m, c_new, n_new, m_new[:, 0, :ML_HEADS], conv_new


HG_HEADS_PER_ITER = 2


def _hgrn_head(cols, q_ref, f_ref, i_ref, g_ref, hl_ref, gn_ref, o_ref, st, *, blk, sub):
    hl = hl_ref[:, cols]
    hmax = jnp.max(hl, axis=0, keepdims=True)
    ex = jnp.exp(hl - hmax)
    p = ex / jnp.sum(ex, axis=0, keepdims=True)
    lower = (p[0:1] + p[1:2]) - p[0:1]
    gn = gn_ref[:, cols]

    f = lower + (1.0 - lower) * _sigmoid(f_ref[:, cols])
    log_k = jnp.log2(jnp.maximum(1.0 - f, 0.0))
    q = _silu(q_ref[:, cols])
    v = i_ref[:, cols]
    gate = _silu(g_ref[:, cols])

    row = lax.broadcasted_iota(jnp.int32, (blk, LANES), 0)
    rsub = row & (sub - 1)
    a = jnp.log2(f)
    sh = 1
    while sh < sub:
        a = a + jnp.where(rsub >= sh, pltpu.roll(a, sh, 0), 0.0)
        sh *= 2
    c = a - log_k
    q_in = q * jnp.exp2(a)

    t_idx = lax.broadcasted_iota(jnp.int32, (SUBLANES, LANES), 0)
    for j in range(blk // sub):
        r0 = j * sub
        a_end = a[r0 + sub - 1:r0 + sub, :]
        inter = _dot_nt(q_in[r0:r0 + sub].astype(BF16), st.astype(BF16))
        k_hat = jnp.exp2(a_end - c[r0:r0 + sub])
        v_j = v[r0:r0 + sub]
        st = st * jnp.exp2(a_end) + _dot_tn(v_j.astype(BF16), k_hat.astype(BF16))
        for part in range(sub // SUBLANES):
            t0 = r0 + part * SUBLANES
            a_t = a[t0:t0 + SUBLANES]
            q_t = q[t0:t0 + SUBLANES]
            o_t = inter[part * SUBLANES:(part + 1) * SUBLANES]
            for s in range((part + 1) * SUBLANES):
                arg = a_t - c[r0 + s:r0 + s + 1, :]
                if s >= part * SUBLANES:
                    arg = jnp.where(t_idx >= s - part * SUBLANES, arg, NEG_INF)
                z = q_t * jnp.exp2(arg)
                o_t = o_t + jnp.sum(z, axis=-1, keepdims=True) * v[r0 + s:r0 + s + 1, :]
            o_t = o_t * lax.rsqrt(jnp.mean(o_t * o_t, axis=-1, keepdims=True) + EPS)
            o_ref[t0:t0 + SUBLANES, cols] = o_t * gn * gate[t0:t0 + SUBLANES]
    return st


def _block_ref_rows(a, level, row8):
    rows = a.shape[0]
    half = level // 2
    if half >= SUBLANES:
        return jnp.concatenate([jnp.broadcast_to(a[b0 + half - 1:b0 + half, :], (level, LANES))
                                for b0 in range(0, rows, level)], axis=0)
    if level == 2:
        return jnp.where((row8 & 1) == 1, pltpu.roll(a, 1, 0), a)
    pieces = []
    for v0 in range(0, rows, SUBLANES):
        picks = [jnp.broadcast_to(a[v0 + b0 + half - 1:v0 + b0 + half, :], (SUBLANES, LANES))
                 for b0 in range(0, SUBLANES, level)]
        piece = picks[-1]
        for i in range(len(picks) - 2, -1, -1):
            piece = jnp.where(row8[:SUBLANES] < (i + 1) * level, picks[i], piece)
        pieces.append(piece)
    return jnp.concatenate(pieces, axis=0)


def _hgrn_head_block(cols, q_ref, f_ref, i_ref, g_ref, hl_ref, gn_ref, o_ref, st, *, blk):
    hl = hl_ref[:, cols]
    hmax = jnp.max(hl, axis=0, keepdims=True)
    ex = jnp.exp(hl - hmax)
    p = ex / jnp.sum(ex, axis=0, keepdims=True)
    lower = (p[0:1] + p[1:2]) - p[0:1]

    f = lower + (1.0 - lower) * _sigmoid(f_ref[:, cols])
    kk = 1.0 - f
    q = _silu(q_ref[:, cols])
    vb = i_ref[:, cols].astype(BF16)
    a = _scan_rows(jnp.log2(f), jnp.add, 0.0)
    c = a - jnp.log2(jnp.maximum(kk, 0.0))
    row8 = lax.broadcasted_iota(jnp.int32, (blk, LANES), 0) & (SUBLANES - 1)
    group = (lax.broadcasted_iota(jnp.int32, (blk, blk), 0)
             ^ lax.broadcasted_iota(jnp.int32, (blk, blk), 1))

    scores = None
    level = blk
    while level >= 2:
        half = level // 2
        if half >= SUBLANES:
            zero = jnp.zeros((half, LANES), F32)
            qs, ks = [], []
            for b0 in range(0, blk, level):
                ref = a[b0 + half - 1:b0 + half, :]
                qs += [zero, q[b0 + half:b0 + level] * jnp.exp2(a[b0 + half:b0 + level] - ref)]
                ks += [jnp.exp2(ref - c[b0:b0 + half]), zero]
            q_l = jnp.concatenate(qs, axis=0)
            k_l = jnp.concatenate(ks, axis=0)
        else:
            ref = _block_ref_rows(a, level, row8)
            upper = (row8 & (level - 1)) >= half
            q_l = jnp.where(upper, q * jnp.exp2(a - ref), 0.0)
            k_l = jnp.where(upper, 0.0, jnp.exp2(ref - c))
        r = _dot_nt(q_l.astype(BF16), k_l.astype(BF16))
        scores = r if scores is None else jnp.where(group < level, r, scores)
        level = half
    scores = jnp.where(group < 1, _dot_nt(q.astype(BF16), kk.astype(BF16)), scores)

    a_end = a[blk - 1:blk, :]
    o = _dot_nt((q * jnp.exp2(a)).astype(BF16), st.astype(BF16)) + _dot(scores.astype(BF16), vb)
    o = o * lax.rsqrt(jnp.mean(o * o, axis=-1, keepdims=True) + EPS)
    o_ref[:, cols] = o * gn_ref[:, cols] * _silu(g_ref[:, cols])
    return st * jnp.exp2(a_end) + _dot_tn(vb, jnp.exp2(a_end - c).astype(BF16))


def _hgrn_body(q_ref, f_ref, i_ref, g_ref, hl_ref, s0_ref, gn_ref, o_ref, s_ref, st_ref, *, blk, sub, bb):
    @pl.when(pl.program_id(1) == 0)
    def _():
        for g, h in [(g, h) for g in range(bb) for h in range(HG_HEADS)]:
            st_ref[g, h] = jnp.transpose(s0_ref[g, h])

    head = functools.partial(_hgrn_head_block, blk=blk) if blk == HG_BLOCK else functools.partial(
        _hgrn_head, blk=blk, sub=sub)

    def heads(it, carry):
        for g, u in [(g, u) for g in range(bb) for u in range(HG_HEADS_PER_ITER)]:
            h = it * HG_HEADS_PER_ITER + u
            cols = pl.ds(pl.multiple_of(h * HEAD_DIM, HEAD_DIM), HEAD_DIM)
            st_ref[g, h] = head(cols, q_ref.at[g], f_ref.at[g], i_ref.at[g], g_ref.at[g], hl_ref, gn_ref,
                                o_ref.at[g], st_ref[g, h])
        return carry

    lax.fori_loop(0, HG_HEADS // HG_HEADS_PER_ITER, heads, 0)

    @pl.when(pl.program_id(1) == pl.num_programs(1) - 1)
    def _():
        for g, h in [(g, h) for g in range(bb) for h in range(HG_HEADS)]:
            s_ref[g, h] = jnp.transpose(st_ref[g, h])


def _hgrn(proj, hg_lower, state, gn, bb):
    b, length, _ = proj.shape
    blk = min(length, HG_BLOCK)
    sub = math.gcd(length, HG_SUB)
    width = HG_HEADS * HEAD_DIM

    def col(i):
        return pl.BlockSpec((bb, blk, width), lambda bi, c: (bi, c, i))

    full = lambda shape: pl.BlockSpec(shape, lambda bi, c: (0,) * len(shape))
    state_spec = pl.BlockSpec((bb, HG_HEADS, HEAD_DIM, HEAD_DIM), lambda bi, c: (bi, 0, 0, 0))
    return pl.pallas_call(
        functools.partial(_hgrn_body, blk=blk, sub=sub, bb=bb),
        grid=(b // bb, length // blk),
        in_specs=[col(0), col(1), col(2), col(3), full(hg_lower.shape), state_spec, full((1, width))],
        out_specs=[pl.BlockSpec((bb, blk, width), lambda bi, c: (bi, c, 0)), state_spec],
        out_shape=[
            jax.ShapeDtypeStruct((b, length, width), F32),
            jax.ShapeDtypeStruct(state.shape, F32),
        ],
        scratch_shapes=[pltpu.VMEM((bb, HG_HEADS, HEAD_DIM, HEAD_DIM), F32)],
        compiler_params=_cparams(("parallel", "arbitrary")),
        name="hgrn2",
    )(proj, proj, proj, proj, hg_lower, state, gn)


def _prep_weights(p):
    w_in_ab = p["w_in_ab"][0]
    w_ab = jnp.zeros((D_MODEL, AB_PAD), BF16)
    w_ab = w_ab.at[:, :AB_MAIN].set(w_in_ab[:, :AB_MAIN].astype(BF16))
    w_ab = w_ab.at[:, AB_MAIN:AB_MAIN + 2 * ML_HEADS].set(w_in_ab[:, AB_MAIN:].astype(BF16))
    w_router = jnp.zeros((D_MODEL, LANES), F32).at[:, :N_EXPERTS].set(p["w_router"][0])
    w_router_hi = w_router.astype(BF16)
    w_router_lo = (w_router - w_router_hi.astype(F32)).astype(BF16)
    b_router = jnp.zeros((1, LANES), F32).at[0, :N_EXPERTS].set(p["b_router"][0])
    return {
        "w_ab": w_ab,
        "w_out_ab": p["w_out_ab"][0].astype(BF16),
        "w_in_c": p["w_in_c"][0].astype(BF16),
        "w_out_c": p["w_out_c"][0].astype(BF16),
        "w_ffn_gate": p["w_ffn_gate"][0].astype(BF16),
        "w_ffn_up": p["w_ffn_up"][0].astype(BF16),
        "w_ffn_down": p["w_ffn_down"][0].astype(BF16),
        "w_router_hi": w_router_hi,
        "w_router_lo": w_router_lo,
        "b_router": b_router,
        "w_moe_gate": p["w_moe_gate"][0].astype(BF16),
        "w_moe_up": p["w_moe_up"][0].astype(BF16),
        "w_moe_down": p["w_moe_down"][0].astype(BF16),
    }


def _trunk(x, pos0, ret_s, mc_s, mn_s, mm_s, conv_s, hg_s, p, w):
    b, length, d = x.shape
    t = b * length
    tm = min(512, t)
    tm_mm = min(1024, t)
    prompt = length >= RET_CHUNK
    bb = 1 if prompt else 4
    assert t % tm_mm == 0 and t % tm == 0 and b % bb == 0 and b % 2 == 0, (b, length)
    row = lambda v: v.reshape(1, -1)
    x0 = x.reshape(t, d)

    proj = _norm_matmul(x0, row(p["ln_mix"][0]), w["w_ab"], tm_mm, AB_PAD // 3).reshape(b, length, AB_PAD)
    o_ret, ret_new = _retention(proj, pos0, ret_s, row(p["ret_gn"][0]), 2 if prompt else bb)
    h_m, c_new, n_new, m_new, conv_new = _mlstm(
        proj, p["b_gates_ab"][0], p["conv_w_ab"][0], row(p["conv_b_ab"][0]), conv_s, mc_s, mn_s, mm_s,
        row(p["ml_gn"][0]), bb)
    half = RET_HEADS * HEAD_DIM
    x2 = _ffn([o_ret.reshape(t, half), h_m.reshape(t, half)], [w["w_out_ab"][:half], w["w_out_ab"][half:]], x0,
              row(p["ln_ffn"][0]), w["w_ffn_gate"], w["w_ffn_up"], w["w_ffn_down"], tm, D_FF // 2)

    proj_c = _norm_matmul(x2, row(p["ln_mix"][1]), w["w_in_c"], tm_mm, 1024).reshape(b, length, 4 * D_MODEL)
    o_hg, hg_new = _hgrn(proj_c, p["hg_lower"], hg_s, row(p["hg_gn"][0]), bb)
    y = _moe(o_hg.reshape(t, d), w["w_out_c"], x2, row(p["ln_ffn"][1]), w, row(p["ln_final"]))
    return (y.reshape(b, length, d), ret_new[None], c_new[None], n_new[None], m_new[None], conv_new[None],
            hg_new[None])


def kernel(x_prompt, x_sample, state_ret, state_mlstm_c, state_mlstm_n, state_mlstm_m, state_conv, state_hgrn,
           ln_mix, ln_ffn, ln_final, w_in_ab, b_gates_ab, conv_w_ab, conv_b_ab, ret_gn, ml_gn, w_out_ab,
           w_in_c, hg_lower, hg_gn, w_out_c, w_ffn_gate, w_ffn_up, w_ffn_down,
           w_router, b_router, w_moe_gate, w_moe_up, w_moe_down):
    p = {"ln_mix": ln_mix, "ln_ffn": ln_ffn, "ln_final": ln_final,
         "w_in_ab": w_in_ab, "b_gates_ab": b_gates_ab, "conv_w_ab": conv_w_ab, "conv_b_ab": conv_b_ab,
         "ret_gn": ret_gn, "ml_gn": ml_gn, "w_out_ab": w_out_ab,
         "w_in_c": w_in_c, "hg_lower": hg_lower, "hg_gn": hg_gn, "w_out_c": w_out_c,
         "w_ffn_gate": w_ffn_gate, "w_ffn_up": w_ffn_up, "w_ffn_down": w_ffn_down,
         "w_router": w_router, "b_router": b_router,
         "w_moe_gate": w_moe_gate, "w_moe_up": w_moe_up, "w_moe_down": w_moe_down}
    w = _prep_weights(p)
    bp = x_prompt.shape[0]
    zeros = lambda *shape: jnp.zeros(shape, F32)
    prompt = _trunk(
        x_prompt, 0,
        zeros(bp, RET_HEADS, HEAD_DIM, HEAD_DIM), zeros(bp, ML_HEADS, HEAD_DIM, HEAD_DIM),
        zeros(bp, ML_HEADS, HEAD_DIM), jnp.full((bp, ML_HEADS), NEG_INF, F32),
        zeros(bp, ML_CONV - 1, 2 * ML_HEADS * HEAD_DIM), zeros(bp, HG_HEADS, HEAD_DIM, HEAD_DIM), p, w)
    sample = _trunk(
        x_sample, 16384,
        state_ret[0], state_mlstm_c[0], state_mlstm_n[0], state_mlstm_m[0], state_conv[0], state_hgrn[0], p, w)
    out = []
    for a, s in zip(prompt, sample):
        out += [a, s]
    return tuple(out)
```

```python
import functools
import math

import numpy as np
import jax
import jax.numpy as jnp
from jax import lax
from jax.experimental import pallas as pl
from jax.experimental.pallas import tpu as pltpu

F32 = jnp.float32
BF16 = jnp.bfloat16

D_MODEL = 1024
HEAD_DIM = 128
RET_HEADS = 4
ML_HEADS = 4
ML_CONV = 4
HG_HEADS = 8
D_FF = 2816
N_EXPERTS = 8
ROPE_BASE = 10000.0
EPS = 1e-6
RET_CHUNK = 128
ML_CHUNK = 128
HG_BLOCK = 128
HG_SUB = 16

LANES = 128
SUBLANES = 8
AB_MAIN = 4096
AB_PAD = 4224
VMEM_LIMIT = 56 * 1024 * 1024

NEG_INF = float("-inf")


def _cparams(sem):
    return pltpu.CompilerParams(dimension_semantics=sem, vmem_limit_bytes=VMEM_LIMIT)


def _sigmoid(x):
    return 1.0 / (1.0 + jnp.exp(-x))


def _silu(x):
    return x * _sigmoid(x)


def _rmsnorm_rows(x, g):
    ms = jnp.mean(x * x, axis=-1, keepdims=True)
    return x * lax.rsqrt(ms + EPS) * g


def _dot(a, b):
    return jnp.dot(a, b, preferred_element_type=F32)


def _dot_nt(a, b):
    return lax.dot_general(a, b, (((1,), (1,)), ((), ())), preferred_element_type=F32)


def _dot_tn(a, b):
    return lax.dot_general(a, b, (((0,), (0,)), ((), ())), preferred_element_type=F32)


def _pad_rows(x, rows):
    if x.shape[0] == rows:
        return x
    return jnp.concatenate([x, jnp.zeros((rows - x.shape[0], x.shape[1]), x.dtype)], axis=0)


def _norm_matmul_body(x_ref, g_ref, w_ref, o_ref, h_ref):
    @pl.when(pl.program_id(1) == 0)
    def _():
        h_ref[...] = _rmsnorm_rows(x_ref[...], g_ref[...]).astype(BF16)

    o_ref[...] = _dot(h_ref[...], w_ref[...])


def _norm_matmul(x, g, w, tm, tn):
    t, d = x.shape
    n = w.shape[1]
    return pl.pallas_call(
        _norm_matmul_body,
        grid=(t // tm, n // tn),
        in_specs=[
            pl.BlockSpec((tm, d), lambda i, j: (i, 0)),
            pl.BlockSpec((1, d), lambda i, j: (0, 0)),
            pl.BlockSpec((d, tn), lambda i, j: (0, j)),
        ],
        out_specs=pl.BlockSpec((tm, tn), lambda i, j: (i, j)),
        out_shape=jax.ShapeDtypeStruct((t, n), F32),
        scratch_shapes=[pltpu.VMEM((tm, d), BF16)],
        compiler_params=_cparams(("parallel", "arbitrary")),
        name="norm_matmul",
    )(x, g, w)


def _mixer_residual(x_ref, a_refs, w_refs):
    x = x_ref[...]
    for a_ref, w_ref in zip(a_refs, w_refs):
        x = x + _dot(a_ref[...].astype(BF16), w_ref[...])
    return x


def _ffn_body(*refs, n_in):
    a_refs, w_refs = refs[:n_in], refs[n_in:2 * n_in]
    x_ref, g_ref, wg_ref, wu_ref, wd_ref, o_ref, h_ref, x1_ref = refs[2 * n_in:]
    f = pl.program_id(1)

    @pl.when(f == 0)
    def _():
        x1 = _mixer_residual(x_ref, a_refs, w_refs)
        x1_ref[...] = x1
        h_ref[...] = _rmsnorm_rows(x1, g_ref[...]).astype(BF16)

    h = h_ref[...]
    a = _dot(h, wg_ref[...])
    u = _dot(h, wu_ref[...])
    act = (_silu(a) * u).astype(BF16)
    x1_ref[...] += _dot(act, wd_ref[...])

    @pl.when(f == pl.num_programs(1) - 1)
    def _():
        o_ref[...] = x1_ref[...]


def _ffn(acts, weights, x, g, wg, wu, wd, tm, tf):
    t, d = x.shape
    dff = wg.shape[1]
    n_in = len(acts)
    in_specs = [pl.BlockSpec((tm, a.shape[1]), lambda i, f: (i, 0)) for a in acts]
    in_specs += [pl.BlockSpec(w.shape, lambda i, f: (0, 0)) for w in weights]
    in_specs += [
        pl.BlockSpec((tm, d), lambda i, f: (i, 0)),
        pl.BlockSpec((1, d), lambda i, f: (0, 0)),
        pl.BlockSpec((d, tf), lambda i, f: (0, f)),
        pl.BlockSpec((d, tf), lambda i, f: (0, f)),
        pl.BlockSpec((tf, d), lambda i, f: (f, 0)),
    ]
    return pl.pallas_call(
        functools.partial(_ffn_body, n_in=n_in),
        grid=(t // tm, dff // tf),
        in_specs=in_specs,
        out_specs=pl.BlockSpec((tm, d), lambda i, f: (i, 0)),
        out_shape=jax.ShapeDtypeStruct((t, d), F32),
        scratch_shapes=[pltpu.VMEM((tm, d), BF16), pltpu.VMEM((tm, d), F32)],
        compiler_params=_cparams(("parallel", "arbitrary")),
        name="ffn",
    )(*acts, *weights, x, g, wg, wu, wd)


MOE_TB = 512
MOE_ALIGN = 16
MOE_TILE = 512
MOE_CHUNKS_PER_TILE = MOE_TILE // MOE_ALIGN


def _moe_block_rows(tb):
    return 2 * tb + N_EXPERTS * MOE_ALIGN


def _route_body(a_ref, wo_ref, x_ref, g_ref, whi_ref, wlo_ref, b_ref, tri_ref,
                x3_ref, h_ref, col_ref, rowf_ref, cnt_ref):
    x3 = _mixer_residual(x_ref, [a_ref], [wo_ref])
    x3_ref[...] = x3
    h = _rmsnorm_rows(x3, g_ref[...])
    h_hi = h.astype(BF16)
    h_ref[...] = h_hi
    h_lo = (h - h_hi.astype(F32)).astype(BF16)
    logits = (_dot(h_hi, whi_ref[...]) + _dot(h_hi, wlo_ref[...]) + _dot(h_lo, whi_ref[...])
              + b_ref[...])
    lane = lax.broadcasted_iota(jnp.int32, logits.shape, 1)
    lg = jnp.where(lane < N_EXPERTS, logits, NEG_INF)
    m1 = jnp.max(lg, axis=-1, keepdims=True)
    i1 = jnp.min(jnp.where(lg == m1, lane, LANES), axis=-1, keepdims=True)
    lg2 = jnp.where(lane == i1, NEG_INF, lg)
    m2 = jnp.max(lg2, axis=-1, keepdims=True)
    i2 = jnp.min(jnp.where(lg2 == m2, lane, LANES), axis=-1, keepdims=True)
    e = jnp.exp(m2 - m1)
    w1 = 1.0 / (1.0 + e)
    w2 = e / (1.0 + e)

    sel = jnp.where(lane == i1, 1.0, jnp.where(lane == i2, 1.0, 0.0))
    before = _dot(tri_ref[...], sel.astype(BF16))
    n = jnp.broadcast_to(jnp.sum(sel, axis=0, keepdims=True), (SUBLANES, LANES))
    padded = jnp.floor((n + (MOE_ALIGN - 1.0)) * (1.0 / MOE_ALIGN)) * MOE_ALIGN
    lane8 = lax.broadcasted_iota(jnp.int32, padded.shape, 1)
    incl = padded
    sh = 1
    while sh < N_EXPERTS:
        incl = incl + jnp.where(lane8 >= sh, pltpu.roll(incl, sh, 1), 0.0)
        sh *= 2
    pos = (incl - padded)[0:1, :] + before
    loc1 = jnp.sum(jnp.where(lane == i1, pos, 0.0), axis=-1, keepdims=True)
    loc2 = jnp.sum(jnp.where(lane == i2, pos, 0.0), axis=-1, keepdims=True)
    col = jnp.where(lane == 0, loc1, jnp.where(lane == 1, loc2, jnp.where(lane == 2, w1,
                                                                          jnp.where(lane == 3, w2, 0.0))))
    col_ref[...] = col
    for c in range(col.shape[0] // LANES):
        rowf_ref[0, :, c * LANES:(c + 1) * LANES] = jnp.transpose(col[c * LANES:(c + 1) * LANES, :])[:SUBLANES, :]
    cnt_ref[0] = n


def _route(act, w_out, x, g, w_hi, w_lo, b, tb):
    t, d = x.shape
    nb = t // tb
    tri = jnp.asarray(np.tril(np.ones((tb, tb), np.float32), -1), BF16)
    full = lambda shape: pl.BlockSpec(shape, lambda i: (0,) * len(shape))
    return pl.pallas_call(
        _route_body,
        grid=(nb,),
        in_specs=[
            pl.BlockSpec((tb, act.shape[1]), lambda i: (i, 0)),
            full(w_out.shape),
            pl.BlockSpec((tb, d), lambda i: (i, 0)),
            full((1, d)), full((d, LANES)), full((d, LANES)), full((1, LANES)), full((tb, tb)),
        ],
        out_specs=[
            pl.BlockSpec((tb, d), lambda i: (i, 0)),
            pl.BlockSpec((tb, d), lambda i: (i, 0)),
            pl.BlockSpec((tb, LANES), lambda i: (i, 0)),
            pl.BlockSpec((1, SUBLANES, tb), lambda i: (i, 0, 0)),
            pl.BlockSpec((1, SUBLANES, LANES), lambda i: (i, 0, 0)),
        ],
        out_shape=[
            jax.ShapeDtypeStruct((t, d), F32),
            jax.ShapeDtypeStruct((t, d), BF16),
            jax.ShapeDtypeStruct((t, LANES), F32),
            jax.ShapeDtypeStruct((nb, SUBLANES, tb), F32),
            jax.ShapeDtypeStruct((nb, SUBLANES, LANES), F32),
        ],
        compiler_params=_cparams(("parallel",)),
        name="route",
    )(act, w_out, x, g, w_hi, w_lo, b, tri)


def _moe_plan(cnt):
    n = cnt[:, 0, :N_EXPERTS].astype(jnp.int32)
    chunks = (n + (MOE_ALIGN - 1)) // MOE_ALIGN
    src = jnp.cumsum(chunks, axis=1) - chunks
    seg_tiles = (jnp.sum(chunks, axis=0) + (MOE_CHUNKS_PER_TILE - 1)) // MOE_CHUNKS_PER_TILE
    seg_end = jnp.cumsum(seg_tiles)
    dst = ((seg_end - seg_tiles) * MOE_CHUNKS_PER_TILE)[None, :] + jnp.cumsum(chunks, axis=0) - chunks
    flat = lambda a: a.reshape(-1).astype(jnp.int32)
    return flat(src), flat(dst), flat(chunks), flat(seg_end - seg_tiles), flat(seg_tiles)


def _chunk_rows(chunk):
    return pl.ds(pl.multiple_of(chunk * MOE_ALIGN, MOE_ALIGN), MOE_ALIGN)


def _for_block_chunks(nch_ref, blk, fn):
    for e in range(N_EXPERTS):
        def body(c, carry, e=e):
            fn(e, c)
            return carry
        lax.fori_loop(0, nch_ref[blk * N_EXPERTS + e], body, 0)


def _block_chunk_count(nch_ref, blk):
    total = nch_ref[blk * N_EXPERTS]
    for e in range(1, N_EXPERTS):
        total = total + nch_ref[blk * N_EXPERTS + e]
    return total


def _gather_body(src_ref, dst_ref, nch_ref, h_ref, rowf_ref, zero_ref, hs_ref, z_ref, sem):
    del zero_ref
    b = pl.program_id(0)
    slot = lax.rem(b, 2)
    tb = h_ref.shape[0]
    loc = rowf_ref[0].astype(jnp.int32)
    r = lax.broadcasted_iota(jnp.int32, (z_ref.shape[1], tb), 0)
    onehot = jnp.where(r == loc[0:1, :], 1.0, jnp.where(r == loc[1:2, :], 1.0, 0.0)).astype(BF16)
    z_ref[slot] = _dot(onehot, h_ref[...]).astype(BF16)

    def copy(buf, src_chunk, dst_chunk):
        return pltpu.make_async_copy(z_ref.at[buf, _chunk_rows(src_chunk)], hs_ref.at[_chunk_rows(dst_chunk)],
                                     sem.at[buf])

    def drain(blk, buf):
        lax.fori_loop(0, _block_chunk_count(nch_ref, blk), lambda c, carry: (copy(buf, 0, 0).wait(), carry)[1], 0)

    @pl.when(b > 0)
    def _():
        drain(b - 1, 1 - slot)

    _for_block_chunks(nch_ref, b, lambda e, c: copy(slot, src_ref[b * N_EXPERTS + e] + c,
                                                    dst_ref[b * N_EXPERTS + e] + c).start())

    @pl.when(b == pl.num_programs(0) - 1)
    def _():
        drain(b, slot)


def _moe_gather(src, dst, nch, h, rowf, n_tiles, tb):
    t, d = h.shape
    rows = _moe_block_rows(tb)
    zeros = jnp.zeros((n_tiles * MOE_TILE, d), BF16)
    return pl.pallas_call(
        _gather_body,
        grid_spec=pltpu.PrefetchScalarGridSpec(
            num_scalar_prefetch=3,
            grid=(t // tb,),
            in_specs=[
                pl.BlockSpec((tb, d), lambda i, *_: (i, 0)),
                pl.BlockSpec((1, SUBLANES, tb), lambda i, *_: (i, 0, 0)),
                pl.BlockSpec(memory_space=pl.ANY),
            ],
            out_specs=pl.BlockSpec(memory_space=pl.ANY),
            scratch_shapes=[pltpu.VMEM((2, rows, d), BF16), pltpu.SemaphoreType.DMA((2,))],
        ),
        out_shape=jax.ShapeDtypeStruct(zeros.shape, BF16),
        input_output_aliases={5: 0},
        compiler_params=_cparams(("arbitrary",)),
        name="moe_gather",
    )(src, dst, nch, h, rowf, zeros)


def _expert_ffn_body(first_ref, count_ref, xs_ref, wg_ref, wu_ref, wd_ref, ys_ref, x_buf, y_buf, in_sem, out_sem,
                     *, n_tiles, halves):
    e = pl.program_id(0)
    first = first_ref[e]
    count = count_ref[e]
    tf = wg_ref.shape[2] // halves

    def tile_rows(tile):
        return pl.ds(pl.multiple_of(tile * MOE_TILE, MOE_TILE), MOE_TILE)

    def load(j, buf):
        return pltpu.make_async_copy(xs_ref.at[tile_rows(first + j)], x_buf.at[buf], in_sem.at[buf])

    def store(tile, buf):
        return pltpu.make_async_copy(y_buf.at[buf], ys_ref.at[tile_rows(tile)], out_sem.at[buf])

    @pl.when(count > 0)
    def _():
        load(0, 0).start()

    def tile_step(j, carry):
        buf = lax.rem(j, 2)

        @pl.when(j + 1 < count)
        def _():
            load(j + 1, 1 - buf).start()

        load(j, buf).wait()
        xs = x_buf[buf]
        y = None
        for f in range(halves):
            a = _dot(xs, wg_ref[0, :, f * tf:(f + 1) * tf])
            u = _dot(xs, wu_ref[0, :, f * tf:(f + 1) * tf])
            part = _dot((_silu(a) * u).astype(BF16), wd_ref[0, f * tf:(f + 1) * tf, :])
            y = part if y is None else y + part

        @pl.when(j >= 2)
        def _():
            store(first, buf).wait()

        y_buf[buf] = y.astype(BF16)
        store(first + j, buf).start()
        return carry

    lax.fori_loop(0, count, tile_step, 0)

    for back in (2, 1):
        @pl.when(count >= back)
        def _(back=back):
            store(first, lax.rem(count - back, 2)).wait()

    @pl.when(e == pl.num_programs(0) - 1)
    def _():
        end = first + count
        y_buf[0] = jnp.zeros(y_buf.shape[1:], BF16)
        lax.fori_loop(end, n_tiles, lambda t, c: (store(t, 0).start(), c)[1], 0)
        lax.fori_loop(end, n_tiles, lambda t, c: (store(t, 0).wait(), c)[1], 0)


def _expert_ffn(first_tile, tile_count, xs, wg, wu, wd):
    rows, d = xs.shape
    dff = wg.shape[2]
    per_expert = lambda shape: pl.BlockSpec((1,) + shape, lambda e, *_: (e, 0, 0))
    return pl.pallas_call(
        functools.partial(_expert_ffn_body, n_tiles=rows // MOE_TILE, halves=2),
        grid_spec=pltpu.PrefetchScalarGridSpec(
            num_scalar_prefetch=2,
            grid=(N_EXPERTS,),
            in_specs=[
                pl.BlockSpec(memory_space=pl.ANY),
                per_expert((d, dff)), per_expert((d, dff)), per_expert((dff, d)),
            ],
            out_specs=pl.BlockSpec(memory_space=pl.ANY),
            scratch_shapes=[pltpu.VMEM((2, MOE_TILE, d), BF16), pltpu.VMEM((2, MOE_TILE, d), BF16),
                            pltpu.SemaphoreType.DMA((2,)), pltpu.SemaphoreType.DMA((2,))],
        ),
        out_shape=jax.ShapeDtypeStruct((rows, d), BF16),
        compiler_params=_cparams(("arbitrary",)),
        name="expert_ffn",
    )(first_tile, tile_count, xs, wg, wu, wd)


def _combine_body(src_ref, dst_ref, nch_ref, x_ref, col_ref, gf_ref, ys_ref, o_ref, y_ref, sem):
    b = pl.program_id(0)
    nb = pl.num_programs(0)
    slot = lax.rem(b, 2)

    def copy(buf, src_chunk, dst_chunk):
        return pltpu.make_async_copy(ys_ref.at[_chunk_rows(dst_chunk)], y_ref.at[buf, _chunk_rows(src_chunk)],
                                     sem.at[buf])

    def fetch(blk, buf):
        _for_block_chunks(nch_ref, blk, lambda e, c: copy(buf, src_ref[blk * N_EXPERTS + e] + c,
                                                          dst_ref[blk * N_EXPERTS + e] + c).start())

    @pl.when(b == 0)
    def _():
        y_ref[...] = jnp.zeros_like(y_ref)
        fetch(0, 0)

    @pl.when(b + 1 < nb)
    def _():
        fetch(b + 1, 1 - slot)

    lax.fori_loop(0, _block_chunk_count(nch_ref, b), lambda c, carry: (copy(slot, 0, 0).wait(), carry)[1], 0)

    col = col_ref[...]
    loc1 = col[:, 0:1].astype(jnp.int32)
    loc2 = col[:, 1:2].astype(jnp.int32)
    r = lax.broadcasted_iota(jnp.int32, (col.shape[0], y_ref.shape[1]), 1)
    weights = jnp.where(r == loc1, col[:, 2:3], jnp.where(r == loc2, col[:, 3:4], 0.0)).astype(BF16)
    o_ref[...] = _rmsnorm_rows(x_ref[...] + _dot(weights, y_ref[slot]), gf_ref[...])


def _moe_combine(src, dst, nch, x, col, g_final, ys, tb):
    t, d = x.shape
    rows = _moe_block_rows(tb)
    return pl.pallas_call(
        _combine_body,
        grid_spec=pltpu.PrefetchScalarGridSpec(
            num_scalar_prefetch=3,
            grid=(t // tb,),
            in_specs=[
                pl.BlockSpec((tb, d), lambda i, *_: (i, 0)),
                pl.BlockSpec((tb, LANES), lambda i, *_: (i, 0)),
                pl.BlockSpec((1, d), lambda i, *_: (0, 0)),
                pl.BlockSpec(memory_space=pl.ANY),
            ],
            out_specs=pl.BlockSpec((tb, d), lambda i, *_: (i, 0)),
            scratch_shapes=[pltpu.VMEM((2, rows, d), BF16), pltpu.SemaphoreType.DMA((2,))],
        ),
        out_shape=jax.ShapeDtypeStruct((t, d), F32),
        compiler_params=_cparams(("arbitrary",)),
        name="moe_combine",
    )(src, dst, nch, x, col, g_final, ys)


def _moe(act, w_out, x_in, g, w, g_final):
    t, _ = x_in.shape
    tb = min(MOE_TB, t)
    nb = t // tb
    x, h, col, rowf, cnt = _route(act, w_out, x_in, g, w["w_router_hi"], w["w_router_lo"], w["b_router"], tb)
    n_tiles = -(-(2 * t + nb * N_EXPERTS * (MOE_ALIGN - 1)) // MOE_TILE) + N_EXPERTS
    src, dst, nch, first_tile, tile_count = _moe_plan(cnt)
    xs = _moe_gather(src, dst, nch, h, rowf, n_tiles, tb)
    ys = _expert_ffn(first_tile, tile_count, xs, w["w_moe_gate"], w["w_moe_up"], w["w_moe_down"])
    return _moe_combine(src, dst, nch, x, col, g_final, ys, tb)


def _retention_tables(c_real, ct):
    h = np.arange(RET_HEADS, dtype=np.float64)
    log_gamma = np.log1p(-np.exp2(-5.0 - h))
    idx = np.arange(ct, dtype=np.float64)
    live = idx < c_real
    diff = idx[:, None] - idx[None, :]
    causal = (diff >= 0) & live[:, None] & live[None, :]
    decay = np.where(causal[None], np.exp(np.where(causal, diff, 0.0)[None] * log_gamma[:, None, None]), 0.0)
    q_dec = np.where(live[None], np.exp((idx + 1.0)[None] * log_gamma[:, None]), 0.0)
    k_dec = np.where(live[None], np.exp((c_real - 1.0 - idx)[None] * log_gamma[:, None]), 0.0)
    q_dec = np.broadcast_to(q_dec[..., None], (RET_HEADS, ct, LANES))
    k_dec = np.broadcast_to(k_dec[..., None], (RET_HEADS, ct, LANES))
    return (jnp.asarray(decay, F32), jnp.asarray(q_dec, F32), jnp.asarray(k_dec, F32))


def _rope_tables(pos0, length):
    half = HEAD_DIM // 2
    inv = ROPE_BASE ** (-np.arange(half, dtype=np.float64) / half)
    ang = (pos0 + np.arange(length, dtype=np.float64))[:, None] * inv[None, :]
    cos = np.concatenate([np.cos(ang), np.cos(ang)], axis=-1)
    sin = np.concatenate([-np.sin(ang), np.sin(ang)], axis=-1)
    return jnp.asarray(cos, F32), jnp.asarray(sin, F32)


def _retention_body(q_ref, k_ref, v_ref, g_ref, cos_ref, sin_ref, dec_ref, qd_ref, kd_ref, s0_ref, gn_ref,
                    o_ref, s_ref, *, c_real, ct, bb):
    @pl.when(pl.program_id(1) == 0)
    def _():
        s_ref[...] = s0_ref[...]

    cos = cos_ref[...]
    sin = sin_ref[...]
    for g, hd in [(g, hd) for g in range(bb) for hd in range(RET_HEADS)]:
        lo, hi = hd * HEAD_DIM, (hd + 1) * HEAD_DIM
        q = q_ref[g, :, lo:hi]
        k = k_ref[g, :, lo:hi]
        q = (q * cos + pltpu.roll(q, HEAD_DIM // 2, 1) * sin) * (HEAD_DIM ** -0.5)
        k = k * cos + pltpu.roll(k, HEAD_DIM // 2, 1) * sin
        q = _pad_rows(q, ct)
        k = _pad_rows(k, ct)
        vb = _pad_rows(v_ref[g, :, lo:hi], ct).astype(BF16)
        qb = q.astype(BF16)
        q_dec = qd_ref[hd]
        state = s_ref[g, hd]

        s = _dot_nt(qb, k.astype(BF16)) * dec_ref[hd]
        o = _dot(s.astype(BF16), vb) + _dot(qb, state.astype(BF16)) * q_dec
        chunk_dec = q_dec[c_real - 1:c_real, :]
        s_ref[g, hd] = state * chunk_dec + _dot_tn((k * kd_ref[hd]).astype(BF16), vb)

        o = o[:c_real]
        o = o - jnp.mean(o, axis=-1, keepdims=True)
        o = o * lax.rsqrt(jnp.mean(o * o, axis=-1, keepdims=True) + EPS)
        o_ref[g, :, lo:hi] = _silu(g_ref[g, :, lo:hi]) * (o * gn_ref[:, lo:hi])


def _retention(proj, pos0, state, gn, bb):
    b, length, _ = proj.shape
    c_real = math.gcd(length, RET_CHUNK)
    ct = RET_CHUNK
    width = RET_HEADS * HEAD_DIM
    cos, sin = _rope_tables(pos0, length)
    decay, q_dec, k_dec = _retention_tables(c_real, ct)

    def col(i):
        return pl.BlockSpec((bb, c_real, width), lambda bi, c: (bi, c, i))

    full = lambda shape: pl.BlockSpec(shape, lambda bi, c: (0,) * len(shape))
    state_spec = pl.BlockSpec((bb, RET_HEADS, HEAD_DIM, HEAD_DIM), lambda bi, c: (bi, 0, 0, 0))
    return pl.pallas_call(
        functools.partial(_retention_body, c_real=c_real, ct=ct, bb=bb),
        grid=(b // bb, length // c_real),
        in_specs=[
            col(0), col(1), col(2), col(3),
            pl.BlockSpec((c_real, HEAD_DIM), lambda bi, c: (c, 0)),
            pl.BlockSpec((c_real, HEAD_DIM), lambda bi, c: (c, 0)),
            full((RET_HEADS, ct, ct)), full((RET_HEADS, ct, LANES)), full((RET_HEADS, ct, LANES)),
            state_spec,
            full((1, width)),
        ],
        out_specs=[
            pl.BlockSpec((bb, c_real, width), lambda bi, c: (bi, c, 0)),
            state_spec,
        ],
        out_shape=[
            jax.ShapeDtypeStruct((b, length, width), F32),
            jax.ShapeDtypeStruct(state.shape, F32),
        ],
        compiler_params=_cparams(("parallel", "arbitrary")),
        name="retention",
    )(proj, proj, proj, proj, cos, sin, decay, q_dec, k_dec, state, gn)


def _scan_rows(x, op, fill):
    rows = x.shape[0]
    row = lax.broadcasted_iota(jnp.int32, x.shape, 0)
    sh = 1
    while sh < rows:
        x = op(x, jnp.where(row >= sh, pltpu.roll(x, sh, 0), fill))
        sh *= 2
    return x


def _mlstm_body(q_ref, k_ref, v_ref, og_ref, gate_ref, bg_ref, cw_ref, cb_ref, cs_ref, c0_ref, n0_ref, m0_ref,
                gn_ref, o_ref, c_ref, n_ref, m_ref, conv_ref, xc_ref, *, c_real, ct, bb):
    tail = ML_CONV - 1

    @pl.when(pl.program_id(1) == 0)
    def _():
        c_ref[...] = c0_ref[...]
        n_ref[...] = n0_ref[...]
        m_ref[...] = m0_ref[...]
        xc_ref[:, SUBLANES - tail:SUBLANES, :] = cs_ref[...]

    for g in range(bb):
        _mlstm_sequence(q_ref.at[g], k_ref.at[g], v_ref.at[g], og_ref.at[g], gate_ref.at[g], bg_ref, cw_ref, cb_ref,
                        gn_ref, o_ref.at[g], c_ref.at[g], n_ref.at[g], m_ref.at[g], conv_ref.at[g], xc_ref.at[g],
                        c_real=c_real, ct=ct)


def _mlstm_sequence(q_ref, k_ref, v_ref, og_ref, gate_ref, bg_ref, cw_ref, cb_ref, gn_ref, o_ref, c_ref, n_ref, m_ref,
                    conv_ref, xc_ref, *, c_real, ct):
    width = ML_HEADS * HEAD_DIM
    tail = ML_CONV - 1

    xc_ref[SUBLANES:SUBLANES + c_real, :width] = q_ref[...]
    xc_ref[SUBLANES:SUBLANES + c_real, width:] = k_ref[...]
    y = cb_ref[...] + cw_ref[tail:tail + 1, :] * xc_ref[SUBLANES:SUBLANES + c_real, :]
    for j in range(tail):
        y = y + cw_ref[j:j + 1, :] * xc_ref[SUBLANES - tail + j:SUBLANES - tail + j + c_real, :]
    new_tail = xc_ref[c_real:c_real + SUBLANES, :]
    conv_ref[...] = new_tail[SUBLANES - tail:, :]
    xc_ref[0:SUBLANES, :] = new_tail
    qk = _silu(y)
    q_all = _pad_rows(qk[:, :width], ct)
    k_all = _pad_rows(qk[:, width:] * (HEAD_DIM ** -0.5), ct)
    v_all = _pad_rows(v_ref[...], ct)

    gates = _pad_rows(gate_ref[...] + bg_ref[...], ct)
    f_pre = pltpu.roll(gates, LANES - ML_HEADS, 1)
    log_f = jnp.minimum(f_pre, 0.0) - jnp.log(1.0 + jnp.exp(-jnp.abs(f_pre)))
    b_cum = _scan_rows(log_f, jnp.add, 0.0)
    src = gates - b_cum
    cmx = _scan_rows(src, jnp.maximum, NEG_INF)
    m_prev = m_ref[...]
    mx = jnp.maximum(m_prev, cmx)
    m_t = b_cum + mx
    w_prev = jnp.exp(m_prev - mx)
    e_neg_m = jnp.exp(-m_t)
    last = c_real - 1
    b_last = b_cum[last:last + 1, :]
    m_new = m_t[last:last + 1, :]
    row = lax.broadcasted_iota(jnp.int32, (ct, LANES), 0)
    w_end = jnp.where(row < c_real, jnp.exp(b_last + src - m_new), 0.0)
    dec = jnp.exp(b_last + m_prev - m_new)
    m_ref[...] = m_new
    src_t = jnp.transpose(src)

    t_idx = lax.broadcasted_iota(jnp.int32, (ct, ct), 0)
    s_idx = lax.broadcasted_iota(jnp.int32, (ct, ct), 1)
    live = (s_idx <= t_idx) & (s_idx < c_real)

    for hd in range(ML_HEADS):
        lo, hi = hd * HEAD_DIM, (hd + 1) * HEAD_DIM
        q = q_all[:, lo:hi]
        k = k_all[:, lo:hi]
        qb = q.astype(BF16)
        vb = v_all[:, lo:hi].astype(BF16)
        c_state = c_ref[hd]
        n_state = n_ref[hd:hd + 1, :]
        wp = w_prev[:, hd:hd + 1]
        w = jnp.where(live, jnp.exp(src_t[hd:hd + 1, :] - mx[:, hd:hd + 1]), 0.0)
        s = _dot_nt(qb, k.astype(BF16)) * w
        num = _dot(s.astype(BF16), vb) + wp * _dot(qb, c_state.astype(BF16))
        den = jnp.sum(s, axis=-1, keepdims=True) + wp * jnp.sum(q * n_state, axis=-1, keepdims=True)
        hh = num / jnp.maximum(jnp.abs(den), e_neg_m[:, hd:hd + 1])
        kw = k * w_end[:, hd:hd + 1]
        dec_h = dec[:, hd:hd + 1]
        c_ref[hd] = dec_h * c_state + _dot_tn(kw.astype(BF16), vb)
        n_ref[hd:hd + 1, :] = dec_h * n_state + jnp.sum(kw, axis=0, keepdims=True)

        hh = hh[:c_real]
        hh = _sigmoid(og_ref[:, lo:hi]) * hh
        hh = hh - jnp.mean(hh, axis=-1, keepdims=True)
        hh = hh * lax.rsqrt(jnp.mean(hh * hh, axis=-1, keepdims=True) + EPS)
        o_ref[:, lo:hi] = hh * gn_ref[:, lo:hi]


def _mlstm(proj, b_gates, conv_w, conv_b, conv_s, c0, n0, m0, gn, bb):
    b, length, _ = proj.shape
    c_real = math.gcd(length, ML_CHUNK)
    ct = ML_CHUNK
    width = ML_HEADS * HEAD_DIM
    base = (2 * RET_HEADS * HEAD_DIM + 2 * RET_HEADS * HEAD_DIM) // width

    def col(i):
        return pl.BlockSpec((bb, c_real, width), lambda bi, c: (bi, c, base + i))

    full = lambda shape: pl.BlockSpec(shape, lambda bi, c: (0,) * len(shape))
    per_b = lambda shape: pl.BlockSpec((bb,) + shape, lambda bi, c: (bi,) + (0,) * len(shape))
    bg = jnp.zeros((1, LANES), F32).at[0, :2 * ML_HEADS].set(b_gates)
    m0p = jnp.zeros((b, 1, LANES), F32).at[:, 0, :ML_HEADS].set(m0)
    outs = pl.pallas_call(
        functools.partial(_mlstm_body, c_real=c_real, ct=ct, bb=bb),
        grid=(b // bb, length // c_real),
        in_specs=[
            col(0), col(1), col(2), col(3),
            pl.BlockSpec((bb, c_real, LANES), lambda bi, c: (bi, c, AB_MAIN // LANES)),
            full((1, LANES)),
            full((ML_CONV, 2 * width)),
            full((1, 2 * width)),
            per_b((ML_CONV - 1, 2 * width)),
            per_b((ML_HEADS, HEAD_DIM, HEAD_DIM)),
            per_b((ML_HEADS, HEAD_DIM)),
            per_b((1, LANES)),
            full((1, width)),
        ],
        out_specs=[
            pl.BlockSpec((bb, c_real, width), lambda bi, c: (bi, c, 0)),
            per_b((ML_HEADS, HEAD_DIM, HEAD_DIM)),
            per_b((ML_HEADS, HEAD_DIM)),
            per_b((1, LANES)),
            per_b((ML_CONV - 1, 2 * width)),
        ],
        out_shape=[
            jax.ShapeDtypeStruct((b, length, width), F32),
            jax.ShapeDtypeStruct(c0.shape, F32),
            jax.ShapeDtypeStruct(n0.shape, F32),
            jax.ShapeDtypeStruct((b, 1, LANES), F32),
            jax.ShapeDtypeStruct(conv_s.shape, F32),
        ],
        scratch_shapes=[pltpu.VMEM((bb, ct + SUBLANES, 2 * width), F32)],
        compiler_params=_cparams(("parallel", "arbitrary")),
        name="mlstm",
    )(proj, proj, proj, proj, proj, bg, conv_w, conv_b, conv_s, c0, n0, m0p, gn)
    h_m, c_new, n_new, m_new, conv_new = outs
    return h_m, c_new, n_new, m_new[:, 0, :ML_HEADS], conv_new


HG_UNROLL_BLOCK = 8
HG_UNROLL_STEP = 8


def _hgrn_head(cols, q_ref, f_ref, i_ref, g_ref, hl_ref, gn_ref, o_ref, st, *, blk, sub):
    hl = hl_ref[:, cols]
    hmax = jnp.max(hl, axis=0, keepdims=True)
    ex = jnp.exp(hl - hmax)
    p = ex / jnp.sum(ex, axis=0, keepdims=True)
    lower = (p[0:1] + p[1:2]) - p[0:1]
    gn = gn_ref[:, cols]

    f = lower + (1.0 - lower) * _sigmoid(f_ref[:, cols])
    log_k = jnp.log2(jnp.maximum(1.0 - f, 0.0))
    q = _silu(q_ref[:, cols])
    v = i_ref[:, cols]
    gate = _silu(g_ref[:, cols])

    row = lax.broadcasted_iota(jnp.int32, (blk, LANES), 0)
    rsub = row & (sub - 1)
    a = jnp.log2(f)
    sh = 1
    while sh < sub:
        a = a + jnp.where(rsub >= sh, pltpu.roll(a, sh, 0), 0.0)
        sh *= 2
    c = a - log_k
    q_in = q * jnp.exp2(a)

    t_idx = lax.broadcasted_iota(jnp.int32, (SUBLANES, LANES), 0)
    for j in range(blk // sub):
        r0 = j * sub
        a_end = a[r0 + sub - 1:r0 + sub, :]
        inter = _dot_nt(q_in[r0:r0 + sub].astype(BF16), st.astype(BF16))
        k_hat = jnp.exp2(a_end - c[r0:r0 + sub])
        v_j = v[r0:r0 + sub]
        st = st * jnp.exp2(a_end) + _dot_tn(v_j.astype(BF16), k_hat.astype(BF16))
        for part in range(sub // SUBLANES):
            t0 = r0 + part * SUBLANES
            a_t = a[t0:t0 + SUBLANES]
            q_t = q[t0:t0 + SUBLANES]
            o_t = inter[part * SUBLANES:(part + 1) * SUBLANES]
            for s in range((part + 1) * SUBLANES):
                arg = a_t - c[r0 + s:r0 + s + 1, :]
                if s >= part * SUBLANES:
                    arg = jnp.where(t_idx >= s - part * SUBLANES, arg, NEG_INF)
                z = q_t * jnp.exp2(arg)
                o_t = o_t + jnp.sum(z, axis=-1, keepdims=True) * v[r0 + s:r0 + s + 1, :]
            o_t = o_t * lax.rsqrt(jnp.mean(o_t * o_t, axis=-1, keepdims=True) + EPS)
            o_ref[t0:t0 + SUBLANES, cols] = o_t * gn * gate[t0:t0 + SUBLANES]
    return st


def _block_ref_rows(a, level, row8):
    rows = a.shape[0]
    half = level // 2
    if half >= SUBLANES:
        return jnp.concatenate([jnp.broadcast_to(a[b0 + half - 1:b0 + half, :], (level, LANES))
                                for b0 in range(0, rows, level)], axis=0)
    if level == 2:
        return jnp.where((row8 & 1) == 1, pltpu.roll(a, 1, 0), a)
    pieces = []
    for v0 in range(0, rows, SUBLANES):
        picks = [jnp.broadcast_to(a[v0 + b0 + half - 1:v0 + b0 + half, :], (SUBLANES, LANES))
                 for b0 in range(0, SUBLANES, level)]
        piece = picks[-1]
        for i in range(len(picks) - 2, -1, -1):
            piece = jnp.where(row8[:SUBLANES] < (i + 1) * level, picks[i], piece)
        pieces.append(piece)
    return jnp.concatenate(pieces, axis=0)


def _hgrn_head_block(cols, q_ref, f_ref, i_ref, g_ref, hl_ref, gn_ref, o_ref, st, *, blk):
    hl = hl_ref[:, cols]
    hmax = jnp.max(hl, axis=0, keepdims=True)
    ex = jnp.exp(hl - hmax)
    p = ex / jnp.sum(ex, axis=0, keepdims=True)
    lower = (p[0:1] + p[1:2]) - p[0:1]

    f = lower + (1.0 - lower) * _sigmoid(f_ref[:, cols])
    kk = 1.0 - f
    q = _silu(q_ref[:, cols])
    vb = i_ref[:, cols].astype(BF16)
    a = _scan_rows(jnp.log2(f), jnp.add, 0.0)
    c = a - jnp.log2(jnp.maximum(kk, 0.0))
    row8 = lax.broadcasted_iota(jnp.int32, (blk, LANES), 0) & (SUBLANES - 1)
    group = (lax.broadcasted_iota(jnp.int32, (blk, blk), 0)
             ^ lax.broadcasted_iota(jnp.int32, (blk, blk), 1))

    scores = None
    level = blk
    while level >= 2:
        half = level // 2
        if half >= SUBLANES:
            zero = jnp.zeros((half, LANES), F32)
            qs, ks = [], []
            for b0 in range(0, blk, level):
                ref = a[b0 + half - 1:b0 + half, :]
                qs += [zero, q[b0 + half:b0 + level] * jnp.exp2(a[b0 + half:b0 + level] - ref)]
                ks += [jnp.exp2(ref - c[b0:b0 + half]), zero]
            q_l = jnp.concatenate(qs, axis=0)
            k_l = jnp.concatenate(ks, axis=0)
        else:
            ref = _block_ref_rows(a, level, row8)
            upper = (row8 & (level - 1)) >= half
            q_l = jnp.where(upper, q * jnp.exp2(a - ref), 0.0)
            k_l = jnp.where(upper, 0.0, jnp.exp2(ref - c))
        r = _dot_nt(q_l.astype(BF16), k_l.astype(BF16))
        scores = r if scores is None else jnp.where(group < level, r, scores)
        level = half
    scores = jnp.where(group < 1, _dot_nt(q.astype(BF16), kk.astype(BF16)), scores)

    a_end = a[blk - 1:blk, :]
    o = _dot_nt((q * jnp.exp2(a)).astype(BF16), st.astype(BF16)) + _dot(scores.astype(BF16), vb)
    o = o * lax.rsqrt(jnp.mean(o * o, axis=-1, keepdims=True) + EPS)
    o_ref[:, cols] = o * gn_ref[:, cols] * _silu(g_ref[:, cols])
    return st * jnp.exp2(a_end) + _dot_tn(vb, jnp.exp2(a_end - c).astype(BF16))


def _hgrn_body(q_ref, f_ref, i_ref, g_ref, hl_ref, s0_ref, gn_ref, o_ref, s_ref, st_ref, *, blk, sub, bb):
    @pl.when(pl.program_id(1) == 0)
    def _():
        for g, h in [(g, h) for g in range(bb) for h in range(HG_HEADS)]:
            st_ref[g, h] = jnp.transpose(s0_ref[g, h])

    if blk == HG_BLOCK:
        head, per_iter = functools.partial(_hgrn_head_block, blk=blk), HG_UNROLL_BLOCK // bb
    else:
        head, per_iter = functools.partial(_hgrn_head, blk=blk, sub=sub), HG_UNROLL_STEP // bb

    def heads(it, carry):
        for g, u in [(g, u) for g in range(bb) for u in range(per_iter)]:
            h = it * per_iter + u
            cols = pl.ds(pl.multiple_of(h * HEAD_DIM, HEAD_DIM), HEAD_DIM)
            st_ref[g, h] = head(cols, q_ref.at[g], f_ref.at[g], i_ref.at[g], g_ref.at[g], hl_ref, gn_ref,
                                o_ref.at[g], st_ref[g, h])
        return carry

    lax.fori_loop(0, HG_HEADS // per_iter, heads, 0)

    @pl.when(pl.program_id(1) == pl.num_programs(1) - 1)
    def _():
        for g, h in [(g, h) for g in range(bb) for h in range(HG_HEADS)]:
            s_ref[g, h] = jnp.transpose(st_ref[g, h])


def _hgrn(proj, hg_lower, state, gn, bb):
    b, length, _ = proj.shape
    blk = min(length, HG_BLOCK)
    sub = math.gcd(length, HG_SUB)
    width = HG_HEADS * HEAD_DIM

    def col(i):
        return pl.BlockSpec((bb, blk, width), lambda bi, c: (bi, c, i))

    full = lambda shape: pl.BlockSpec(shape, lambda bi, c: (0,) * len(shape))
    state_spec = pl.BlockSpec((bb, HG_HEADS, HEAD_DIM, HEAD_DIM), lambda bi, c: (bi, 0, 0, 0))
    return pl.pallas_call(
        functools.partial(_hgrn_body, blk=blk, sub=sub, bb=bb),
        grid=(b // bb, length // blk),
        in_specs=[col(0), col(1), col(2), col(3), full(hg_lower.shape), state_spec, full((1, width))],
        out_specs=[pl.BlockSpec((bb, blk, width), lambda bi, c: (bi, c, 0)), state_spec],
        out_shape=[
            jax.ShapeDtypeStruct((b, length, width), F32),
            jax.ShapeDtypeStruct(state.shape, F32),
        ],
        scratch_shapes=[pltpu.VMEM((bb, HG_HEADS, HEAD_DIM, HEAD_DIM), F32)],
        compiler_params=_cparams(("parallel", "arbitrary")),
        name="hgrn2",
    )(proj, proj, proj, proj, hg_lower, state, gn)


def _prep_weights(p):
    w_in_ab = p["w_in_ab"][0]
    w_ab = jnp.zeros((D_MODEL, AB_PAD), BF16)
    w_ab = w_ab.at[:, :AB_MAIN].set(w_in_ab[:, :AB_MAIN].astype(BF16))
    w_ab = w_ab.at[:, AB_MAIN:AB_MAIN + 2 * ML_HEADS].set(w_in_ab[:, AB_MAIN:].astype(BF16))
    w_router = jnp.zeros((D_MODEL, LANES), F32).at[:, :N_EXPERTS].set(p["w_router"][0])
    w_router_hi = w_router.astype(BF16)
    w_router_lo = (w_router - w_router_hi.astype(F32)).astype(BF16)
    b_router = jnp.zeros((1, LANES), F32).at[0, :N_EXPERTS].set(p["b_router"][0])
    return {
        "w_ab": w_ab,
        "w_out_ab": p["w_out_ab"][0].astype(BF16),
        "w_in_c": p["w_in_c"][0].astype(BF16),
        "w_out_c": p["w_out_c"][0].astype(BF16),
        "w_ffn_gate": p["w_ffn_gate"][0].astype(BF16),
        "w_ffn_up": p["w_ffn_up"][0].astype(BF16),
        "w_ffn_down": p["w_ffn_down"][0].astype(BF16),
        "w_router_hi": w_router_hi,
        "w_router_lo": w_router_lo,
        "b_router": b_router,
        "w_moe_gate": p["w_moe_gate"][0].astype(BF16),
        "w_moe_up": p["w_moe_up"][0].astype(BF16),
        "w_moe_down": p["w_moe_down"][0].astype(BF16),
    }


def _trunk(x, pos0, ret_s, mc_s, mn_s, mm_s, conv_s, hg_s, p, w):
    b, length, d = x.shape
    t = b * length
    tm = min(512, t)
    tm_mm = min(1024, t)
    prompt = length >= RET_CHUNK
    bb = 1 if prompt else 4
    assert t % tm_mm == 0 and t % tm == 0 and b % bb == 0 and b % 2 == 0, (b, length)
    row = lambda v: v.reshape(1, -1)
    x0 = x.reshape(t, d)

    proj = _norm_matmul(x0, row(p["ln_mix"][0]), w["w_ab"], tm_mm, AB_PAD // 3).reshape(b, length, AB_PAD)
    o_ret, ret_new = _retention(proj, pos0, ret_s, row(p["ret_gn"][0]), 2 if prompt else bb)
    h_m, c_new, n_new, m_new, conv_new = _mlstm(
        proj, p["b_gates_ab"][0], p["conv_w_ab"][0], row(p["conv_b_ab"][0]), conv_s, mc_s, mn_s, mm_s,
        row(p["ml_gn"][0]), bb)
    half = RET_HEADS * HEAD_DIM
    x2 = _ffn([o_ret.reshape(t, half), h_m.reshape(t, half)], [w["w_out_ab"][:half], w["w_out_ab"][half:]], x0,
              row(p["ln_ffn"][0]), w["w_ffn_gate"], w["w_ffn_up"], w["w_ffn_down"], tm, D_FF // 2)

    proj_c = _norm_matmul(x2, row(p["ln_mix"][1]), w["w_in_c"], tm_mm, 1024).reshape(b, length, 4 * D_MODEL)
    o_hg, hg_new = _hgrn(proj_c, p["hg_lower"], hg_s, row(p["hg_gn"][0]), bb)
    y = _moe(o_hg.reshape(t, d), w["w_out_c"], x2, row(p["ln_ffn"][1]), w, row(p["ln_final"]))
    return (y.reshape(b, length, d), ret_new[None], c_new[None], n_new[None], m_new[None], conv_new[None],
            hg_new[None])


def kernel(x_prompt, x_sample, state_ret, state_mlstm_c, state_mlstm_n, state_mlstm_m, state_conv, state_hgrn,
           ln_mix, ln_ffn, ln_final, w_in_ab, b_gates_ab, conv_w_ab, conv_b_ab, ret_gn, ml_gn, w_out_ab,
           w_in_c, hg_lower, hg_gn, w_out_c, w_ffn_gate, w_ffn_up, w_ffn_down,
           w_router, b_router, w_moe_gate, w_moe_up, w_moe_down):
    p = {"ln_mix": ln_mix, "ln_ffn": ln_ffn, "ln_final": ln_final,
         "w_in_ab": w_in_ab, "b_gates_ab": b_gates_ab, "conv_w_ab": conv_w_ab, "conv_b_ab": conv_b_ab,
         "ret_gn": ret_gn, "ml_gn": ml_gn, "w_out_ab": w_out_ab,
         "w_in_c": w_in_c, "hg_lower": hg_lower, "hg_gn": hg_gn, "w_out_c": w_out_c,
         "w_ffn_gate": w_ffn_gate, "w_ffn_up": w_ffn_up, "w_ffn_down": w_ffn_down,
         "w_router": w_router, "b_router": b_router,
         "w_moe_gate": w_moe_gate, "w_moe_up": w_moe_up, "w_moe_down": w_moe_down}
    w = _prep_weights(p)
    bp = x_prompt.shape[0]
    zeros = lambda *shape: jnp.zeros(shape, F32)
    prompt = _trunk(
        x_prompt, 0,
        zeros(bp, RET_HEADS, HEAD_DIM, HEAD_DIM), zeros(bp, ML_HEADS, HEAD_DIM, HEAD_DIM),
        zeros(bp, ML_HEADS, HEAD_DIM), jnp.full((bp, ML_HEADS), NEG_INF, F32),
        zeros(bp, ML_CONV - 1, 2 * ML_HEADS * HEAD_DIM), zeros(bp, HG_HEADS, HEAD_DIM, HEAD_DIM), p, w)
    sample = _trunk(
        x_sample, 16384,
        state_ret[0], state_mlstm_c[0], state_mlstm_n[0], state_mlstm_m[0], state_conv[0], state_hgrn[0], p, w)
    out = []
    for a, s in zip(prompt, sample):
        out += [a, s]
    return tuple(out)
```

```python
import functools
import math

import numpy as np
import jax
import jax.numpy as jnp
from jax import lax
from jax.experimental import pallas as pl
from jax.experimental.pallas import tpu as pltpu

F32 = jnp.float32
BF16 = jnp.bfloat16

D_MODEL = 1024
HEAD_DIM = 128
RET_HEADS = 4
ML_HEADS = 4
ML_CONV = 4
HG_HEADS = 8
D_FF = 2816
N_EXPERTS = 8
ROPE_BASE = 10000.0
EPS = 1e-6
RET_CHUNK = 128
ML_CHUNK = 128
HG_BLOCK = 128
HG_SUB = 16

LANES = 128
SUBLANES = 8
AB_MAIN = 4096
VMEM_LIMIT = 56 * 1024 * 1024

NEG_INF = float("-inf")


def _cparams(sem):
    return pltpu.CompilerParams(dimension_semantics=sem, vmem_limit_bytes=VMEM_LIMIT)


def _sigmoid(x):
    return 1.0 / (1.0 + jnp.exp(-x))


def _silu(x):
    return x * _sigmoid(x)


def _rmsnorm_rows(x, g):
    ms = jnp.mean(x * x, axis=-1, keepdims=True)
    return x * lax.rsqrt(ms + EPS) * g


def _dot(a, b):
    return jnp.dot(a, b, preferred_element_type=F32)


def _dot_nt(a, b):
    return lax.dot_general(a, b, (((1,), (1,)), ((), ())), preferred_element_type=F32)


def _dot_tn(a, b):
    return lax.dot_general(a, b, (((0,), (0,)), ((), ())), preferred_element_type=F32)


def _pad_rows(x, rows):
    if x.shape[0] == rows:
        return x
    return jnp.concatenate([x, jnp.zeros((rows - x.shape[0], x.shape[1]), x.dtype)], axis=0)


def _norm_matmul_body(x_ref, g_ref, w_ref, ws_ref, o_ref, os_ref, h_ref):
    @pl.when(pl.program_id(1) == 0)
    def _():
        h = _rmsnorm_rows(x_ref[...], g_ref[...]).astype(BF16)
        h_ref[...] = h
        os_ref[...] = _dot(h, ws_ref[...])

    o_ref[...] = _dot(h_ref[...], w_ref[...]).astype(BF16)


def _norm_matmul(x, g, w, w_side, tm, tn):
    t, d = x.shape
    n, ns = w.shape[1], w_side.shape[1]
    return pl.pallas_call(
        _norm_matmul_body,
        grid=(t // tm, n // tn),
        in_specs=[
            pl.BlockSpec((tm, d), lambda i, j: (i, 0)),
            pl.BlockSpec((1, d), lambda i, j: (0, 0)),
            pl.BlockSpec((d, tn), lambda i, j: (0, j)),
            pl.BlockSpec((d, ns), lambda i, j: (0, 0)),
        ],
        out_specs=[pl.BlockSpec((tm, tn), lambda i, j: (i, j)), pl.BlockSpec((tm, ns), lambda i, j: (i, 0))],
        out_shape=[jax.ShapeDtypeStruct((t, n), BF16), jax.ShapeDtypeStruct((t, ns), F32)],
        scratch_shapes=[pltpu.VMEM((tm, d), BF16)],
        compiler_params=_cparams(("parallel", "arbitrary")),
        name="norm_matmul",
    )(x, g, w, w_side)


def _mixer_residual(x_ref, a_refs, w_refs):
    x = x_ref[...]
    for a_ref, w_ref in zip(a_refs, w_refs):
        x = x + _dot(a_ref[...].astype(BF16), w_ref[...])
    return x


def _ffn_body(*refs, n_in):
    a_refs, w_refs = refs[:n_in], refs[n_in:2 * n_in]
    x_ref, g_ref, wg_ref, wu_ref, wd_ref, o_ref, h_ref, x1_ref = refs[2 * n_in:]
    f = pl.program_id(1)

    @pl.when(f == 0)
    def _():
        x1 = _mixer_residual(x_ref, a_refs, w_refs)
        x1_ref[...] = x1
        h_ref[...] = _rmsnorm_rows(x1, g_ref[...]).astype(BF16)

    h = h_ref[...]
    a = _dot(h, wg_ref[...])
    u = _dot(h, wu_ref[...])
    act = (_silu(a) * u).astype(BF16)
    x1_ref[...] += _dot(act, wd_ref[...])

    @pl.when(f == pl.num_programs(1) - 1)
    def _():
        o_ref[...] = x1_ref[...]


def _ffn(acts, weights, x, g, wg, wu, wd, tm, tf):
    t, d = x.shape
    dff = wg.shape[1]
    n_in = len(acts)
    in_specs = [pl.BlockSpec((tm, a.shape[1]), lambda i, f: (i, 0)) for a in acts]
    in_specs += [pl.BlockSpec(w.shape, lambda i, f: (0, 0)) for w in weights]
    in_specs += [
        pl.BlockSpec((tm, d), lambda i, f: (i, 0)),
        pl.BlockSpec((1, d), lambda i, f: (0, 0)),
        pl.BlockSpec((d, tf), lambda i, f: (0, f)),
        pl.BlockSpec((d, tf), lambda i, f: (0, f)),
        pl.BlockSpec((tf, d), lambda i, f: (f, 0)),
    ]
    return pl.pallas_call(
        functools.partial(_ffn_body, n_in=n_in),
        grid=(t // tm, dff // tf),
        in_specs=in_specs,
        out_specs=pl.BlockSpec((tm, d), lambda i, f: (i, 0)),
        out_shape=jax.ShapeDtypeStruct((t, d), F32),
        scratch_shapes=[pltpu.VMEM((tm, d), BF16), pltpu.VMEM((tm, d), F32)],
        compiler_params=_cparams(("parallel", "arbitrary")),
        name="ffn",
    )(*acts, *weights, x, g, wg, wu, wd)


MOE_TB = 512
MOE_ALIGN = 16
MOE_TILE = 256
MOE_CHUNKS_PER_TILE = MOE_TILE // MOE_ALIGN


def _moe_block_rows(tb):
    return 2 * tb + N_EXPERTS * MOE_ALIGN


def _route_body(a_ref, wo_ref, x_ref, g_ref, whi_ref, wlo_ref, b_ref, tri_ref,
                x3_ref, h_ref, col_ref, rowf_ref, cnt_ref):
    x3 = _mixer_residual(x_ref, [a_ref], [wo_ref])
    x3_ref[...] = x3
    h = _rmsnorm_rows(x3, g_ref[...])
    h_hi = h.astype(BF16)
    h_ref[...] = h_hi
    h_lo = (h - h_hi.astype(F32)).astype(BF16)
    logits = (_dot(h_hi, whi_ref[...]) + _dot(h_hi, wlo_ref[...]) + _dot(h_lo, whi_ref[...])
              + b_ref[...])
    lane = lax.broadcasted_iota(jnp.int32, logits.shape, 1)
    lg = jnp.where(lane < N_EXPERTS, logits, NEG_INF)
    m1 = jnp.max(lg, axis=-1, keepdims=True)
    i1 = jnp.min(jnp.where(lg == m1, lane, LANES), axis=-1, keepdims=True)
    lg2 = jnp.where(lane == i1, NEG_INF, lg)
    m2 = jnp.max(lg2, axis=-1, keepdims=True)
    i2 = jnp.min(jnp.where(lg2 == m2, lane, LANES), axis=-1, keepdims=True)
    e = jnp.exp(m2 - m1)
    w1 = 1.0 / (1.0 + e)
    w2 = e / (1.0 + e)

    sel = jnp.where(lane == i1, 1.0, jnp.where(lane == i2, 1.0, 0.0))
    before = _dot(tri_ref[...], sel.astype(BF16))
    n = jnp.broadcast_to(jnp.sum(sel, axis=0, keepdims=True), (SUBLANES, LANES))
    padded = jnp.floor((n + (MOE_ALIGN - 1.0)) * (1.0 / MOE_ALIGN)) * MOE_ALIGN
    lane8 = lax.broadcasted_iota(jnp.int32, padded.shape, 1)
    incl = padded
    sh = 1
    while sh < N_EXPERTS:
        incl = incl + jnp.where(lane8 >= sh, pltpu.roll(incl, sh, 1), 0.0)
        sh *= 2
    pos = (incl - padded)[0:1, :] + before
    loc1 = jnp.sum(jnp.where(lane == i1, pos, 0.0), axis=-1, keepdims=True)
    loc2 = jnp.sum(jnp.where(lane == i2, pos, 0.0), axis=-1, keepdims=True)
    col = jnp.where(lane == 0, loc1, jnp.where(lane == 1, loc2, jnp.where(lane == 2, w1,
                                                                          jnp.where(lane == 3, w2, 0.0))))
    col_ref[...] = col
    for c in range(col.shape[0] // LANES):
        rowf_ref[0, :, c * LANES:(c + 1) * LANES] = jnp.transpose(col[c * LANES:(c + 1) * LANES, :])[:SUBLANES, :]
    cnt_ref[0] = n


def _route(act, w_out, x, g, w_hi, w_lo, b, tb):
    t, d = x.shape
    nb = t // tb
    tri = jnp.asarray(np.tril(np.ones((tb, tb), np.float32), -1), BF16)
    full = lambda shape: pl.BlockSpec(shape, lambda i: (0,) * len(shape))
    return pl.pallas_call(
        _route_body,
        grid=(nb,),
        in_specs=[
            pl.BlockSpec((tb, act.shape[1]), lambda i: (i, 0)),
            full(w_out.shape),
            pl.BlockSpec((tb, d), lambda i: (i, 0)),
            full((1, d)), full((d, LANES)), full((d, LANES)), full((1, LANES)), full((tb, tb)),
        ],
        out_specs=[
            pl.BlockSpec((tb, d), lambda i: (i, 0)),
            pl.BlockSpec((tb, d), lambda i: (i, 0)),
            pl.BlockSpec((tb, LANES), lambda i: (i, 0)),
            pl.BlockSpec((1, SUBLANES, tb), lambda i: (i, 0, 0)),
            pl.BlockSpec((1, SUBLANES, LANES), lambda i: (i, 0, 0)),
        ],
        out_shape=[
            jax.ShapeDtypeStruct((t, d), F32),
            jax.ShapeDtypeStruct((t, d), BF16),
            jax.ShapeDtypeStruct((t, LANES), F32),
            jax.ShapeDtypeStruct((nb, SUBLANES, tb), F32),
            jax.ShapeDtypeStruct((nb, SUBLANES, LANES), F32),
        ],
        compiler_params=_cparams(("parallel",)),
        name="route",
    )(act, w_out, x, g, w_hi, w_lo, b, tri)


def _moe_plan(cnt):
    n = cnt[:, 0, :N_EXPERTS].astype(jnp.int32)
    chunks = (n + (MOE_ALIGN - 1)) // MOE_ALIGN
    src = jnp.cumsum(chunks, axis=1) - chunks
    seg_chunks = jnp.sum(chunks, axis=0)
    seg_tiles = (seg_chunks + (MOE_CHUNKS_PER_TILE - 1)) // MOE_CHUNKS_PER_TILE
    seg_end = jnp.cumsum(seg_tiles)
    seg_first = (seg_end - seg_tiles) * MOE_CHUNKS_PER_TILE
    dst = seg_first[None, :] + jnp.cumsum(chunks, axis=0) - chunks
    flat = lambda a: a.reshape(-1).astype(jnp.int32)
    return {"src": flat(src), "dst": flat(dst), "chunks": flat(chunks),
            "first_tile": flat(seg_end - seg_tiles), "tile_count": flat(seg_tiles),
            "tail_start": flat(seg_first + seg_chunks), "tail_count": flat(seg_tiles * MOE_CHUNKS_PER_TILE - seg_chunks)}


def _chunk_rows(chunk):
    return pl.ds(pl.multiple_of(chunk * MOE_ALIGN, MOE_ALIGN), MOE_ALIGN)


def _for_block_chunks(nch_ref, blk, fn):
    for e in range(N_EXPERTS):
        def body(c, carry, e=e):
            fn(e, c)
            return carry
        lax.fori_loop(0, nch_ref[blk * N_EXPERTS + e], body, 0)


def _block_chunk_count(nch_ref, blk):
    total = nch_ref[blk * N_EXPERTS]
    for e in range(1, N_EXPERTS):
        total = total + nch_ref[blk * N_EXPERTS + e]
    return total


def _gather_body(src_ref, dst_ref, nch_ref, tail_ref, ntail_ref, last_ref, h_ref, rowf_ref, hs_ref, z_ref, sem,
                 *, n_tiles):
    b = pl.program_id(0)
    slot = lax.rem(b, 2)
    tb = h_ref.shape[0]
    loc = rowf_ref[0].astype(jnp.int32)
    r = lax.broadcasted_iota(jnp.int32, (z_ref.shape[1], tb), 0)
    onehot = jnp.where(r == loc[0:1, :], 1.0, jnp.where(r == loc[1:2, :], 1.0, 0.0)).astype(BF16)
    z_ref[slot] = _dot(onehot, h_ref[...]).astype(BF16)

    def copy(buf, src_chunk, dst_chunk):
        return pltpu.make_async_copy(z_ref.at[buf, _chunk_rows(src_chunk)], hs_ref.at[_chunk_rows(dst_chunk)],
                                     sem.at[buf])

    def drain(blk, buf):
        lax.fori_loop(0, _block_chunk_count(nch_ref, blk), lambda c, carry: (copy(buf, 0, 0).wait(), carry)[1], 0)

    @pl.when(b > 0)
    def _():
        drain(b - 1, 1 - slot)

    _for_block_chunks(nch_ref, b, lambda e, c: copy(slot, src_ref[b * N_EXPERTS + e] + c,
                                                    dst_ref[b * N_EXPERTS + e] + c).start())

    @pl.when(b == pl.num_programs(0) - 1)
    def _():
        drain(b, slot)
        z_ref[0, 0:MOE_ALIGN, :] = jnp.zeros((MOE_ALIGN, z_ref.shape[2]), BF16)
        past = last_ref[0]
        total = n_tiles * MOE_CHUNKS_PER_TILE
        for e in range(N_EXPERTS):
            lax.fori_loop(0, ntail_ref[e], lambda c, carry, e=e: (copy(0, 0, tail_ref[e] + c).start(), carry)[1], 0)
        lax.fori_loop(past, total, lambda c, carry: (copy(0, 0, c).start(), carry)[1], 0)
        for e in range(N_EXPERTS):
            lax.fori_loop(0, ntail_ref[e], lambda c, carry: (copy(0, 0, 0).wait(), carry)[1], 0)
        lax.fori_loop(past, total, lambda c, carry: (copy(0, 0, 0).wait(), carry)[1], 0)


def _moe_gather(plan, h, rowf, n_tiles, tb):
    t, d = h.shape
    rows = _moe_block_rows(tb)
    past = (plan["first_tile"][-1:] + plan["tile_count"][-1:]) * MOE_CHUNKS_PER_TILE
    return pl.pallas_call(
        functools.partial(_gather_body, n_tiles=n_tiles),
        grid_spec=pltpu.PrefetchScalarGridSpec(
            num_scalar_prefetch=6,
            grid=(t // tb,),
            in_specs=[
                pl.BlockSpec((tb, d), lambda i, *_: (i, 0)),
                pl.BlockSpec((1, SUBLANES, tb), lambda i, *_: (i, 0, 0)),
            ],
            out_specs=pl.BlockSpec(memory_space=pl.ANY),
            scratch_shapes=[pltpu.VMEM((2, rows, d), BF16), pltpu.SemaphoreType.DMA((2,))],
        ),
        out_shape=jax.ShapeDtypeStruct((n_tiles * MOE_TILE, d), BF16),
        compiler_params=_cparams(("arbitrary",)),
        name="moe_gather",
    )(plan["src"], plan["dst"], plan["chunks"], plan["tail_start"], plan["tail_count"], past, h, rowf)


def _expert_ffn_body(first_ref, count_ref, xs_ref, wg_ref, wu_ref, wd_ref, ys_ref, x_buf, y_buf, in_sem, out_sem,
                     *, n_tiles, halves):
    e = pl.program_id(0)
    first = first_ref[e]
    count = count_ref[e]
    tf = wg_ref.shape[2] // halves

    def tile_rows(tile):
        return pl.ds(pl.multiple_of(tile * MOE_TILE, MOE_TILE), MOE_TILE)

    def load(j, buf):
        return pltpu.make_async_copy(xs_ref.at[tile_rows(first + j)], x_buf.at[buf], in_sem.at[buf])

    def store(tile, buf):
        return pltpu.make_async_copy(y_buf.at[buf], ys_ref.at[tile_rows(tile)], out_sem.at[buf])

    @pl.when(count > 0)
    def _():
        load(0, 0).start()

    def tile_step(j, carry):
        buf = lax.rem(j, 2)

        @pl.when(j + 1 < count)
        def _():
            load(j + 1, 1 - buf).start()

        load(j, buf).wait()
        xs = x_buf[buf]
        y = None
        for f in range(halves):
            a = _dot(xs, wg_ref[0, :, f * tf:(f + 1) * tf])
            u = _dot(xs, wu_ref[0, :, f * tf:(f + 1) * tf])
            part = _dot((_silu(a) * u).astype(BF16), wd_ref[0, f * tf:(f + 1) * tf, :])
            y = part if y is None else y + part

        @pl.when(j >= 2)
        def _():
            store(first, buf).wait()

        y_buf[buf] = y.astype(BF16)
        store(first + j, buf).start()
        return carry

    lax.fori_loop(0, count, tile_step, 0)

    for back in (2, 1):
        @pl.when(count >= back)
        def _(back=back):
            store(first, lax.rem(count - back, 2)).wait()

    @pl.when(e == pl.num_programs(0) - 1)
    def _():
        end = first + count
        y_buf[0] = jnp.zeros(y_buf.shape[1:], BF16)
        lax.fori_loop(end, n_tiles, lambda t, c: (store(t, 0).start(), c)[1], 0)
        lax.fori_loop(end, n_tiles, lambda t, c: (store(t, 0).wait(), c)[1], 0)


def _expert_ffn(first_tile, tile_count, xs, wg, wu, wd):
    rows, d = xs.shape
    dff = wg.shape[2]
    per_expert = lambda shape: pl.BlockSpec((1,) + shape, lambda e, *_: (e, 0, 0))
    return pl.pallas_call(
        functools.partial(_expert_ffn_body, n_tiles=rows // MOE_TILE, halves=2),
        grid_spec=pltpu.PrefetchScalarGridSpec(
            num_scalar_prefetch=2,
            grid=(N_EXPERTS,),
            in_specs=[
                pl.BlockSpec(memory_space=pl.ANY),
                per_expert((d, dff)), per_expert((d, dff)), per_expert((dff, d)),
            ],
            out_specs=pl.BlockSpec(memory_space=pl.ANY),
            scratch_shapes=[pltpu.VMEM((2, MOE_TILE, d), BF16), pltpu.VMEM((2, MOE_TILE, d), BF16),
                            pltpu.SemaphoreType.DMA((2,)), pltpu.SemaphoreType.DMA((2,))],
        ),
        out_shape=jax.ShapeDtypeStruct((rows, d), BF16),
        compiler_params=_cparams(("arbitrary",)),
        name="expert_ffn",
    )(first_tile, tile_count, xs, wg, wu, wd)


def _combine_body(src_ref, dst_ref, nch_ref, x_ref, col_ref, gf_ref, ys_ref, o_ref, y_ref, sem):
    b = pl.program_id(0)
    nb = pl.num_programs(0)
    slot = lax.rem(b, 2)

    def copy(buf, src_chunk, dst_chunk):
        return pltpu.make_async_copy(ys_ref.at[_chunk_rows(dst_chunk)], y_ref.at[buf, _chunk_rows(src_chunk)],
                                     sem.at[buf])

    def fetch(blk, buf):
        _for_block_chunks(nch_ref, blk, lambda e, c: copy(buf, src_ref[blk * N_EXPERTS + e] + c,
                                                          dst_ref[blk * N_EXPERTS + e] + c).start())

    @pl.when(b == 0)
    def _():
        y_ref[...] = jnp.zeros_like(y_ref)
        fetch(0, 0)

    @pl.when(b + 1 < nb)
    def _():
        fetch(b + 1, 1 - slot)

    lax.fori_loop(0, _block_chunk_count(nch_ref, b), lambda c, carry: (copy(slot, 0, 0).wait(), carry)[1], 0)

    col = col_ref[...]
    loc1 = col[:, 0:1].astype(jnp.int32)
    loc2 = col[:, 1:2].astype(jnp.int32)
    r = lax.broadcasted_iota(jnp.int32, (col.shape[0], y_ref.shape[1]), 1)
    weights = jnp.where(r == loc1, col[:, 2:3], jnp.where(r == loc2, col[:, 3:4], 0.0)).astype(BF16)
    o_ref[...] = _rmsnorm_rows(x_ref[...] + _dot(weights, y_ref[slot]), gf_ref[...])


def _moe_combine(src, dst, nch, x, col, g_final, ys, tb):
    t, d = x.shape
    rows = _moe_block_rows(tb)
    return pl.pallas_call(
        _combine_body,
        grid_spec=pltpu.PrefetchScalarGridSpec(
            num_scalar_prefetch=3,
            grid=(t // tb,),
            in_specs=[
                pl.BlockSpec((tb, d), lambda i, *_: (i, 0)),
                pl.BlockSpec((tb, LANES), lambda i, *_: (i, 0)),
                pl.BlockSpec((1, d), lambda i, *_: (0, 0)),
                pl.BlockSpec(memory_space=pl.ANY),
            ],
            out_specs=pl.BlockSpec((tb, d), lambda i, *_: (i, 0)),
            scratch_shapes=[pltpu.VMEM((2, rows, d), BF16), pltpu.SemaphoreType.DMA((2,))],
        ),
        out_shape=jax.ShapeDtypeStruct((t, d), F32),
        compiler_params=_cparams(("arbitrary",)),
        name="moe_combine",
    )(src, dst, nch, x, col, g_final, ys)


def _moe(act, w_out, x_in, g, w, g_final):
    t, _ = x_in.shape
    tb = min(MOE_TB, t)
    nb = t // tb
    x, h, col, rowf, cnt = _route(act, w_out, x_in, g, w["w_router_hi"], w["w_router_lo"], w["b_router"], tb)
    n_tiles = -(-(2 * t + nb * N_EXPERTS * (MOE_ALIGN - 1)) // MOE_TILE) + N_EXPERTS
    plan = _moe_plan(cnt)
    xs = _moe_gather(plan, h, rowf, n_tiles, tb)
    ys = _expert_ffn(plan["first_tile"], plan["tile_count"], xs, w["w_moe_gate"], w["w_moe_up"], w["w_moe_down"])
    return _moe_combine(plan["src"], plan["dst"], plan["chunks"], x, col, g_final, ys, tb)


def _retention_tables(c_real, ct):
    h = np.arange(RET_HEADS, dtype=np.float64)
    log_gamma = np.log1p(-np.exp2(-5.0 - h))
    idx = np.arange(ct, dtype=np.float64)
    live = idx < c_real
    diff = idx[:, None] - idx[None, :]
    causal = (diff >= 0) & live[:, None] & live[None, :]
    decay = np.where(causal[None], np.exp(np.where(causal, diff, 0.0)[None] * log_gamma[:, None, None]), 0.0)
    q_dec = np.where(live[None], np.exp((idx + 1.0)[None] * log_gamma[:, None]), 0.0)
    k_dec = np.where(live[None], np.exp((c_real - 1.0 - idx)[None] * log_gamma[:, None]), 0.0)
    q_dec = np.broadcast_to(q_dec[..., None], (RET_HEADS, ct, LANES))
    k_dec = np.broadcast_to(k_dec[..., None], (RET_HEADS, ct, LANES))
    return (jnp.asarray(decay, F32), jnp.asarray(q_dec, F32), jnp.asarray(k_dec, F32))


def _rope_tables(pos0, length):
    half = HEAD_DIM // 2
    inv = ROPE_BASE ** (-np.arange(half, dtype=np.float64) / half)
    ang = (pos0 + np.arange(length, dtype=np.float64))[:, None] * inv[None, :]
    cos = np.concatenate([np.cos(ang), np.cos(ang)], axis=-1)
    sin = np.concatenate([-np.sin(ang), np.sin(ang)], axis=-1)
    return jnp.asarray(cos, F32), jnp.asarray(sin, F32)


def _retention_body(q_ref, k_ref, v_ref, g_ref, cos_ref, sin_ref, dec_ref, qd_ref, kd_ref, s0_ref, gn_ref,
                    o_ref, s_ref, *, c_real, ct, bb):
    @pl.when(pl.program_id(1) == 0)
    def _():
        s_ref[...] = s0_ref[...]

    cos = cos_ref[...]
    sin = sin_ref[...]
    for g, hd in [(g, hd) for g in range(bb) for hd in range(RET_HEADS)]:
        lo, hi = hd * HEAD_DIM, (hd + 1) * HEAD_DIM
        q = q_ref[g, :, lo:hi].astype(F32)
        k = k_ref[g, :, lo:hi].astype(F32)
        q = (q * cos + pltpu.roll(q, HEAD_DIM // 2, 1) * sin) * (HEAD_DIM ** -0.5)
        k = k * cos + pltpu.roll(k, HEAD_DIM // 2, 1) * sin
        q = _pad_rows(q, ct)
        k = _pad_rows(k, ct)
        vb = _pad_rows(v_ref[g, :, lo:hi].astype(F32), ct).astype(BF16)
        qb = q.astype(BF16)
        q_dec = qd_ref[hd]
        state = s_ref[g, hd]

        s = _dot_nt(qb, k.astype(BF16)) * dec_ref[hd]
        o = _dot(s.astype(BF16), vb) + _dot(qb, state.astype(BF16)) * q_dec
        chunk_dec = q_dec[c_real - 1:c_real, :]
        s_ref[g, hd] = state * chunk_dec + _dot_tn((k * kd_ref[hd]).astype(BF16), vb)

        o = o[:c_real]
        o = o - jnp.mean(o, axis=-1, keepdims=True)
        o = o * lax.rsqrt(jnp.mean(o * o, axis=-1, keepdims=True) + EPS)
        o_ref[g, :, lo:hi] = _silu(g_ref[g, :, lo:hi].astype(F32)) * (o * gn_ref[:, lo:hi])


def _retention(proj, pos0, state, gn, bb):
    b, length, _ = proj.shape
    c_real = math.gcd(length, RET_CHUNK)
    ct = RET_CHUNK
    width = RET_HEADS * HEAD_DIM
    cos, sin = _rope_tables(pos0, length)
    decay, q_dec, k_dec = _retention_tables(c_real, ct)

    def col(i):
        return pl.BlockSpec((bb, c_real, width), lambda bi, c: (bi, c, i))

    full = lambda shape: pl.BlockSpec(shape, lambda bi, c: (0,) * len(shape))
    state_spec = pl.BlockSpec((bb, RET_HEADS, HEAD_DIM, HEAD_DIM), lambda bi, c: (bi, 0, 0, 0))
    return pl.pallas_call(
        functools.partial(_retention_body, c_real=c_real, ct=ct, bb=bb),
        grid=(b // bb, length // c_real),
        in_specs=[
            col(0), col(1), col(2), col(3),
            pl.BlockSpec((c_real, HEAD_DIM), lambda bi, c: (c, 0)),
            pl.BlockSpec((c_real, HEAD_DIM), lambda bi, c: (c, 0)),
            full((RET_HEADS, ct, ct)), full((RET_HEADS, ct, LANES)), full((RET_HEADS, ct, LANES)),
            state_spec,
            full((1, width)),
        ],
        out_specs=[
            pl.BlockSpec((bb, c_real, width), lambda bi, c: (bi, c, 0)),
            state_spec,
        ],
        out_shape=[
            jax.ShapeDtypeStruct((b, length, width), F32),
            jax.ShapeDtypeStruct(state.shape, F32),
        ],
        compiler_params=_cparams(("parallel", "arbitrary")),
        name="retention",
    )(proj, proj, proj, proj, cos, sin, decay, q_dec, k_dec, state, gn)


def _scan_rows(x, op, fill):
    rows = x.shape[0]
    row = lax.broadcasted_iota(jnp.int32, x.shape, 0)
    sh = 1
    while sh < rows:
        x = op(x, jnp.where(row >= sh, pltpu.roll(x, sh, 0), fill))
        sh *= 2
    return x


def _mlstm_body(q_ref, k_ref, v_ref, og_ref, gate_ref, bg_ref, cw_ref, cb_ref, cs_ref, c0_ref, n0_ref, m0_ref,
                gn_ref, o_ref, c_ref, n_ref, m_ref, conv_ref, xc_ref, *, c_real, ct, bb):
    tail = ML_CONV - 1

    @pl.when(pl.program_id(1) == 0)
    def _():
        c_ref[...] = c0_ref[...]
        n_ref[...] = n0_ref[...]
        m_ref[...] = m0_ref[...]
        xc_ref[:, SUBLANES - tail:SUBLANES, :] = cs_ref[...]

    for g in range(bb):
        _mlstm_sequence(q_ref.at[g], k_ref.at[g], v_ref.at[g], og_ref.at[g], gate_ref.at[g], bg_ref, cw_ref, cb_ref,
                        gn_ref, o_ref.at[g], c_ref.at[g], n_ref.at[g], m_ref.at[g], conv_ref.at[g], xc_ref.at[g],
                        c_real=c_real, ct=ct)


def _mlstm_sequence(q_ref, k_ref, v_ref, og_ref, gate_ref, bg_ref, cw_ref, cb_ref, gn_ref, o_ref, c_ref, n_ref, m_ref,
                    conv_ref, xc_ref, *, c_real, ct):
    width = ML_HEADS * HEAD_DIM
    tail = ML_CONV - 1

    xc_ref[SUBLANES:SUBLANES + c_real, :width] = q_ref[...].astype(F32)
    xc_ref[SUBLANES:SUBLANES + c_real, width:] = k_ref[...].astype(F32)
    y = cb_ref[...] + cw_ref[tail:tail + 1, :] * xc_ref[SUBLANES:SUBLANES + c_real, :]
    for j in range(tail):
        y = y + cw_ref[j:j + 1, :] * xc_ref[SUBLANES - tail + j:SUBLANES - tail + j + c_real, :]
    new_tail = xc_ref[c_real:c_real + SUBLANES, :]
    conv_ref[...] = new_tail[SUBLANES - tail:, :]
    xc_ref[0:SUBLANES, :] = new_tail
    qk = _silu(y)
    q_all = _pad_rows(qk[:, :width], ct)
    k_all = _pad_rows(qk[:, width:] * (HEAD_DIM ** -0.5), ct)
    v_all = _pad_rows(v_ref[...].astype(F32), ct)

    gates = _pad_rows(gate_ref[...] + bg_ref[...], ct)
    f_pre = pltpu.roll(gates, LANES - ML_HEADS, 1)
    log_f = jnp.minimum(f_pre, 0.0) - jnp.log(1.0 + jnp.exp(-jnp.abs(f_pre)))
    b_cum = _scan_rows(log_f, jnp.add, 0.0)
    src = gates - b_cum
    cmx = _scan_rows(src, jnp.maximum, NEG_INF)
    m_prev = m_ref[...]
    mx = jnp.maximum(m_prev, cmx)
    m_t = b_cum + mx
    w_prev = jnp.exp(m_prev - mx)
    e_neg_m = jnp.exp(-m_t)
    last = c_real - 1
    b_last = b_cum[last:last + 1, :]
    m_new = m_t[last:last + 1, :]
    row = lax.broadcasted_iota(jnp.int32, (ct, LANES), 0)
    w_end = jnp.where(row < c_real, jnp.exp(b_last + src - m_new), 0.0)
    dec = jnp.exp(b_last + m_prev - m_new)
    m_ref[...] = m_new
    src_t = jnp.transpose(src)

    t_idx = lax.broadcasted_iota(jnp.int32, (ct, ct), 0)
    s_idx = lax.broadcasted_iota(jnp.int32, (ct, ct), 1)
    live = (s_idx <= t_idx) & (s_idx < c_real)

    for hd in range(ML_HEADS):
        lo, hi = hd * HEAD_DIM, (hd + 1) * HEAD_DIM
        q = q_all[:, lo:hi]
        k = k_all[:, lo:hi]
        qb = q.astype(BF16)
        vb = v_all[:, lo:hi].astype(BF16)
        c_state = c_ref[hd]
        n_state = n_ref[hd:hd + 1, :]
        wp = w_prev[:, hd:hd + 1]
        w = jnp.where(live, jnp.exp(src_t[hd:hd + 1, :] - mx[:, hd:hd + 1]), 0.0)
        s = _dot_nt(qb, k.astype(BF16)) * w
        num = _dot(s.astype(BF16), vb) + wp * _dot(qb, c_state.astype(BF16))
        den = jnp.sum(s, axis=-1, keepdims=True) + wp * jnp.sum(q * n_state, axis=-1, keepdims=True)
        hh = num / jnp.maximum(jnp.abs(den), e_neg_m[:, hd:hd + 1])
        kw = k * w_end[:, hd:hd + 1]
        dec_h = dec[:, hd:hd + 1]
        c_ref[hd] = dec_h * c_state + _dot_tn(kw.astype(BF16), vb)
        n_ref[hd:hd + 1, :] = dec_h * n_state + jnp.sum(kw, axis=0, keepdims=True)

        hh = hh[:c_real]
        hh = _sigmoid(og_ref[:, lo:hi].astype(F32)) * hh
        hh = hh - jnp.mean(hh, axis=-1, keepdims=True)
        hh = hh * lax.rsqrt(jnp.mean(hh * hh, axis=-1, keepdims=True) + EPS)
        o_ref[:, lo:hi] = hh * gn_ref[:, lo:hi]


def _mlstm(proj, gates, b_gates, conv_w, conv_b, conv_s, c0, n0, m0, gn, bb):
    b, length, _ = proj.shape
    c_real = math.gcd(length, ML_CHUNK)
    ct = ML_CHUNK
    width = ML_HEADS * HEAD_DIM
    base = (2 * RET_HEADS * HEAD_DIM + 2 * RET_HEADS * HEAD_DIM) // width

    def col(i):
        return pl.BlockSpec((bb, c_real, width), lambda bi, c: (bi, c, base + i))

    full = lambda shape: pl.BlockSpec(shape, lambda bi, c: (0,) * len(shape))
    per_b = lambda shape: pl.BlockSpec((bb,) + shape, lambda bi, c: (bi,) + (0,) * len(shape))
    bg = jnp.zeros((1, LANES), F32).at[0, :2 * ML_HEADS].set(b_gates)
    m0p = jnp.zeros((b, 1, LANES), F32).at[:, 0, :ML_HEADS].set(m0)
    outs = pl.pallas_call(
        functools.partial(_mlstm_body, c_real=c_real, ct=ct, bb=bb),
        grid=(b // bb, length // c_real),
        in_specs=[
            col(0), col(1), col(2), col(3),
            pl.BlockSpec((bb, c_real, LANES), lambda bi, c: (bi, c, 0)),
            full((1, LANES)),
            full((ML_CONV, 2 * width)),
            full((1, 2 * width)),
            per_b((ML_CONV - 1, 2 * width)),
            per_b((ML_HEADS, HEAD_DIM, HEAD_DIM)),
            per_b((ML_HEADS, HEAD_DIM)),
            per_b((1, LANES)),
            full((1, width)),
        ],
        out_specs=[
            pl.BlockSpec((bb, c_real, width), lambda bi, c: (bi, c, 0)),
            per_b((ML_HEADS, HEAD_DIM, HEAD_DIM)),
            per_b((ML_HEADS, HEAD_DIM)),
            per_b((1, LANES)),
            per_b((ML_CONV - 1, 2 * width)),
        ],
        out_shape=[
            jax.ShapeDtypeStruct((b, length, width), F32),
            jax.ShapeDtypeStruct(c0.shape, F32),
            jax.ShapeDtypeStruct(n0.shape, F32),
            jax.ShapeDtypeStruct((b, 1, LANES), F32),
            jax.ShapeDtypeStruct(conv_s.shape, F32),
        ],
        scratch_shapes=[pltpu.VMEM((bb, ct + SUBLANES, 2 * width), F32)],
        compiler_params=_cparams(("parallel", "arbitrary")),
        name="mlstm",
    )(proj, proj, proj, proj, gates, bg, conv_w, conv_b, conv_s, c0, n0, m0p, gn)
    h_m, c_new, n_new, m_new, conv_new = outs
    return h_m, c_new, n_new, m_new[:, 0, :ML_HEADS], conv_new


HG_UNROLL_BLOCK = 8
HG_UNROLL_STEP = 8


def _hgrn_head(cols, q_ref, f_ref, i_ref, g_ref, hl_ref, gn_ref, o_ref, st, *, blk, sub):
    hl = hl_ref[:, cols]
    hmax = jnp.max(hl, axis=0, keepdims=True)
    ex = jnp.exp(hl - hmax)
    p = ex / jnp.sum(ex, axis=0, keepdims=True)
    lower = (p[0:1] + p[1:2]) - p[0:1]
    gn = gn_ref[:, cols]

    f = lower + (1.0 - lower) * _sigmoid(f_ref[:, cols])
    log_k = jnp.log2(jnp.maximum(1.0 - f, 0.0))
    q = _silu(q_ref[:, cols].astype(F32))
    v = i_ref[:, cols].astype(F32)
    gate = _silu(g_ref[:, cols].astype(F32))

    row = lax.broadcasted_iota(jnp.int32, (blk, LANES), 0)
    rsub = row & (sub - 1)
    a = jnp.log2(f)
    sh = 1
    while sh < sub:
        a = a + jnp.where(rsub >= sh, pltpu.roll(a, sh, 0), 0.0)
        sh *= 2
    c = a - log_k
    q_in = q * jnp.exp2(a)

    t_idx = lax.broadcasted_iota(jnp.int32, (SUBLANES, LANES), 0)
    for j in range(blk // sub):
        r0 = j * sub
        a_end = a[r0 + sub - 1:r0 + sub, :]
        inter = _dot_nt(q_in[r0:r0 + sub].astype(BF16), st.astype(BF16))
        k_hat = jnp.exp2(a_end - c[r0:r0 + sub])
        v_j = v[r0:r0 + sub]
        st = st * jnp.exp2(a_end) + _dot_tn(v_j.astype(BF16), k_hat.astype(BF16))
        for part in range(sub // SUBLANES):
            t0 = r0 + part * SUBLANES
            a_t = a[t0:t0 + SUBLANES]
            q_t = q[t0:t0 + SUBLANES]
            o_t = inter[part * SUBLANES:(part + 1) * SUBLANES]
            for s in range((part + 1) * SUBLANES):
                arg = a_t - c[r0 + s:r0 + s + 1, :]
                if s >= part * SUBLANES:
                    arg = jnp.where(t_idx >= s - part * SUBLANES, arg, NEG_INF)
                z = q_t * jnp.exp2(arg)
                o_t = o_t + jnp.sum(z, axis=-1, keepdims=True) * v[r0 + s:r0 + s + 1, :]
            o_t = o_t * lax.rsqrt(jnp.mean(o_t * o_t, axis=-1, keepdims=True) + EPS)
            o_ref[t0:t0 + SUBLANES, cols] = o_t * gn * gate[t0:t0 + SUBLANES]
    return st


def _block_ref_rows(a, level, row8):
    rows = a.shape[0]
    half = level // 2
    if half >= SUBLANES:
        return jnp.concatenate([jnp.broadcast_to(a[b0 + half - 1:b0 + half, :], (level, LANES))
                                for b0 in range(0, rows, level)], axis=0)
    if level == 2:
        return jnp.where((row8 & 1) == 1, pltpu.roll(a, 1, 0), a)
    pieces = []
    for v0 in range(0, rows, SUBLANES):
        picks = [jnp.broadcast_to(a[v0 + b0 + half - 1:v0 + b0 + half, :], (SUBLANES, LANES))
                 for b0 in range(0, SUBLANES, level)]
        piece = picks[-1]
        for i in range(len(picks) - 2, -1, -1):
            piece = jnp.where(row8[:SUBLANES] < (i + 1) * level, picks[i], piece)
        pieces.append(piece)
    return jnp.concatenate(pieces, axis=0)


def _hgrn_head_block(cols, q_ref, f_ref, i_ref, g_ref, hl_ref, gn_ref, o_ref, st, *, blk):
    hl = hl_ref[:, cols]
    hmax = jnp.max(hl, axis=0, keepdims=True)
    ex = jnp.exp(hl - hmax)
    p = ex / jnp.sum(ex, axis=0, keepdims=True)
    lower = (p[0:1] + p[1:2]) - p[0:1]

    f = lower + (1.0 - lower) * _sigmoid(f_ref[:, cols])
    kk = 1.0 - f
    q = _silu(q_ref[:, cols].astype(F32))
    vb = i_ref[:, cols]
    a = _scan_rows(jnp.log2(f), jnp.add, 0.0)
    c = a - jnp.log2(jnp.maximum(kk, 0.0))
    row8 = lax.broadcasted_iota(jnp.int32, (blk, LANES), 0) & (SUBLANES - 1)
    group = (lax.broadcasted_iota(jnp.int32, (blk, blk), 0)
             ^ lax.broadcasted_iota(jnp.int32, (blk, blk), 1))

    scores = None
    level = blk
    while level >= 2:
        half = level // 2
        if half >= SUBLANES:
            zero = jnp.zeros((half, LANES), F32)
            qs, ks = [], []
            for b0 in range(0, blk, level):
                ref = a[b0 + half - 1:b0 + half, :]
                qs += [zero, q[b0 + half:b0 + level] * jnp.exp2(a[b0 + half:b0 + level] - ref)]
                ks += [jnp.exp2(ref - c[b0:b0 + half]), zero]
            q_l = jnp.concatenate(qs, axis=0)
            k_l = jnp.concatenate(ks, axis=0)
        else:
            ref = _block_ref_rows(a, level, row8)
            upper = (row8 & (level - 1)) >= half
            q_l = jnp.where(upper, q * jnp.exp2(a - ref), 0.0)
            k_l = jnp.where(upper, 0.0, jnp.exp2(ref - c))
        r = _dot_nt(q_l.astype(BF16), k_l.astype(BF16))
        scores = r if scores is None else jnp.where(group < level, r, scores)
        level = half
    scores = jnp.where(group < 1, _dot_nt(q.astype(BF16), kk.astype(BF16)), scores)

    a_end = a[blk - 1:blk, :]
    o = _dot_nt((q * jnp.exp2(a)).astype(BF16), st.astype(BF16)) + _dot(scores.astype(BF16), vb)
    o = o * lax.rsqrt(jnp.mean(o * o, axis=-1, keepdims=True) + EPS)
    o_ref[:, cols] = o * gn_ref[:, cols] * _silu(g_ref[:, cols].astype(F32))
    return st * jnp.exp2(a_end) + _dot_tn(vb, jnp.exp2(a_end - c).astype(BF16))


def _hgrn_body(q_ref, f_ref, i_ref, g_ref, hl_ref, s0_ref, gn_ref, o_ref, s_ref, st_ref, *, blk, sub, bb):
    @pl.when(pl.program_id(1) == 0)
    def _():
        for g, h in [(g, h) for g in range(bb) for h in range(HG_HEADS)]:
            st_ref[g, h] = jnp.transpose(s0_ref[g, h])

    if blk == HG_BLOCK:
        head, per_iter = functools.partial(_hgrn_head_block, blk=blk), HG_UNROLL_BLOCK // bb
    else:
        head, per_iter = functools.partial(_hgrn_head, blk=blk, sub=sub), HG_UNROLL_STEP // bb

    def heads(it, carry):
        for g, u in [(g, u) for g in range(bb) for u in range(per_iter)]:
            h = it * per_iter + u
            cols = pl.ds(pl.multiple_of(h * HEAD_DIM, HEAD_DIM), HEAD_DIM)
            st_ref[g, h] = head(cols, q_ref.at[g], f_ref.at[g], i_ref.at[g], g_ref.at[g], hl_ref, gn_ref,
                                o_ref.at[g], st_ref[g, h])
        return carry

    lax.fori_loop(0, HG_HEADS // per_iter, heads, 0)

    @pl.when(pl.program_id(1) == pl.num_programs(1) - 1)
    def _():
        for g, h in [(g, h) for g in range(bb) for h in range(HG_HEADS)]:
            s_ref[g, h] = jnp.transpose(st_ref[g, h])


def _hgrn(proj, f_pre, hg_lower, state, gn, bb):
    b, length, _ = proj.shape
    blk = min(length, HG_BLOCK)
    sub = math.gcd(length, HG_SUB)
    width = HG_HEADS * HEAD_DIM

    def col(i):
        return pl.BlockSpec((bb, blk, width), lambda bi, c: (bi, c, i))

    full = lambda shape: pl.BlockSpec(shape, lambda bi, c: (0,) * len(shape))
    state_spec = pl.BlockSpec((bb, HG_HEADS, HEAD_DIM, HEAD_DIM), lambda bi, c: (bi, 0, 0, 0))
    return pl.pallas_call(
        functools.partial(_hgrn_body, blk=blk, sub=sub, bb=bb),
        grid=(b // bb, length // blk),
        in_specs=[col(0), col(0), col(1), col(2), full(hg_lower.shape), state_spec, full((1, width))],
        out_specs=[pl.BlockSpec((bb, blk, width), lambda bi, c: (bi, c, 0)), state_spec],
        out_shape=[
            jax.ShapeDtypeStruct((b, length, width), F32),
            jax.ShapeDtypeStruct(state.shape, F32),
        ],
        scratch_shapes=[pltpu.VMEM((bb, HG_HEADS, HEAD_DIM, HEAD_DIM), F32)],
        compiler_params=_cparams(("parallel", "arbitrary")),
        name="hgrn2",
    )(proj, f_pre, proj, proj, hg_lower, state, gn)


def _prep_weights(p):
    w_in_ab = p["w_in_ab"][0]
    w_in_c = p["w_in_c"][0]
    width = HG_HEADS * HEAD_DIM
    w_gates = jnp.zeros((D_MODEL, LANES), BF16).at[:, :2 * ML_HEADS].set(w_in_ab[:, AB_MAIN:].astype(BF16))
    w_router = jnp.zeros((D_MODEL, LANES), F32).at[:, :N_EXPERTS].set(p["w_router"][0])
    w_router_hi = w_router.astype(BF16)
    w_router_lo = (w_router - w_router_hi.astype(F32)).astype(BF16)
    b_router = jnp.zeros((1, LANES), F32).at[0, :N_EXPERTS].set(p["b_router"][0])
    return {
        "w_ab": w_in_ab[:, :AB_MAIN].astype(BF16),
        "w_ab_gates": w_gates,
        "w_out_ab": p["w_out_ab"][0].astype(BF16),
        "w_c": jnp.concatenate([w_in_c[:, :width], w_in_c[:, 2 * width:]], axis=1).astype(BF16),
        "w_c_forget": w_in_c[:, width:2 * width].astype(BF16),
        "w_out_c": p["w_out_c"][0].astype(BF16),
        "w_ffn_gate": p["w_ffn_gate"][0].astype(BF16),
        "w_ffn_up": p["w_ffn_up"][0].astype(BF16),
        "w_ffn_down": p["w_ffn_down"][0].astype(BF16),
        "w_router_hi": w_router_hi,
        "w_router_lo": w_router_lo,
        "b_router": b_router,
        "w_moe_gate": p["w_moe_gate"][0].astype(BF16),
        "w_moe_up": p["w_moe_up"][0].astype(BF16),
        "w_moe_down": p["w_moe_down"][0].astype(BF16),
    }


def _trunk(x, pos0, ret_s, mc_s, mn_s, mm_s, conv_s, hg_s, p, w):
    b, length, d = x.shape
    t = b * length
    tm = min(512, t)
    tm_mm = min(1024, t)
    prompt = length >= RET_CHUNK
    bb = 1 if prompt else 4
    assert t % tm_mm == 0 and t % tm == 0 and b % bb == 0 and b % 2 == 0, (b, length)
    row = lambda v: v.reshape(1, -1)
    x0 = x.reshape(t, d)

    proj, gates = _norm_matmul(x0, row(p["ln_mix"][0]), w["w_ab"], w["w_ab_gates"], tm_mm, 1024)
    proj = proj.reshape(b, length, AB_MAIN)
    o_ret, ret_new = _retention(proj, pos0, ret_s, row(p["ret_gn"][0]), 2 if prompt else bb)
    h_m, c_new, n_new, m_new, conv_new = _mlstm(
        proj, gates.reshape(b, length, LANES), p["b_gates_ab"][0], p["conv_w_ab"][0], row(p["conv_b_ab"][0]),
        conv_s, mc_s, mn_s, mm_s, row(p["ml_gn"][0]), bb)
    half = RET_HEADS * HEAD_DIM
    x2 = _ffn([o_ret.reshape(t, half), h_m.reshape(t, half)], [w["w_out_ab"][:half], w["w_out_ab"][half:]], x0,
              row(p["ln_ffn"][0]), w["w_ffn_gate"], w["w_ffn_up"], w["w_ffn_down"], tm, D_FF // 2)

    proj_c, f_pre = _norm_matmul(x2, row(p["ln_mix"][1]), w["w_c"], w["w_c_forget"], tm_mm, 1024)
    o_hg, hg_new = _hgrn(proj_c.reshape(b, length, 3 * d), f_pre.reshape(b, length, d), p["hg_lower"], hg_s,
                         row(p["hg_gn"][0]), bb)
    y = _moe(o_hg.reshape(t, d), w["w_out_c"], x2, row(p["ln_ffn"][1]), w, row(p["ln_final"]))
    return (y.reshape(b, length, d), ret_new[None], c_new[None], n_new[None], m_new[None], conv_new[None],
            hg_new[None])


def kernel(x_prompt, x_sample, state_ret, state_mlstm_c, state_mlstm_n, state_mlstm_m, state_conv, state_hgrn,
           ln_mix, ln_ffn, ln_final, w_in_ab, b_gates_ab, conv_w_ab, conv_b_ab, ret_gn, ml_gn, w_out_ab,
           w_in_c, hg_lower, hg_gn, w_out_c, w_ffn_gate, w_ffn_up, w_ffn_down,
           w_router, b_router, w_moe_gate, w_moe_up, w_moe_down):
    p = {"ln_mix": ln_mix, "ln_ffn": ln_ffn, "ln_final": ln_final,
         "w_in_ab": w_in_ab, "b_gates_ab": b_gates_ab, "conv_w_ab": conv_w_ab, "conv_b_ab": conv_b_ab,
         "ret_gn": ret_gn, "ml_gn": ml_gn, "w_out_ab": w_out_ab,
         "w_in_c": w_in_c, "hg_lower": hg_lower, "hg_gn": hg_gn, "w_out_c": w_out_c,
         "w_ffn_gate": w_ffn_gate, "w_ffn_up": w_ffn_up, "w_ffn_down": w_ffn_down,
         "w_router": w_router, "b_router": b_router,
         "w_moe_gate": w_moe_gate, "w_moe_up": w_moe_up, "w_moe_down": w_moe_down}
    w = _prep_weights(p)
    bp = x_prompt.shape[0]
    zeros = lambda *shape: jnp.zeros(shape, F32)
    prompt = _trunk(
        x_prompt, 0,
        zeros(bp, RET_HEADS, HEAD_DIM, HEAD_DIM), zeros(bp, ML_HEADS, HEAD_DIM, HEAD_DIM),
        zeros(bp, ML_HEADS, HEAD_DIM), jnp.full((bp, ML_HEADS), NEG_INF, F32),
        zeros(bp, ML_CONV - 1, 2 * ML_HEADS * HEAD_DIM), zeros(bp, HG_HEADS, HEAD_DIM, HEAD_DIM), p, w)
    sample = _trunk(
        x_sample, 16384,
        state_ret[0], state_mlstm_c[0], state_mlstm_n[0], state_mlstm_m[0], state_conv[0], state_hgrn[0], p, w)
    out = []
    for a, s in zip(prompt, sample):
        out += [a, s]
    return tuple(out)
```

```python
import functools
import math

import numpy as np
import jax
import jax.numpy as jnp
from jax import lax
from jax.experimental import pallas as pl
from jax.experimental.pallas import tpu as pltpu

F32 = jnp.float32
BF16 = jnp.bfloat16

D_MODEL = 1024
HEAD_DIM = 128
RET_HEADS = 4
ML_HEADS = 4
ML_CONV = 4
HG_HEADS = 8
D_FF = 2816
N_EXPERTS = 8
ROPE_BASE = 10000.0
EPS = 1e-6
RET_CHUNK = 128
ML_CHUNK = 128
HG_BLOCK = 128
HG_SUB = 16

LANES = 128
SUBLANES = 8
AB_MAIN = 4096
VMEM_LIMIT = 56 * 1024 * 1024

NEG_INF = float("-inf")


def _cparams(sem):
    return pltpu.CompilerParams(dimension_semantics=sem, vmem_limit_bytes=VMEM_LIMIT)


def _sigmoid(x):
    return 1.0 / (1.0 + jnp.exp(-x))


def _silu(x):
    return x * _sigmoid(x)


def _rmsnorm_rows(x, g):
    ms = jnp.mean(x * x, axis=-1, keepdims=True)
    return x * lax.rsqrt(ms + EPS) * g


def _dot(a, b):
    return jnp.dot(a, b, preferred_element_type=F32)


def _dot_nt(a, b):
    return lax.dot_general(a, b, (((1,), (1,)), ((), ())), preferred_element_type=F32)


def _dot_tn(a, b):
    return lax.dot_general(a, b, (((0,), (0,)), ((), ())), preferred_element_type=F32)


def _pad_rows(x, rows):
    if x.shape[0] == rows:
        return x
    return jnp.concatenate([x, jnp.zeros((rows - x.shape[0], x.shape[1]), x.dtype)], axis=0)


def _norm_matmul_body(x_ref, g_ref, w_ref, ws_ref, o_ref, os_ref, h_ref):
    @pl.when(pl.program_id(1) == 0)
    def _():
        h = _rmsnorm_rows(x_ref[...], g_ref[...]).astype(BF16)
        h_ref[...] = h
        os_ref[...] = _dot(h, ws_ref[...])

    o_ref[...] = _dot(h_ref[...], w_ref[...]).astype(BF16)


def _norm_matmul(x, g, w, w_side, tm, tn):
    t, d = x.shape
    n, ns = w.shape[1], w_side.shape[1]
    return pl.pallas_call(
        _norm_matmul_body,
        grid=(t // tm, n // tn),
        in_specs=[
            pl.BlockSpec((tm, d), lambda i, j: (i, 0)),
            pl.BlockSpec((1, d), lambda i, j: (0, 0)),
            pl.BlockSpec((d, tn), lambda i, j: (0, j)),
            pl.BlockSpec((d, ns), lambda i, j: (0, 0)),
        ],
        out_specs=[pl.BlockSpec((tm, tn), lambda i, j: (i, j)), pl.BlockSpec((tm, ns), lambda i, j: (i, 0))],
        out_shape=[jax.ShapeDtypeStruct((t, n), BF16), jax.ShapeDtypeStruct((t, ns), F32)],
        scratch_shapes=[pltpu.VMEM((tm, d), BF16)],
        compiler_params=_cparams(("parallel", "arbitrary")),
        name="norm_matmul",
    )(x, g, w, w_side)


def _mixer_residual(x_ref, a_refs, w_refs):
    x = x_ref[...]
    for a_ref, w_ref in zip(a_refs, w_refs):
        x = x + _dot(a_ref[...].astype(BF16), w_ref[...])
    return x


def _ffn_body(*refs, n_in):
    a_refs, w_refs = refs[:n_in], refs[n_in:2 * n_in]
    x_ref, g_ref, wg_ref, wu_ref, wd_ref, o_ref, h_ref, x1_ref = refs[2 * n_in:]
    f = pl.program_id(1)

    @pl.when(f == 0)
    def _():
        x1 = _mixer_residual(x_ref, a_refs, w_refs)
        x1_ref[...] = x1
        h_ref[...] = _rmsnorm_rows(x1, g_ref[...]).astype(BF16)

    h = h_ref[...]
    a = _dot(h, wg_ref[...])
    u = _dot(h, wu_ref[...])
    act = (_silu(a) * u).astype(BF16)
    x1_ref[...] += _dot(act, wd_ref[...])

    @pl.when(f == pl.num_programs(1) - 1)
    def _():
        o_ref[...] = x1_ref[...]


def _ffn(acts, weights, x, g, wg, wu, wd, tm, tf):
    t, d = x.shape
    dff = wg.shape[1]
    n_in = len(acts)
    in_specs = [pl.BlockSpec((tm, a.shape[1]), lambda i, f: (i, 0)) for a in acts]
    in_specs += [pl.BlockSpec(w.shape, lambda i, f: (0, 0)) for w in weights]
    in_specs += [
        pl.BlockSpec((tm, d), lambda i, f: (i, 0)),
        pl.BlockSpec((1, d), lambda i, f: (0, 0)),
        pl.BlockSpec((d, tf), lambda i, f: (0, f)),
        pl.BlockSpec((d, tf), lambda i, f: (0, f)),
        pl.BlockSpec((tf, d), lambda i, f: (f, 0)),
    ]
    return pl.pallas_call(
        functools.partial(_ffn_body, n_in=n_in),
        grid=(t // tm, dff // tf),
        in_specs=in_specs,
        out_specs=pl.BlockSpec((tm, d), lambda i, f: (i, 0)),
        out_shape=jax.ShapeDtypeStruct((t, d), F32),
        scratch_shapes=[pltpu.VMEM((tm, d), BF16), pltpu.VMEM((tm, d), F32)],
        compiler_params=_cparams(("parallel", "arbitrary")),
        name="ffn",
    )(*acts, *weights, x, g, wg, wu, wd)


MOE_TB = 512
MOE_ALIGN = 16
MOE_TILE = 256
MOE_CHUNKS_PER_TILE = MOE_TILE // MOE_ALIGN


def _moe_block_rows(tb):
    return 2 * tb + N_EXPERTS * MOE_ALIGN


def _route_body(a_ref, wo_ref, x_ref, g_ref, whi_ref, wlo_ref, b_ref, tri_ref,
                x3_ref, h_ref, col_ref, rowf_ref, cnt_ref):
    x3 = _mixer_residual(x_ref, [a_ref], [wo_ref])
    x3_ref[...] = x3
    h = _rmsnorm_rows(x3, g_ref[...])
    h_hi = h.astype(BF16)
    h_ref[...] = h_hi
    h_lo = (h - h_hi.astype(F32)).astype(BF16)
    logits = (_dot(h_hi, whi_ref[...]) + _dot(h_hi, wlo_ref[...]) + _dot(h_lo, whi_ref[...])
              + b_ref[...])
    lane = lax.broadcasted_iota(jnp.int32, logits.shape, 1)
    lg = jnp.where(lane < N_EXPERTS, logits, NEG_INF)
    m1 = jnp.max(lg, axis=-1, keepdims=True)
    i1 = jnp.min(jnp.where(lg == m1, lane, LANES), axis=-1, keepdims=True)
    lg2 = jnp.where(lane == i1, NEG_INF, lg)
    m2 = jnp.max(lg2, axis=-1, keepdims=True)
    i2 = jnp.min(jnp.where(lg2 == m2, lane, LANES), axis=-1, keepdims=True)
    e = jnp.exp(m2 - m1)
    w1 = 1.0 / (1.0 + e)
    w2 = e / (1.0 + e)

    sel = jnp.where(lane == i1, 1.0, jnp.where(lane == i2, 1.0, 0.0))
    before = _dot(tri_ref[...], sel.astype(BF16))
    n = jnp.broadcast_to(jnp.sum(sel, axis=0, keepdims=True), (SUBLANES, LANES))
    padded = jnp.floor((n + (MOE_ALIGN - 1.0)) * (1.0 / MOE_ALIGN)) * MOE_ALIGN
    lane8 = lax.broadcasted_iota(jnp.int32, padded.shape, 1)
    incl = padded
    sh = 1
    while sh < N_EXPERTS:
        incl = incl + jnp.where(lane8 >= sh, pltpu.roll(incl, sh, 1), 0.0)
        sh *= 2
    pos = (incl - padded)[0:1, :] + before
    loc1 = jnp.sum(jnp.where(lane == i1, pos, 0.0), axis=-1, keepdims=True)
    loc2 = jnp.sum(jnp.where(lane == i2, pos, 0.0), axis=-1, keepdims=True)
    col = jnp.where(lane == 0, loc1, jnp.where(lane == 1, loc2, jnp.where(lane == 2, w1,
                                                                          jnp.where(lane == 3, w2, 0.0))))
    col_ref[...] = col
    for c in range(col.shape[0] // LANES):
        rowf_ref[0, :, c * LANES:(c + 1) * LANES] = jnp.transpose(col[c * LANES:(c + 1) * LANES, :])[:SUBLANES, :]
    cnt_ref[0] = n


def _route(act, w_out, x, g, w_hi, w_lo, b, tb):
    t, d = x.shape
    nb = t // tb
    tri = jnp.asarray(np.tril(np.ones((tb, tb), np.float32), -1), BF16)
    full = lambda shape: pl.BlockSpec(shape, lambda i: (0,) * len(shape))
    return pl.pallas_call(
        _route_body,
        grid=(nb,),
        in_specs=[
            pl.BlockSpec((tb, act.shape[1]), lambda i: (i, 0)),
            full(w_out.shape),
            pl.BlockSpec((tb, d), lambda i: (i, 0)),
            full((1, d)), full((d, LANES)), full((d, LANES)), full((1, LANES)), full((tb, tb)),
        ],
        out_specs=[
            pl.BlockSpec((tb, d), lambda i: (i, 0)),
            pl.BlockSpec((tb, d), lambda i: (i, 0)),
            pl.BlockSpec((tb, LANES), lambda i: (i, 0)),
            pl.BlockSpec((1, SUBLANES, tb), lambda i: (i, 0, 0)),
            pl.BlockSpec((1, SUBLANES, LANES), lambda i: (i, 0, 0)),
        ],
        out_shape=[
            jax.ShapeDtypeStruct((t, d), F32),
            jax.ShapeDtypeStruct((t, d), BF16),
            jax.ShapeDtypeStruct((t, LANES), F32),
            jax.ShapeDtypeStruct((nb, SUBLANES, tb), F32),
            jax.ShapeDtypeStruct((nb, SUBLANES, LANES), F32),
        ],
        compiler_params=_cparams(("parallel",)),
        name="route",
    )(act, w_out, x, g, w_hi, w_lo, b, tri)


def _moe_plan(cnt):
    n = cnt[:, 0, :N_EXPERTS].astype(jnp.int32)
    chunks = (n + (MOE_ALIGN - 1)) // MOE_ALIGN
    src = jnp.cumsum(chunks, axis=1) - chunks
    seg_chunks = jnp.sum(chunks, axis=0)
    seg_tiles = (seg_chunks + (MOE_CHUNKS_PER_TILE - 1)) // MOE_CHUNKS_PER_TILE
    seg_end = jnp.cumsum(seg_tiles)
    seg_first = (seg_end - seg_tiles) * MOE_CHUNKS_PER_TILE
    dst = seg_first[None, :] + jnp.cumsum(chunks, axis=0) - chunks
    flat = lambda a: a.reshape(-1).astype(jnp.int32)
    return {"src": flat(src), "dst": flat(dst), "chunks": flat(chunks),
            "first_tile": flat(seg_end - seg_tiles), "tile_count": flat(seg_tiles),
            "tail_start": flat(seg_first + seg_chunks), "tail_count": flat(seg_tiles * MOE_CHUNKS_PER_TILE - seg_chunks)}


def _chunk_rows(chunk):
    return pl.ds(pl.multiple_of(chunk * MOE_ALIGN, MOE_ALIGN), MOE_ALIGN)


def _for_block_chunks(nch_ref, blk, fn):
    for e in range(N_EXPERTS):
        def body(c, carry, e=e):
            fn(e, c)
            return carry
        lax.fori_loop(0, nch_ref[blk * N_EXPERTS + e], body, 0)


def _block_chunk_count(nch_ref, blk):
    total = nch_ref[blk * N_EXPERTS]
    for e in range(1, N_EXPERTS):
        total = total + nch_ref[blk * N_EXPERTS + e]
    return total


def _gather_body(src_ref, dst_ref, nch_ref, tail_ref, ntail_ref, last_ref, h_ref, rowf_ref, hs_ref, z_ref, sem,
                 *, n_tiles):
    b = pl.program_id(0)
    slot = lax.rem(b, 2)
    tb = h_ref.shape[0]
    loc = rowf_ref[0].astype(jnp.int32)
    r = lax.broadcasted_iota(jnp.int32, (z_ref.shape[1], tb), 0)
    onehot = jnp.where(r == loc[0:1, :], 1.0, jnp.where(r == loc[1:2, :], 1.0, 0.0)).astype(BF16)
    z_ref[slot] = _dot(onehot, h_ref[...]).astype(BF16)

    def copy(buf, src_chunk, dst_chunk):
        return pltpu.make_async_copy(z_ref.at[buf, _chunk_rows(src_chunk)], hs_ref.at[_chunk_rows(dst_chunk)],
                                     sem.at[buf])

    def drain(blk, buf):
        lax.fori_loop(0, _block_chunk_count(nch_ref, blk), lambda c, carry: (copy(buf, 0, 0).wait(), carry)[1], 0)

    @pl.when(b > 0)
    def _():
        drain(b - 1, 1 - slot)

    _for_block_chunks(nch_ref, b, lambda e, c: copy(slot, src_ref[b * N_EXPERTS + e] + c,
                                                    dst_ref[b * N_EXPERTS + e] + c).start())

    @pl.when(b == pl.num_programs(0) - 1)
    def _():
        drain(b, slot)
        z_ref[0, 0:MOE_ALIGN, :] = jnp.zeros((MOE_ALIGN, z_ref.shape[2]), BF16)
        past = last_ref[0]
        total = n_tiles * MOE_CHUNKS_PER_TILE
        for e in range(N_EXPERTS):
            lax.fori_loop(0, ntail_ref[e], lambda c, carry, e=e: (copy(0, 0, tail_ref[e] + c).start(), carry)[1], 0)
        lax.fori_loop(past, total, lambda c, carry: (copy(0, 0, c).start(), carry)[1], 0)
        for e in range(N_EXPERTS):
            lax.fori_loop(0, ntail_ref[e], lambda c, carry: (copy(0, 0, 0).wait(), carry)[1], 0)
        lax.fori_loop(past, total, lambda c, carry: (copy(0, 0, 0).wait(), carry)[1], 0)


def _moe_gather(plan, h, rowf, n_tiles, tb):
    t, d = h.shape
    rows = _moe_block_rows(tb)
    past = (plan["first_tile"][-1:] + plan["tile_count"][-1:]) * MOE_CHUNKS_PER_TILE
    return pl.pallas_call(
        functools.partial(_gather_body, n_tiles=n_tiles),
        grid_spec=pltpu.PrefetchScalarGridSpec(
            num_scalar_prefetch=6,
            grid=(t // tb,),
            in_specs=[
                pl.BlockSpec((tb, d), lambda i, *_: (i, 0)),
                pl.BlockSpec((1, SUBLANES, tb), lambda i, *_: (i, 0, 0)),
            ],
            out_specs=pl.BlockSpec(memory_space=pl.ANY),
            scratch_shapes=[pltpu.VMEM((2, rows, d), BF16), pltpu.SemaphoreType.DMA((2,))],
        ),
        out_shape=jax.ShapeDtypeStruct((n_tiles * MOE_TILE, d), BF16),
        compiler_params=_cparams(("arbitrary",)),
        name="moe_gather",
    )(plan["src"], plan["dst"], plan["chunks"], plan["tail_start"], plan["tail_count"], past, h, rowf)


def _expert_ffn_body(first_ref, count_ref, xs_ref, wg_ref, wu_ref, wd_ref, ys_ref, x_buf, y_buf, in_sem, out_sem,
                     *, n_tiles, halves):
    e = pl.program_id(0)
    first = first_ref[e]
    count = count_ref[e]
    tf = wg_ref.shape[2] // halves

    def tile_rows(tile):
        return pl.ds(pl.multiple_of(tile * MOE_TILE, MOE_TILE), MOE_TILE)

    def load(j, buf):
        return pltpu.make_async_copy(xs_ref.at[tile_rows(first + j)], x_buf.at[buf], in_sem.at[buf])

    def store(tile, buf):
        return pltpu.make_async_copy(y_buf.at[buf], ys_ref.at[tile_rows(tile)], out_sem.at[buf])

    @pl.when(count > 0)
    def _():
        load(0, 0).start()

    def tile_step(j, carry):
        buf = lax.rem(j, 2)

        @pl.when(j + 1 < count)
        def _():
            load(j + 1, 1 - buf).start()

        load(j, buf).wait()
        xs = x_buf[buf]
        y = None
        for f in range(halves):
            a = _dot(xs, wg_ref[0, :, f * tf:(f + 1) * tf])
            u = _dot(xs, wu_ref[0, :, f * tf:(f + 1) * tf])
            part = _dot((_silu(a) * u).astype(BF16), wd_ref[0, f * tf:(f + 1) * tf, :])
            y = part if y is None else y + part

        @pl.when(j >= 2)
        def _():
            store(first, buf).wait()

        y_buf[buf] = y.astype(BF16)
        store(first + j, buf).start()
        return carry

    lax.fori_loop(0, count, tile_step, 0)

    for back in (2, 1):
        @pl.when(count >= back)
        def _(back=back):
            store(first, lax.rem(count - back, 2)).wait()

    @pl.when(e == pl.num_programs(0) - 1)
    def _():
        end = first + count
        y_buf[0] = jnp.zeros(y_buf.shape[1:], BF16)
        lax.fori_loop(end, n_tiles, lambda t, c: (store(t, 0).start(), c)[1], 0)
        lax.fori_loop(end, n_tiles, lambda t, c: (store(t, 0).wait(), c)[1], 0)


def _expert_ffn(first_tile, tile_count, xs, wg, wu, wd):
    rows, d = xs.shape
    dff = wg.shape[2]
    per_expert = lambda shape: pl.BlockSpec((1,) + shape, lambda e, *_: (e, 0, 0))
    return pl.pallas_call(
        functools.partial(_expert_ffn_body, n_tiles=rows // MOE_TILE, halves=2),
        grid_spec=pltpu.PrefetchScalarGridSpec(
            num_scalar_prefetch=2,
            grid=(N_EXPERTS,),
            in_specs=[
                pl.BlockSpec(memory_space=pl.ANY),
                per_expert((d, dff)), per_expert((d, dff)), per_expert((dff, d)),
            ],
            out_specs=pl.BlockSpec(memory_space=pl.ANY),
            scratch_shapes=[pltpu.VMEM((2, MOE_TILE, d), BF16), pltpu.VMEM((2, MOE_TILE, d), BF16),
                            pltpu.SemaphoreType.DMA((2,)), pltpu.SemaphoreType.DMA((2,))],
        ),
        out_shape=jax.ShapeDtypeStruct((rows, d), BF16),
        compiler_params=_cparams(("arbitrary",)),
        name="expert_ffn",
    )(first_tile, tile_count, xs, wg, wu, wd)


def _combine_body(src_ref, dst_ref, nch_ref, x_ref, col_ref, gf_ref, ys_ref, o_ref, y_ref, sem):
    b = pl.program_id(0)
    nb = pl.num_programs(0)
    slot = lax.rem(b, 2)

    def copy(buf, src_chunk, dst_chunk):
        return pltpu.make_async_copy(ys_ref.at[_chunk_rows(dst_chunk)], y_ref.at[buf, _chunk_rows(src_chunk)],
                                     sem.at[buf])

    def fetch(blk, buf):
        _for_block_chunks(nch_ref, blk, lambda e, c: copy(buf, src_ref[blk * N_EXPERTS + e] + c,
                                                          dst_ref[blk * N_EXPERTS + e] + c).start())

    @pl.when(b == 0)
    def _():
        y_ref[...] = jnp.zeros_like(y_ref)
        fetch(0, 0)

    @pl.when(b + 1 < nb)
    def _():
        fetch(b + 1, 1 - slot)

    lax.fori_loop(0, _block_chunk_count(nch_ref, b), lambda c, carry: (copy(slot, 0, 0).wait(), carry)[1], 0)

    col = col_ref[...]
    loc1 = col[:, 0:1].astype(jnp.int32)
    loc2 = col[:, 1:2].astype(jnp.int32)
    r = lax.broadcasted_iota(jnp.int32, (col.shape[0], y_ref.shape[1]), 1)
    weights = jnp.where(r == loc1, col[:, 2:3], jnp.where(r == loc2, col[:, 3:4], 0.0)).astype(BF16)
    o_ref[...] = _rmsnorm_rows(x_ref[...] + _dot(weights, y_ref[slot]), gf_ref[...])


def _moe_combine(src, dst, nch, x, col, g_final, ys, tb):
    t, d = x.shape
    rows = _moe_block_rows(tb)
    return pl.pallas_call(
        _combine_body,
        grid_spec=pltpu.PrefetchScalarGridSpec(
            num_scalar_prefetch=3,
            grid=(t // tb,),
            in_specs=[
                pl.BlockSpec((tb, d), lambda i, *_: (i, 0)),
                pl.BlockSpec((tb, LANES), lambda i, *_: (i, 0)),
                pl.BlockSpec((1, d), lambda i, *_: (0, 0)),
                pl.BlockSpec(memory_space=pl.ANY),
            ],
            out_specs=pl.BlockSpec((tb, d), lambda i, *_: (i, 0)),
            scratch_shapes=[pltpu.VMEM((2, rows, d), BF16), pltpu.SemaphoreType.DMA((2,))],
        ),
        out_shape=jax.ShapeDtypeStruct((t, d), F32),
        compiler_params=_cparams(("arbitrary",)),
        name="moe_combine",
    )(src, dst, nch, x, col, g_final, ys)


def _moe(act, w_out, x_in, g, w, g_final):
    t, _ = x_in.shape
    tb = min(MOE_TB, t)
    nb = t // tb
    x, h, col, rowf, cnt = _route(act, w_out, x_in, g, w["w_router_hi"], w["w_router_lo"], w["b_router"], tb)
    n_tiles = -(-(2 * t + nb * N_EXPERTS * (MOE_ALIGN - 1)) // MOE_TILE) + N_EXPERTS
    plan = _moe_plan(cnt)
    xs = _moe_gather(plan, h, rowf, n_tiles, tb)
    ys = _expert_ffn(plan["first_tile"], plan["tile_count"], xs, w["w_moe_gate"], w["w_moe_up"], w["w_moe_down"])
    return _moe_combine(plan["src"], plan["dst"], plan["chunks"], x, col, g_final, ys, tb)


def _retention_tables(c_real, ct):
    h = np.arange(RET_HEADS, dtype=np.float64)
    log_gamma = np.log1p(-np.exp2(-5.0 - h))
    idx = np.arange(ct, dtype=np.float64)
    live = idx < c_real
    diff = idx[:, None] - idx[None, :]
    causal = (diff >= 0) & live[:, None] & live[None, :]
    decay = np.where(causal[None], np.exp(np.where(causal, diff, 0.0)[None] * log_gamma[:, None, None]), 0.0)
    q_dec = np.where(live[None], np.exp((idx + 1.0)[None] * log_gamma[:, None]), 0.0)
    k_dec = np.where(live[None], np.exp((c_real - 1.0 - idx)[None] * log_gamma[:, None]), 0.0)
    q_dec = np.broadcast_to(q_dec[..., None], (RET_HEADS, ct, LANES))
    k_dec = np.broadcast_to(k_dec[..., None], (RET_HEADS, ct, LANES))
    return (jnp.asarray(decay, F32), jnp.asarray(q_dec, F32), jnp.asarray(k_dec, F32))


def _rope_tables(pos0, length):
    half = HEAD_DIM // 2
    inv = ROPE_BASE ** (-np.arange(half, dtype=np.float64) / half)
    ang = (pos0 + np.arange(length, dtype=np.float64))[:, None] * inv[None, :]
    cos = np.concatenate([np.cos(ang), np.cos(ang)], axis=-1)
    sin = np.concatenate([-np.sin(ang), np.sin(ang)], axis=-1)
    return jnp.asarray(cos, F32), jnp.asarray(sin, F32)


def _retention_body(q_ref, k_ref, v_ref, g_ref, cos_ref, sin_ref, dec_ref, qd_ref, kd_ref, s0_ref, gn_ref,
                    o_ref, s_ref, *, c_real, ct, bb):
    @pl.when(pl.program_id(1) == 0)
    def _():
        s_ref[...] = s0_ref[...]

    cos = cos_ref[...]
    sin = sin_ref[...]
    for g, hd in [(g, hd) for g in range(bb) for hd in range(RET_HEADS)]:
        lo, hi = hd * HEAD_DIM, (hd + 1) * HEAD_DIM
        q = q_ref[g, :, lo:hi].astype(F32)
        k = k_ref[g, :, lo:hi].astype(F32)
        q = (q * cos + pltpu.roll(q, HEAD_DIM // 2, 1) * sin) * (HEAD_DIM ** -0.5)
        k = k * cos + pltpu.roll(k, HEAD_DIM // 2, 1) * sin
        v = v_ref[g, :, lo:hi].astype(F32)
        qb = q.astype(BF16)
        q_dec = qd_ref[hd, :c_real, :]
        state = s_ref[g, hd]

        s = _dot_nt(qb, _pad_rows(k, ct).astype(BF16)) * dec_ref[hd, :c_real, :]
        o = _dot(s.astype(BF16), _pad_rows(v, ct).astype(BF16)) + _dot(qb, state.astype(BF16)) * q_dec
        chunk_dec = q_dec[c_real - 1:c_real, :]
        s_ref[g, hd] = state * chunk_dec + _dot_tn((k * kd_ref[hd, :c_real, :]).astype(BF16), v.astype(BF16))

        o = o - jnp.mean(o, axis=-1, keepdims=True)
        o = o * lax.rsqrt(jnp.mean(o * o, axis=-1, keepdims=True) + EPS)
        o_ref[g, :, lo:hi] = _silu(g_ref[g, :, lo:hi].astype(F32)) * (o * gn_ref[:, lo:hi])


def _retention(proj, pos0, state, gn, bb):
    b, length, _ = proj.shape
    c_real = math.gcd(length, RET_CHUNK)
    ct = RET_CHUNK
    width = RET_HEADS * HEAD_DIM
    cos, sin = _rope_tables(pos0, length)
    decay, q_dec, k_dec = _retention_tables(c_real, ct)

    def col(i):
        return pl.BlockSpec((bb, c_real, width), lambda bi, c: (bi, c, i))

    full = lambda shape: pl.BlockSpec(shape, lambda bi, c: (0,) * len(shape))
    state_spec = pl.BlockSpec((bb, RET_HEADS, HEAD_DIM, HEAD_DIM), lambda bi, c: (bi, 0, 0, 0))
    return pl.pallas_call(
        functools.partial(_retention_body, c_real=c_real, ct=ct, bb=bb),
        grid=(b // bb, length // c_real),
        in_specs=[
            col(0), col(1), col(2), col(3),
            pl.BlockSpec((c_real, HEAD_DIM), lambda bi, c: (c, 0)),
            pl.BlockSpec((c_real, HEAD_DIM), lambda bi, c: (c, 0)),
            full((RET_HEADS, ct, ct)), full((RET_HEADS, ct, LANES)), full((RET_HEADS, ct, LANES)),
            state_spec,
            full((1, width)),
        ],
        out_specs=[
            pl.BlockSpec((bb, c_real, width), lambda bi, c: (bi, c, 0)),
            state_spec,
        ],
        out_shape=[
            jax.ShapeDtypeStruct((b, length, width), F32),
            jax.ShapeDtypeStruct(state.shape, F32),
        ],
        compiler_params=_cparams(("parallel", "arbitrary")),
        name="retention",
    )(proj, proj, proj, proj, cos, sin, decay, q_dec, k_dec, state, gn)


def _scan_rows(x, op, fill):
    rows = x.shape[0]
    row = lax.broadcasted_iota(jnp.int32, x.shape, 0)
    sh = 1
    while sh < rows:
        x = op(x, jnp.where(row >= sh, pltpu.roll(x, sh, 0), fill))
        sh *= 2
    return x


def _mlstm_body(q_ref, k_ref, v_ref, og_ref, gate_ref, bg_ref, cw_ref, cb_ref, cs_ref, c0_ref, n0_ref, m0_ref,
                gn_ref, o_ref, c_ref, n_ref, m_ref, conv_ref, xc_ref, *, c_real, ct, bb):
    tail = ML_CONV - 1

    @pl.when(pl.program_id(1) == 0)
    def _():
        c_ref[...] = c0_ref[...]
        n_ref[...] = n0_ref[...]
        m_ref[...] = m0_ref[...]
        xc_ref[:, SUBLANES - tail:SUBLANES, :] = cs_ref[...]

    for g in range(bb):
        _mlstm_sequence(q_ref.at[g], k_ref.at[g], v_ref.at[g], og_ref.at[g], gate_ref.at[g], bg_ref, cw_ref, cb_ref,
                        gn_ref, o_ref.at[g], c_ref.at[g], n_ref.at[g], m_ref.at[g], conv_ref.at[g], xc_ref.at[g],
                        c_real=c_real, ct=ct)


def _mlstm_sequence(q_ref, k_ref, v_ref, og_ref, gate_ref, bg_ref, cw_ref, cb_ref, gn_ref, o_ref, c_ref, n_ref, m_ref,
                    conv_ref, xc_ref, *, c_real, ct):
    width = ML_HEADS * HEAD_DIM
    tail = ML_CONV - 1

    xc_ref[SUBLANES:SUBLANES + c_real, :width] = q_ref[...].astype(F32)
    xc_ref[SUBLANES:SUBLANES + c_real, width:] = k_ref[...].astype(F32)
    y = cb_ref[...] + cw_ref[tail:tail + 1, :] * xc_ref[SUBLANES:SUBLANES + c_real, :]
    for j in range(tail):
        y = y + cw_ref[j:j + 1, :] * xc_ref[SUBLANES - tail + j:SUBLANES - tail + j + c_real, :]
    new_tail = xc_ref[c_real:c_real + SUBLANES, :]
    conv_ref[...] = new_tail[SUBLANES - tail:, :]
    xc_ref[0:SUBLANES, :] = new_tail
    qk = _silu(y)
    q_all = qk[:, :width]
    k_all = qk[:, width:] * (HEAD_DIM ** -0.5)
    v_all = v_ref[...].astype(F32)

    gates = gate_ref[...] + bg_ref[...]
    f_pre = pltpu.roll(gates, LANES - ML_HEADS, 1)
    log_f = jnp.minimum(f_pre, 0.0) - jnp.log(1.0 + jnp.exp(-jnp.abs(f_pre)))
    b_cum = _scan_rows(log_f, jnp.add, 0.0)
    src = gates - b_cum
    cmx = _scan_rows(src, jnp.maximum, NEG_INF)
    m_prev = m_ref[...]
    mx = jnp.maximum(m_prev, cmx)
    m_t = b_cum + mx
    w_prev = jnp.exp(m_prev - mx)
    e_neg_m = jnp.exp(-m_t)
    last = c_real - 1
    b_last = b_cum[last:last + 1, :]
    m_new = m_t[last:last + 1, :]
    w_end = jnp.exp(b_last + src - m_new)
    dec = jnp.exp(b_last + m_prev - m_new)
    m_ref[...] = m_new
    src_t = jnp.transpose(_pad_rows(src, ct))

    t_idx = lax.broadcasted_iota(jnp.int32, (c_real, ct), 0)
    s_idx = lax.broadcasted_iota(jnp.int32, (c_real, ct), 1)
    live = (s_idx <= t_idx) & (s_idx < c_real)

    for hd in range(ML_HEADS):
        lo, hi = hd * HEAD_DIM, (hd + 1) * HEAD_DIM
        q = q_all[:, lo:hi]
        k = k_all[:, lo:hi]
        v = v_all[:, lo:hi]
        qb = q.astype(BF16)
        c_state = c_ref[hd]
        n_state = n_ref[hd:hd + 1, :]
        wp = w_prev[:, hd:hd + 1]
        w = jnp.where(live, jnp.exp(src_t[hd:hd + 1, :] - mx[:, hd:hd + 1]), 0.0)
        s = _dot_nt(qb, _pad_rows(k, ct).astype(BF16)) * w
        num = _dot(s.astype(BF16), _pad_rows(v, ct).astype(BF16)) + wp * _dot(qb, c_state.astype(BF16))
        den = jnp.sum(s, axis=-1, keepdims=True) + wp * jnp.sum(q * n_state, axis=-1, keepdims=True)
        hh = num / jnp.maximum(jnp.abs(den), e_neg_m[:, hd:hd + 1])
        kw = k * w_end[:, hd:hd + 1]
        dec_h = dec[:, hd:hd + 1]
        c_ref[hd] = dec_h * c_state + _dot_tn(kw.astype(BF16), v.astype(BF16))
        n_ref[hd:hd + 1, :] = dec_h * n_state + jnp.sum(kw, axis=0, keepdims=True)

        hh = _sigmoid(og_ref[:, lo:hi].astype(F32)) * hh
        hh = hh - jnp.mean(hh, axis=-1, keepdims=True)
        hh = hh * lax.rsqrt(jnp.mean(hh * hh, axis=-1, keepdims=True) + EPS)
        o_ref[:, lo:hi] = hh * gn_ref[:, lo:hi]


def _mlstm(proj, gates, b_gates, conv_w, conv_b, conv_s, c0, n0, m0, gn, bb):
    b, length, _ = proj.shape
    c_real = math.gcd(length, ML_CHUNK)
    ct = ML_CHUNK
    width = ML_HEADS * HEAD_DIM
    base = (2 * RET_HEADS * HEAD_DIM + 2 * RET_HEADS * HEAD_DIM) // width

    def col(i):
        return pl.BlockSpec((bb, c_real, width), lambda bi, c: (bi, c, base + i))

    full = lambda shape: pl.BlockSpec(shape, lambda bi, c: (0,) * len(shape))
    per_b = lambda shape: pl.BlockSpec((bb,) + shape, lambda bi, c: (bi,) + (0,) * len(shape))
    bg = jnp.zeros((1, LANES), F32).at[0, :2 * ML_HEADS].set(b_gates)
    m0p = jnp.zeros((b, 1, LANES), F32).at[:, 0, :ML_HEADS].set(m0)
    outs = pl.pallas_call(
        functools.partial(_mlstm_body, c_real=c_real, ct=ct, bb=bb),
        grid=(b // bb, length // c_real),
        in_specs=[
            col(0), col(1), col(2), col(3),
            pl.BlockSpec((bb, c_real, LANES), lambda bi, c: (bi, c, 0)),
            full((1, LANES)),
            full((ML_CONV, 2 * width)),
            full((1, 2 * width)),
            per_b((ML_CONV - 1, 2 * width)),
            per_b((ML_HEADS, HEAD_DIM, HEAD_DIM)),
            per_b((ML_HEADS, HEAD_DIM)),
            per_b((1, LANES)),
            full((1, width)),
        ],
        out_specs=[
            pl.BlockSpec((bb, c_real, width), lambda bi, c: (bi, c, 0)),
            per_b((ML_HEADS, HEAD_DIM, HEAD_DIM)),
            per_b((ML_HEADS, HEAD_DIM)),
            per_b((1, LANES)),
            per_b((ML_CONV - 1, 2 * width)),
        ],
        out_shape=[
            jax.ShapeDtypeStruct((b, length, width), F32),
            jax.ShapeDtypeStruct(c0.shape, F32),
            jax.ShapeDtypeStruct(n0.shape, F32),
            jax.ShapeDtypeStruct((b, 1, LANES), F32),
            jax.ShapeDtypeStruct(conv_s.shape, F32),
        ],
        scratch_shapes=[pltpu.VMEM((bb, ct + SUBLANES, 2 * width), F32)],
        compiler_params=_cparams(("parallel", "arbitrary")),
        name="mlstm",
    )(proj, proj, proj, proj, gates, bg, conv_w, conv_b, conv_s, c0, n0, m0p, gn)
    h_m, c_new, n_new, m_new, conv_new = outs
    return h_m, c_new, n_new, m_new[:, 0, :ML_HEADS], conv_new


HG_UNROLL_BLOCK = 8
HG_UNROLL_STEP = 8


def _hgrn_head(cols, q_ref, f_ref, i_ref, g_ref, hl_ref, gn_ref, o_ref, st, *, blk, sub):
    hl = hl_ref[:, cols]
    hmax = jnp.max(hl, axis=0, keepdims=True)
    ex = jnp.exp(hl - hmax)
    p = ex / jnp.sum(ex, axis=0, keepdims=True)
    lower = (p[0:1] + p[1:2]) - p[0:1]
    gn = gn_ref[:, cols]

    f = lower + (1.0 - lower) * _sigmoid(f_ref[:, cols])
    log_k = jnp.log2(jnp.maximum(1.0 - f, 0.0))
    q = _silu(q_ref[:, cols].astype(F32))
    v = i_ref[:, cols].astype(F32)
    gate = _silu(g_ref[:, cols].astype(F32))

    row = lax.broadcasted_iota(jnp.int32, (blk, LANES), 0)
    rsub = row & (sub - 1)
    a = jnp.log2(f)
    sh = 1
    while sh < sub:
        a = a + jnp.where(rsub >= sh, pltpu.roll(a, sh, 0), 0.0)
        sh *= 2
    c = a - log_k
    q_in = q * jnp.exp2(a)

    t_idx = lax.broadcasted_iota(jnp.int32, (SUBLANES, LANES), 0)
    for j in range(blk // sub):
        r0 = j * sub
        a_end = a[r0 + sub - 1:r0 + sub, :]
        inter = _dot_nt(q_in[r0:r0 + sub].astype(BF16), st.astype(BF16))
        k_hat = jnp.exp2(a_end - c[r0:r0 + sub])
        v_j = v[r0:r0 + sub]
        st = st * jnp.exp2(a_end) + _dot_tn(v_j.astype(BF16), k_hat.astype(BF16))
        for part in range(sub // SUBLANES):
            t0 = r0 + part * SUBLANES
            a_t = a[t0:t0 + SUBLANES]
            q_t = q[t0:t0 + SUBLANES]
            o_t = inter[part * SUBLANES:(part + 1) * SUBLANES]
            for s in range((part + 1) * SUBLANES):
                arg = a_t - c[r0 + s:r0 + s + 1, :]
                if s >= part * SUBLANES:
                    arg = jnp.where(t_idx >= s - part * SUBLANES, arg, NEG_INF)
                z = q_t * jnp.exp2(arg)
                o_t = o_t + jnp.sum(z, axis=-1, keepdims=True) * v[r0 + s:r0 + s + 1, :]
            o_t = o_t * lax.rsqrt(jnp.mean(o_t * o_t, axis=-1, keepdims=True) + EPS)
            o_ref[t0:t0 + SUBLANES, cols] = o_t * gn * gate[t0:t0 + SUBLANES]
    return st


def _block_ref_rows(a, level, row8):
    rows = a.shape[0]
    half = level // 2
    if half >= SUBLANES:
        return jnp.concatenate([jnp.broadcast_to(a[b0 + half - 1:b0 + half, :], (level, LANES))
                                for b0 in range(0, rows, level)], axis=0)
    if level == 2:
        return jnp.where((row8 & 1) == 1, pltpu.roll(a, 1, 0), a)
    pieces = []
    for v0 in range(0, rows, SUBLANES):
        picks = [jnp.broadcast_to(a[v0 + b0 + half - 1:v0 + b0 + half, :], (SUBLANES, LANES))
                 for b0 in range(0, SUBLANES, level)]
        piece = picks[-1]
        for i in range(len(picks) - 2, -1, -1):
            piece = jnp.where(row8[:SUBLANES] < (i + 1) * level, picks[i], piece)
        pieces.append(piece)
    return jnp.concatenate(pieces, axis=0)


def _hgrn_head_block(cols, q_ref, f_ref, i_ref, g_ref, hl_ref, gn_ref, o_ref, st, *, blk):
    hl = hl_ref[:, cols]
    hmax = jnp.max(hl, axis=0, keepdims=True)
    ex = jnp.exp(hl - hmax)
    p = ex / jnp.sum(ex, axis=0, keepdims=True)
    lower = (p[0:1] + p[1:2]) - p[0:1]

    f = lower + (1.0 - lower) * _sigmoid(f_ref[:, cols])
    kk = 1.0 - f
    q = _silu(q_ref[:, cols].astype(F32))
    vb = i_ref[:, cols]
    a = _scan_rows(jnp.log2(f), jnp.add, 0.0)
    c = a - jnp.log2(jnp.maximum(kk, 0.0))
    row8 = lax.broadcasted_iota(jnp.int32, (blk, LANES), 0) & (SUBLANES - 1)
    group = (lax.broadcasted_iota(jnp.int32, (blk, blk), 0)
             ^ lax.broadcasted_iota(jnp.int32, (blk, blk), 1))

    scores = None
    level = blk
    while level >= 2:
        half = level // 2
        if half >= SUBLANES:
            zero = jnp.zeros((half, LANES), F32)
            qs, ks = [], []
            for b0 in range(0, blk, level):
                ref = a[b0 + half - 1:b0 + half, :]
                qs += [zero, q[b0 + half:b0 + level] * jnp.exp2(a[b0 + half:b0 + level] - ref)]
                ks += [jnp.exp2(ref - c[b0:b0 + half]), zero]
            q_l = jnp.concatenate(qs, axis=0)
            k_l = jnp.concatenate(ks, axis=0)
        else:
            ref = _block_ref_rows(a, level, row8)
            upper = (row8 & (level - 1)) >= half
            q_l = jnp.where(upper, q * jnp.exp2(a - ref), 0.0)
            k_l = jnp.where(upper, 0.0, jnp.exp2(ref - c))
        r = _dot_nt(q_l.astype(BF16), k_l.astype(BF16))
        scores = r if scores is None else jnp.where(group < level, r, scores)
        level = half
    scores = jnp.where(group < 1, _dot_nt(q.astype(BF16), kk.astype(BF16)), scores)

    a_end = a[blk - 1:blk, :]
    o = _dot_nt((q * jnp.exp2(a)).astype(BF16), st.astype(BF16)) + _dot(scores.astype(BF16), vb)
    o = o * lax.rsqrt(jnp.mean(o * o, axis=-1, keepdims=True) + EPS)
    o_ref[:, cols] = o * gn_ref[:, cols] * _silu(g_ref[:, cols].astype(F32))
    return st * jnp.exp2(a_end) + _dot_tn(vb, jnp.exp2(a_end - c).astype(BF16))


def _hgrn_body(q_ref, f_ref, i_ref, g_ref, hl_ref, s0_ref, gn_ref, o_ref, s_ref, st_ref, *, blk, sub, bb):
    @pl.when(pl.program_id(1) == 0)
    def _():
        for g, h in [(g, h) for g in range(bb) for h in range(HG_HEADS)]:
            st_ref[g, h] = jnp.transpose(s0_ref[g, h])

    if blk == HG_BLOCK:
        head, per_iter = functools.partial(_hgrn_head_block, blk=blk), HG_UNROLL_BLOCK // bb
    else:
        head, per_iter = functools.partial(_hgrn_head, blk=blk, sub=sub), HG_UNROLL_STEP // bb

    def heads(it, carry):
        for g, u in [(g, u) for g in range(bb) for u in range(per_iter)]:
            h = it * per_iter + u
            cols = pl.ds(pl.multiple_of(h * HEAD_DIM, HEAD_DIM), HEAD_DIM)
            st_ref[g, h] = head(cols, q_ref.at[g], f_ref.at[g], i_ref.at[g], g_ref.at[g], hl_ref, gn_ref,
                                o_ref.at[g], st_ref[g, h])
        return carry

    lax.fori_loop(0, HG_HEADS // per_iter, heads, 0)

    @pl.when(pl.program_id(1) == pl.num_programs(1) - 1)
    def _():
        for g, h in [(g, h) for g in range(bb) for h in range(HG_HEADS)]:
            s_ref[g, h] = jnp.transpose(st_ref[g, h])


def _hgrn(proj, f_pre, hg_lower, state, gn, bb):
    b, length, _ = proj.shape
    blk = min(length, HG_BLOCK)
    sub = math.gcd(length, HG_SUB)
    width = HG_HEADS * HEAD_DIM

    def col(i):
        return pl.BlockSpec((bb, blk, width), lambda bi, c: (bi, c, i))

    full = lambda shape: pl.BlockSpec(shape, lambda bi, c: (0,) * len(shape))
    state_spec = pl.BlockSpec((bb, HG_HEADS, HEAD_DIM, HEAD_DIM), lambda bi, c: (bi, 0, 0, 0))
    return pl.pallas_call(
        functools.partial(_hgrn_body, blk=blk, sub=sub, bb=bb),
        grid=(b // bb, length // blk),
        in_specs=[col(0), col(0), col(1), col(2), full(hg_lower.shape), state_spec, full((1, width))],
        out_specs=[pl.BlockSpec((bb, blk, width), lambda bi, c: (bi, c, 0)), state_spec],
        out_shape=[
            jax.ShapeDtypeStruct((b, length, width), F32),
            jax.ShapeDtypeStruct(state.shape, F32),
        ],
        scratch_shapes=[pltpu.VMEM((bb, HG_HEADS, HEAD_DIM, HEAD_DIM), F32)],
        compiler_params=_cparams(("parallel", "arbitrary")),
        name="hgrn2",
    )(proj, f_pre, proj, proj, hg_lower, state, gn)


def _prep_weights(p):
    w_in_ab = p["w_in_ab"][0]
    w_in_c = p["w_in_c"][0]
    width = HG_HEADS * HEAD_DIM
    w_gates = jnp.zeros((D_MODEL, LANES), BF16).at[:, :2 * ML_HEADS].set(w_in_ab[:, AB_MAIN:].astype(BF16))
    w_router = jnp.zeros((D_MODEL, LANES), F32).at[:, :N_EXPERTS].set(p["w_router"][0])
    w_router_hi = w_router.astype(BF16)
    w_router_lo = (w_router - w_router_hi.astype(F32)).astype(BF16)
    b_router = jnp.zeros((1, LANES), F32).at[0, :N_EXPERTS].set(p["b_router"][0])
    return {
        "w_ab": w_in_ab[:, :AB_MAIN].astype(BF16),
        "w_ab_gates": w_gates,
        "w_out_ab": p["w_out_ab"][0].astype(BF16),
        "w_c": jnp.concatenate([w_in_c[:, :width], w_in_c[:, 2 * width:]], axis=1).astype(BF16),
        "w_c_forget": w_in_c[:, width:2 * width].astype(BF16),
        "w_out_c": p["w_out_c"][0].astype(BF16),
        "w_ffn_gate": p["w_ffn_gate"][0].astype(BF16),
        "w_ffn_up": p["w_ffn_up"][0].astype(BF16),
        "w_ffn_down": p["w_ffn_down"][0].astype(BF16),
        "w_router_hi": w_router_hi,
        "w_router_lo": w_router_lo,
        "b_router": b_router,
        "w_moe_gate": p["w_moe_gate"][0].astype(BF16),
        "w_moe_up": p["w_moe_up"][0].astype(BF16),
        "w_moe_down": p["w_moe_down"][0].astype(BF16),
    }


def _trunk(x, pos0, ret_s, mc_s, mn_s, mm_s, conv_s, hg_s, p, w):
    b, length, d = x.shape
    t = b * length
    tm = min(512, t)
    tm_mm = min(1024, t)
    prompt = length >= RET_CHUNK
    bb = 1 if prompt else 8
    assert t % tm_mm == 0 and t % tm == 0 and b % bb == 0 and b % 2 == 0, (b, length)
    row = lambda v: v.reshape(1, -1)
    x0 = x.reshape(t, d)

    proj, gates = _norm_matmul(x0, row(p["ln_mix"][0]), w["w_ab"], w["w_ab_gates"], tm_mm, 1024)
    proj = proj.reshape(b, length, AB_MAIN)
    o_ret, ret_new = _retention(proj, pos0, ret_s, row(p["ret_gn"][0]), 2 if prompt else bb)
    h_m, c_new, n_new, m_new, conv_new = _mlstm(
        proj, gates.reshape(b, length, LANES), p["b_gates_ab"][0], p["conv_w_ab"][0], row(p["conv_b_ab"][0]),
        conv_s, mc_s, mn_s, mm_s, row(p["ml_gn"][0]), bb)
    half = RET_HEADS * HEAD_DIM
    x2 = _ffn([o_ret.reshape(t, half), h_m.reshape(t, half)], [w["w_out_ab"][:half], w["w_out_ab"][half:]], x0,
              row(p["ln_ffn"][0]), w["w_ffn_gate"], w["w_ffn_up"], w["w_ffn_down"], tm, D_FF // 2)

    proj_c, f_pre = _norm_matmul(x2, row(p["ln_mix"][1]), w["w_c"], w["w_c_forget"], tm_mm, 1024)
    o_hg, hg_new = _hgrn(proj_c.reshape(b, length, 3 * d), f_pre.reshape(b, length, d), p["hg_lower"], hg_s,
                         row(p["hg_gn"][0]), bb)
    y = _moe(o_hg.reshape(t, d), w["w_out_c"], x2, row(p["ln_ffn"][1]), w, row(p["ln_final"]))
    return (y.reshape(b, length, d), ret_new[None], c_new[None], n_new[None], m_new[None], conv_new[None],
            hg_new[None])


def kernel(x_prompt, x_sample, state_ret, state_mlstm_c, state_mlstm_n, state_mlstm_m, state_conv, state_hgrn,
           ln_mix, ln_ffn, ln_final, w_in_ab, b_gates_ab, conv_w_ab, conv_b_ab, ret_gn, ml_gn, w_out_ab,
           w_in_c, hg_lower, hg_gn, w_out_c, w_ffn_gate, w_ffn_up, w_ffn_down,
           w_router, b_router, w_moe_gate, w_moe_up, w_moe_down):
    p = {"ln_mix": ln_mix, "ln_ffn": ln_ffn, "ln_final": ln_final,
         "w_in_ab": w_in_ab, "b_gates_ab": b_gates_ab, "conv_w_ab": conv_w_ab, "conv_b_ab": conv_b_ab,
         "ret_gn": ret_gn, "ml_gn": ml_gn, "w_out_ab": w_out_ab,
         "w_in_c": w_in_c, "hg_lower": hg_lower, "hg_gn": hg_gn, "w_out_c": w_out_c,
         "w_ffn_gate": w_ffn_gate, "w_ffn_up": w_ffn_up, "w_ffn_down": w_ffn_down,
         "w_router": w_router, "b_router": b_router,
         "w_moe_gate": w_moe_gate, "w_moe_up": w_moe_up, "w_moe_down": w_moe_down}
    w = _prep_weights(p)
    bp = x_prompt.shape[0]
    zeros = lambda *shape: jnp.zeros(shape, F32)
    prompt = _trunk(
        x_prompt, 0,
        zeros(bp, RET_HEADS, HEAD_DIM, HEAD_DIM), zeros(bp, ML_HEADS, HEAD_DIM, HEAD_DIM),
        zeros(bp, ML_HEADS, HEAD_DIM), jnp.full((bp, ML_HEADS), NEG_INF, F32),
        zeros(bp, ML_CONV - 1, 2 * ML_HEADS * HEAD_DIM), zeros(bp, HG_HEADS, HEAD_DIM, HEAD_DIM), p, w)
    sample = _trunk(
        x_sample, 16384,
        state_ret[0], state_mlstm_c[0], state_mlstm_n[0], state_mlstm_m[0], state_conv[0], state_hgrn[0], p, w)
    out = []
    for a, s in zip(prompt, sample):
        out += [a, s]
    return tuple(out)
```

```python
import functools
import math

import numpy as np
import jax
import jax.numpy as jnp
from jax import lax
from jax.experimental import pallas as pl
from jax.experimental.pallas import tpu as pltpu

F32 = jnp.float32
BF16 = jnp.bfloat16

D_MODEL = 1024
HEAD_DIM = 128
RET_HEADS = 4
ML_HEADS = 4
ML_CONV = 4
HG_HEADS = 8
D_FF = 2816
N_EXPERTS = 8
ROPE_BASE = 10000.0
EPS = 1e-6
RET_CHUNK = 128
ML_CHUNK = 128
HG_BLOCK = 128
HG_SUB = 16

LANES = 128
SUBLANES = 8
AB_MAIN = 4096
VMEM_LIMIT = 56 * 1024 * 1024

NEG_INF = float("-inf")


def _cparams(sem):
    return pltpu.CompilerParams(dimension_semantics=sem, vmem_limit_bytes=VMEM_LIMIT)


def _sigmoid(x):
    return 1.0 / (1.0 + jnp.exp(-x))


def _silu(x):
    return x * _sigmoid(x)


def _rmsnorm_rows(x, g):
    ms = jnp.mean(x * x, axis=-1, keepdims=True)
    return x * lax.rsqrt(ms + EPS) * g


def _dot(a, b):
    return jnp.dot(a, b, preferred_element_type=F32)


def _dot_nt(a, b):
    return lax.dot_general(a, b, (((1,), (1,)), ((), ())), preferred_element_type=F32)


def _dot_tn(a, b):
    return lax.dot_general(a, b, (((0,), (0,)), ((), ())), preferred_element_type=F32)


def _pad_rows(x, rows):
    if x.shape[0] == rows:
        return x
    return jnp.concatenate([x, jnp.zeros((rows - x.shape[0], x.shape[1]), x.dtype)], axis=0)


def _norm_matmul_body(x_ref, g_ref, w_ref, ws_ref, o_ref, os_ref, h_ref):
    @pl.when(pl.program_id(1) == 0)
    def _():
        h = _rmsnorm_rows(x_ref[...], g_ref[...]).astype(BF16)
        h_ref[...] = h
        os_ref[...] = _dot(h, ws_ref[...])

    o_ref[...] = _dot(h_ref[...], w_ref[...]).astype(BF16)


def _norm_matmul(x, g, w, w_side, tm, tn):
    t, d = x.shape
    n, ns = w.shape[1], w_side.shape[1]
    return pl.pallas_call(
        _norm_matmul_body,
        grid=(t // tm, n // tn),
        in_specs=[
            pl.BlockSpec((tm, d), lambda i, j: (i, 0)),
            pl.BlockSpec((1, d), lambda i, j: (0, 0)),
            pl.BlockSpec((d, tn), lambda i, j: (0, j)),
            pl.BlockSpec((d, ns), lambda i, j: (0, 0)),
        ],
        out_specs=[pl.BlockSpec((tm, tn), lambda i, j: (i, j)), pl.BlockSpec((tm, ns), lambda i, j: (i, 0))],
        out_shape=[jax.ShapeDtypeStruct((t, n), BF16), jax.ShapeDtypeStruct((t, ns), F32)],
        scratch_shapes=[pltpu.VMEM((tm, d), BF16)],
        compiler_params=_cparams(("parallel", "arbitrary")),
        name="norm_matmul",
    )(x, g, w, w_side)


def _mixer_residual(x_ref, a_refs, w_refs):
    x = x_ref[...]
    for a_ref, w_ref in zip(a_refs, w_refs):
        x = x + _dot(a_ref[...].astype(BF16), w_ref[...])
    return x


def _ffn_body(*refs, n_in):
    a_refs, w_refs = refs[:n_in], refs[n_in:2 * n_in]
    x_ref, g_ref, wg_ref, wu_ref, wd_ref, o_ref, h_ref, x1_ref = refs[2 * n_in:]
    f = pl.program_id(1)

    @pl.when(f == 0)
    def _():
        x1 = _mixer_residual(x_ref, a_refs, w_refs)
        x1_ref[...] = x1
        h_ref[...] = _rmsnorm_rows(x1, g_ref[...]).astype(BF16)

    h = h_ref[...]
    a = _dot(h, wg_ref[...])
    u = _dot(h, wu_ref[...])
    act = (_silu(a) * u).astype(BF16)
    x1_ref[...] += _dot(act, wd_ref[...])

    @pl.when(f == pl.num_programs(1) - 1)
    def _():
        o_ref[...] = x1_ref[...]


def _ffn(acts, weights, x, g, wg, wu, wd, tm, tf):
    t, d = x.shape
    dff = wg.shape[1]
    n_in = len(acts)
    in_specs = [pl.BlockSpec((tm, a.shape[1]), lambda i, f: (i, 0)) for a in acts]
    in_specs += [pl.BlockSpec(w.shape, lambda i, f: (0, 0)) for w in weights]
    in_specs += [
        pl.BlockSpec((tm, d), lambda i, f: (i, 0)),
        pl.BlockSpec((1, d), lambda i, f: (0, 0)),
        pl.BlockSpec((d, tf), lambda i, f: (0, f)),
        pl.BlockSpec((d, tf), lambda i, f: (0, f)),
        pl.BlockSpec((tf, d), lambda i, f: (f, 0)),
    ]
    return pl.pallas_call(
        functools.partial(_ffn_body, n_in=n_in),
        grid=(t // tm, dff // tf),
        in_specs=in_specs,
        out_specs=pl.BlockSpec((tm, d), lambda i, f: (i, 0)),
        out_shape=jax.ShapeDtypeStruct((t, d), F32),
        scratch_shapes=[pltpu.VMEM((tm, d), BF16), pltpu.VMEM((tm, d), F32)],
        compiler_params=_cparams(("parallel", "arbitrary")),
        name="ffn",
    )(*acts, *weights, x, g, wg, wu, wd)


MOE_TB = 512
MOE_ALIGN = 16
MOE_BURST = 4
MOE_TILE = 256
MOE_CHUNKS_PER_TILE = MOE_TILE // MOE_ALIGN


def _moe_block_rows(tb):
    return 2 * tb + N_EXPERTS * MOE_ALIGN


def _route_body(a_ref, wo_ref, x_ref, g_ref, whi_ref, wlo_ref, b_ref, tri_ref,
                x3_ref, h_ref, col_ref, rowf_ref, cnt_ref):
    x3 = _mixer_residual(x_ref, [a_ref], [wo_ref])
    x3_ref[...] = x3
    h = _rmsnorm_rows(x3, g_ref[...])
    h_hi = h.astype(BF16)
    h_ref[...] = h_hi
    h_lo = (h - h_hi.astype(F32)).astype(BF16)

    logits = (_dot_nt(whi_ref[...], h_hi) + _dot_nt(wlo_ref[...], h_hi) + _dot_nt(whi_ref[...], h_lo))
    logits = logits[:N_EXPERTS, :] + b_ref[:, 0:1]
    row = lax.broadcasted_iota(jnp.int32, logits.shape, 0)
    m1 = jnp.max(logits, axis=0, keepdims=True)
    i1 = jnp.min(jnp.where(logits == m1, row, N_EXPERTS), axis=0, keepdims=True)
    lg2 = jnp.where(row == i1, NEG_INF, logits)
    m2 = jnp.max(lg2, axis=0, keepdims=True)
    i2 = jnp.min(jnp.where(lg2 == m2, row, N_EXPERTS), axis=0, keepdims=True)
    e = jnp.exp(m2 - m1)
    w1 = 1.0 / (1.0 + e)
    w2 = e / (1.0 + e)

    sel = jnp.where(row == i1, 1.0, jnp.where(row == i2, 1.0, 0.0))
    before = _dot(sel.astype(BF16), tri_ref[...])
    n = jnp.broadcast_to(jnp.sum(sel, axis=1, keepdims=True), (N_EXPERTS, LANES))
    padded = jnp.floor((n + (MOE_ALIGN - 1.0)) * (1.0 / MOE_ALIGN)) * MOE_ALIGN
    base = _scan_rows(padded, jnp.add, 0.0) - padded
    pos = base[:, 0:1] + before
    loc1 = jnp.sum(jnp.where(row == i1, pos, 0.0), axis=0, keepdims=True)
    loc2 = jnp.sum(jnp.where(row == i2, pos, 0.0), axis=0, keepdims=True)
    rowf = jnp.where(row == 0, loc1, jnp.where(row == 1, loc2, jnp.where(row == 2, w1,
                                                                          jnp.where(row == 3, w2, 0.0))))
    rowf_ref[0] = rowf
    for c in range(rowf.shape[1] // LANES):
        col_ref[c * LANES:(c + 1) * LANES, :] = jnp.transpose(_pad_rows(rowf[:, c * LANES:(c + 1) * LANES], LANES))
    cnt_ref[0] = n


def _route(act, w_out, x, g, w_hi, w_lo, b, tb):
    t, d = x.shape
    nb = t // tb
    tri = jnp.asarray(np.triu(np.ones((tb, tb), np.float32), 1), BF16)
    full = lambda shape: pl.BlockSpec(shape, lambda i: (0,) * len(shape))
    return pl.pallas_call(
        _route_body,
        grid=(nb,),
        in_specs=[
            pl.BlockSpec((tb, act.shape[1]), lambda i: (i, 0)),
            full(w_out.shape),
            pl.BlockSpec((tb, d), lambda i: (i, 0)),
            full((1, d)), full((LANES, d)), full((LANES, d)), full((SUBLANES, LANES)), full((tb, tb)),
        ],
        out_specs=[
            pl.BlockSpec((tb, d), lambda i: (i, 0)),
            pl.BlockSpec((tb, d), lambda i: (i, 0)),
            pl.BlockSpec((tb, LANES), lambda i: (i, 0)),
            pl.BlockSpec((1, SUBLANES, tb), lambda i: (i, 0, 0)),
            pl.BlockSpec((1, SUBLANES, LANES), lambda i: (i, 0, 0)),
        ],
        out_shape=[
            jax.ShapeDtypeStruct((t, d), F32),
            jax.ShapeDtypeStruct((t, d), BF16),
            jax.ShapeDtypeStruct((t, LANES), F32),
            jax.ShapeDtypeStruct((nb, SUBLANES, tb), F32),
            jax.ShapeDtypeStruct((nb, SUBLANES, LANES), F32),
        ],
        compiler_params=_cparams(("parallel",)),
        name="route",
    )(act, w_out, x, g, w_hi, w_lo, b, tri)


def _moe_plan(cnt):
    n = cnt[:, :N_EXPERTS, 0].astype(jnp.int32)
    chunks = (n + (MOE_ALIGN - 1)) // MOE_ALIGN
    src = jnp.cumsum(chunks, axis=1) - chunks
    seg_chunks = jnp.sum(chunks, axis=0)
    seg_tiles = (seg_chunks + (MOE_CHUNKS_PER_TILE - 1)) // MOE_CHUNKS_PER_TILE
    seg_end = jnp.cumsum(seg_tiles)
    seg_first = (seg_end - seg_tiles) * MOE_CHUNKS_PER_TILE
    dst = seg_first[None, :] + jnp.cumsum(chunks, axis=0) - chunks
    flat = lambda a: a.reshape(-1).astype(jnp.int32)
    return {"src": flat(src), "dst": flat(dst), "chunks": flat(chunks),
            "first_tile": flat(seg_end - seg_tiles), "tile_count": flat(seg_tiles),
            "tail_start": flat(seg_first + seg_chunks), "tail_count": flat(seg_tiles * MOE_CHUNKS_PER_TILE - seg_chunks)}


def _chunk_rows(chunk, n=1):
    return pl.ds(pl.multiple_of(chunk * MOE_ALIGN, MOE_ALIGN), n * MOE_ALIGN)


def _for_block_copies(nch_ref, blk, fn):
    for e in range(N_EXPERTS):
        count = nch_ref[blk * N_EXPERTS + e]
        bursts = count // MOE_BURST
        lax.fori_loop(0, bursts, lambda i, carry, e=e: (fn(e, i * MOE_BURST, MOE_BURST), carry)[1], 0)
        lax.fori_loop(bursts * MOE_BURST, count, lambda c, carry, e=e: (fn(e, c, 1), carry)[1], 0)


def _block_copy_counts(nch_ref, blk):
    bursts = singles = 0
    for e in range(N_EXPERTS):
        count = nch_ref[blk * N_EXPERTS + e]
        bursts = bursts + count // MOE_BURST
        singles = singles + lax.rem(count, MOE_BURST)
    return bursts, singles


def _wait_block_copies(nch_ref, blk, copy):
    bursts, singles = _block_copy_counts(nch_ref, blk)
    lax.fori_loop(0, bursts, lambda i, carry: (copy(MOE_BURST).wait(), carry)[1], 0)
    lax.fori_loop(0, singles, lambda i, carry: (copy(1).wait(), carry)[1], 0)


def _gather_body(src_ref, dst_ref, nch_ref, tail_ref, ntail_ref, last_ref, h_ref, rowf_ref, hs_ref, z_ref, sem,
                 *, n_tiles):
    b = pl.program_id(0)
    slot = lax.rem(b, 2)
    tb = h_ref.shape[0]
    loc = rowf_ref[0].astype(jnp.int32)
    r = lax.broadcasted_iota(jnp.int32, (z_ref.shape[1], tb), 0)
    onehot = jnp.where(r == loc[0:1, :], 1.0, jnp.where(r == loc[1:2, :], 1.0, 0.0)).astype(BF16)
    z_ref[slot] = _dot(onehot, h_ref[...]).astype(BF16)

    def copy(buf, src_chunk, dst_chunk, n=1):
        return pltpu.make_async_copy(z_ref.at[buf, _chunk_rows(src_chunk, n)], hs_ref.at[_chunk_rows(dst_chunk, n)],
                                     sem.at[buf])

    def drain(blk, buf):
        _wait_block_copies(nch_ref, blk, lambda n: copy(buf, 0, 0, n))

    @pl.when(b > 0)
    def _():
        drain(b - 1, 1 - slot)

    _for_block_copies(nch_ref, b, lambda e, c, n: copy(slot, src_ref[b * N_EXPERTS + e] + c,
                                                       dst_ref[b * N_EXPERTS + e] + c, n).start())

    @pl.when(b == pl.num_programs(0) - 1)
    def _():
        drain(b, slot)
        z_ref[0, 0:MOE_ALIGN, :] = jnp.zeros((MOE_ALIGN, z_ref.shape[2]), BF16)
        past = last_ref[0]
        total = n_tiles * MOE_CHUNKS_PER_TILE
        for e in range(N_EXPERTS):
            lax.fori_loop(0, ntail_ref[e], lambda c, carry, e=e: (copy(0, 0, tail_ref[e] + c).start(), carry)[1], 0)
        lax.fori_loop(past, total, lambda c, carry: (copy(0, 0, c).start(), carry)[1], 0)
        for e in range(N_EXPERTS):
            lax.fori_loop(0, ntail_ref[e], lambda c, carry: (copy(0, 0, 0).wait(), carry)[1], 0)
        lax.fori_loop(past, total, lambda c, carry: (copy(0, 0, 0).wait(), carry)[1], 0)


def _moe_gather(plan, h, rowf, n_tiles, tb):
    t, d = h.shape
    rows = _moe_block_rows(tb)
    past = (plan["first_tile"][-1:] + plan["tile_count"][-1:]) * MOE_CHUNKS_PER_TILE
    return pl.pallas_call(
        functools.partial(_gather_body, n_tiles=n_tiles),
        grid_spec=pltpu.PrefetchScalarGridSpec(
            num_scalar_prefetch=6,
            grid=(t // tb,),
            in_specs=[
                pl.BlockSpec((tb, d), lambda i, *_: (i, 0)),
                pl.BlockSpec((1, SUBLANES, tb), lambda i, *_: (i, 0, 0)),
            ],
            out_specs=pl.BlockSpec(memory_space=pl.ANY),
            scratch_shapes=[pltpu.VMEM((2, rows, d), BF16), pltpu.SemaphoreType.DMA((2,))],
        ),
        out_shape=jax.ShapeDtypeStruct((n_tiles * MOE_TILE, d), BF16),
        compiler_params=_cparams(("arbitrary",)),
        name="moe_gather",
    )(plan["src"], plan["dst"], plan["chunks"], plan["tail_start"], plan["tail_count"], past, h, rowf)


def _expert_ffn_body(first_ref, count_ref, xs_ref, wg_ref, wu_ref, wd_ref, ys_ref, x_buf, y_buf, in_sem, out_sem,
                     *, n_tiles, halves):
    e = pl.program_id(0)
    first = first_ref[e]
    count = count_ref[e]
    tf = wg_ref.shape[2] // halves

    def tile_rows(tile):
        return pl.ds(pl.multiple_of(tile * MOE_TILE, MOE_TILE), MOE_TILE)

    def load(j, buf):
        return pltpu.make_async_copy(xs_ref.at[tile_rows(first + j)], x_buf.at[buf], in_sem.at[buf])

    def store(tile, buf):
        return pltpu.make_async_copy(y_buf.at[buf], ys_ref.at[tile_rows(tile)], out_sem.at[buf])

    @pl.when(count > 0)
    def _():
        load(0, 0).start()

    def tile_step(j, carry):
        buf = lax.rem(j, 2)

        @pl.when(j + 1 < count)
        def _():
            load(j + 1, 1 - buf).start()

        load(j, buf).wait()
        xs = x_buf[buf]
        y = None
        for f in range(halves):
            a = _dot(xs, wg_ref[0, :, f * tf:(f + 1) * tf])
            u = _dot(xs, wu_ref[0, :, f * tf:(f + 1) * tf])
            part = _dot((_silu(a) * u).astype(BF16), wd_ref[0, f * tf:(f + 1) * tf, :])
            y = part if y is None else y + part

        @pl.when(j >= 2)
        def _():
            store(first, buf).wait()

        y_buf[buf] = y.astype(BF16)
        store(first + j, buf).start()
        return carry

    lax.fori_loop(0, count, tile_step, 0)

    for back in (2, 1):
        @pl.when(count >= back)
        def _(back=back):
            store(first, lax.rem(count - back, 2)).wait()

    @pl.when(e == pl.num_programs(0) - 1)
    def _():
        end = first + count
        y_buf[0] = jnp.zeros(y_buf.shape[1:], BF16)
        lax.fori_loop(end, n_tiles, lambda t, c: (store(t, 0).start(), c)[1], 0)
        lax.fori_loop(end, n_tiles, lambda t, c: (store(t, 0).wait(), c)[1], 0)


def _expert_ffn(first_tile, tile_count, xs, wg, wu, wd):
    rows, d = xs.shape
    dff = wg.shape[2]
    per_expert = lambda shape: pl.BlockSpec((1,) + shape, lambda e, *_: (e, 0, 0))
    return pl.pallas_call(
        functools.partial(_expert_ffn_body, n_tiles=rows // MOE_TILE, halves=2),
        grid_spec=pltpu.PrefetchScalarGridSpec(
            num_scalar_prefetch=2,
            grid=(N_EXPERTS,),
            in_specs=[
                pl.BlockSpec(memory_space=pl.ANY),
                per_expert((d, dff)), per_expert((d, dff)), per_expert((dff, d)),
            ],
            out_specs=pl.BlockSpec(memory_space=pl.ANY),
            scratch_shapes=[pltpu.VMEM((2, MOE_TILE, d), BF16), pltpu.VMEM((2, MOE_TILE, d), BF16),
                            pltpu.SemaphoreType.DMA((2,)), pltpu.SemaphoreType.DMA((2,))],
        ),
        out_shape=jax.ShapeDtypeStruct((rows, d), BF16),
        compiler_params=_cparams(("arbitrary",)),
        name="expert_ffn",
    )(first_tile, tile_count, xs, wg, wu, wd)


def _combine_body(src_ref, dst_ref, nch_ref, x_ref, col_ref, gf_ref, ys_ref, o_ref, y_ref, sem):
    b = pl.program_id(0)
    nb = pl.num_programs(0)
    slot = lax.rem(b, 2)

    def copy(buf, src_chunk, dst_chunk, n=1):
        return pltpu.make_async_copy(ys_ref.at[_chunk_rows(dst_chunk, n)], y_ref.at[buf, _chunk_rows(src_chunk, n)],
                                     sem.at[buf])

    def fetch(blk, buf):
        _for_block_copies(nch_ref, blk, lambda e, c, n: copy(buf, src_ref[blk * N_EXPERTS + e] + c,
                                                             dst_ref[blk * N_EXPERTS + e] + c, n).start())

    @pl.when(b == 0)
    def _():
        y_ref[...] = jnp.zeros_like(y_ref)
        fetch(0, 0)

    @pl.when(b + 1 < nb)
    def _():
        fetch(b + 1, 1 - slot)

    _wait_block_copies(nch_ref, b, lambda n: copy(slot, 0, 0, n))

    col = col_ref[...]
    loc1 = col[:, 0:1].astype(jnp.int32)
    loc2 = col[:, 1:2].astype(jnp.int32)
    r = lax.broadcasted_iota(jnp.int32, (col.shape[0], y_ref.shape[1]), 1)
    weights = jnp.where(r == loc1, col[:, 2:3], jnp.where(r == loc2, col[:, 3:4], 0.0)).astype(BF16)
    o_ref[...] = _rmsnorm_rows(x_ref[...] + _dot(weights, y_ref[slot]), gf_ref[...])


def _moe_combine(src, dst, nch, x, col, g_final, ys, tb):
    t, d = x.shape
    rows = _moe_block_rows(tb)
    return pl.pallas_call(
        _combine_body,
        grid_spec=pltpu.PrefetchScalarGridSpec(
            num_scalar_prefetch=3,
            grid=(t // tb,),
            in_specs=[
                pl.BlockSpec((tb, d), lambda i, *_: (i, 0)),
                pl.BlockSpec((tb, LANES), lambda i, *_: (i, 0)),
                pl.BlockSpec((1, d), lambda i, *_: (0, 0)),
                pl.BlockSpec(memory_space=pl.ANY),
            ],
            out_specs=pl.BlockSpec((tb, d), lambda i, *_: (i, 0)),
            scratch_shapes=[pltpu.VMEM((2, rows, d), BF16), pltpu.SemaphoreType.DMA((2,))],
        ),
        out_shape=jax.ShapeDtypeStruct((t, d), F32),
        compiler_params=_cparams(("arbitrary",)),
        name="moe_combine",
    )(src, dst, nch, x, col, g_final, ys)


def _moe(act, w_out, x_in, g, w, g_final):
    t, _ = x_in.shape
    tb = min(MOE_TB, t)
    nb = t // tb
    x, h, col, rowf, cnt = _route(act, w_out, x_in, g, w["w_router_hi"], w["w_router_lo"], w["b_router"], tb)
    n_tiles = -(-(2 * t + nb * N_EXPERTS * (MOE_ALIGN - 1)) // MOE_TILE) + N_EXPERTS
    plan = _moe_plan(cnt)
    xs = _moe_gather(plan, h, rowf, n_tiles, tb)
    ys = _expert_ffn(plan["first_tile"], plan["tile_count"], xs, w["w_moe_gate"], w["w_moe_up"], w["w_moe_down"])
    return _moe_combine(plan["src"], plan["dst"], plan["chunks"], x, col, g_final, ys, tb)


def _retention_tables(c_real, ct):
    h = np.arange(RET_HEADS, dtype=np.float64)
    log_gamma = np.log1p(-np.exp2(-5.0 - h))
    idx = np.arange(ct, dtype=np.float64)
    live = idx < c_real
    diff = idx[:, None] - idx[None, :]
    causal = (diff >= 0) & live[:, None] & live[None, :]
    decay = np.where(causal[None], np.exp(np.where(causal, diff, 0.0)[None] * log_gamma[:, None, None]), 0.0)
    q_dec = np.where(live[None], np.exp((idx + 1.0)[None] * log_gamma[:, None]), 0.0)
    k_dec = np.where(live[None], np.exp((c_real - 1.0 - idx)[None] * log_gamma[:, None]), 0.0)
    q_dec = np.broadcast_to(q_dec[..., None], (RET_HEADS, ct, LANES))
    k_dec = np.broadcast_to(k_dec[..., None], (RET_HEADS, ct, LANES))
    return (jnp.asarray(decay, F32), jnp.asarray(q_dec, F32), jnp.asarray(k_dec, F32))


def _rope_tables(pos0, length):
    half = HEAD_DIM // 2
    inv = ROPE_BASE ** (-np.arange(half, dtype=np.float64) / half)
    ang = (pos0 + np.arange(length, dtype=np.float64))[:, None] * inv[None, :]
    cos = np.concatenate([np.cos(ang), np.cos(ang)], axis=-1)
    sin = np.concatenate([-np.sin(ang), np.sin(ang)], axis=-1)
    return jnp.asarray(cos, F32), jnp.asarray(sin, F32)


def _retention_body(q_ref, k_ref, v_ref, g_ref, cos_ref, sin_ref, dec_ref, qd_ref, kd_ref, s0_ref, gn_ref,
                    o_ref, s_ref, *, c_real, ct, bb):
    @pl.when(pl.program_id(1) == 0)
    def _():
        s_ref[...] = s0_ref[...]

    cos = cos_ref[...]
    sin = sin_ref[...]
    for g, hd in [(g, hd) for g in range(bb) for hd in range(RET_HEADS)]:
        lo, hi = hd * HEAD_DIM, (hd + 1) * HEAD_DIM
        q = q_ref[g, :, lo:hi].astype(F32)
        k = k_ref[g, :, lo:hi].astype(F32)
        q = (q * cos + pltpu.roll(q, HEAD_DIM // 2, 1) * sin) * (HEAD_DIM ** -0.5)
        k = k * cos + pltpu.roll(k, HEAD_DIM // 2, 1) * sin
        v = v_ref[g, :, lo:hi].astype(F32)
        qb = q.astype(BF16)
        q_dec = qd_ref[hd, :c_real, :]
        state = s_ref[g, hd]

        s = _dot_nt(qb, _pad_rows(k, ct).astype(BF16)) * dec_ref[hd, :c_real, :]
        o = _dot(s.astype(BF16), _pad_rows(v, ct).astype(BF16)) + _dot(qb, state.astype(BF16)) * q_dec
        chunk_dec = q_dec[c_real - 1:c_real, :]
        s_ref[g, hd] = state * chunk_dec + _dot_tn((k * kd_ref[hd, :c_real, :]).astype(BF16), v.astype(BF16))

        o = o - jnp.mean(o, axis=-1, keepdims=True)
        o = o * lax.rsqrt(jnp.mean(o * o, axis=-1, keepdims=True) + EPS)
        o_ref[g, :, lo:hi] = _silu(g_ref[g, :, lo:hi].astype(F32)) * (o * gn_ref[:, lo:hi])


def _retention(proj, pos0, state, gn, bb):
    b, length, _ = proj.shape
    c_real = math.gcd(length, RET_CHUNK)
    ct = RET_CHUNK
    width = RET_HEADS * HEAD_DIM
    cos, sin = _rope_tables(pos0, length)
    decay, q_dec, k_dec = _retention_tables(c_real, ct)

    def col(i):
        return pl.BlockSpec((bb, c_real, width), lambda bi, c: (bi, c, i))

    full = lambda shape: pl.BlockSpec(shape, lambda bi, c: (0,) * len(shape))
    state_spec = pl.BlockSpec((bb, RET_HEADS, HEAD_DIM, HEAD_DIM), lambda bi, c: (bi, 0, 0, 0))
    return pl.pallas_call(
        functools.partial(_retention_body, c_real=c_real, ct=ct, bb=bb),
        grid=(b // bb, length // c_real),
        in_specs=[
            col(0), col(1), col(2), col(3),
            pl.BlockSpec((c_real, HEAD_DIM), lambda bi, c: (c, 0)),
            pl.BlockSpec((c_real, HEAD_DIM), lambda bi, c: (c, 0)),
            full((RET_HEADS, ct, ct)), full((RET_HEADS, ct, LANES)), full((RET_HEADS, ct, LANES)),
            state_spec,
            full((1, width)),
        ],
        out_specs=[
            pl.BlockSpec((bb, c_real, width), lambda bi, c: (bi, c, 0)),
            state_spec,
        ],
        out_shape=[
            jax.ShapeDtypeStruct((b, length, width), F32),
            jax.ShapeDtypeStruct(state.shape, F32),
        ],
        compiler_params=_cparams(("parallel", "arbitrary")),
        name="retention",
    )(proj, proj, proj, proj, cos, sin, decay, q_dec, k_dec, state, gn)


def _scan_rows(x, op, fill):
    rows = x.shape[0]
    row = lax.broadcasted_iota(jnp.int32, x.shape, 0)
    sh = 1
    while sh < rows:
        x = op(x, jnp.where(row >= sh, pltpu.roll(x, sh, 0), fill))
        sh *= 2
    return x


def _mlstm_body(q_ref, k_ref, v_ref, og_ref, gate_ref, bg_ref, cw_ref, cb_ref, cs_ref, c0_ref, n0_ref, m0_ref,
                gn_ref, o_ref, c_ref, n_ref, m_ref, conv_ref, xc_ref, *, c_real, ct, bb):
    tail = ML_CONV - 1

    @pl.when(pl.program_id(1) == 0)
    def _():
        c_ref[...] = c0_ref[...]
        n_ref[...] = n0_ref[...]
        m_ref[...] = m0_ref[...]
        xc_ref[:, SUBLANES - tail:SUBLANES, :] = cs_ref[...]

    for g in range(bb):
        _mlstm_sequence(q_ref.at[g], k_ref.at[g], v_ref.at[g], og_ref.at[g], gate_ref.at[g], bg_ref, cw_ref, cb_ref,
                        gn_ref, o_ref.at[g], c_ref.at[g], n_ref.at[g], m_ref.at[g], conv_ref.at[g], xc_ref.at[g],
                        c_real=c_real, ct=ct)


def _mlstm_sequence(q_ref, k_ref, v_ref, og_ref, gate_ref, bg_ref, cw_ref, cb_ref, gn_ref, o_ref, c_ref, n_ref, m_ref,
                    conv_ref, xc_ref, *, c_real, ct):
    width = ML_HEADS * HEAD_DIM
    tail = ML_CONV - 1

    xc_ref[SUBLANES:SUBLANES + c_real, :width] = q_ref[...].astype(F32)
    xc_ref[SUBLANES:SUBLANES + c_real, width:] = k_ref[...].astype(F32)
    y = cb_ref[...] + cw_ref[tail:tail + 1, :] * xc_ref[SUBLANES:SUBLANES + c_real, :]
    for j in range(tail):
        y = y + cw_ref[j:j + 1, :] * xc_ref[SUBLANES - tail + j:SUBLANES - tail + j + c_real, :]
    new_tail = xc_ref[c_real:c_real + SUBLANES, :]
    conv_ref[...] = new_tail[SUBLANES - tail:, :]
    xc_ref[0:SUBLANES, :] = new_tail
    qk = _silu(y)
    q_all = qk[:, :width]
    k_all = qk[:, width:] * (HEAD_DIM ** -0.5)
    v_all = v_ref[...].astype(F32)

    gates = gate_ref[...] + bg_ref[...]
    f_pre = pltpu.roll(gates, LANES - ML_HEADS, 1)
    log_f = jnp.minimum(f_pre, 0.0) - jnp.log(1.0 + jnp.exp(-jnp.abs(f_pre)))
    b_cum = _scan_rows(log_f, jnp.add, 0.0)
    src = gates - b_cum
    cmx = _scan_rows(src, jnp.maximum, NEG_INF)
    m_prev = m_ref[...]
    mx = jnp.maximum(m_prev, cmx)
    m_t = b_cum + mx
    w_prev = jnp.exp(m_prev - mx)
    e_neg_m = jnp.exp(-m_t)
    last = c_real - 1
    b_last = b_cum[last:last + 1, :]
    m_new = m_t[last:last + 1, :]
    w_end = jnp.exp(b_last + src - m_new)
    dec = jnp.exp(b_last + m_prev - m_new)
    m_ref[...] = m_new
    src_t = jnp.transpose(_pad_rows(src, ct))

    t_idx = lax.broadcasted_iota(jnp.int32, (c_real, ct), 0)
    s_idx = lax.broadcasted_iota(jnp.int32, (c_real, ct), 1)
    live = (s_idx <= t_idx) & (s_idx < c_real)

    for hd in range(ML_HEADS):
        lo, hi = hd * HEAD_DIM, (hd + 1) * HEAD_DIM
        q = q_all[:, lo:hi]
        k = k_all[:, lo:hi]
        v = v_all[:, lo:hi]
        qb = q.astype(BF16)
        c_state = c_ref[hd]
        n_state = n_ref[hd:hd + 1, :]
        wp = w_prev[:, hd:hd + 1]
        w = jnp.where(live, jnp.exp(src_t[hd:hd + 1, :] - mx[:, hd:hd + 1]), 0.0)
        s = _dot_nt(qb, _pad_rows(k, ct).astype(BF16)) * w
        num = _dot(s.astype(BF16), _pad_rows(v, ct).astype(BF16)) + wp * _dot(qb, c_state.astype(BF16))
        den = jnp.sum(s, axis=-1, keepdims=True) + wp * jnp.sum(q * n_state, axis=-1, keepdims=True)
        hh = num / jnp.maximum(jnp.abs(den), e_neg_m[:, hd:hd + 1])
        kw = k * w_end[:, hd:hd + 1]
        dec_h = dec[:, hd:hd + 1]
        c_ref[hd] = dec_h * c_state + _dot_tn(kw.astype(BF16), v.astype(BF16))
        n_ref[hd:hd + 1, :] = dec_h * n_state + jnp.sum(kw, axis=0, keepdims=True)

        hh = _sigmoid(og_ref[:, lo:hi].astype(F32)) * hh
        hh = hh - jnp.mean(hh, axis=-1, keepdims=True)
        hh = hh * lax.rsqrt(jnp.mean(hh * hh, axis=-1, keepdims=True) + EPS)
        o_ref[:, lo:hi] = hh * gn_ref[:, lo:hi]


def _mlstm(proj, gates, b_gates, conv_w, conv_b, conv_s, c0, n0, m0, gn, bb):
    b, length, _ = proj.shape
    c_real = math.gcd(length, ML_CHUNK)
    ct = ML_CHUNK
    width = ML_HEADS * HEAD_DIM
    base = (2 * RET_HEADS * HEAD_DIM + 2 * RET_HEADS * HEAD_DIM) // width

    def col(i):
        return pl.BlockSpec((bb, c_real, width), lambda bi, c: (bi, c, base + i))

    full = lambda shape: pl.BlockSpec(shape, lambda bi, c: (0,) * len(shape))
    per_b = lambda shape: pl.BlockSpec((bb,) + shape, lambda bi, c: (bi,) + (0,) * len(shape))
    bg = jnp.zeros((1, LANES), F32).at[0, :2 * ML_HEADS].set(b_gates)
    m0p = jnp.zeros((b, 1, LANES), F32).at[:, 0, :ML_HEADS].set(m0)
    outs = pl.pallas_call(
        functools.partial(_mlstm_body, c_real=c_real, ct=ct, bb=bb),
        grid=(b // bb, length // c_real),
        in_specs=[
            col(0), col(1), col(2), col(3),
            pl.BlockSpec((bb, c_real, LANES), lambda bi, c: (bi, c, 0)),
            full((1, LANES)),
            full((ML_CONV, 2 * width)),
            full((1, 2 * width)),
            per_b((ML_CONV - 1, 2 * width)),
            per_b((ML_HEADS, HEAD_DIM, HEAD_DIM)),
            per_b((ML_HEADS, HEAD_DIM)),
            per_b((1, LANES)),
            full((1, width)),
        ],
        out_specs=[
            pl.BlockSpec((bb, c_real, width), lambda bi, c: (bi, c, 0)),
            per_b((ML_HEADS, HEAD_DIM, HEAD_DIM)),
            per_b((ML_HEADS, HEAD_DIM)),
            per_b((1, LANES)),
            per_b((ML_CONV - 1, 2 * width)),
        ],
        out_shape=[
            jax.ShapeDtypeStruct((b, length, width), F32),
            jax.ShapeDtypeStruct(c0.shape, F32),
            jax.ShapeDtypeStruct(n0.shape, F32),
            jax.ShapeDtypeStruct((b, 1, LANES), F32),
            jax.ShapeDtypeStruct(conv_s.shape, F32),
        ],
        scratch_shapes=[pltpu.VMEM((bb, ct + SUBLANES, 2 * width), F32)],
        compiler_params=_cparams(("parallel", "arbitrary")),
        name="mlstm",
    )(proj, proj, proj, proj, gates, bg, conv_w, conv_b, conv_s, c0, n0, m0p, gn)
    h_m, c_new, n_new, m_new, conv_new = outs
    return h_m, c_new, n_new, m_new[:, 0, :ML_HEADS], conv_new


HG_UNROLL_BLOCK = 8
HG_UNROLL_STEP = 8


def _hgrn_head(cols, q_ref, f_ref, i_ref, g_ref, hl_ref, gn_ref, o_ref, st, *, blk, sub):
    hl = hl_ref[:, cols]
    hmax = jnp.max(hl, axis=0, keepdims=True)
    ex = jnp.exp(hl - hmax)
    p = ex / jnp.sum(ex, axis=0, keepdims=True)
    lower = (p[0:1] + p[1:2]) - p[0:1]
    gn = gn_ref[:, cols]

    f = lower + (1.0 - lower) * _sigmoid(f_ref[:, cols])
    log_k = jnp.log2(jnp.maximum(1.0 - f, 0.0))
    q = _silu(q_ref[:, cols].astype(F32))
    v = i_ref[:, cols].astype(F32)
    gate = _silu(g_ref[:, cols].astype(F32))

    row = lax.broadcasted_iota(jnp.int32, (blk, LANES), 0)
    rsub = row & (sub - 1)
    a = jnp.log2(f)
    sh = 1
    while sh < sub:
        a = a + jnp.where(rsub >= sh, pltpu.roll(a, sh, 0), 0.0)
        sh *= 2
    c = a - log_k
    q_in = q * jnp.exp2(a)

    t_idx = lax.broadcasted_iota(jnp.int32, (SUBLANES, LANES), 0)
    for j in range(blk // sub):
        r0 = j * sub
        a_end = a[r0 + sub - 1:r0 + sub, :]
        inter = _dot_nt(q_in[r0:r0 + sub].astype(BF16), st.astype(BF16))
        k_hat = jnp.exp2(a_end - c[r0:r0 + sub])
        v_j = v[r0:r0 + sub]
        st = st * jnp.exp2(a_end) + _dot_tn(v_j.astype(BF16), k_hat.astype(BF16))
        for part in range(sub // SUBLANES):
            t0 = r0 + part * SUBLANES
            a_t = a[t0:t0 + SUBLANES]
            q_t = q[t0:t0 + SUBLANES]
            o_t = inter[part * SUBLANES:(part + 1) * SUBLANES]
            for s in range((part + 1) * SUBLANES):
                arg = a_t - c[r0 + s:r0 + s + 1, :]
                if s >= part * SUBLANES:
                    arg = jnp.where(t_idx >= s - part * SUBLANES, arg, NEG_INF)
                z = q_t * jnp.exp2(arg)
                o_t = o_t + jnp.sum(z, axis=-1, keepdims=True) * v[r0 + s:r0 + s + 1, :]
            o_t = o_t * lax.rsqrt(jnp.mean(o_t * o_t, axis=-1, keepdims=True) + EPS)
            o_ref[t0:t0 + SUBLANES, cols] = o_t * gn * gate[t0:t0 + SUBLANES]
    return st


def _block_ref_rows(a, level, row8):
    rows = a.shape[0]
    half = level // 2
    if half >= SUBLANES:
        return jnp.concatenate([jnp.broadcast_to(a[b0 + half - 1:b0 + half, :], (level, LANES))
                                for b0 in range(0, rows, level)], axis=0)
    if level == 2:
        return jnp.where((row8 & 1) == 1, pltpu.roll(a, 1, 0), a)
    pieces = []
    for v0 in range(0, rows, SUBLANES):
        picks = [jnp.broadcast_to(a[v0 + b0 + half - 1:v0 + b0 + half, :], (SUBLANES, LANES))
                 for b0 in range(0, SUBLANES, level)]
        piece = picks[-1]
        for i in range(len(picks) - 2, -1, -1):
            piece = jnp.where(row8[:SUBLANES] < (i + 1) * level, picks[i], piece)
        pieces.append(piece)
    return jnp.concatenate(pieces, axis=0)


def _hgrn_head_block(cols, q_ref, f_ref, i_ref, g_ref, hl_ref, gn_ref, o_ref, st, *, blk):
    hl = hl_ref[:, cols]
    hmax = jnp.max(hl, axis=0, keepdims=True)
    ex = jnp.exp(hl - hmax)
    p = ex / jnp.sum(ex, axis=0, keepdims=True)
    lower = (p[0:1] + p[1:2]) - p[0:1]

    f = lower + (1.0 - lower) * _sigmoid(f_ref[:, cols])
    kk = 1.0 - f
    q = _silu(q_ref[:, cols].astype(F32))
    vb = i_ref[:, cols]
    a = _scan_rows(jnp.log2(f), jnp.add, 0.0)
    c = a - jnp.log2(jnp.maximum(kk, 0.0))
    row8 = lax.broadcasted_iota(jnp.int32, (blk, LANES), 0) & (SUBLANES - 1)
    group = (lax.broadcasted_iota(jnp.int32, (blk, blk), 0)
             ^ lax.broadcasted_iota(jnp.int32, (blk, blk), 1))

    scores = None
    level = blk
    while level >= 2:
        half = level // 2
        if half >= SUBLANES:
            zero = jnp.zeros((half, LANES), F32)
            qs, ks = [], []
            for b0 in range(0, blk, level):
                ref = a[b0 + half - 1:b0 + half, :]
                qs += [zero, q[b0 + half:b0 + level] * jnp.exp2(a[b0 + half:b0 + level] - ref)]
                ks += [jnp.exp2(ref - c[b0:b0 + half]), zero]
            q_l = jnp.concatenate(qs, axis=0)
            k_l = jnp.concatenate(ks, axis=0)
        else:
            ref = _block_ref_rows(a, level, row8)
            upper = (row8 & (level - 1)) >= half
            q_l = jnp.where(upper, q * jnp.exp2(a - ref), 0.0)
            k_l = jnp.where(upper, 0.0, jnp.exp2(ref - c))
        r = _dot_nt(q_l.astype(BF16), k_l.astype(BF16))
        scores = r if scores is None else jnp.where(group < level, r, scores)
        level = half
    scores = jnp.where(group < 1, _dot_nt(q.astype(BF16), kk.astype(BF16)), scores)

    a_end = a[blk - 1:blk, :]
    o = _dot_nt((q * jnp.exp2(a)).astype(BF16), st.astype(BF16)) + _dot(scores.astype(BF16), vb)
    o = o * lax.rsqrt(jnp.mean(o * o, axis=-1, keepdims=True) + EPS)
    o_ref[:, cols] = o * gn_ref[:, cols] * _silu(g_ref[:, cols].astype(F32))
    return st * jnp.exp2(a_end) + _dot_tn(vb, jnp.exp2(a_end - c).astype(BF16))


def _hgrn_body(q_ref, f_ref, i_ref, g_ref, hl_ref, s0_ref, gn_ref, o_ref, s_ref, st_ref, *, blk, sub, bb):
    @pl.when(pl.program_id(1) == 0)
    def _():
        for g, h in [(g, h) for g in range(bb) for h in range(HG_HEADS)]:
            st_ref[g, h] = jnp.transpose(s0_ref[g, h])

    if blk == HG_BLOCK:
        head, per_iter = functools.partial(_hgrn_head_block, blk=blk), HG_UNROLL_BLOCK // bb
    else:
        head, per_iter = functools.partial(_hgrn_head, blk=blk, sub=sub), HG_UNROLL_STEP // bb

    def heads(it, carry):
        for g, u in [(g, u) for g in range(bb) for u in range(per_iter)]:
            h = it * per_iter + u
            cols = pl.ds(pl.multiple_of(h * HEAD_DIM, HEAD_DIM), HEAD_DIM)
            st_ref[g, h] = head(cols, q_ref.at[g], f_ref.at[g], i_ref.at[g], g_ref.at[g], hl_ref, gn_ref,
                                o_ref.at[g], st_ref[g, h])
        return carry

    lax.fori_loop(0, HG_HEADS // per_iter, heads, 0)

    @pl.when(pl.program_id(1) == pl.num_programs(1) - 1)
    def _():
        for g, h in [(g, h) for g in range(bb) for h in range(HG_HEADS)]:
            s_ref[g, h] = jnp.transpose(st_ref[g, h])


def _hgrn(proj, f_pre, hg_lower, state, gn, bb):
    b, length, _ = proj.shape
    blk = min(length, HG_BLOCK)
    sub = math.gcd(length, HG_SUB)
    width = HG_HEADS * HEAD_DIM

    def col(i):
        return pl.BlockSpec((bb, blk, width), lambda bi, c: (bi, c, i))

    full = lambda shape: pl.BlockSpec(shape, lambda bi, c: (0,) * len(shape))
    state_spec = pl.BlockSpec((bb, HG_HEADS, HEAD_DIM, HEAD_DIM), lambda bi, c: (bi, 0, 0, 0))
    return pl.pallas_call(
        functools.partial(_hgrn_body, blk=blk, sub=sub, bb=bb),
        grid=(b // bb, length // blk),
        in_specs=[col(0), col(0), col(1), col(2), full(hg_lower.shape), state_spec, full((1, width))],
        out_specs=[pl.BlockSpec((bb, blk, width), lambda bi, c: (bi, c, 0)), state_spec],
        out_shape=[
            jax.ShapeDtypeStruct((b, length, width), F32),
            jax.ShapeDtypeStruct(state.shape, F32),
        ],
        scratch_shapes=[pltpu.VMEM((bb, HG_HEADS, HEAD_DIM, HEAD_DIM), F32)],
        compiler_params=_cparams(("parallel", "arbitrary")),
        name="hgrn2",
    )(proj, f_pre, proj, proj, hg_lower, state, gn)


def _prep_weights(p):
    w_in_ab = p["w_in_ab"][0]
    w_in_c = p["w_in_c"][0]
    width = HG_HEADS * HEAD_DIM
    w_gates = jnp.zeros((D_MODEL, LANES), BF16).at[:, :2 * ML_HEADS].set(w_in_ab[:, AB_MAIN:].astype(BF16))
    w_router = jnp.zeros((LANES, D_MODEL), F32).at[:N_EXPERTS, :].set(p["w_router"][0].T)
    w_router_hi = w_router.astype(BF16)
    w_router_lo = (w_router - w_router_hi.astype(F32)).astype(BF16)
    b_router = jnp.broadcast_to(p["b_router"][0][:, None], (N_EXPERTS, LANES))
    return {
        "w_ab": w_in_ab[:, :AB_MAIN].astype(BF16),
        "w_ab_gates": w_gates,
        "w_out_ab": p["w_out_ab"][0].astype(BF16),
        "w_c": jnp.concatenate([w_in_c[:, :width], w_in_c[:, 2 * width:]], axis=1).astype(BF16),
        "w_c_forget": w_in_c[:, width:2 * width].astype(BF16),
        "w_out_c": p["w_out_c"][0].astype(BF16),
        "w_ffn_gate": p["w_ffn_gate"][0].astype(BF16),
        "w_ffn_up": p["w_ffn_up"][0].astype(BF16),
        "w_ffn_down": p["w_ffn_down"][0].astype(BF16),
        "w_router_hi": w_router_hi,
        "w_router_lo": w_router_lo,
        "b_router": b_router,
        "w_moe_gate": p["w_moe_gate"][0].astype(BF16),
        "w_moe_up": p["w_moe_up"][0].astype(BF16),
        "w_moe_down": p["w_moe_down"][0].astype(BF16),
    }


def _trunk(x, pos0, ret_s, mc_s, mn_s, mm_s, conv_s, hg_s, p, w):
    b, length, d = x.shape
    t = b * length
    tm = min(512, t)
    tm_mm = min(1024, t)
    prompt = length >= RET_CHUNK
    bb = 1 if prompt else 8
    assert t % tm_mm == 0 and t % tm == 0 and b % bb == 0 and b % 2 == 0, (b, length)
    row = lambda v: v.reshape(1, -1)
    x0 = x.reshape(t, d)

    proj, gates = _norm_matmul(x0, row(p["ln_mix"][0]), w["w_ab"], w["w_ab_gates"], tm_mm, 1024)
    proj = proj.reshape(b, length, AB_MAIN)
    o_ret, ret_new = _retention(proj, pos0, ret_s, row(p["ret_gn"][0]), 2 if prompt else bb)
    h_m, c_new, n_new, m_new, conv_new = _mlstm(
        proj, gates.reshape(b, length, LANES), p["b_gates_ab"][0], p["conv_w_ab"][0], row(p["conv_b_ab"][0]),
        conv_s, mc_s, mn_s, mm_s, row(p["ml_gn"][0]), bb)
    half = RET_HEADS * HEAD_DIM
    x2 = _ffn([o_ret.reshape(t, half), h_m.reshape(t, half)], [w["w_out_ab"][:half], w["w_out_ab"][half:]], x0,
              row(p["ln_ffn"][0]), w["w_ffn_gate"], w["w_ffn_up"], w["w_ffn_down"], tm, D_FF // 2)

    proj_c, f_pre = _norm_matmul(x2, row(p["ln_mix"][1]), w["w_c"], w["w_c_forget"], tm_mm, 1024)
    o_hg, hg_new = _hgrn(proj_c.reshape(b, length, 3 * d), f_pre.reshape(b, length, d), p["hg_lower"], hg_s,
                         row(p["hg_gn"][0]), bb)
    y = _moe(o_hg.reshape(t, d), w["w_out_c"], x2, row(p["ln_ffn"][1]), w, row(p["ln_final"]))
    return (y.reshape(b, length, d), ret_new[None], c_new[None], n_new[None], m_new[None], conv_new[None],
            hg_new[None])


def kernel(x_prompt, x_sample, state_ret, state_mlstm_c, state_mlstm_n, state_mlstm_m, state_conv, state_hgrn,
           ln_mix, ln_ffn, ln_final, w_in_ab, b_gates_ab, conv_w_ab, conv_b_ab, ret_gn, ml_gn, w_out_ab,
           w_in_c, hg_lower, hg_gn, w_out_c, w_ffn_gate, w_ffn_up, w_ffn_down,
           w_router, b_router, w_moe_gate, w_moe_up, w_moe_down):
    p = {"ln_mix": ln_mix, "ln_ffn": ln_ffn, "ln_final": ln_final,
         "w_in_ab": w_in_ab, "b_gates_ab": b_gates_ab, "conv_w_ab": conv_w_ab, "conv_b_ab": conv_b_ab,
         "ret_gn": ret_gn, "ml_gn": ml_gn, "w_out_ab": w_out_ab,
         "w_in_c": w_in_c, "hg_lower": hg_lower, "hg_gn": hg_gn, "w_out_c": w_out_c,
         "w_ffn_gate": w_ffn_gate, "w_ffn_up": w_ffn_up, "w_ffn_down": w_ffn_down,
         "w_router": w_router, "b_router": b_router,
         "w_moe_gate": w_moe_gate, "w_moe_up": w_moe_up, "w_moe_down": w_moe_down}
    w = _prep_weights(p)
    bp = x_prompt.shape[0]
    zeros = lambda *shape: jnp.zeros(shape, F32)
    prompt = _trunk(
        x_prompt, 0,
        zeros(bp, RET_HEADS, HEAD_DIM, HEAD_DIM), zeros(bp, ML_HEADS, HEAD_DIM, HEAD_DIM),
        zeros(bp, ML_HEADS, HEAD_DIM), jnp.full((bp, ML_HEADS), NEG_INF, F32),
        zeros(bp, ML_CONV - 1, 2 * ML_HEADS * HEAD_DIM), zeros(bp, HG_HEADS, HEAD_DIM, HEAD_DIM), p, w)
    sample = _trunk(
        x_sample, 16384,
        state_ret[0], state_mlstm_c[0], state_mlstm_n[0], state_mlstm_m[0], state_conv[0], state_hgrn[0], p, w)
    out = []
    for a, s in zip(prompt, sample):
        out += [a, s]
    return tuple(out)
```

```python
import functools
import math

import numpy as np
import jax
import jax.numpy as jnp
from jax import lax
from jax.experimental import pallas as pl
from jax.experimental.pallas import tpu as pltpu

F32 = jnp.float32
BF16 = jnp.bfloat16

D_MODEL = 1024
HEAD_DIM = 128
RET_HEADS = 4
ML_HEADS = 4
ML_CONV = 4
HG_HEADS = 8
D_FF = 2816
N_EXPERTS = 8
ROPE_BASE = 10000.0
EPS = 1e-6
RET_CHUNK = 128
ML_CHUNK = 128
HG_BLOCK = 128
HG_SUB = 16

LANES = 128
SUBLANES = 8
AB_MAIN = 4096
VMEM_LIMIT = 56 * 1024 * 1024

NEG_INF = float("-inf")


def _cparams(sem):
    return pltpu.CompilerParams(dimension_semantics=sem, vmem_limit_bytes=VMEM_LIMIT)


def _sigmoid(x):
    return 1.0 / (1.0 + jnp.exp(-x))


def _silu(x):
    return x * _sigmoid(x)


def _rmsnorm_rows(x, g):
    ms = jnp.mean(x * x, axis=-1, keepdims=True)
    return x * lax.rsqrt(ms + EPS) * g


def _dot(a, b):
    return jnp.dot(a, b, preferred_element_type=F32)


def _dot_nt(a, b):
    return lax.dot_general(a, b, (((1,), (1,)), ((), ())), preferred_element_type=F32)


def _dot_tn(a, b):
    return lax.dot_general(a, b, (((0,), (0,)), ((), ())), preferred_element_type=F32)


def _pad_rows(x, rows):
    if x.shape[0] == rows:
        return x
    return jnp.concatenate([x, jnp.zeros((rows - x.shape[0], x.shape[1]), x.dtype)], axis=0)


def _norm_matmul_body(x_ref, g_ref, w_ref, ws_ref, o_ref, os_ref, h_ref):
    @pl.when(pl.program_id(1) == 0)
    def _():
        h = _rmsnorm_rows(x_ref[...], g_ref[...]).astype(BF16)
        h_ref[...] = h
        os_ref[...] = _dot(h, ws_ref[...])

    o_ref[...] = _dot(h_ref[...], w_ref[...]).astype(BF16)


def _norm_matmul(x, g, w, w_side, tm, tn):
    t, d = x.shape
    n, ns = w.shape[1], w_side.shape[1]
    return pl.pallas_call(
        _norm_matmul_body,
        grid=(t // tm, n // tn),
        in_specs=[
            pl.BlockSpec((tm, d), lambda i, j: (i, 0)),
            pl.BlockSpec((1, d), lambda i, j: (0, 0)),
            pl.BlockSpec((d, tn), lambda i, j: (0, j)),
            pl.BlockSpec((d, ns), lambda i, j: (0, 0)),
        ],
        out_specs=[pl.BlockSpec((tm, tn), lambda i, j: (i, j)), pl.BlockSpec((tm, ns), lambda i, j: (i, 0))],
        out_shape=[jax.ShapeDtypeStruct((t, n), BF16), jax.ShapeDtypeStruct((t, ns), F32)],
        scratch_shapes=[pltpu.VMEM((tm, d), BF16)],
        compiler_params=_cparams(("parallel", "arbitrary")),
        name="norm_matmul",
    )(x, g, w, w_side)


def _mixer_residual(x_ref, a_refs, w_refs):
    x = x_ref[...]
    for a_ref, w_ref in zip(a_refs, w_refs):
        x = x + _dot(a_ref[...].astype(BF16), w_ref[...])
    return x


def _ffn_body(*refs, n_in):
    a_refs, w_refs = refs[:n_in], refs[n_in:2 * n_in]
    x_ref, g_ref, wg_ref, wu_ref, wd_ref, o_ref, h_ref, x1_ref = refs[2 * n_in:]
    f = pl.program_id(1)

    @pl.when(f == 0)
    def _():
        x1 = _mixer_residual(x_ref, a_refs, w_refs)
        x1_ref[...] = x1
        h_ref[...] = _rmsnorm_rows(x1, g_ref[...]).astype(BF16)

    h = h_ref[...]
    a = _dot(h, wg_ref[...])
    u = _dot(h, wu_ref[...])
    act = (_silu(a) * u).astype(BF16)
    x1_ref[...] += _dot(act, wd_ref[...])

    @pl.when(f == pl.num_programs(1) - 1)
    def _():
        o_ref[...] = x1_ref[...]


def _ffn(acts, weights, x, g, wg, wu, wd, tm, tf):
    t, d = x.shape
    dff = wg.shape[1]
    n_in = len(acts)
    in_specs = [pl.BlockSpec((tm, a.shape[1]), lambda i, f: (i, 0)) for a in acts]
    in_specs += [pl.BlockSpec(w.shape, lambda i, f: (0, 0)) for w in weights]
    in_specs += [
        pl.BlockSpec((tm, d), lambda i, f: (i, 0)),
        pl.BlockSpec((1, d), lambda i, f: (0, 0)),
        pl.BlockSpec((d, tf), lambda i, f: (0, f)),
        pl.BlockSpec((d, tf), lambda i, f: (0, f)),
        pl.BlockSpec((tf, d), lambda i, f: (f, 0)),
    ]
    return pl.pallas_call(
        functools.partial(_ffn_body, n_in=n_in),
        grid=(t // tm, dff // tf),
        in_specs=in_specs,
        out_specs=pl.BlockSpec((tm, d), lambda i, f: (i, 0)),
        out_shape=jax.ShapeDtypeStruct((t, d), F32),
        scratch_shapes=[pltpu.VMEM((tm, d), BF16), pltpu.VMEM((tm, d), F32)],
        compiler_params=_cparams(("parallel", "arbitrary")),
        name="ffn",
    )(*acts, *weights, x, g, wg, wu, wd)


MOE_TB = 512
MOE_ALIGN = 16
MOE_BURST = 4
MOE_TILE = 256
MOE_CHUNKS_PER_TILE = MOE_TILE // MOE_ALIGN


def _moe_block_rows(tb):
    return 2 * tb + N_EXPERTS * MOE_ALIGN


def _group_block_specs(shape, blocks):
    specs, first = [], 0
    for nb in blocks:
        specs.append(pl.BlockSpec(shape, lambda i, *_, first=first, nb=nb: (jnp.clip(i - first, 0, nb - 1), 0)))
        first += nb
    return specs


def _route_body(*refs, blocks):
    n = len(blocks)
    a_refs, wo_ref, x_refs = refs[:n], refs[n], refs[n + 1:2 * n + 1]
    g_ref, whi_ref, wlo_ref, b_ref, tri_ref, x3_ref, h_ref, col_ref, rowf_ref, cnt_ref = refs[2 * n + 1:]
    i = pl.program_id(0)
    act, x_in = a_refs[0][...], x_refs[0][...]
    first = blocks[0]
    for a_ref, x_ref, nb in zip(a_refs[1:], x_refs[1:], blocks[1:]):
        act = jnp.where(i >= first, a_ref[...], act)
        x_in = jnp.where(i >= first, x_ref[...], x_in)
        first += nb
    x3 = x_in + _dot(act.astype(BF16), wo_ref[...])
    x3_ref[...] = x3
    h = _rmsnorm_rows(x3, g_ref[...])
    h_hi = h.astype(BF16)
    h_ref[...] = h_hi
    h_lo = (h - h_hi.astype(F32)).astype(BF16)

    logits = (_dot_nt(whi_ref[...], h_hi) + _dot_nt(wlo_ref[...], h_hi) + _dot_nt(whi_ref[...], h_lo))
    logits = logits[:N_EXPERTS, :] + b_ref[:, 0:1]
    row = lax.broadcasted_iota(jnp.int32, logits.shape, 0)
    m1 = jnp.max(logits, axis=0, keepdims=True)
    i1 = jnp.min(jnp.where(logits == m1, row, N_EXPERTS), axis=0, keepdims=True)
    lg2 = jnp.where(row == i1, NEG_INF, logits)
    m2 = jnp.max(lg2, axis=0, keepdims=True)
    i2 = jnp.min(jnp.where(lg2 == m2, row, N_EXPERTS), axis=0, keepdims=True)
    e = jnp.exp(m2 - m1)
    w1 = 1.0 / (1.0 + e)
    w2 = e / (1.0 + e)

    sel = jnp.where(row == i1, 1.0, jnp.where(row == i2, 1.0, 0.0))
    before = _dot(sel.astype(BF16), tri_ref[...])
    n = jnp.broadcast_to(jnp.sum(sel, axis=1, keepdims=True), (N_EXPERTS, LANES))
    padded = jnp.floor((n + (MOE_ALIGN - 1.0)) * (1.0 / MOE_ALIGN)) * MOE_ALIGN
    base = _scan_rows(padded, jnp.add, 0.0) - padded
    pos = base[:, 0:1] + before
    loc1 = jnp.sum(jnp.where(row == i1, pos, 0.0), axis=0, keepdims=True)
    loc2 = jnp.sum(jnp.where(row == i2, pos, 0.0), axis=0, keepdims=True)
    rowf = jnp.where(row == 0, loc1, jnp.where(row == 1, loc2, jnp.where(row == 2, w1,
                                                                          jnp.where(row == 3, w2, 0.0))))
    rowf_ref[0] = rowf
    for c in range(rowf.shape[1] // LANES):
        col_ref[c * LANES:(c + 1) * LANES, :] = jnp.transpose(_pad_rows(rowf[:, c * LANES:(c + 1) * LANES], LANES))
    cnt_ref[0] = n


def _route(acts, w_out, xs, g, w_hi, w_lo, b, tb):
    d = xs[0].shape[1]
    blocks = [x.shape[0] // tb for x in xs]
    nb = sum(blocks)
    t = nb * tb
    tri = jnp.asarray(np.triu(np.ones((tb, tb), np.float32), 1), BF16)
    full = lambda shape: pl.BlockSpec(shape, lambda i: (0,) * len(shape))
    return pl.pallas_call(
        functools.partial(_route_body, blocks=blocks),
        grid=(nb,),
        in_specs=(_group_block_specs((tb, acts[0].shape[1]), blocks) + [full(w_out.shape)]
                  + _group_block_specs((tb, d), blocks)
                  + [full((1, d)), full((LANES, d)), full((LANES, d)), full((SUBLANES, LANES)), full((tb, tb))]),
        out_specs=[
            pl.BlockSpec((tb, d), lambda i: (i, 0)),
            pl.BlockSpec((tb, d), lambda i: (i, 0)),
            pl.BlockSpec((tb, LANES), lambda i: (i, 0)),
            pl.BlockSpec((1, SUBLANES, tb), lambda i: (i, 0, 0)),
            pl.BlockSpec((1, SUBLANES, LANES), lambda i: (i, 0, 0)),
        ],
        out_shape=[
            jax.ShapeDtypeStruct((t, d), F32),
            jax.ShapeDtypeStruct((t, d), BF16),
            jax.ShapeDtypeStruct((t, LANES), F32),
            jax.ShapeDtypeStruct((nb, SUBLANES, tb), F32),
            jax.ShapeDtypeStruct((nb, SUBLANES, LANES), F32),
        ],
        compiler_params=_cparams(("parallel",)),
        name="route",
    )(*acts, w_out, *xs, g, w_hi, w_lo, b, tri)


def _moe_plan(cnt):
    n = cnt[:, :N_EXPERTS, 0].astype(jnp.int32)
    chunks = (n + (MOE_ALIGN - 1)) // MOE_ALIGN
    src = jnp.cumsum(chunks, axis=1) - chunks
    seg_chunks = jnp.sum(chunks, axis=0)
    seg_tiles = (seg_chunks + (MOE_CHUNKS_PER_TILE - 1)) // MOE_CHUNKS_PER_TILE
    seg_end = jnp.cumsum(seg_tiles)
    seg_first = (seg_end - seg_tiles) * MOE_CHUNKS_PER_TILE
    dst = seg_first[None, :] + jnp.cumsum(chunks, axis=0) - chunks
    flat = lambda a: a.reshape(-1).astype(jnp.int32)
    return {"src": flat(src), "dst": flat(dst), "chunks": flat(chunks),
            "first_tile": flat(seg_end - seg_tiles), "tile_count": flat(seg_tiles),
            "tail_start": flat(seg_first + seg_chunks), "tail_count": flat(seg_tiles * MOE_CHUNKS_PER_TILE - seg_chunks)}


def _chunk_rows(chunk, n=1):
    return pl.ds(pl.multiple_of(chunk * MOE_ALIGN, MOE_ALIGN), n * MOE_ALIGN)


def _for_block_copies(nch_ref, blk, fn):
    for e in range(N_EXPERTS):
        count = nch_ref[blk * N_EXPERTS + e]
        bursts = count // MOE_BURST
        lax.fori_loop(0, bursts, lambda i, carry, e=e: (fn(e, i * MOE_BURST, MOE_BURST), carry)[1], 0)
        lax.fori_loop(bursts * MOE_BURST, count, lambda c, carry, e=e: (fn(e, c, 1), carry)[1], 0)


def _block_copy_counts(nch_ref, blk):
    bursts = singles = 0
    for e in range(N_EXPERTS):
        count = nch_ref[blk * N_EXPERTS + e]
        bursts = bursts + count // MOE_BURST
        singles = singles + lax.rem(count, MOE_BURST)
    return bursts, singles


def _wait_block_copies(nch_ref, blk, copy):
    bursts, singles = _block_copy_counts(nch_ref, blk)
    lax.fori_loop(0, bursts, lambda i, carry: (copy(MOE_BURST).wait(), carry)[1], 0)
    lax.fori_loop(0, singles, lambda i, carry: (copy(1).wait(), carry)[1], 0)


def _gather_body(src_ref, dst_ref, nch_ref, tail_ref, ntail_ref, last_ref, h_ref, rowf_ref, hs_ref, z_ref, sem,
                 *, n_tiles):
    b = pl.program_id(0)
    slot = lax.rem(b, 2)
    tb = h_ref.shape[0]
    loc = rowf_ref[0].astype(jnp.int32)
    r = lax.broadcasted_iota(jnp.int32, (z_ref.shape[1], tb), 0)
    onehot = jnp.where(r == loc[0:1, :], 1.0, jnp.where(r == loc[1:2, :], 1.0, 0.0)).astype(BF16)
    z_ref[slot] = _dot(onehot, h_ref[...]).astype(BF16)

    def copy(buf, src_chunk, dst_chunk, n=1):
        return pltpu.make_async_copy(z_ref.at[buf, _chunk_rows(src_chunk, n)], hs_ref.at[_chunk_rows(dst_chunk, n)],
                                     sem.at[buf])

    def drain(blk, buf):
        _wait_block_copies(nch_ref, blk, lambda n: copy(buf, 0, 0, n))

    @pl.when(b > 0)
    def _():
        drain(b - 1, 1 - slot)

    _for_block_copies(nch_ref, b, lambda e, c, n: copy(slot, src_ref[b * N_EXPERTS + e] + c,
                                                       dst_ref[b * N_EXPERTS + e] + c, n).start())

    @pl.when(b == pl.num_programs(0) - 1)
    def _():
        drain(b, slot)
        z_ref[0, 0:MOE_ALIGN, :] = jnp.zeros((MOE_ALIGN, z_ref.shape[2]), BF16)
        past = last_ref[0]
        total = n_tiles * MOE_CHUNKS_PER_TILE
        for e in range(N_EXPERTS):
            lax.fori_loop(0, ntail_ref[e], lambda c, carry, e=e: (copy(0, 0, tail_ref[e] + c).start(), carry)[1], 0)
        lax.fori_loop(past, total, lambda c, carry: (copy(0, 0, c).start(), carry)[1], 0)
        for e in range(N_EXPERTS):
            lax.fori_loop(0, ntail_ref[e], lambda c, carry: (copy(0, 0, 0).wait(), carry)[1], 0)
        lax.fori_loop(past, total, lambda c, carry: (copy(0, 0, 0).wait(), carry)[1], 0)


def _moe_gather(plan, h, rowf, n_tiles, tb):
    t, d = h.shape
    rows = _moe_block_rows(tb)
    past = (plan["first_tile"][-1:] + plan["tile_count"][-1:]) * MOE_CHUNKS_PER_TILE
    return pl.pallas_call(
        functools.partial(_gather_body, n_tiles=n_tiles),
        grid_spec=pltpu.PrefetchScalarGridSpec(
            num_scalar_prefetch=6,
            grid=(t // tb,),
            in_specs=[
                pl.BlockSpec((tb, d), lambda i, *_: (i, 0)),
                pl.BlockSpec((1, SUBLANES, tb), lambda i, *_: (i, 0, 0)),
            ],
            out_specs=pl.BlockSpec(memory_space=pl.ANY),
            scratch_shapes=[pltpu.VMEM((2, rows, d), BF16), pltpu.SemaphoreType.DMA((2,))],
        ),
        out_shape=jax.ShapeDtypeStruct((n_tiles * MOE_TILE, d), BF16),
        compiler_params=_cparams(("arbitrary",)),
        name="moe_gather",
    )(plan["src"], plan["dst"], plan["chunks"], plan["tail_start"], plan["tail_count"], past, h, rowf)


def _expert_ffn_body(first_ref, count_ref, xs_ref, wg_ref, wu_ref, wd_ref, ys_ref, x_buf, y_buf, in_sem, out_sem,
                     *, n_tiles, halves):
    e = pl.program_id(0)
    first = first_ref[e]
    count = count_ref[e]
    tf = wg_ref.shape[2] // halves

    def tile_rows(tile):
        return pl.ds(pl.multiple_of(tile * MOE_TILE, MOE_TILE), MOE_TILE)

    def load(j, buf):
        return pltpu.make_async_copy(xs_ref.at[tile_rows(first + j)], x_buf.at[buf], in_sem.at[buf])

    def store(tile, buf):
        return pltpu.make_async_copy(y_buf.at[buf], ys_ref.at[tile_rows(tile)], out_sem.at[buf])

    @pl.when(count > 0)
    def _():
        load(0, 0).start()

    def tile_step(j, carry):
        buf = lax.rem(j, 2)

        @pl.when(j + 1 < count)
        def _():
            load(j + 1, 1 - buf).start()

        load(j, buf).wait()
        xs = x_buf[buf]
        y = None
        for f in range(halves):
            a = _dot(xs, wg_ref[0, :, f * tf:(f + 1) * tf])
            u = _dot(xs, wu_ref[0, :, f * tf:(f + 1) * tf])
            part = _dot((_silu(a) * u).astype(BF16), wd_ref[0, f * tf:(f + 1) * tf, :])
            y = part if y is None else y + part

        @pl.when(j >= 2)
        def _():
            store(first, buf).wait()

        y_buf[buf] = y.astype(BF16)
        store(first + j, buf).start()
        return carry

    lax.fori_loop(0, count, tile_step, 0)

    for back in (2, 1):
        @pl.when(count >= back)
        def _(back=back):
            store(first, lax.rem(count - back, 2)).wait()

    @pl.when(e == pl.num_programs(0) - 1)
    def _():
        end = first + count
        y_buf[0] = jnp.zeros(y_buf.shape[1:], BF16)
        lax.fori_loop(end, n_tiles, lambda t, c: (store(t, 0).start(), c)[1], 0)
        lax.fori_loop(end, n_tiles, lambda t, c: (store(t, 0).wait(), c)[1], 0)


def _expert_ffn(first_tile, tile_count, xs, wg, wu, wd):
    rows, d = xs.shape
    dff = wg.shape[2]
    per_expert = lambda shape: pl.BlockSpec((1,) + shape, lambda e, *_: (e, 0, 0))
    return pl.pallas_call(
        functools.partial(_expert_ffn_body, n_tiles=rows // MOE_TILE, halves=2),
        grid_spec=pltpu.PrefetchScalarGridSpec(
            num_scalar_prefetch=2,
            grid=(N_EXPERTS,),
            in_specs=[
                pl.BlockSpec(memory_space=pl.ANY),
                per_expert((d, dff)), per_expert((d, dff)), per_expert((dff, d)),
            ],
            out_specs=pl.BlockSpec(memory_space=pl.ANY),
            scratch_shapes=[pltpu.VMEM((2, MOE_TILE, d), BF16), pltpu.VMEM((2, MOE_TILE, d), BF16),
                            pltpu.SemaphoreType.DMA((2,)), pltpu.SemaphoreType.DMA((2,))],
        ),
        out_shape=jax.ShapeDtypeStruct((rows, d), BF16),
        compiler_params=_cparams(("arbitrary",)),
        name="expert_ffn",
    )(first_tile, tile_count, xs, wg, wu, wd)


def _combine_body(src_ref, dst_ref, nch_ref, x_ref, col_ref, gf_ref, ys_ref, *refs, blocks):
    o_refs, (y_ref, sem) = refs[:len(blocks)], refs[len(blocks):]
    b = pl.program_id(0)
    nb = pl.num_programs(0)
    slot = lax.rem(b, 2)

    def copy(buf, src_chunk, dst_chunk, n=1):
        return pltpu.make_async_copy(ys_ref.at[_chunk_rows(dst_chunk, n)], y_ref.at[buf, _chunk_rows(src_chunk, n)],
                                     sem.at[buf])

    def fetch(blk, buf):
        _for_block_copies(nch_ref, blk, lambda e, c, n: copy(buf, src_ref[blk * N_EXPERTS + e] + c,
                                                             dst_ref[blk * N_EXPERTS + e] + c, n).start())

    @pl.when(b == 0)
    def _():
        y_ref[...] = jnp.zeros_like(y_ref)
        fetch(0, 0)

    @pl.when(b + 1 < nb)
    def _():
        fetch(b + 1, 1 - slot)

    _wait_block_copies(nch_ref, b, lambda n: copy(slot, 0, 0, n))

    col = col_ref[...]
    loc1 = col[:, 0:1].astype(jnp.int32)
    loc2 = col[:, 1:2].astype(jnp.int32)
    r = lax.broadcasted_iota(jnp.int32, (col.shape[0], y_ref.shape[1]), 1)
    weights = jnp.where(r == loc1, col[:, 2:3], jnp.where(r == loc2, col[:, 3:4], 0.0)).astype(BF16)
    out = _rmsnorm_rows(x_ref[...] + _dot(weights, y_ref[slot]), gf_ref[...])
    first = 0
    for o_ref, n_blocks in zip(o_refs, blocks):
        @pl.when((b >= first) & (b < first + n_blocks))
        def _(o_ref=o_ref):
            o_ref[...] = out
        first += n_blocks


def _moe_combine(src, dst, nch, x, col, g_final, ys, tb, blocks):
    t, d = x.shape
    rows = _moe_block_rows(tb)
    return pl.pallas_call(
        functools.partial(_combine_body, blocks=blocks),
        grid_spec=pltpu.PrefetchScalarGridSpec(
            num_scalar_prefetch=3,
            grid=(t // tb,),
            in_specs=[
                pl.BlockSpec((tb, d), lambda i, *_: (i, 0)),
                pl.BlockSpec((tb, LANES), lambda i, *_: (i, 0)),
                pl.BlockSpec((1, d), lambda i, *_: (0, 0)),
                pl.BlockSpec(memory_space=pl.ANY),
            ],
            out_specs=_group_block_specs((tb, d), blocks),
            scratch_shapes=[pltpu.VMEM((2, rows, d), BF16), pltpu.SemaphoreType.DMA((2,))],
        ),
        out_shape=[jax.ShapeDtypeStruct((nb * tb, d), F32) for nb in blocks],
        compiler_params=_cparams(("arbitrary",)),
        name="moe_combine",
    )(src, dst, nch, x, col, g_final, ys)


def _moe(acts, w_out, xs, g, w, g_final):
    tb = MOE_TB
    assert all(x.shape[0] % tb == 0 for x in xs), [x.shape for x in xs]
    blocks = [x.shape[0] // tb for x in xs]
    nb = sum(blocks)
    t = nb * tb
    x, h, col, rowf, cnt = _route(acts, w_out, xs, g, w["w_router_hi"], w["w_router_lo"], w["b_router"], tb)
    n_tiles = -(-(2 * t + nb * N_EXPERTS * (MOE_ALIGN - 1)) // MOE_TILE) + N_EXPERTS
    plan = _moe_plan(cnt)
    xg = _moe_gather(plan, h, rowf, n_tiles, tb)
    yg = _expert_ffn(plan["first_tile"], plan["tile_count"], xg, w["w_moe_gate"], w["w_moe_up"], w["w_moe_down"])
    return _moe_combine(plan["src"], plan["dst"], plan["chunks"], x, col, g_final, yg, tb, blocks)


def _retention_tables(c_real, ct):
    h = np.arange(RET_HEADS, dtype=np.float64)
    log_gamma = np.log1p(-np.exp2(-5.0 - h))
    idx = np.arange(ct, dtype=np.float64)
    live = idx < c_real
    diff = idx[:, None] - idx[None, :]
    causal = (diff >= 0) & live[:, None] & live[None, :]
    decay = np.where(causal[None], np.exp(np.where(causal, diff, 0.0)[None] * log_gamma[:, None, None]), 0.0)
    q_dec = np.where(live[None], np.exp((idx + 1.0)[None] * log_gamma[:, None]), 0.0)
    k_dec = np.where(live[None], np.exp((c_real - 1.0 - idx)[None] * log_gamma[:, None]), 0.0)
    q_dec = np.broadcast_to(q_dec[..., None], (RET_HEADS, ct, LANES))
    k_dec = np.broadcast_to(k_dec[..., None], (RET_HEADS, ct, LANES))
    return (jnp.asarray(decay, F32), jnp.asarray(q_dec, F32), jnp.asarray(k_dec, F32))


def _rope_tables(pos0, length):
    half = HEAD_DIM // 2
    inv = ROPE_BASE ** (-np.arange(half, dtype=np.float64) / half)
    ang = (pos0 + np.arange(length, dtype=np.float64))[:, None] * inv[None, :]
    cos = np.concatenate([np.cos(ang), np.cos(ang)], axis=-1)
    sin = np.concatenate([-np.sin(ang), np.sin(ang)], axis=-1)
    return jnp.asarray(cos, F32), jnp.asarray(sin, F32)


def _retention_body(q_ref, k_ref, v_ref, g_ref, cos_ref, sin_ref, dec_ref, qd_ref, kd_ref, s0_ref, gn_ref,
                    o_ref, s_ref, *, c_real, ct, bb):
    @pl.when(pl.program_id(1) == 0)
    def _():
        s_ref[...] = s0_ref[...]

    cos = cos_ref[...]
    sin = sin_ref[...]
    for g, hd in [(g, hd) for g in range(bb) for hd in range(RET_HEADS)]:
        lo, hi = hd * HEAD_DIM, (hd + 1) * HEAD_DIM
        q = q_ref[g, :, lo:hi].astype(F32)
        k = k_ref[g, :, lo:hi].astype(F32)
        q = (q * cos + pltpu.roll(q, HEAD_DIM // 2, 1) * sin) * (HEAD_DIM ** -0.5)
        k = k * cos + pltpu.roll(k, HEAD_DIM // 2, 1) * sin
        v = v_ref[g, :, lo:hi].astype(F32)
        qb = q.astype(BF16)
        q_dec = qd_ref[hd, :c_real, :]
        state = s_ref[g, hd]

        s = _dot_nt(qb, _pad_rows(k, ct).astype(BF16)) * dec_ref[hd, :c_real, :]
        o = _dot(s.astype(BF16), _pad_rows(v, ct).astype(BF16)) + _dot(qb, state.astype(BF16)) * q_dec
        chunk_dec = q_dec[c_real - 1:c_real, :]
        s_ref[g, hd] = state * chunk_dec + _dot_tn((k * kd_ref[hd, :c_real, :]).astype(BF16), v.astype(BF16))

        o = o - jnp.mean(o, axis=-1, keepdims=True)
        o = o * lax.rsqrt(jnp.mean(o * o, axis=-1, keepdims=True) + EPS)
        o_ref[g, :, lo:hi] = _silu(g_ref[g, :, lo:hi].astype(F32)) * (o * gn_ref[:, lo:hi])


def _retention(proj, pos0, state, gn, bb):
    b, length, _ = proj.shape
    c_real = math.gcd(length, RET_CHUNK)
    ct = RET_CHUNK
    width = RET_HEADS * HEAD_DIM
    cos, sin = _rope_tables(pos0, length)
    decay, q_dec, k_dec = _retention_tables(c_real, ct)

    def col(i):
        return pl.BlockSpec((bb, c_real, width), lambda bi, c: (bi, c, i))

    full = lambda shape: pl.BlockSpec(shape, lambda bi, c: (0,) * len(shape))
    state_spec = pl.BlockSpec((bb, RET_HEADS, HEAD_DIM, HEAD_DIM), lambda bi, c: (bi, 0, 0, 0))
    return pl.pallas_call(
        functools.partial(_retention_body, c_real=c_real, ct=ct, bb=bb),
        grid=(b // bb, length // c_real),
        in_specs=[
            col(0), col(1), col(2), col(3),
            pl.BlockSpec((c_real, HEAD_DIM), lambda bi, c: (c, 0)),
            pl.BlockSpec((c_real, HEAD_DIM), lambda bi, c: (c, 0)),
            full((RET_HEADS, ct, ct)), full((RET_HEADS, ct, LANES)), full((RET_HEADS, ct, LANES)),
            state_spec,
            full((1, width)),
        ],
        out_specs=[
            pl.BlockSpec((bb, c_real, width), lambda bi, c: (bi, c, 0)),
            state_spec,
        ],
        out_shape=[
            jax.ShapeDtypeStruct((b, length, width), F32),
            jax.ShapeDtypeStruct(state.shape, F32),
        ],
        compiler_params=_cparams(("parallel", "arbitrary")),
        name="retention",
    )(proj, proj, proj, proj, cos, sin, decay, q_dec, k_dec, state, gn)


def _scan_rows(x, op, fill):
    rows = x.shape[0]
    row = lax.broadcasted_iota(jnp.int32, x.shape, 0)
    sh = 1
    while sh < rows:
        x = op(x, jnp.where(row >= sh, pltpu.roll(x, sh, 0), fill))
        sh *= 2
    return x


def _mlstm_body(q_ref, k_ref, v_ref, og_ref, gate_ref, bg_ref, cw_ref, cb_ref, cs_ref, c0_ref, n0_ref, m0_ref,
                gn_ref, o_ref, c_ref, n_ref, m_ref, conv_ref, xc_ref, *, c_real, ct, bb):
    tail = ML_CONV - 1

    @pl.when(pl.program_id(1) == 0)
    def _():
        c_ref[...] = c0_ref[...]
        n_ref[...] = n0_ref[...]
        m_ref[...] = m0_ref[...]
        xc_ref[:, SUBLANES - tail:SUBLANES, :] = cs_ref[...]

    for g in range(bb):
        _mlstm_sequence(q_ref.at[g], k_ref.at[g], v_ref.at[g], og_ref.at[g], gate_ref.at[g], bg_ref, cw_ref, cb_ref,
                        gn_ref, o_ref.at[g], c_ref.at[g], n_ref.at[g], m_ref.at[g], conv_ref.at[g], xc_ref.at[g],
                        c_real=c_real, ct=ct)


def _mlstm_sequence(q_ref, k_ref, v_ref, og_ref, gate_ref, bg_ref, cw_ref, cb_ref, gn_ref, o_ref, c_ref, n_ref, m_ref,
                    conv_ref, xc_ref, *, c_real, ct):
    width = ML_HEADS * HEAD_DIM
    tail = ML_CONV - 1

    xc_ref[SUBLANES:SUBLANES + c_real, :width] = q_ref[...].astype(F32)
    xc_ref[SUBLANES:SUBLANES + c_real, width:] = k_ref[...].astype(F32)
    y = cb_ref[...] + cw_ref[tail:tail + 1, :] * xc_ref[SUBLANES:SUBLANES + c_real, :]
    for j in range(tail):
        y = y + cw_ref[j:j + 1, :] * xc_ref[SUBLANES - tail + j:SUBLANES - tail + j + c_real, :]
    new_tail = xc_ref[c_real:c_real + SUBLANES, :]
    conv_ref[...] = new_tail[SUBLANES - tail:, :]
    xc_ref[0:SUBLANES, :] = new_tail
    qk = _silu(y)
    q_all = qk[:, :width]
    k_all = qk[:, width:] * (HEAD_DIM ** -0.5)
    v_all = v_ref[...].astype(F32)

    gates = gate_ref[...] + bg_ref[...]
    f_pre = pltpu.roll(gates, LANES - ML_HEADS, 1)
    log_f = jnp.minimum(f_pre, 0.0) - jnp.log(1.0 + jnp.exp(-jnp.abs(f_pre)))
    b_cum = _scan_rows(log_f, jnp.add, 0.0)
    src = gates - b_cum
    cmx = _scan_rows(src, jnp.maximum, NEG_INF)
    m_prev = m_ref[...]
    mx = jnp.maximum(m_prev, cmx)
    m_t = b_cum + mx
    w_prev = jnp.exp(m_prev - mx)
    e_neg_m = jnp.exp(-m_t)
    last = c_real - 1
    b_last = b_cum[last:last + 1, :]
    m_new = m_t[last:last + 1, :]
    w_end = jnp.exp(b_last + src - m_new)
    dec = jnp.exp(b_last + m_prev - m_new)
    m_ref[...] = m_new
    src_t = jnp.transpose(_pad_rows(src, ct))

    t_idx = lax.broadcasted_iota(jnp.int32, (c_real, ct), 0)
    s_idx = lax.broadcasted_iota(jnp.int32, (c_real, ct), 1)
    live = (s_idx <= t_idx) & (s_idx < c_real)

    for hd in range(ML_HEADS):
        lo, hi = hd * HEAD_DIM, (hd + 1) * HEAD_DIM
        q = q_all[:, lo:hi]
        k = k_all[:, lo:hi]
        v = v_all[:, lo:hi]
        qb = q.astype(BF16)
        c_state = c_ref[hd]
        n_state = n_ref[hd:hd + 1, :]
        wp = w_prev[:, hd:hd + 1]
        w = jnp.where(live, jnp.exp(src_t[hd:hd + 1, :] - mx[:, hd:hd + 1]), 0.0)
        s = _dot_nt(qb, _pad_rows(k, ct).astype(BF16)) * w
        num = _dot(s.astype(BF16), _pad_rows(v, ct).astype(BF16)) + wp * _dot(qb, c_state.astype(BF16))
        den = jnp.sum(s, axis=-1, keepdims=True) + wp * jnp.sum(q * n_state, axis=-1, keepdims=True)
        hh = num / jnp.maximum(jnp.abs(den), e_neg_m[:, hd:hd + 1])
        kw = k * w_end[:, hd:hd + 1]
        dec_h = dec[:, hd:hd + 1]
        c_ref[hd] = dec_h * c_state + _dot_tn(kw.astype(BF16), v.astype(BF16))
        n_ref[hd:hd + 1, :] = dec_h * n_state + jnp.sum(kw, axis=0, keepdims=True)

        hh = _sigmoid(og_ref[:, lo:hi].astype(F32)) * hh
        hh = hh - jnp.mean(hh, axis=-1, keepdims=True)
        hh = hh * lax.rsqrt(jnp.mean(hh * hh, axis=-1, keepdims=True) + EPS)
        o_ref[:, lo:hi] = hh * gn_ref[:, lo:hi]


def _mlstm(proj, gates, b_gates, conv_w, conv_b, conv_s, c0, n0, m0, gn, bb):
    b, length, _ = proj.shape
    c_real = math.gcd(length, ML_CHUNK)
    ct = ML_CHUNK
    width = ML_HEADS * HEAD_DIM
    base = (2 * RET_HEADS * HEAD_DIM + 2 * RET_HEADS * HEAD_DIM) // width

    def col(i):
        return pl.BlockSpec((bb, c_real, width), lambda bi, c: (bi, c, base + i))

    full = lambda shape: pl.BlockSpec(shape, lambda bi, c: (0,) * len(shape))
    per_b = lambda shape: pl.BlockSpec((bb,) + shape, lambda bi, c: (bi,) + (0,) * len(shape))
    bg = jnp.zeros((1, LANES), F32).at[0, :2 * ML_HEADS].set(b_gates)
    m0p = jnp.zeros((b, 1, LANES), F32).at[:, 0, :ML_HEADS].set(m0)
    outs = pl.pallas_call(
        functools.partial(_mlstm_body, c_real=c_real, ct=ct, bb=bb),
        grid=(b // bb, length // c_real),
        in_specs=[
            col(0), col(1), col(2), col(3),
            pl.BlockSpec((bb, c_real, LANES), lambda bi, c: (bi, c, 0)),
            full((1, LANES)),
            full((ML_CONV, 2 * width)),
            full((1, 2 * width)),
            per_b((ML_CONV - 1, 2 * width)),
            per_b((ML_HEADS, HEAD_DIM, HEAD_DIM)),
            per_b((ML_HEADS, HEAD_DIM)),
            per_b((1, LANES)),
            full((1, width)),
        ],
        out_specs=[
            pl.BlockSpec((bb, c_real, width), lambda bi, c: (bi, c, 0)),
            per_b((ML_HEADS, HEAD_DIM, HEAD_DIM)),
            per_b((ML_HEADS, HEAD_DIM)),
            per_b((1, LANES)),
            per_b((ML_CONV - 1, 2 * width)),
        ],
        out_shape=[
            jax.ShapeDtypeStruct((b, length, width), F32),
            jax.ShapeDtypeStruct(c0.shape, F32),
            jax.ShapeDtypeStruct(n0.shape, F32),
            jax.ShapeDtypeStruct((b, 1, LANES), F32),
            jax.ShapeDtypeStruct(conv_s.shape, F32),
        ],
        scratch_shapes=[pltpu.VMEM((bb, ct + SUBLANES, 2 * width), F32)],
        compiler_params=_cparams(("parallel", "arbitrary")),
        name="mlstm",
    )(proj, proj, proj, proj, gates, bg, conv_w, conv_b, conv_s, c0, n0, m0p, gn)
    h_m, c_new, n_new, m_new, conv_new = outs
    return h_m, c_new, n_new, m_new[:, 0, :ML_HEADS], conv_new


HG_UNROLL_BLOCK = 8
HG_UNROLL_STEP = 8


def _hgrn_head(cols, q_ref, f_ref, i_ref, g_ref, hl_ref, gn_ref, o_ref, st, *, blk, sub):
    hl = hl_ref[:, cols]
    hmax = jnp.max(hl, axis=0, keepdims=True)
    ex = jnp.exp(hl - hmax)
    p = ex / jnp.sum(ex, axis=0, keepdims=True)
    lower = (p[0:1] + p[1:2]) - p[0:1]
    gn = gn_ref[:, cols]

    f = lower + (1.0 - lower) * _sigmoid(f_ref[:, cols])
    log_k = jnp.log2(jnp.maximum(1.0 - f, 0.0))
    q = _silu(q_ref[:, cols].astype(F32))
    v = i_ref[:, cols].astype(F32)
    gate = _silu(g_ref[:, cols].astype(F32))

    row = lax.broadcasted_iota(jnp.int32, (blk, LANES), 0)
    rsub = row & (sub - 1)
    a = jnp.log2(f)
    sh = 1
    while sh < sub:
        a = a + jnp.where(rsub >= sh, pltpu.roll(a, sh, 0), 0.0)
        sh *= 2
    c = a - log_k
    q_in = q * jnp.exp2(a)

    t_idx = lax.broadcasted_iota(jnp.int32, (SUBLANES, LANES), 0)
    for j in range(blk // sub):
        r0 = j * sub
        a_end = a[r0 + sub - 1:r0 + sub, :]
        inter = _dot_nt(q_in[r0:r0 + sub].astype(BF16), st.astype(BF16))
        k_hat = jnp.exp2(a_end - c[r0:r0 + sub])
        v_j = v[r0:r0 + sub]
        st = st * jnp.exp2(a_end) + _dot_tn(v_j.astype(BF16), k_hat.astype(BF16))
        for part in range(sub // SUBLANES):
            t0 = r0 + part * SUBLANES
            a_t = a[t0:t0 + SUBLANES]
            q_t = q[t0:t0 + SUBLANES]
            o_t = inter[part * SUBLANES:(part + 1) * SUBLANES]
            for s in range((part + 1) * SUBLANES):
                arg = a_t - c[r0 + s:r0 + s + 1, :]
                if s >= part * SUBLANES:
                    arg = jnp.where(t_idx >= s - part * SUBLANES, arg, NEG_INF)
                z = q_t * jnp.exp2(arg)
                o_t = o_t + jnp.sum(z, axis=-1, keepdims=True) * v[r0 + s:r0 + s + 1, :]
            o_t = o_t * lax.rsqrt(jnp.mean(o_t * o_t, axis=-1, keepdims=True) + EPS)
            o_ref[t0:t0 + SUBLANES, cols] = o_t * gn * gate[t0:t0 + SUBLANES]
    return st


def _block_ref_rows(a, level, row8):
    rows = a.shape[0]
    half = level // 2
    if half >= SUBLANES:
        return jnp.concatenate([jnp.broadcast_to(a[b0 + half - 1:b0 + half, :], (level, LANES))
                                for b0 in range(0, rows, level)], axis=0)
    if level == 2:
        return jnp.where((row8 & 1) == 1, pltpu.roll(a, 1, 0), a)
    pieces = []
    for v0 in range(0, rows, SUBLANES):
        picks = [jnp.broadcast_to(a[v0 + b0 + half - 1:v0 + b0 + half, :], (SUBLANES, LANES))
                 for b0 in range(0, SUBLANES, level)]
        piece = picks[-1]
        for i in range(len(picks) - 2, -1, -1):
            piece = jnp.where(row8[:SUBLANES] < (i + 1) * level, picks[i], piece)
        pieces.append(piece)
    return jnp.concatenate(pieces, axis=0)


def _hgrn_head_block(cols, q_ref, f_ref, i_ref, g_ref, hl_ref, gn_ref, o_ref, st, *, blk):
    hl = hl_ref[:, cols]
    hmax = jnp.max(hl, axis=0, keepdims=True)
    ex = jnp.exp(hl - hmax)
    p = ex / jnp.sum(ex, axis=0, keepdims=True)
    lower = (p[0:1] + p[1:2]) - p[0:1]

    f = lower + (1.0 - lower) * _sigmoid(f_ref[:, cols])
    kk = 1.0 - f
    q = _silu(q_ref[:, cols].astype(F32))
    vb = i_ref[:, cols]
    a = _scan_rows(jnp.log2(f), jnp.add, 0.0)
    c = a - jnp.log2(jnp.maximum(kk, 0.0))
    row8 = lax.broadcasted_iota(jnp.int32, (blk, LANES), 0) & (SUBLANES - 1)
    group = (lax.broadcasted_iota(jnp.int32, (blk, blk), 0)
             ^ lax.broadcasted_iota(jnp.int32, (blk, blk), 1))

    scores = None
    level = blk
    while level >= 2:
        half = level // 2
        if half >= SUBLANES:
            zero = jnp.zeros((half, LANES), F32)
            qs, ks = [], []
            for b0 in range(0, blk, level):
                ref = a[b0 + half - 1:b0 + half, :]
                qs += [zero, q[b0 + half:b0 + level] * jnp.exp2(a[b0 + half:b0 + level] - ref)]
                ks += [jnp.exp2(ref - c[b0:b0 + half]), zero]
            q_l = jnp.concatenate(qs, axis=0)
            k_l = jnp.concatenate(ks, axis=0)
        else:
            ref = _block_ref_rows(a, level, row8)
            upper = (row8 & (level - 1)) >= half
            q_l = jnp.where(upper, q * jnp.exp2(a - ref), 0.0)
            k_l = jnp.where(upper, 0.0, jnp.exp2(ref - c))
        r = _dot_nt(q_l.astype(BF16), k_l.astype(BF16))
        scores = r if scores is None else jnp.where(group < level, r, scores)
        level = half
    scores = jnp.where(group < 1, _dot_nt(q.astype(BF16), kk.astype(BF16)), scores)

    a_end = a[blk - 1:blk, :]
    o = _dot_nt((q * jnp.exp2(a)).astype(BF16), st.astype(BF16)) + _dot(scores.astype(BF16), vb)
    o = o * lax.rsqrt(jnp.mean(o * o, axis=-1, keepdims=True) + EPS)
    o_ref[:, cols] = o * gn_ref[:, cols] * _silu(g_ref[:, cols].astype(F32))
    return st * jnp.exp2(a_end) + _dot_tn(vb, jnp.exp2(a_end - c).astype(BF16))


def _hgrn_body(q_ref, f_ref, i_ref, g_ref, hl_ref, s0_ref, gn_ref, o_ref, s_ref, st_ref, *, blk, sub, bb):
    @pl.when(pl.program_id(1) == 0)
    def _():
        for g, h in [(g, h) for g in range(bb) for h in range(HG_HEADS)]:
            st_ref[g, h] = jnp.transpose(s0_ref[g, h])

    if blk == HG_BLOCK:
        head, per_iter = functools.partial(_hgrn_head_block, blk=blk), HG_UNROLL_BLOCK // bb
    else:
        head, per_iter = functools.partial(_hgrn_head, blk=blk, sub=sub), HG_UNROLL_STEP // bb

    def heads(it, carry):
        for g, u in [(g, u) for g in range(bb) for u in range(per_iter)]:
            h = it * per_iter + u
            cols = pl.ds(pl.multiple_of(h * HEAD_DIM, HEAD_DIM), HEAD_DIM)
            st_ref[g, h] = head(cols, q_ref.at[g], f_ref.at[g], i_ref.at[g], g_ref.at[g], hl_ref, gn_ref,
                                o_ref.at[g], st_ref[g, h])
        return carry

    lax.fori_loop(0, HG_HEADS // per_iter, heads, 0)

    @pl.when(pl.program_id(1) == pl.num_programs(1) - 1)
    def _():
        for g, h in [(g, h) for g in range(bb) for h in range(HG_HEADS)]:
            s_ref[g, h] = jnp.transpose(st_ref[g, h])


def _hgrn(proj, f_pre, hg_lower, state, gn, bb):
    b, length, _ = proj.shape
    blk = min(length, HG_BLOCK)
    sub = math.gcd(length, HG_SUB)
    width = HG_HEADS * HEAD_DIM

    def col(i):
        return pl.BlockSpec((bb, blk, width), lambda bi, c: (bi, c, i))

    full = lambda shape: pl.BlockSpec(shape, lambda bi, c: (0,) * len(shape))
    state_spec = pl.BlockSpec((bb, HG_HEADS, HEAD_DIM, HEAD_DIM), lambda bi, c: (bi, 0, 0, 0))
    return pl.pallas_call(
        functools.partial(_hgrn_body, blk=blk, sub=sub, bb=bb),
        grid=(b // bb, length // blk),
        in_specs=[col(0), col(0), col(1), col(2), full(hg_lower.shape), state_spec, full((1, width))],
        out_specs=[pl.BlockSpec((bb, blk, width), lambda bi, c: (bi, c, 0)), state_spec],
        out_shape=[
            jax.ShapeDtypeStruct((b, length, width), F32),
            jax.ShapeDtypeStruct(state.shape, F32),
        ],
        scratch_shapes=[pltpu.VMEM((bb, HG_HEADS, HEAD_DIM, HEAD_DIM), F32)],
        compiler_params=_cparams(("parallel", "arbitrary")),
        name="hgrn2",
    )(proj, f_pre, proj, proj, hg_lower, state, gn)


def _prep_weights(p):
    w_in_ab = p["w_in_ab"][0]
    w_in_c = p["w_in_c"][0]
    width = HG_HEADS * HEAD_DIM
    w_gates = jnp.zeros((D_MODEL, LANES), BF16).at[:, :2 * ML_HEADS].set(w_in_ab[:, AB_MAIN:].astype(BF16))
    w_router = jnp.zeros((LANES, D_MODEL), F32).at[:N_EXPERTS, :].set(p["w_router"][0].T)
    w_router_hi = w_router.astype(BF16)
    w_router_lo = (w_router - w_router_hi.astype(F32)).astype(BF16)
    b_router = jnp.broadcast_to(p["b_router"][0][:, None], (N_EXPERTS, LANES))
    return {
        "w_ab": w_in_ab[:, :AB_MAIN].astype(BF16),
        "w_ab_gates": w_gates,
        "w_out_ab": p["w_out_ab"][0].astype(BF16),
        "w_c": jnp.concatenate([w_in_c[:, :width], w_in_c[:, 2 * width:]], axis=1).astype(BF16),
        "w_c_forget": w_in_c[:, width:2 * width].astype(BF16),
        "w_out_c": p["w_out_c"][0].astype(BF16),
        "w_ffn_gate": p["w_ffn_gate"][0].astype(BF16),
        "w_ffn_up": p["w_ffn_up"][0].astype(BF16),
        "w_ffn_down": p["w_ffn_down"][0].astype(BF16),
        "w_router_hi": w_router_hi,
        "w_router_lo": w_router_lo,
        "b_router": b_router,
        "w_moe_gate": p["w_moe_gate"][0].astype(BF16),
        "w_moe_up": p["w_moe_up"][0].astype(BF16),
        "w_moe_down": p["w_moe_down"][0].astype(BF16),
    }


def _mixers(x, pos0, ret_s, mc_s, mn_s, mm_s, conv_s, hg_s, p, w):
    b, length, d = x.shape
    t = b * length
    tm = min(512, t)
    tm_mm = min(1024, t)
    prompt = length >= RET_CHUNK
    bb = 1 if prompt else 8
    assert t % tm_mm == 0 and t % tm == 0 and b % bb == 0 and b % 2 == 0, (b, length)
    row = lambda v: v.reshape(1, -1)
    x0 = x.reshape(t, d)

    proj, gates = _norm_matmul(x0, row(p["ln_mix"][0]), w["w_ab"], w["w_ab_gates"], tm_mm, 1024)
    proj = proj.reshape(b, length, AB_MAIN)
    o_ret, ret_new = _retention(proj, pos0, ret_s, row(p["ret_gn"][0]), 2 if prompt else bb)
    h_m, c_new, n_new, m_new, conv_new = _mlstm(
        proj, gates.reshape(b, length, LANES), p["b_gates_ab"][0], p["conv_w_ab"][0], row(p["conv_b_ab"][0]),
        conv_s, mc_s, mn_s, mm_s, row(p["ml_gn"][0]), bb)
    half = RET_HEADS * HEAD_DIM
    x2 = _ffn([o_ret.reshape(t, half), h_m.reshape(t, half)], [w["w_out_ab"][:half], w["w_out_ab"][half:]], x0,
              row(p["ln_ffn"][0]), w["w_ffn_gate"], w["w_ffn_up"], w["w_ffn_down"], tm, D_FF // 2)

    proj_c, f_pre = _norm_matmul(x2, row(p["ln_mix"][1]), w["w_c"], w["w_c_forget"], tm_mm, 1024)
    o_hg, hg_new = _hgrn(proj_c.reshape(b, length, 3 * d), f_pre.reshape(b, length, d), p["hg_lower"], hg_s,
                         row(p["hg_gn"][0]), bb)
    states = (ret_new[None], c_new[None], n_new[None], m_new[None], conv_new[None], hg_new[None])
    return x2, o_hg.reshape(t, d), states


def _experts(groups, shapes, p, w):
    row = lambda v: v.reshape(1, -1)
    ys = _moe([o for _, o in groups], w["w_out_c"], [x for x, _ in groups], row(p["ln_ffn"][1]), w,
              row(p["ln_final"]))
    return [y.reshape(shape) for y, shape in zip(ys, shapes)]


def _trunk(x, pos0, ret_s, mc_s, mn_s, mm_s, conv_s, hg_s, p, w):
    x2, o_hg, states = _mixers(x, pos0, ret_s, mc_s, mn_s, mm_s, conv_s, hg_s, p, w)
    (y,) = _experts([(x2, o_hg)], [x.shape], p, w)
    return (y,) + states


def kernel(x_prompt, x_sample, state_ret, state_mlstm_c, state_mlstm_n, state_mlstm_m, state_conv, state_hgrn,
           ln_mix, ln_ffn, ln_final, w_in_ab, b_gates_ab, conv_w_ab, conv_b_ab, ret_gn, ml_gn, w_out_ab,
           w_in_c, hg_lower, hg_gn, w_out_c, w_ffn_gate, w_ffn_up, w_ffn_down,
           w_router, b_router, w_moe_gate, w_moe_up, w_moe_down):
    p = {"ln_mix": ln_mix, "ln_ffn": ln_ffn, "ln_final": ln_final,
         "w_in_ab": w_in_ab, "b_gates_ab": b_gates_ab, "conv_w_ab": conv_w_ab, "conv_b_ab": conv_b_ab,
         "ret_gn": ret_gn, "ml_gn": ml_gn, "w_out_ab": w_out_ab,
         "w_in_c": w_in_c, "hg_lower": hg_lower, "hg_gn": hg_gn, "w_out_c": w_out_c,
         "w_ffn_gate": w_ffn_gate, "w_ffn_up": w_ffn_up, "w_ffn_down": w_ffn_down,
         "w_router": w_router, "b_router": b_router,
         "w_moe_gate": w_moe_gate, "w_moe_up": w_moe_up, "w_moe_down": w_moe_down}
    w = _prep_weights(p)
    bp = x_prompt.shape[0]
    zeros = lambda *shape: jnp.zeros(shape, F32)
    x2_p, o_p, states_p = _mixers(
        x_prompt, 0,
        zeros(bp, RET_HEADS, HEAD_DIM, HEAD_DIM), zeros(bp, ML_HEADS, HEAD_DIM, HEAD_DIM),
        zeros(bp, ML_HEADS, HEAD_DIM), jnp.full((bp, ML_HEADS), NEG_INF, F32),
        zeros(bp, ML_CONV - 1, 2 * ML_HEADS * HEAD_DIM), zeros(bp, HG_HEADS, HEAD_DIM, HEAD_DIM), p, w)
    x2_s, o_s, states_s = _mixers(
        x_sample, 16384,
        state_ret[0], state_mlstm_c[0], state_mlstm_n[0], state_mlstm_m[0], state_conv[0], state_hgrn[0], p, w)
    y_p, y_s = _experts([(x2_p, o_p), (x2_s, o_s)], [x_prompt.shape, x_sample.shape], p, w)
    out = [y_p, y_s]
    for a, b in zip(states_p, states_s):
        out += [a, b]
    return tuple(out)
```

```python
import functools
import math

import numpy as np
import jax
import jax.numpy as jnp
from jax import lax
from jax.experimental import pallas as pl
from jax.experimental.pallas import tpu as pltpu

F32 = jnp.float32
BF16 = jnp.bfloat16

D_MODEL = 1024
HEAD_DIM = 128
RET_HEADS = 4
ML_HEADS = 4
ML_CONV = 4
HG_HEADS = 8
D_FF = 2816
N_EXPERTS = 8
ROPE_BASE = 10000.0
EPS = 1e-6
RET_CHUNK = 128
ML_CHUNK = 128
HG_BLOCK = 128
HG_SUB = 16

LANES = 128
SUBLANES = 8
AB_MAIN = 4096
VMEM_LIMIT = 56 * 1024 * 1024

NEG_INF = float("-inf")


def _cparams(sem):
    return pltpu.CompilerParams(dimension_semantics=sem, vmem_limit_bytes=VMEM_LIMIT)


def _sigmoid(x):
    return 1.0 / (1.0 + jnp.exp(-x))


def _silu(x):
    return x * _sigmoid(x)


def _rmsnorm_rows(x, g):
    ms = jnp.mean(x * x, axis=-1, keepdims=True)
    return x * lax.rsqrt(ms + EPS) * g


def _dot(a, b):
    return jnp.dot(a, b, preferred_element_type=F32)


def _dot_nt(a, b):
    return lax.dot_general(a, b, (((1,), (1,)), ((), ())), preferred_element_type=F32)


def _dot_tn(a, b):
    return lax.dot_general(a, b, (((0,), (0,)), ((), ())), preferred_element_type=F32)


def _pad_rows(x, rows):
    if x.shape[0] == rows:
        return x
    return jnp.concatenate([x, jnp.zeros((rows - x.shape[0], x.shape[1]), x.dtype)], axis=0)


def _norm_matmul_body(x_ref, g_ref, w_ref, ws_ref, o_ref, os_ref, h_ref):
    @pl.when(pl.program_id(1) == 0)
    def _():
        h = _rmsnorm_rows(x_ref[...], g_ref[...]).astype(BF16)
        h_ref[...] = h
        os_ref[...] = _dot(h, ws_ref[...])

    o_ref[...] = _dot(h_ref[...], w_ref[...]).astype(BF16)


def _norm_matmul(x, g, w, w_side, tm, tn):
    t, d = x.shape
    n, ns = w.shape[1], w_side.shape[1]
    return pl.pallas_call(
        _norm_matmul_body,
        grid=(t // tm, n // tn),
        in_specs=[
            pl.BlockSpec((tm, d), lambda i, j: (i, 0)),
            pl.BlockSpec((1, d), lambda i, j: (0, 0)),
            pl.BlockSpec((d, tn), lambda i, j: (0, j)),
            pl.BlockSpec((d, ns), lambda i, j: (0, 0)),
        ],
        out_specs=[pl.BlockSpec((tm, tn), lambda i, j: (i, j)), pl.BlockSpec((tm, ns), lambda i, j: (i, 0))],
        out_shape=[jax.ShapeDtypeStruct((t, n), BF16), jax.ShapeDtypeStruct((t, ns), F32)],
        scratch_shapes=[pltpu.VMEM((tm, d), BF16)],
        compiler_params=_cparams(("parallel", "arbitrary")),
        name="norm_matmul",
    )(x, g, w, w_side)


def _mixer_residual(x_ref, a_refs, w_refs):
    x = x_ref[...]
    for a_ref, w_ref in zip(a_refs, w_refs):
        x = x + _dot(a_ref[...].astype(BF16), w_ref[...])
    return x


def _ffn_body(*refs, n_in):
    a_refs, w_refs = refs[:n_in], refs[n_in:2 * n_in]
    x_ref, g_ref, wg_ref, wu_ref, wd_ref, o_ref, h_ref, x1_ref = refs[2 * n_in:]
    f = pl.program_id(1)

    @pl.when(f == 0)
    def _():
        x1 = _mixer_residual(x_ref, a_refs, w_refs)
        x1_ref[...] = x1
        h_ref[...] = _rmsnorm_rows(x1, g_ref[...]).astype(BF16)

    h = h_ref[...]
    a = _dot(h, wg_ref[...])
    u = _dot(h, wu_ref[...])
    act = (_silu(a) * u).astype(BF16)
    x1_ref[...] += _dot(act, wd_ref[...])

    @pl.when(f == pl.num_programs(1) - 1)
    def _():
        o_ref[...] = x1_ref[...]


def _ffn(acts, weights, x, g, wg, wu, wd, tm, tf):
    t, d = x.shape
    dff = wg.shape[1]
    n_in = len(acts)
    in_specs = [pl.BlockSpec((tm, a.shape[1]), lambda i, f: (i, 0)) for a in acts]
    in_specs += [pl.BlockSpec(w.shape, lambda i, f: (0, 0)) for w in weights]
    in_specs += [
        pl.BlockSpec((tm, d), lambda i, f: (i, 0)),
        pl.BlockSpec((1, d), lambda i, f: (0, 0)),
        pl.BlockSpec((d, tf), lambda i, f: (0, f)),
        pl.BlockSpec((d, tf), lambda i, f: (0, f)),
        pl.BlockSpec((tf, d), lambda i, f: (f, 0)),
    ]
    return pl.pallas_call(
        functools.partial(_ffn_body, n_in=n_in),
        grid=(t // tm, dff // tf),
        in_specs=in_specs,
        out_specs=pl.BlockSpec((tm, d), lambda i, f: (i, 0)),
        out_shape=jax.ShapeDtypeStruct((t, d), F32),
        scratch_shapes=[pltpu.VMEM((tm, d), BF16), pltpu.VMEM((tm, d), F32)],
        compiler_params=_cparams(("parallel", "arbitrary")),
        name="ffn",
    )(*acts, *weights, x, g, wg, wu, wd)


MOE_TB = 512
MOE_ALIGN = 16
MOE_BURST = 4
MOE_TILE = 256
MOE_CHUNKS_PER_TILE = MOE_TILE // MOE_ALIGN


def _moe_block_rows(tb):
    return 2 * tb + N_EXPERTS * MOE_ALIGN


def _group_block_specs(shape, blocks):
    specs, first = [], 0
    for nb in blocks:
        specs.append(pl.BlockSpec(shape, lambda i, *_, first=first, nb=nb: (jnp.clip(i - first, 0, nb - 1), 0)))
        first += nb
    return specs


def _route_body(*refs, blocks):
    n = len(blocks)
    a_refs, wo_ref, x_refs = refs[:n], refs[n], refs[n + 1:2 * n + 1]
    g_ref, whi_ref, wlo_ref, b_ref, tri_ref, x3_ref, h_ref, col_ref, rowf_ref, cnt_ref = refs[2 * n + 1:]
    i = pl.program_id(0)
    act, x_in = a_refs[0][...], x_refs[0][...]
    first = blocks[0]
    for a_ref, x_ref, nb in zip(a_refs[1:], x_refs[1:], blocks[1:]):
        act = jnp.where(i >= first, a_ref[...], act)
        x_in = jnp.where(i >= first, x_ref[...], x_in)
        first += nb
    x3 = x_in + _dot(act.astype(BF16), wo_ref[...])
    x3_ref[...] = x3
    h = _rmsnorm_rows(x3, g_ref[...])
    h_hi = h.astype(BF16)
    h_ref[...] = h_hi
    h_lo = (h - h_hi.astype(F32)).astype(BF16)

    logits = (_dot_nt(whi_ref[...], h_hi) + _dot_nt(wlo_ref[...], h_hi) + _dot_nt(whi_ref[...], h_lo))
    logits = logits[:N_EXPERTS, :] + b_ref[:, 0:1]
    row = lax.broadcasted_iota(jnp.int32, logits.shape, 0)
    m1 = jnp.max(logits, axis=0, keepdims=True)
    i1 = jnp.min(jnp.where(logits == m1, row, N_EXPERTS), axis=0, keepdims=True)
    lg2 = jnp.where(row == i1, NEG_INF, logits)
    m2 = jnp.max(lg2, axis=0, keepdims=True)
    i2 = jnp.min(jnp.where(lg2 == m2, row, N_EXPERTS), axis=0, keepdims=True)
    e = jnp.exp(m2 - m1)
    w1 = 1.0 / (1.0 + e)
    w2 = e / (1.0 + e)

    sel = jnp.where(row == i1, 1.0, jnp.where(row == i2, 1.0, 0.0))
    before = _dot(sel.astype(BF16), tri_ref[...])
    n = jnp.broadcast_to(jnp.sum(sel, axis=1, keepdims=True), (N_EXPERTS, LANES))
    padded = jnp.floor((n + (MOE_ALIGN - 1.0)) * (1.0 / MOE_ALIGN)) * MOE_ALIGN
    base = _scan_rows(padded, jnp.add, 0.0) - padded
    pos = base[:, 0:1] + before
    loc1 = jnp.sum(jnp.where(row == i1, pos, 0.0), axis=0, keepdims=True)
    loc2 = jnp.sum(jnp.where(row == i2, pos, 0.0), axis=0, keepdims=True)
    rowf = jnp.where(row == 0, loc1, jnp.where(row == 1, loc2, jnp.where(row == 2, w1,
                                                                          jnp.where(row == 3, w2, 0.0))))
    rowf_ref[0] = rowf
    for c in range(rowf.shape[1] // LANES):
        col_ref[c * LANES:(c + 1) * LANES, :] = jnp.transpose(_pad_rows(rowf[:, c * LANES:(c + 1) * LANES], LANES))
    cnt_ref[0] = n


def _route(acts, w_out, xs, g, w_hi, w_lo, b, tb):
    d = xs[0].shape[1]
    blocks = [x.shape[0] // tb for x in xs]
    nb = sum(blocks)
    t = nb * tb
    tri = jnp.asarray(np.triu(np.ones((tb, tb), np.float32), 1), BF16)
    full = lambda shape: pl.BlockSpec(shape, lambda i: (0,) * len(shape))
    return pl.pallas_call(
        functools.partial(_route_body, blocks=blocks),
        grid=(nb,),
        in_specs=(_group_block_specs((tb, acts[0].shape[1]), blocks) + [full(w_out.shape)]
                  + _group_block_specs((tb, d), blocks)
                  + [full((1, d)), full((LANES, d)), full((LANES, d)), full((SUBLANES, LANES)), full((tb, tb))]),
        out_specs=[
            pl.BlockSpec((tb, d), lambda i: (i, 0)),
            pl.BlockSpec((tb, d), lambda i: (i, 0)),
            pl.BlockSpec((tb, LANES), lambda i: (i, 0)),
            pl.BlockSpec((1, SUBLANES, tb), lambda i: (i, 0, 0)),
            pl.BlockSpec((1, SUBLANES, LANES), lambda i: (i, 0, 0)),
        ],
        out_shape=[
            jax.ShapeDtypeStruct((t, d), F32),
            jax.ShapeDtypeStruct((t, d), BF16),
            jax.ShapeDtypeStruct((t, LANES), F32),
            jax.ShapeDtypeStruct((nb, SUBLANES, tb), F32),
            jax.ShapeDtypeStruct((nb, SUBLANES, LANES), F32),
        ],
        compiler_params=_cparams(("parallel",)),
        name="route",
    )(*acts, w_out, *xs, g, w_hi, w_lo, b, tri)


def _moe_plan(cnt):
    n = cnt[:, :N_EXPERTS, 0].astype(jnp.int32)
    chunks = (n + (MOE_ALIGN - 1)) // MOE_ALIGN
    src = jnp.cumsum(chunks, axis=1) - chunks
    seg_chunks = jnp.sum(chunks, axis=0)
    seg_tiles = (seg_chunks + (MOE_CHUNKS_PER_TILE - 1)) // MOE_CHUNKS_PER_TILE
    seg_end = jnp.cumsum(seg_tiles)
    seg_first = (seg_end - seg_tiles) * MOE_CHUNKS_PER_TILE
    dst = seg_first[None, :] + jnp.cumsum(chunks, axis=0) - chunks
    flat = lambda a: a.reshape(-1).astype(jnp.int32)
    return {"src": flat(src), "dst": flat(dst), "chunks": flat(chunks),
            "first_tile": flat(seg_end - seg_tiles), "tile_count": flat(seg_tiles),
            "tail_start": flat(seg_first + seg_chunks), "tail_count": flat(seg_tiles * MOE_CHUNKS_PER_TILE - seg_chunks)}


def _chunk_rows(chunk, n=1):
    return pl.ds(pl.multiple_of(chunk * MOE_ALIGN, MOE_ALIGN), n * MOE_ALIGN)


def _for_block_copies(nch_ref, blk, fn):
    for e in range(N_EXPERTS):
        count = nch_ref[blk * N_EXPERTS + e]
        bursts = count // MOE_BURST
        lax.fori_loop(0, bursts, lambda i, carry, e=e: (fn(e, i * MOE_BURST, MOE_BURST), carry)[1], 0)
        lax.fori_loop(bursts * MOE_BURST, count, lambda c, carry, e=e: (fn(e, c, 1), carry)[1], 0)


def _block_copy_counts(nch_ref, blk):
    bursts = singles = 0
    for e in range(N_EXPERTS):
        count = nch_ref[blk * N_EXPERTS + e]
        bursts = bursts + count // MOE_BURST
        singles = singles + lax.rem(count, MOE_BURST)
    return bursts, singles


def _wait_block_copies(nch_ref, blk, copy):
    bursts, singles = _block_copy_counts(nch_ref, blk)
    lax.fori_loop(0, bursts, lambda i, carry: (copy(MOE_BURST).wait(), carry)[1], 0)
    lax.fori_loop(0, singles, lambda i, carry: (copy(1).wait(), carry)[1], 0)


def _gather_body(src_ref, dst_ref, nch_ref, tail_ref, ntail_ref, last_ref, h_ref, rowf_ref, hs_ref, z_ref, sem,
                 *, n_tiles):
    b = pl.program_id(0)
    slot = lax.rem(b, 2)
    tb = h_ref.shape[0]
    loc = rowf_ref[0].astype(jnp.int32)
    r = lax.broadcasted_iota(jnp.int32, (z_ref.shape[1], tb), 0)
    onehot = jnp.where(r == loc[0:1, :], 1.0, jnp.where(r == loc[1:2, :], 1.0, 0.0)).astype(BF16)
    z_ref[slot] = _dot(onehot, h_ref[...]).astype(BF16)

    def copy(buf, src_chunk, dst_chunk, n=1):
        return pltpu.make_async_copy(z_ref.at[buf, _chunk_rows(src_chunk, n)], hs_ref.at[_chunk_rows(dst_chunk, n)],
                                     sem.at[buf])

    def drain(blk, buf):
        _wait_block_copies(nch_ref, blk, lambda n: copy(buf, 0, 0, n))

    @pl.when(b > 0)
    def _():
        drain(b - 1, 1 - slot)

    _for_block_copies(nch_ref, b, lambda e, c, n: copy(slot, src_ref[b * N_EXPERTS + e] + c,
                                                       dst_ref[b * N_EXPERTS + e] + c, n).start())

    @pl.when(b == pl.num_programs(0) - 1)
    def _():
        drain(b, slot)
        z_ref[0, 0:MOE_ALIGN, :] = jnp.zeros((MOE_ALIGN, z_ref.shape[2]), BF16)
        past = last_ref[0]
        total = n_tiles * MOE_CHUNKS_PER_TILE
        for e in range(N_EXPERTS):
            lax.fori_loop(0, ntail_ref[e], lambda c, carry, e=e: (copy(0, 0, tail_ref[e] + c).start(), carry)[1], 0)
        lax.fori_loop(past, total, lambda c, carry: (copy(0, 0, c).start(), carry)[1], 0)
        for e in range(N_EXPERTS):
            lax.fori_loop(0, ntail_ref[e], lambda c, carry: (copy(0, 0, 0).wait(), carry)[1], 0)
        lax.fori_loop(past, total, lambda c, carry: (copy(0, 0, 0).wait(), carry)[1], 0)


def _moe_gather(plan, h, rowf, n_tiles, tb):
    t, d = h.shape
    rows = _moe_block_rows(tb)
    past = (plan["first_tile"][-1:] + plan["tile_count"][-1:]) * MOE_CHUNKS_PER_TILE
    return pl.pallas_call(
        functools.partial(_gather_body, n_tiles=n_tiles),
        grid_spec=pltpu.PrefetchScalarGridSpec(
            num_scalar_prefetch=6,
            grid=(t // tb,),
            in_specs=[
                pl.BlockSpec((tb, d), lambda i, *_: (i, 0)),
                pl.BlockSpec((1, SUBLANES, tb), lambda i, *_: (i, 0, 0)),
            ],
            out_specs=pl.BlockSpec(memory_space=pl.ANY),
            scratch_shapes=[pltpu.VMEM((2, rows, d), BF16), pltpu.SemaphoreType.DMA((2,))],
        ),
        out_shape=jax.ShapeDtypeStruct((n_tiles * MOE_TILE, d), BF16),
        compiler_params=_cparams(("arbitrary",)),
        name="moe_gather",
    )(plan["src"], plan["dst"], plan["chunks"], plan["tail_start"], plan["tail_count"], past, h, rowf)


W_CHUNK = 256


def _expert_ffn_body(first_ref, count_ref, xs_ref, wg_hbm, wu_hbm, wd_hbm, ys_ref,
                     wg_b, wu_b, wd_b, stage_c, stage_r, x_buf, y_buf, w_sem, in_sem, out_sem, *, n_tiles, halves):
    e = pl.program_id(0)
    last = pl.num_programs(0) - 1
    first = first_ref[e]
    count = count_ref[e]
    cur = lax.rem(e, 2)
    dff = wg_b.shape[2]
    tf = dff // halves
    per = dff // W_CHUNK
    n_chunks = 3 * per

    def span(j):
        return pl.ds(pl.multiple_of(j * W_CHUNK, W_CHUNK), W_CHUNK)

    def fetch(kind, expert, j, slot):
        if kind == 2:
            return pltpu.make_async_copy(wd_hbm.at[expert, span(j), :], stage_r.at[slot], w_sem.at[slot])
        return pltpu.make_async_copy((wg_hbm, wu_hbm)[kind].at[expert, :, span(j)], stage_c.at[slot], w_sem.at[slot])

    def for_kind(c, fn):
        kind = c // per
        for k in range(3):
            @pl.when(kind == k)
            def _(k=k):
                fn(k, c - k * per)

    def convert_step(expert, dst, c):
        slot = lax.rem(c, 2)

        @pl.when(c + 1 < n_chunks)
        def _():
            for_kind(c + 1, lambda k, j: fetch(k, expert, j, 1 - slot).start())

        def land(k, j):
            fetch(k, expert, j, slot).wait()
            if k == 2:
                wd_b[dst, span(j), :] = stage_r[slot].astype(BF16)
            else:
                (wg_b, wu_b)[k][dst, :, span(j)] = stage_c[slot].astype(BF16)

        for_kind(c, land)

    def maybe_convert(c):
        @pl.when(c < n_chunks)
        def _():
            convert_step(e + 1, 1 - cur, c)
        return jnp.minimum(c + 1, n_chunks)

    @pl.when(e == 0)
    def _():
        fetch(0, 0, 0, 0).start()
        lax.fori_loop(0, n_chunks, lambda c, carry: (convert_step(0, 0, c), carry)[1], 0)

    @pl.when(e < last)
    def _():
        fetch(0, e + 1, 0, 0).start()

    def tile_rows(tile):
        return pl.ds(pl.multiple_of(tile * MOE_TILE, MOE_TILE), MOE_TILE)

    def load(j, buf):
        return pltpu.make_async_copy(xs_ref.at[tile_rows(first + j)], x_buf.at[buf], in_sem.at[buf])

    def store(tile, buf):
        return pltpu.make_async_copy(y_buf.at[buf], ys_ref.at[tile_rows(tile)], out_sem.at[buf])

    @pl.when(count > 0)
    def _():
        load(0, 0).start()

    def tile_step(j, c):
        buf = lax.rem(j, 2)

        @pl.when(j + 1 < count)
        def _():
            load(j + 1, 1 - buf).start()

        load(j, buf).wait()
        xs = x_buf[buf]
        y = None
        for f in range(halves):
            a = _dot(xs, wg_b[cur, :, f * tf:(f + 1) * tf])
            u = _dot(xs, wu_b[cur, :, f * tf:(f + 1) * tf])
            part = _dot((_silu(a) * u).astype(BF16), wd_b[cur, f * tf:(f + 1) * tf, :])
            y = part if y is None else y + part

        @pl.when(j >= 2)
        def _():
            store(first, buf).wait()

        y_buf[buf] = y.astype(BF16)
        store(first + j, buf).start()
        return maybe_convert(maybe_convert(c))

    c_done = lax.fori_loop(0, count, tile_step, jnp.where(e < last, 0, n_chunks))
    lax.fori_loop(c_done, n_chunks, lambda c, carry: (convert_step(e + 1, 1 - cur, c), carry)[1], 0)

    for back in (2, 1):
        @pl.when(count >= back)
        def _(back=back):
            store(first, lax.rem(count - back, 2)).wait()

    @pl.when(e == last)
    def _():
        end = first + count
        y_buf[0] = jnp.zeros(y_buf.shape[1:], BF16)
        lax.fori_loop(end, n_tiles, lambda t, c: (store(t, 0).start(), c)[1], 0)
        lax.fori_loop(end, n_tiles, lambda t, c: (store(t, 0).wait(), c)[1], 0)


def _expert_ffn(first_tile, tile_count, xs, wg, wu, wd):
    rows, d = xs.shape
    dff = wg.shape[2]
    any_space = pl.BlockSpec(memory_space=pl.ANY)
    return pl.pallas_call(
        functools.partial(_expert_ffn_body, n_tiles=rows // MOE_TILE, halves=2),
        grid_spec=pltpu.PrefetchScalarGridSpec(
            num_scalar_prefetch=2,
            grid=(N_EXPERTS,),
            in_specs=[any_space, any_space, any_space, any_space],
            out_specs=any_space,
            scratch_shapes=[
                pltpu.VMEM((2, d, dff), BF16), pltpu.VMEM((2, d, dff), BF16), pltpu.VMEM((2, dff, d), BF16),
                pltpu.VMEM((2, d, W_CHUNK), F32), pltpu.VMEM((2, W_CHUNK, d), F32),
                pltpu.VMEM((2, MOE_TILE, d), BF16), pltpu.VMEM((2, MOE_TILE, d), BF16),
                pltpu.SemaphoreType.DMA((2,)), pltpu.SemaphoreType.DMA((2,)), pltpu.SemaphoreType.DMA((2,))],
        ),
        out_shape=jax.ShapeDtypeStruct((rows, d), BF16),
        compiler_params=_cparams(("arbitrary",)),
        name="expert_ffn",
    )(first_tile, tile_count, xs, wg, wu, wd)


def _combine_body(src_ref, dst_ref, nch_ref, x_ref, col_ref, gf_ref, ys_ref, *refs, blocks):
    o_refs, (y_ref, sem) = refs[:len(blocks)], refs[len(blocks):]
    b = pl.program_id(0)
    nb = pl.num_programs(0)
    slot = lax.rem(b, 2)

    def copy(buf, src_chunk, dst_chunk, n=1):
        return pltpu.make_async_copy(ys_ref.at[_chunk_rows(dst_chunk, n)], y_ref.at[buf, _chunk_rows(src_chunk, n)],
                                     sem.at[buf])

    def fetch(blk, buf):
        _for_block_copies(nch_ref, blk, lambda e, c, n: copy(buf, src_ref[blk * N_EXPERTS + e] + c,
                                                             dst_ref[blk * N_EXPERTS + e] + c, n).start())

    @pl.when(b == 0)
    def _():
        y_ref[...] = jnp.zeros_like(y_ref)
        fetch(0, 0)

    @pl.when(b + 1 < nb)
    def _():
        fetch(b + 1, 1 - slot)

    _wait_block_copies(nch_ref, b, lambda n: copy(slot, 0, 0, n))

    col = col_ref[...]
    loc1 = col[:, 0:1].astype(jnp.int32)
    loc2 = col[:, 1:2].astype(jnp.int32)
    r = lax.broadcasted_iota(jnp.int32, (col.shape[0], y_ref.shape[1]), 1)
    weights = jnp.where(r == loc1, col[:, 2:3], jnp.where(r == loc2, col[:, 3:4], 0.0)).astype(BF16)
    out = _rmsnorm_rows(x_ref[...] + _dot(weights, y_ref[slot]), gf_ref[...])
    first = 0
    for o_ref, n_blocks in zip(o_refs, blocks):
        @pl.when((b >= first) & (b < first + n_blocks))
        def _(o_ref=o_ref):
            o_ref[...] = out
        first += n_blocks


def _moe_combine(src, dst, nch, x, col, g_final, ys, tb, blocks):
    t, d = x.shape
    rows = _moe_block_rows(tb)
    return pl.pallas_call(
        functools.partial(_combine_body, blocks=blocks),
        grid_spec=pltpu.PrefetchScalarGridSpec(
            num_scalar_prefetch=3,
            grid=(t // tb,),
            in_specs=[
                pl.BlockSpec((tb, d), lambda i, *_: (i, 0)),
                pl.BlockSpec((tb, LANES), lambda i, *_: (i, 0)),
                pl.BlockSpec((1, d), lambda i, *_: (0, 0)),
                pl.BlockSpec(memory_space=pl.ANY),
            ],
            out_specs=_group_block_specs((tb, d), blocks),
            scratch_shapes=[pltpu.VMEM((2, rows, d), BF16), pltpu.SemaphoreType.DMA((2,))],
        ),
        out_shape=[jax.ShapeDtypeStruct((nb * tb, d), F32) for nb in blocks],
        compiler_params=_cparams(("arbitrary",)),
        name="moe_combine",
    )(src, dst, nch, x, col, g_final, ys)


def _moe(acts, w_out, xs, g, w, moe_weights, g_final):
    tb = MOE_TB
    assert all(x.shape[0] % tb == 0 for x in xs), [x.shape for x in xs]
    blocks = [x.shape[0] // tb for x in xs]
    nb = sum(blocks)
    t = nb * tb
    x, h, col, rowf, cnt = _route(acts, w_out, xs, g, w["w_router_hi"], w["w_router_lo"], w["b_router"], tb)
    n_tiles = -(-(2 * t + nb * N_EXPERTS * (MOE_ALIGN - 1)) // MOE_TILE) + N_EXPERTS
    plan = _moe_plan(cnt)
    xg = _moe_gather(plan, h, rowf, n_tiles, tb)
    yg = _expert_ffn(plan["first_tile"], plan["tile_count"], xg, *moe_weights)
    return _moe_combine(plan["src"], plan["dst"], plan["chunks"], x, col, g_final, yg, tb, blocks)


def _retention_tables(c_real, ct):
    h = np.arange(RET_HEADS, dtype=np.float64)
    log_gamma = np.log1p(-np.exp2(-5.0 - h))
    idx = np.arange(ct, dtype=np.float64)
    live = idx < c_real
    diff = idx[:, None] - idx[None, :]
    causal = (diff >= 0) & live[:, None] & live[None, :]
    decay = np.where(causal[None], np.exp(np.where(causal, diff, 0.0)[None] * log_gamma[:, None, None]), 0.0)
    q_dec = np.where(live[None], np.exp((idx + 1.0)[None] * log_gamma[:, None]), 0.0)
    k_dec = np.where(live[None], np.exp((c_real - 1.0 - idx)[None] * log_gamma[:, None]), 0.0)
    q_dec = np.broadcast_to(q_dec[..., None], (RET_HEADS, ct, LANES))
    k_dec = np.broadcast_to(k_dec[..., None], (RET_HEADS, ct, LANES))
    return (jnp.asarray(decay, F32), jnp.asarray(q_dec, F32), jnp.asarray(k_dec, F32))


def _rope_tables(pos0, length):
    half = HEAD_DIM // 2
    inv = ROPE_BASE ** (-np.arange(half, dtype=np.float64) / half)
    ang = (pos0 + np.arange(length, dtype=np.float64))[:, None] * inv[None, :]
    cos = np.concatenate([np.cos(ang), np.cos(ang)], axis=-1)
    sin = np.concatenate([-np.sin(ang), np.sin(ang)], axis=-1)
    return jnp.asarray(cos, F32), jnp.asarray(sin, F32)


def _retention_body(q_ref, k_ref, v_ref, g_ref, cos_ref, sin_ref, dec_ref, qd_ref, kd_ref, s0_ref, gn_ref,
                    o_ref, s_ref, *, c_real, ct, bb):
    @pl.when(pl.program_id(1) == 0)
    def _():
        s_ref[...] = s0_ref[...]

    cos = cos_ref[...]
    sin = sin_ref[...]
    for g, hd in [(g, hd) for g in range(bb) for hd in range(RET_HEADS)]:
        lo, hi = hd * HEAD_DIM, (hd + 1) * HEAD_DIM
        q = q_ref[g, :, lo:hi].astype(F32)
        k = k_ref[g, :, lo:hi].astype(F32)
        q = (q * cos + pltpu.roll(q, HEAD_DIM // 2, 1) * sin) * (HEAD_DIM ** -0.5)
        k = k * cos + pltpu.roll(k, HEAD_DIM // 2, 1) * sin
        v = v_ref[g, :, lo:hi].astype(F32)
        qb = q.astype(BF16)
        q_dec = qd_ref[hd, :c_real, :]
        state = s_ref[g, hd]

        s = _dot_nt(qb, _pad_rows(k, ct).astype(BF16)) * dec_ref[hd, :c_real, :]
        o = _dot(s.astype(BF16), _pad_rows(v, ct).astype(BF16)) + _dot(qb, state.astype(BF16)) * q_dec
        chunk_dec = q_dec[c_real - 1:c_real, :]
        s_ref[g, hd] = state * chunk_dec + _dot_tn((k * kd_ref[hd, :c_real, :]).astype(BF16), v.astype(BF16))

        o = o - jnp.mean(o, axis=-1, keepdims=True)
        o = o * lax.rsqrt(jnp.mean(o * o, axis=-1, keepdims=True) + EPS)
        o_ref[g, :, lo:hi] = _silu(g_ref[g, :, lo:hi].astype(F32)) * (o * gn_ref[:, lo:hi])


def _retention(proj, pos0, state, gn, bb):
    b, length, _ = proj.shape
    c_real = math.gcd(length, RET_CHUNK)
    ct = RET_CHUNK
    width = RET_HEADS * HEAD_DIM
    cos, sin = _rope_tables(pos0, length)
    decay, q_dec, k_dec = _retention_tables(c_real, ct)

    def col(i):
        return pl.BlockSpec((bb, c_real, width), lambda bi, c: (bi, c, i))

    full = lambda shape: pl.BlockSpec(shape, lambda bi, c: (0,) * len(shape))
    state_spec = pl.BlockSpec((bb, RET_HEADS, HEAD_DIM, HEAD_DIM), lambda bi, c: (bi, 0, 0, 0))
    return pl.pallas_call(
        functools.partial(_retention_body, c_real=c_real, ct=ct, bb=bb),
        grid=(b // bb, length // c_real),
        in_specs=[
            col(0), col(1), col(2), col(3),
            pl.BlockSpec((c_real, HEAD_DIM), lambda bi, c: (c, 0)),
            pl.BlockSpec((c_real, HEAD_DIM), lambda bi, c: (c, 0)),
            full((RET_HEADS, ct, ct)), full((RET_HEADS, ct, LANES)), full((RET_HEADS, ct, LANES)),
            state_spec,
            full((1, width)),
        ],
        out_specs=[
            pl.BlockSpec((bb, c_real, width), lambda bi, c: (bi, c, 0)),
            state_spec,
        ],
        out_shape=[
            jax.ShapeDtypeStruct((b, length, width), F32),
            jax.ShapeDtypeStruct(state.shape, F32),
        ],
        compiler_params=_cparams(("parallel", "arbitrary")),
        name="retention",
    )(proj, proj, proj, proj, cos, sin, decay, q_dec, k_dec, state, gn)


def _scan_rows(x, op, fill):
    rows = x.shape[0]
    row = lax.broadcasted_iota(jnp.int32, x.shape, 0)
    sh = 1
    while sh < rows:
        x = op(x, jnp.where(row >= sh, pltpu.roll(x, sh, 0), fill))
        sh *= 2
    return x


def _mlstm_body(q_ref, k_ref, v_ref, og_ref, gate_ref, bg_ref, cw_ref, cb_ref, cs_ref, c0_ref, n0_ref, m0_ref,
                gn_ref, o_ref, c_ref, n_ref, m_ref, conv_ref, xc_ref, *, c_real, ct, bb):
    tail = ML_CONV - 1

    @pl.when(pl.program_id(1) == 0)
    def _():
        c_ref[...] = c0_ref[...]
        n_ref[...] = n0_ref[...]
        m_ref[...] = m0_ref[...]
        xc_ref[:, SUBLANES - tail:SUBLANES, :] = cs_ref[...]

    for g in range(bb):
        _mlstm_sequence(q_ref.at[g], k_ref.at[g], v_ref.at[g], og_ref.at[g], gate_ref.at[g], bg_ref, cw_ref, cb_ref,
                        gn_ref, o_ref.at[g], c_ref.at[g], n_ref.at[g], m_ref.at[g], conv_ref.at[g], xc_ref.at[g],
                        c_real=c_real, ct=ct)


def _mlstm_sequence(q_ref, k_ref, v_ref, og_ref, gate_ref, bg_ref, cw_ref, cb_ref, gn_ref, o_ref, c_ref, n_ref, m_ref,
                    conv_ref, xc_ref, *, c_real, ct):
    width = ML_HEADS * HEAD_DIM
    tail = ML_CONV - 1

    xc_ref[SUBLANES:SUBLANES + c_real, :width] = q_ref[...].astype(F32)
    xc_ref[SUBLANES:SUBLANES + c_real, width:] = k_ref[...].astype(F32)
    y = cb_ref[...] + cw_ref[tail:tail + 1, :] * xc_ref[SUBLANES:SUBLANES + c_real, :]
    for j in range(tail):
        y = y + cw_ref[j:j + 1, :] * xc_ref[SUBLANES - tail + j:SUBLANES - tail + j + c_real, :]
    new_tail = xc_ref[c_real:c_real + SUBLANES, :]
    conv_ref[...] = new_tail[SUBLANES - tail:, :]
    xc_ref[0:SUBLANES, :] = new_tail
    qk = _silu(y)
    q_all = qk[:, :width]
    k_all = qk[:, width:] * (HEAD_DIM ** -0.5)
    v_all = v_ref[...].astype(F32)

    gates = gate_ref[...] + bg_ref[...]
    f_pre = pltpu.roll(gates, LANES - ML_HEADS, 1)
    log_f = jnp.minimum(f_pre, 0.0) - jnp.log(1.0 + jnp.exp(-jnp.abs(f_pre)))
    b_cum = _scan_rows(log_f, jnp.add, 0.0)
    src = gates - b_cum
    cmx = _scan_rows(src, jnp.maximum, NEG_INF)
    m_prev = m_ref[...]
    mx = jnp.maximum(m_prev, cmx)
    m_t = b_cum + mx
    w_prev = jnp.exp(m_prev - mx)
    e_neg_m = jnp.exp(-m_t)
    last = c_real - 1
    b_last = b_cum[last:last + 1, :]
    m_new = m_t[last:last + 1, :]
    w_end = jnp.exp(b_last + src - m_new)
    dec = jnp.exp(b_last + m_prev - m_new)
    m_ref[...] = m_new
    src_t = jnp.transpose(_pad_rows(src, ct))

    t_idx = lax.broadcasted_iota(jnp.int32, (c_real, ct), 0)
    s_idx = lax.broadcasted_iota(jnp.int32, (c_real, ct), 1)
    live = (s_idx <= t_idx) & (s_idx < c_real)

    for hd in range(ML_HEADS):
        lo, hi = hd * HEAD_DIM, (hd + 1) * HEAD_DIM
        q = q_all[:, lo:hi]
        k = k_all[:, lo:hi]
        v = v_all[:, lo:hi]
        qb = q.astype(BF16)
        c_state = c_ref[hd]
        n_state = n_ref[hd:hd + 1, :]
        wp = w_prev[:, hd:hd + 1]
        w = jnp.where(live, jnp.exp(src_t[hd:hd + 1, :] - mx[:, hd:hd + 1]), 0.0)
        s = _dot_nt(qb, _pad_rows(k, ct).astype(BF16)) * w
        num = _dot(s.astype(BF16), _pad_rows(v, ct).astype(BF16)) + wp * _dot(qb, c_state.astype(BF16))
        den = jnp.sum(s, axis=-1, keepdims=True) + wp * jnp.sum(q * n_state, axis=-1, keepdims=True)
        hh = num / jnp.maximum(jnp.abs(den), e_neg_m[:, hd:hd + 1])
        kw = k * w_end[:, hd:hd + 1]
        dec_h = dec[:, hd:hd + 1]
        c_ref[hd] = dec_h * c_state + _dot_tn(kw.astype(BF16), v.astype(BF16))
        n_ref[hd:hd + 1, :] = dec_h * n_state + jnp.sum(kw, axis=0, keepdims=True)

        hh = _sigmoid(og_ref[:, lo:hi].astype(F32)) * hh
        hh = hh - jnp.mean(hh, axis=-1, keepdims=True)
        hh = hh * lax.rsqrt(jnp.mean(hh * hh, axis=-1, keepdims=True) + EPS)
        o_ref[:, lo:hi] = hh * gn_ref[:, lo:hi]


def _mlstm(proj, gates, b_gates, conv_w, conv_b, conv_s, c0, n0, m0, gn, bb):
    b, length, _ = proj.shape
    c_real = math.gcd(length, ML_CHUNK)
    ct = ML_CHUNK
    width = ML_HEADS * HEAD_DIM
    base = (2 * RET_HEADS * HEAD_DIM + 2 * RET_HEADS * HEAD_DIM) // width

    def col(i):
        return pl.BlockSpec((bb, c_real, width), lambda bi, c: (bi, c, base + i))

    full = lambda shape: pl.BlockSpec(shape, lambda bi, c: (0,) * len(shape))
    per_b = lambda shape: pl.BlockSpec((bb,) + shape, lambda bi, c: (bi,) + (0,) * len(shape))
    bg = jnp.zeros((1, LANES), F32).at[0, :2 * ML_HEADS].set(b_gates)
    m0p = jnp.zeros((b, 1, LANES), F32).at[:, 0, :ML_HEADS].set(m0)
    outs = pl.pallas_call(
        functools.partial(_mlstm_body, c_real=c_real, ct=ct, bb=bb),
        grid=(b // bb, length // c_real),
        in_specs=[
            col(0), col(1), col(2), col(3),
            pl.BlockSpec((bb, c_real, LANES), lambda bi, c: (bi, c, 0)),
            full((1, LANES)),
            full((ML_CONV, 2 * width)),
            full((1, 2 * width)),
            per_b((ML_CONV - 1, 2 * width)),
            per_b((ML_HEADS, HEAD_DIM, HEAD_DIM)),
            per_b((ML_HEADS, HEAD_DIM)),
            per_b((1, LANES)),
            full((1, width)),
        ],
        out_specs=[
            pl.BlockSpec((bb, c_real, width), lambda bi, c: (bi, c, 0)),
            per_b((ML_HEADS, HEAD_DIM, HEAD_DIM)),
            per_b((ML_HEADS, HEAD_DIM)),
            per_b((1, LANES)),
            per_b((ML_CONV - 1, 2 * width)),
        ],
        out_shape=[
            jax.ShapeDtypeStruct((b, length, width), F32),
            jax.ShapeDtypeStruct(c0.shape, F32),
            jax.ShapeDtypeStruct(n0.shape, F32),
            jax.ShapeDtypeStruct((b, 1, LANES), F32),
            jax.ShapeDtypeStruct(conv_s.shape, F32),
        ],
        scratch_shapes=[pltpu.VMEM((bb, ct + SUBLANES, 2 * width), F32)],
        compiler_params=_cparams(("parallel", "arbitrary")),
        name="mlstm",
    )(proj, proj, proj, proj, gates, bg, conv_w, conv_b, conv_s, c0, n0, m0p, gn)
    h_m, c_new, n_new, m_new, conv_new = outs
    return h_m, c_new, n_new, m_new[:, 0, :ML_HEADS], conv_new


HG_UNROLL_BLOCK = 8
HG_UNROLL_STEP = 8


def _hgrn_head(cols, q_ref, f_ref, i_ref, g_ref, hl_ref, gn_ref, o_ref, st, *, blk, sub):
    hl = hl_ref[:, cols]
    hmax = jnp.max(hl, axis=0, keepdims=True)
    ex = jnp.exp(hl - hmax)
    p = ex / jnp.sum(ex, axis=0, keepdims=True)
    lower = (p[0:1] + p[1:2]) - p[0:1]
    gn = gn_ref[:, cols]

    f = lower + (1.0 - lower) * _sigmoid(f_ref[:, cols])
    log_k = jnp.log2(jnp.maximum(1.0 - f, 0.0))
    q = _silu(q_ref[:, cols].astype(F32))
    v = i_ref[:, cols].astype(F32)
    gate = _silu(g_ref[:, cols].astype(F32))

    row = lax.broadcasted_iota(jnp.int32, (blk, LANES), 0)
    rsub = row & (sub - 1)
    a = jnp.log2(f)
    sh = 1
    while sh < sub:
        a = a + jnp.where(rsub >= sh, pltpu.roll(a, sh, 0), 0.0)
        sh *= 2
    c = a - log_k
    q_in = q * jnp.exp2(a)

    t_idx = lax.broadcasted_iota(jnp.int32, (SUBLANES, LANES), 0)
    for j in range(blk // sub):
        r0 = j * sub
        a_end = a[r0 + sub - 1:r0 + sub, :]
        inter = _dot_nt(q_in[r0:r0 + sub].astype(BF16), st.astype(BF16))
        k_hat = jnp.exp2(a_end - c[r0:r0 + sub])
        v_j = v[r0:r0 + sub]
        st = st * jnp.exp2(a_end) + _dot_tn(v_j.astype(BF16), k_hat.astype(BF16))
        for part in range(sub // SUBLANES):
            t0 = r0 + part * SUBLANES
            a_t = a[t0:t0 + SUBLANES]
            q_t = q[t0:t0 + SUBLANES]
            o_t = inter[part * SUBLANES:(part + 1) * SUBLANES]
            for s in range((part + 1) * SUBLANES):
                arg = a_t - c[r0 + s:r0 + s + 1, :]
                if s >= part * SUBLANES:
                    arg = jnp.where(t_idx >= s - part * SUBLANES, arg, NEG_INF)
                z = q_t * jnp.exp2(arg)
                o_t = o_t + jnp.sum(z, axis=-1, keepdims=True) * v[r0 + s:r0 + s + 1, :]
            o_t = o_t * lax.rsqrt(jnp.mean(o_t * o_t, axis=-1, keepdims=True) + EPS)
            o_ref[t0:t0 + SUBLANES, cols] = o_t * gn * gate[t0:t0 + SUBLANES]
    return st


def _block_ref_rows(a, level, row8):
    rows = a.shape[0]
    half = level // 2
    if half >= SUBLANES:
        return jnp.concatenate([jnp.broadcast_to(a[b0 + half - 1:b0 + half, :], (level, LANES))
                                for b0 in range(0, rows, level)], axis=0)
    if level == 2:
        return jnp.where((row8 & 1) == 1, pltpu.roll(a, 1, 0), a)
    pieces = []
    for v0 in range(0, rows, SUBLANES):
        picks = [jnp.broadcast_to(a[v0 + b0 + half - 1:v0 + b0 + half, :], (SUBLANES, LANES))
                 for b0 in range(0, SUBLANES, level)]
        piece = picks[-1]
        for i in range(len(picks) - 2, -1, -1):
            piece = jnp.where(row8[:SUBLANES] < (i + 1) * level, picks[i], piece)
        pieces.append(piece)
    return jnp.concatenate(pieces, axis=0)


def _hgrn_head_block(cols, q_ref, f_ref, i_ref, g_ref, hl_ref, gn_ref, o_ref, st, *, blk):
    hl = hl_ref[:, cols]
    hmax = jnp.max(hl, axis=0, keepdims=True)
    ex = jnp.exp(hl - hmax)
    p = ex / jnp.sum(ex, axis=0, keepdims=True)
    lower = (p[0:1] + p[1:2]) - p[0:1]

    f = lower + (1.0 - lower) * _sigmoid(f_ref[:, cols])
    kk = 1.0 - f
    q = _silu(q_ref[:, cols].astype(F32))
    vb = i_ref[:, cols]
    a = _scan_rows(jnp.log2(f), jnp.add, 0.0)
    c = a - jnp.log2(jnp.maximum(kk, 0.0))
    row8 = lax.broadcasted_iota(jnp.int32, (blk, LANES), 0) & (SUBLANES - 1)
    group = (lax.broadcasted_iota(jnp.int32, (blk, blk), 0)
             ^ lax.broadcasted_iota(jnp.int32, (blk, blk), 1))

    scores = None
    level = blk
    while level >= 2:
        half = level // 2
        if half >= SUBLANES:
            zero = jnp.zeros((half, LANES), F32)
            qs, ks = [], []
            for b0 in range(0, blk, level):
                ref = a[b0 + half - 1:b0 + half, :]
                qs += [zero, q[b0 + half:b0 + level] * jnp.exp2(a[b0 + half:b0 + level] - ref)]
                ks += [jnp.exp2(ref - c[b0:b0 + half]), zero]
            q_l = jnp.concatenate(qs, axis=0)
            k_l = jnp.concatenate(ks, axis=0)
        else:
            ref = _block_ref_rows(a, level, row8)
            upper = (row8 & (level - 1)) >= half
            q_l = jnp.where(upper, q * jnp.exp2(a - ref), 0.0)
            k_l = jnp.where(upper, 0.0, jnp.exp2(ref - c))
        r = _dot_nt(q_l.astype(BF16), k_l.astype(BF16))
        scores = r if scores is None else jnp.where(group < level, r, scores)
        level = half
    scores = jnp.where(group < 1, _dot_nt(q.astype(BF16), kk.astype(BF16)), scores)

    a_end = a[blk - 1:blk, :]
    o = _dot_nt((q * jnp.exp2(a)).astype(BF16), st.astype(BF16)) + _dot(scores.astype(BF16), vb)
    o = o * lax.rsqrt(jnp.mean(o * o, axis=-1, keepdims=True) + EPS)
    o_ref[:, cols] = o * gn_ref[:, cols] * _silu(g_ref[:, cols].astype(F32))
    return st * jnp.exp2(a_end) + _dot_tn(vb, jnp.exp2(a_end - c).astype(BF16))


def _hgrn_body(q_ref, f_ref, i_ref, g_ref, hl_ref, s0_ref, gn_ref, o_ref, s_ref, st_ref, *, blk, sub, bb):
    @pl.when(pl.program_id(1) == 0)
    def _():
        for g, h in [(g, h) for g in range(bb) for h in range(HG_HEADS)]:
            st_ref[g, h] = jnp.transpose(s0_ref[g, h])

    if blk == HG_BLOCK:
        head, per_iter = functools.partial(_hgrn_head_block, blk=blk), HG_UNROLL_BLOCK // bb
    else:
        head, per_iter = functools.partial(_hgrn_head, blk=blk, sub=sub), HG_UNROLL_STEP // bb

    def heads(it, carry):
        for g, u in [(g, u) for g in range(bb) for u in range(per_iter)]:
            h = it * per_iter + u
            cols = pl.ds(pl.multiple_of(h * HEAD_DIM, HEAD_DIM), HEAD_DIM)
            st_ref[g, h] = head(cols, q_ref.at[g], f_ref.at[g], i_ref.at[g], g_ref.at[g], hl_ref, gn_ref,
                                o_ref.at[g], st_ref[g, h])
        return carry

    lax.fori_loop(0, HG_HEADS // per_iter, heads, 0)

    @pl.when(pl.program_id(1) == pl.num_programs(1) - 1)
    def _():
        for g, h in [(g, h) for g in range(bb) for h in range(HG_HEADS)]:
            s_ref[g, h] = jnp.transpose(st_ref[g, h])


def _hgrn(proj, f_pre, hg_lower, state, gn, bb):
    b, length, _ = proj.shape
    blk = min(length, HG_BLOCK)
    sub = math.gcd(length, HG_SUB)
    width = HG_HEADS * HEAD_DIM

    def col(i):
        return pl.BlockSpec((bb, blk, width), lambda bi, c: (bi, c, i))

    full = lambda shape: pl.BlockSpec(shape, lambda bi, c: (0,) * len(shape))
    state_spec = pl.BlockSpec((bb, HG_HEADS, HEAD_DIM, HEAD_DIM), lambda bi, c: (bi, 0, 0, 0))
    return pl.pallas_call(
        functools.partial(_hgrn_body, blk=blk, sub=sub, bb=bb),
        grid=(b // bb, length // blk),
        in_specs=[col(0), col(0), col(1), col(2), full(hg_lower.shape), state_spec, full((1, width))],
        out_specs=[pl.BlockSpec((bb, blk, width), lambda bi, c: (bi, c, 0)), state_spec],
        out_shape=[
            jax.ShapeDtypeStruct((b, length, width), F32),
            jax.ShapeDtypeStruct(state.shape, F32),
        ],
        scratch_shapes=[pltpu.VMEM((bb, HG_HEADS, HEAD_DIM, HEAD_DIM), F32)],
        compiler_params=_cparams(("parallel", "arbitrary")),
        name="hgrn2",
    )(proj, f_pre, proj, proj, hg_lower, state, gn)


def _prep_weights(p):
    w_in_ab = p["w_in_ab"][0]
    w_in_c = p["w_in_c"][0]
    width = HG_HEADS * HEAD_DIM
    w_gates = jnp.zeros((D_MODEL, LANES), BF16).at[:, :2 * ML_HEADS].set(w_in_ab[:, AB_MAIN:].astype(BF16))
    w_router = jnp.zeros((LANES, D_MODEL), F32).at[:N_EXPERTS, :].set(p["w_router"][0].T)
    w_router_hi = w_router.astype(BF16)
    w_router_lo = (w_router - w_router_hi.astype(F32)).astype(BF16)
    b_router = jnp.broadcast_to(p["b_router"][0][:, None], (N_EXPERTS, LANES))
    return {
        "w_ab": w_in_ab[:, :AB_MAIN].astype(BF16),
        "w_ab_gates": w_gates,
        "w_out_ab": p["w_out_ab"][0].astype(BF16),
        "w_c": jnp.concatenate([w_in_c[:, :width], w_in_c[:, 2 * width:]], axis=1).astype(BF16),
        "w_c_forget": w_in_c[:, width:2 * width].astype(BF16),
        "w_out_c": p["w_out_c"][0].astype(BF16),
        "w_ffn_gate": p["w_ffn_gate"][0].astype(BF16),
        "w_ffn_up": p["w_ffn_up"][0].astype(BF16),
        "w_ffn_down": p["w_ffn_down"][0].astype(BF16),
        "w_router_hi": w_router_hi,
        "w_router_lo": w_router_lo,
        "b_router": b_router,
    }


def _mixers(x, pos0, ret_s, mc_s, mn_s, mm_s, conv_s, hg_s, p, w):
    b, length, d = x.shape
    t = b * length
    tm = min(512, t)
    tm_mm = min(1024, t)
    prompt = length >= RET_CHUNK
    bb = 1 if prompt else 8
    assert t % tm_mm == 0 and t % tm == 0 and b % bb == 0 and b % 2 == 0, (b, length)
    row = lambda v: v.reshape(1, -1)
    x0 = x.reshape(t, d)

    proj, gates = _norm_matmul(x0, row(p["ln_mix"][0]), w["w_ab"], w["w_ab_gates"], tm_mm, 1024)
    proj = proj.reshape(b, length, AB_MAIN)
    o_ret, ret_new = _retention(proj, pos0, ret_s, row(p["ret_gn"][0]), 2 if prompt else bb)
    h_m, c_new, n_new, m_new, conv_new = _mlstm(
        proj, gates.reshape(b, length, LANES), p["b_gates_ab"][0], p["conv_w_ab"][0], row(p["conv_b_ab"][0]),
        conv_s, mc_s, mn_s, mm_s, row(p["ml_gn"][0]), bb)
    half = RET_HEADS * HEAD_DIM
    x2 = _ffn([o_ret.reshape(t, half), h_m.reshape(t, half)], [w["w_out_ab"][:half], w["w_out_ab"][half:]], x0,
              row(p["ln_ffn"][0]), w["w_ffn_gate"], w["w_ffn_up"], w["w_ffn_down"], tm, D_FF // 2)

    proj_c, f_pre = _norm_matmul(x2, row(p["ln_mix"][1]), w["w_c"], w["w_c_forget"], tm_mm, 1024)
    o_hg, hg_new = _hgrn(proj_c.reshape(b, length, 3 * d), f_pre.reshape(b, length, d), p["hg_lower"], hg_s,
                         row(p["hg_gn"][0]), bb)
    states = (ret_new[None], c_new[None], n_new[None], m_new[None], conv_new[None], hg_new[None])
    return x2, o_hg.reshape(t, d), states


def _experts(groups, shapes, p, w):
    row = lambda v: v.reshape(1, -1)
    ys = _moe([o for _, o in groups], w["w_out_c"], [x for x, _ in groups], row(p["ln_ffn"][1]), w,
              (p["w_moe_gate"][0], p["w_moe_up"][0], p["w_moe_down"][0]), row(p["ln_final"]))
    return [y.reshape(shape) for y, shape in zip(ys, shapes)]


def _trunk(x, pos0, ret_s, mc_s, mn_s, mm_s, conv_s, hg_s, p, w):
    x2, o_hg, states = _mixers(x, pos0, ret_s, mc_s, mn_s, mm_s, conv_s, hg_s, p, w)
    (y,) = _experts([(x2, o_hg)], [x.shape], p, w)
    return (y,) + states


def kernel(x_prompt, x_sample, state_ret, state_mlstm_c, state_mlstm_n, state_mlstm_m, state_conv, state_hgrn,
           ln_mix, ln_ffn, ln_final, w_in_ab, b_gates_ab, conv_w_ab, conv_b_ab, ret_gn, ml_gn, w_out_ab,
           w_in_c, hg_lower, hg_gn, w_out_c, w_ffn_gate, w_ffn_up, w_ffn_down,
           w_router, b_router, w_moe_gate, w_moe_up, w_moe_down):
    p = {"ln_mix": ln_mix, "ln_ffn": ln_ffn, "ln_final": ln_final,
         "w_in_ab": w_in_ab, "b_gates_ab": b_gates_ab, "conv_w_ab": conv_w_ab, "conv_b_ab": conv_b_ab,
         "ret_gn": ret_gn, "ml_gn": ml_gn, "w_out_ab": w_out_ab,
         "w_in_c": w_in_c, "hg_lower": hg_lower, "hg_gn": hg_gn, "w_out_c": w_out_c,
         "w_ffn_gate": w_ffn_gate, "w_ffn_up": w_ffn_up, "w_ffn_down": w_ffn_down,
         "w_router": w_router, "b_router": b_router,
         "w_moe_gate": w_moe_gate, "w_moe_up": w_moe_up, "w_moe_down": w_moe_down}
    w = _prep_weights(p)
    bp = x_prompt.shape[0]
    zeros = lambda *shape: jnp.zeros(shape, F32)
    x2_p, o_p, states_p = _mixers(
        x_prompt, 0,
        zeros(bp, RET_HEADS, HEAD_DIM, HEAD_DIM), zeros(bp, ML_HEADS, HEAD_DIM, HEAD_DIM),
        zeros(bp, ML_HEADS, HEAD_DIM), jnp.full((bp, ML_HEADS), NEG_INF, F32),
        zeros(bp, ML_CONV - 1, 2 * ML_HEADS * HEAD_DIM), zeros(bp, HG_HEADS, HEAD_DIM, HEAD_DIM), p, w)
    x2_s, o_s, states_s = _mixers(
        x_sample, 16384,
        state_ret[0], state_mlstm_c[0], state_mlstm_n[0], state_mlstm_m[0], state_conv[0], state_hgrn[0], p, w)
    y_p, y_s = _experts([(x2_p, o_p), (x2_s, o_s)], [x_prompt.shape, x_sample.shape], p, w)
    out = [y_p, y_s]
    for a, b in zip(states_p, states_s):
        out += [a, b]
    return tuple(out)
```

```python
import functools
import math

import numpy as np
import jax
import jax.numpy as jnp
from jax import lax
from jax.experimental import pallas as pl
from jax.experimental.pallas import tpu as pltpu

F32 = jnp.float32
BF16 = jnp.bfloat16

D_MODEL = 1024
HEAD_DIM = 128
RET_HEADS = 4
ML_HEADS = 4
ML_CONV = 4
HG_HEADS = 8
D_FF = 2816
N_EXPERTS = 8
ROPE_BASE = 10000.0
EPS = 1e-6
RET_CHUNK = 128
ML_CHUNK = 128
HG_BLOCK = 128
HG_SUB = 16

LANES = 128
SUBLANES = 8
AB_MAIN = 4096
VMEM_LIMIT = 56 * 1024 * 1024

NEG_INF = float("-inf")


def _cparams(sem):
    return pltpu.CompilerParams(dimension_semantics=sem, vmem_limit_bytes=VMEM_LIMIT)


def _sigmoid(x):
    return 1.0 / (1.0 + jnp.exp(-x))


def _silu(x):
    return x * _sigmoid(x)


def _rmsnorm_rows(x, g):
    ms = jnp.mean(x * x, axis=-1, keepdims=True)
    return x * lax.rsqrt(ms + EPS) * g


def _dot(a, b):
    return jnp.dot(a, b, preferred_element_type=F32)


def _dot_nt(a, b):
    return lax.dot_general(a, b, (((1,), (1,)), ((), ())), preferred_element_type=F32)


def _dot_tn(a, b):
    return lax.dot_general(a, b, (((0,), (0,)), ((), ())), preferred_element_type=F32)


def _pad_rows(x, rows):
    if x.shape[0] == rows:
        return x
    return jnp.concatenate([x, jnp.zeros((rows - x.shape[0], x.shape[1]), x.dtype)], axis=0)


def _norm_matmul_body(x_ref, g_ref, w_ref, ws_ref, o_ref, os_ref, h_ref):
    @pl.when(pl.program_id(1) == 0)
    def _():
        h = _rmsnorm_rows(x_ref[...], g_ref[...]).astype(BF16)
        h_ref[...] = h
        os_ref[...] = _dot(h, ws_ref[...])

    o_ref[...] = _dot(h_ref[...], w_ref[...]).astype(BF16)


def _norm_matmul(x, g, w, w_side, tm, tn):
    t, d = x.shape
    n, ns = w.shape[1], w_side.shape[1]
    return pl.pallas_call(
        _norm_matmul_body,
        grid=(t // tm, n // tn),
        in_specs=[
            pl.BlockSpec((tm, d), lambda i, j: (i, 0)),
            pl.BlockSpec((1, d), lambda i, j: (0, 0)),
            pl.BlockSpec((d, tn), lambda i, j: (0, j)),
            pl.BlockSpec((d, ns), lambda i, j: (0, 0)),
        ],
        out_specs=[pl.BlockSpec((tm, tn), lambda i, j: (i, j)), pl.BlockSpec((tm, ns), lambda i, j: (i, 0))],
        out_shape=[jax.ShapeDtypeStruct((t, n), BF16), jax.ShapeDtypeStruct((t, ns), F32)],
        scratch_shapes=[pltpu.VMEM((tm, d), BF16)],
        compiler_params=_cparams(("parallel", "arbitrary")),
        name="norm_matmul",
    )(x, g, w, w_side)


def _mixer_residual(x_ref, a_refs, w_refs):
    x = x_ref[...]
    for a_ref, w_ref in zip(a_refs, w_refs):
        x = x + _dot(a_ref[...].astype(BF16), w_ref[...])
    return x


def _ffn_body(*refs, n_in):
    a_refs, w_refs = refs[:n_in], refs[n_in:2 * n_in]
    x_ref, g_ref, wg_ref, wu_ref, wd_ref, o_ref, h_ref, x1_ref = refs[2 * n_in:]
    f = pl.program_id(1)

    @pl.when(f == 0)
    def _():
        x1 = _mixer_residual(x_ref, a_refs, w_refs)
        x1_ref[...] = x1
        h_ref[...] = _rmsnorm_rows(x1, g_ref[...]).astype(BF16)

    h = h_ref[...]
    a = _dot(h, wg_ref[...])
    u = _dot(h, wu_ref[...])
    act = (_silu(a) * u).astype(BF16)
    x1_ref[...] += _dot(act, wd_ref[...])

    @pl.when(f == pl.num_programs(1) - 1)
    def _():
        o_ref[...] = x1_ref[...]


def _ffn(acts, weights, x, g, wg, wu, wd, tm, tf):
    t, d = x.shape
    dff = wg.shape[1]
    n_in = len(acts)
    in_specs = [pl.BlockSpec((tm, a.shape[1]), lambda i, f: (i, 0)) for a in acts]
    in_specs += [pl.BlockSpec(w.shape, lambda i, f: (0, 0)) for w in weights]
    in_specs += [
        pl.BlockSpec((tm, d), lambda i, f: (i, 0)),
        pl.BlockSpec((1, d), lambda i, f: (0, 0)),
        pl.BlockSpec((d, tf), lambda i, f: (0, f)),
        pl.BlockSpec((d, tf), lambda i, f: (0, f)),
        pl.BlockSpec((tf, d), lambda i, f: (f, 0)),
    ]
    return pl.pallas_call(
        functools.partial(_ffn_body, n_in=n_in),
        grid=(t // tm, dff // tf),
        in_specs=in_specs,
        out_specs=pl.BlockSpec((tm, d), lambda i, f: (i, 0)),
        out_shape=jax.ShapeDtypeStruct((t, d), F32),
        scratch_shapes=[pltpu.VMEM((tm, d), BF16), pltpu.VMEM((tm, d), F32)],
        compiler_params=_cparams(("parallel", "arbitrary")),
        name="ffn",
    )(*acts, *weights, x, g, wg, wu, wd)


MOE_TB = 512
MOE_ALIGN = 16
MOE_BURST = 4
MOE_TILE = 256
MOE_CHUNKS_PER_TILE = MOE_TILE // MOE_ALIGN


def _moe_block_rows(tb):
    return 2 * tb + N_EXPERTS * MOE_ALIGN


def _group_block_specs(shape, blocks):
    specs, first = [], 0
    for nb in blocks:
        specs.append(pl.BlockSpec(shape, lambda i, *_, first=first, nb=nb: (jnp.clip(i - first, 0, nb - 1), 0)))
        first += nb
    return specs


def _route_body(*refs, blocks):
    n = len(blocks)
    a_refs, wo_ref, x_refs = refs[:n], refs[n], refs[n + 1:2 * n + 1]
    g_ref, whi_ref, wlo_ref, b_ref, tri_ref, x3_ref, h_ref, col_ref, rowf_ref, cnt_ref = refs[2 * n + 1:]
    i = pl.program_id(0)
    act, x_in = a_refs[0][...], x_refs[0][...]
    first = blocks[0]
    for a_ref, x_ref, nb in zip(a_refs[1:], x_refs[1:], blocks[1:]):
        act = jnp.where(i >= first, a_ref[...], act)
        x_in = jnp.where(i >= first, x_ref[...], x_in)
        first += nb
    x3 = x_in + _dot(act.astype(BF16), wo_ref[...])
    x3_ref[...] = x3
    h = _rmsnorm_rows(x3, g_ref[...])
    h_hi = h.astype(BF16)
    h_ref[...] = h_hi
    h_lo = (h - h_hi.astype(F32)).astype(BF16)

    logits = (_dot_nt(whi_ref[...], h_hi) + _dot_nt(wlo_ref[...], h_hi) + _dot_nt(whi_ref[...], h_lo))
    logits = logits[:N_EXPERTS, :] + b_ref[:, 0:1]
    row = lax.broadcasted_iota(jnp.int32, logits.shape, 0)
    m1 = jnp.max(logits, axis=0, keepdims=True)
    i1 = jnp.min(jnp.where(logits == m1, row, N_EXPERTS), axis=0, keepdims=True)
    lg2 = jnp.where(row == i1, NEG_INF, logits)
    m2 = jnp.max(lg2, axis=0, keepdims=True)
    i2 = jnp.min(jnp.where(lg2 == m2, row, N_EXPERTS), axis=0, keepdims=True)
    e = jnp.exp(m2 - m1)
    w1 = 1.0 / (1.0 + e)
    w2 = e / (1.0 + e)

    sel = jnp.where(row == i1, 1.0, jnp.where(row == i2, 1.0, 0.0))
    before = _dot(sel.astype(BF16), tri_ref[...])
    n = jnp.broadcast_to(jnp.sum(sel, axis=1, keepdims=True), (N_EXPERTS, LANES))
    padded = jnp.floor((n + (MOE_ALIGN - 1.0)) * (1.0 / MOE_ALIGN)) * MOE_ALIGN
    base = _scan_rows(padded, jnp.add, 0.0) - padded
    pos = base[:, 0:1] + before
    loc1 = jnp.sum(jnp.where(row == i1, pos, 0.0), axis=0, keepdims=True)
    loc2 = jnp.sum(jnp.where(row == i2, pos, 0.0), axis=0, keepdims=True)
    rowf = jnp.where(row == 0, loc1, jnp.where(row == 1, loc2, jnp.where(row == 2, w1,
                                                                          jnp.where(row == 3, w2, 0.0))))
    rowf_ref[0] = rowf
    for c in range(rowf.shape[1] // LANES):
        col_ref[c * LANES:(c + 1) * LANES, :] = jnp.transpose(_pad_rows(rowf[:, c * LANES:(c + 1) * LANES], LANES))
    cnt_ref[0] = n


def _route(acts, w_out, xs, g, w_hi, w_lo, b, tb):
    d = xs[0].shape[1]
    blocks = [x.shape[0] // tb for x in xs]
    nb = sum(blocks)
    t = nb * tb
    tri = jnp.asarray(np.triu(np.ones((tb, tb), np.float32), 1), BF16)
    full = lambda shape: pl.BlockSpec(shape, lambda i: (0,) * len(shape))
    return pl.pallas_call(
        functools.partial(_route_body, blocks=blocks),
        grid=(nb,),
        in_specs=(_group_block_specs((tb, acts[0].shape[1]), blocks) + [full(w_out.shape)]
                  + _group_block_specs((tb, d), blocks)
                  + [full((1, d)), full((LANES, d)), full((LANES, d)), full((SUBLANES, LANES)), full((tb, tb))]),
        out_specs=[
            pl.BlockSpec((tb, d), lambda i: (i, 0)),
            pl.BlockSpec((tb, d), lambda i: (i, 0)),
            pl.BlockSpec((tb, LANES), lambda i: (i, 0)),
            pl.BlockSpec((1, SUBLANES, tb), lambda i: (i, 0, 0)),
            pl.BlockSpec((1, SUBLANES, LANES), lambda i: (i, 0, 0)),
        ],
        out_shape=[
            jax.ShapeDtypeStruct((t, d), F32),
            jax.ShapeDtypeStruct((t, d), BF16),
            jax.ShapeDtypeStruct((t, LANES), F32),
            jax.ShapeDtypeStruct((nb, SUBLANES, tb), F32),
            jax.ShapeDtypeStruct((nb, SUBLANES, LANES), F32),
        ],
        compiler_params=_cparams(("parallel",)),
        name="route",
    )(*acts, w_out, *xs, g, w_hi, w_lo, b, tri)


def _moe_plan(cnt):
    n = cnt[:, :N_EXPERTS, 0].astype(jnp.int32)
    chunks = (n + (MOE_ALIGN - 1)) // MOE_ALIGN
    src = jnp.cumsum(chunks, axis=1) - chunks
    seg_chunks = jnp.sum(chunks, axis=0)
    seg_tiles = (seg_chunks + (MOE_CHUNKS_PER_TILE - 1)) // MOE_CHUNKS_PER_TILE
    seg_end = jnp.cumsum(seg_tiles)
    seg_first = (seg_end - seg_tiles) * MOE_CHUNKS_PER_TILE
    dst = seg_first[None, :] + jnp.cumsum(chunks, axis=0) - chunks
    flat = lambda a: a.reshape(-1).astype(jnp.int32)
    return {"src": flat(src), "dst": flat(dst), "chunks": flat(chunks),
            "first_tile": flat(seg_end - seg_tiles), "tile_count": flat(seg_tiles),
            "tail_start": flat(seg_first + seg_chunks), "tail_count": flat(seg_tiles * MOE_CHUNKS_PER_TILE - seg_chunks)}


def _chunk_rows(chunk, n=1):
    return pl.ds(pl.multiple_of(chunk * MOE_ALIGN, MOE_ALIGN), n * MOE_ALIGN)


def _for_block_copies(nch_ref, blk, fn):
    for e in range(N_EXPERTS):
        count = nch_ref[blk * N_EXPERTS + e]
        bursts = count // MOE_BURST
        lax.fori_loop(0, bursts, lambda i, carry, e=e: (fn(e, i * MOE_BURST, MOE_BURST), carry)[1], 0)
        lax.fori_loop(bursts * MOE_BURST, count, lambda c, carry, e=e: (fn(e, c, 1), carry)[1], 0)


def _block_copy_counts(nch_ref, blk):
    bursts = singles = 0
    for e in range(N_EXPERTS):
        count = nch_ref[blk * N_EXPERTS + e]
        bursts = bursts + count // MOE_BURST
        singles = singles + lax.rem(count, MOE_BURST)
    return bursts, singles


def _wait_block_copies(nch_ref, blk, copy):
    bursts, singles = _block_copy_counts(nch_ref, blk)
    lax.fori_loop(0, bursts, lambda i, carry: (copy(MOE_BURST).wait(), carry)[1], 0)
    lax.fori_loop(0, singles, lambda i, carry: (copy(1).wait(), carry)[1], 0)


def _gather_body(src_ref, dst_ref, nch_ref, tail_ref, ntail_ref, last_ref, h_ref, rowf_ref, hs_ref, z_ref, sem,
                 *, n_tiles):
    b = pl.program_id(0)
    slot = lax.rem(b, 2)
    tb = h_ref.shape[0]
    loc = rowf_ref[0].astype(jnp.int32)
    r = lax.broadcasted_iota(jnp.int32, (z_ref.shape[1], tb), 0)
    onehot = jnp.where(r == loc[0:1, :], 1.0, jnp.where(r == loc[1:2, :], 1.0, 0.0)).astype(BF16)
    z_ref[slot] = _dot(onehot, h_ref[...]).astype(BF16)

    def copy(buf, src_chunk, dst_chunk, n=1):
        return pltpu.make_async_copy(z_ref.at[buf, _chunk_rows(src_chunk, n)], hs_ref.at[_chunk_rows(dst_chunk, n)],
                                     sem.at[buf])

    def drain(blk, buf):
        _wait_block_copies(nch_ref, blk, lambda n: copy(buf, 0, 0, n))

    @pl.when(b > 0)
    def _():
        drain(b - 1, 1 - slot)

    _for_block_copies(nch_ref, b, lambda e, c, n: copy(slot, src_ref[b * N_EXPERTS + e] + c,
                                                       dst_ref[b * N_EXPERTS + e] + c, n).start())

    @pl.when(b == pl.num_programs(0) - 1)
    def _():
        drain(b, slot)
        z_ref[0, 0:MOE_ALIGN, :] = jnp.zeros((MOE_ALIGN, z_ref.shape[2]), BF16)
        past = last_ref[0]
        total = n_tiles * MOE_CHUNKS_PER_TILE
        for e in range(N_EXPERTS):
            lax.fori_loop(0, ntail_ref[e], lambda c, carry, e=e: (copy(0, 0, tail_ref[e] + c).start(), carry)[1], 0)
        lax.fori_loop(past, total, lambda c, carry: (copy(0, 0, c).start(), carry)[1], 0)
        for e in range(N_EXPERTS):
            lax.fori_loop(0, ntail_ref[e], lambda c, carry: (copy(0, 0, 0).wait(), carry)[1], 0)
        lax.fori_loop(past, total, lambda c, carry: (copy(0, 0, 0).wait(), carry)[1], 0)


def _moe_gather(plan, h, rowf, n_tiles, tb):
    t, d = h.shape
    rows = _moe_block_rows(tb)
    past = (plan["first_tile"][-1:] + plan["tile_count"][-1:]) * MOE_CHUNKS_PER_TILE
    return pl.pallas_call(
        functools.partial(_gather_body, n_tiles=n_tiles),
        grid_spec=pltpu.PrefetchScalarGridSpec(
            num_scalar_prefetch=6,
            grid=(t // tb,),
            in_specs=[
                pl.BlockSpec((tb, d), lambda i, *_: (i, 0)),
                pl.BlockSpec((1, SUBLANES, tb), lambda i, *_: (i, 0, 0)),
            ],
            out_specs=pl.BlockSpec(memory_space=pl.ANY),
            scratch_shapes=[pltpu.VMEM((2, rows, d), BF16), pltpu.SemaphoreType.DMA((2,))],
        ),
        out_shape=jax.ShapeDtypeStruct((n_tiles * MOE_TILE, d), BF16),
        compiler_params=_cparams(("arbitrary",)),
        name="moe_gather",
    )(plan["src"], plan["dst"], plan["chunks"], plan["tail_start"], plan["tail_count"], past, h, rowf)


W_CHUNK = 256


def _expert_ffn_body(first_ref, count_ref, xs_ref, wg_hbm, wu_hbm, wd_hbm, ys_ref,
                     wg_b, wu_b, wd_b, stage_c, stage_r, x_buf, y_buf, w_sem, in_sem, out_sem, *, n_tiles, halves):
    e = pl.program_id(0)
    last = pl.num_programs(0) - 1
    first = first_ref[e]
    count = count_ref[e]
    cur = lax.rem(e, 2)
    dff = wg_b.shape[2]
    tf = dff // halves
    per = dff // W_CHUNK
    n_chunks = 3 * per

    def span(j):
        return pl.ds(pl.multiple_of(j * W_CHUNK, W_CHUNK), W_CHUNK)

    def fetch(kind, expert, j, slot):
        if kind == 2:
            return pltpu.make_async_copy(wd_hbm.at[expert, span(j), :], stage_r.at[slot], w_sem.at[slot])
        return pltpu.make_async_copy((wg_hbm, wu_hbm)[kind].at[expert, :, span(j)], stage_c.at[slot], w_sem.at[slot])

    def for_kind(c, fn):
        kind = c // per
        for k in range(3):
            @pl.when(kind == k)
            def _(k=k):
                fn(k, c - k * per)

    n_pairs = -(-n_chunks // 2)

    def start_pair(expert, p):
        for s in range(2):
            c = 2 * p + s

            @pl.when(c < n_chunks)
            def _(c=c, s=s):
                for_kind(c, lambda k, j: fetch(k, expert, j, s).start())

    def convert_pair(expert, dst, p):
        for s in range(2):
            c = 2 * p + s

            def land(k, j, s=s):
                fetch(k, expert, j, s).wait()
                if k == 2:
                    wd_b[dst, span(j), :] = stage_r[s].astype(BF16)
                else:
                    (wg_b, wu_b)[k][dst, :, span(j)] = stage_c[s].astype(BF16)

            @pl.when(c < n_chunks)
            def _(c=c, land=land):
                for_kind(c, land)

        start_pair(expert, p + 1)

    def maybe_convert(p):
        @pl.when(p < n_pairs)
        def _():
            convert_pair(e + 1, 1 - cur, p)
        return jnp.minimum(p + 1, n_pairs)

    @pl.when(e == 0)
    def _():
        start_pair(0, 0)
        lax.fori_loop(0, n_pairs, lambda p, carry: (convert_pair(0, 0, p), carry)[1], 0)

    @pl.when(e < last)
    def _():
        start_pair(e + 1, 0)

    def tile_rows(tile):
        return pl.ds(pl.multiple_of(tile * MOE_TILE, MOE_TILE), MOE_TILE)

    def load(j, buf):
        return pltpu.make_async_copy(xs_ref.at[tile_rows(first + j)], x_buf.at[buf], in_sem.at[buf])

    def store(tile, buf):
        return pltpu.make_async_copy(y_buf.at[buf], ys_ref.at[tile_rows(tile)], out_sem.at[buf])

    @pl.when(count > 0)
    def _():
        load(0, 0).start()

    def tile_step(j, c):
        buf = lax.rem(j, 2)

        @pl.when(j + 1 < count)
        def _():
            load(j + 1, 1 - buf).start()

        load(j, buf).wait()
        xs = x_buf[buf]
        y = None
        for f in range(halves):
            a = _dot(xs, wg_b[cur, :, f * tf:(f + 1) * tf])
            u = _dot(xs, wu_b[cur, :, f * tf:(f + 1) * tf])
            part = _dot((_silu(a) * u).astype(BF16), wd_b[cur, f * tf:(f + 1) * tf, :])
            y = part if y is None else y + part

        @pl.when(j >= 2)
        def _():
            store(first, buf).wait()

        y_buf[buf] = y.astype(BF16)
        store(first + j, buf).start()
        return maybe_convert(c)

    p_done = lax.fori_loop(0, count, tile_step, jnp.where(e < last, 0, n_pairs))
    lax.fori_loop(p_done, n_pairs, lambda p, carry: (convert_pair(e + 1, 1 - cur, p), carry)[1], 0)

    for back in (2, 1):
        @pl.when(count >= back)
        def _(back=back):
            store(first, lax.rem(count - back, 2)).wait()

    @pl.when(e == last)
    def _():
        end = first + count
        y_buf[0] = jnp.zeros(y_buf.shape[1:], BF16)
        lax.fori_loop(end, n_tiles, lambda t, c: (store(t, 0).start(), c)[1], 0)
        lax.fori_loop(end, n_tiles, lambda t, c: (store(t, 0).wait(), c)[1], 0)


def _expert_ffn(first_tile, tile_count, xs, wg, wu, wd):
    rows, d = xs.shape
    dff = wg.shape[2]
    any_space = pl.BlockSpec(memory_space=pl.ANY)
    return pl.pallas_call(
        functools.partial(_expert_ffn_body, n_tiles=rows // MOE_TILE, halves=2),
        grid_spec=pltpu.PrefetchScalarGridSpec(
            num_scalar_prefetch=2,
            grid=(N_EXPERTS,),
            in_specs=[any_space, any_space, any_space, any_space],
            out_specs=any_space,
            scratch_shapes=[
                pltpu.VMEM((2, d, dff), BF16), pltpu.VMEM((2, d, dff), BF16), pltpu.VMEM((2, dff, d), BF16),
                pltpu.VMEM((2, d, W_CHUNK), F32), pltpu.VMEM((2, W_CHUNK, d), F32),
                pltpu.VMEM((2, MOE_TILE, d), BF16), pltpu.VMEM((2, MOE_TILE, d), BF16),
                pltpu.SemaphoreType.DMA((2,)), pltpu.SemaphoreType.DMA((2,)), pltpu.SemaphoreType.DMA((2,))],
        ),
        out_shape=jax.ShapeDtypeStruct((rows, d), BF16),
        compiler_params=_cparams(("arbitrary",)),
        name="expert_ffn",
    )(first_tile, tile_count, xs, wg, wu, wd)


def _combine_body(src_ref, dst_ref, nch_ref, x_ref, col_ref, gf_ref, ys_ref, *refs, blocks):
    o_refs, (y_ref, sem) = refs[:len(blocks)], refs[len(blocks):]
    b = pl.program_id(0)
    nb = pl.num_programs(0)
    slot = lax.rem(b, 2)

    def copy(buf, src_chunk, dst_chunk, n=1):
        return pltpu.make_async_copy(ys_ref.at[_chunk_rows(dst_chunk, n)], y_ref.at[buf, _chunk_rows(src_chunk, n)],
                                     sem.at[buf])

    def fetch(blk, buf):
        _for_block_copies(nch_ref, blk, lambda e, c, n: copy(buf, src_ref[blk * N_EXPERTS + e] + c,
                                                             dst_ref[blk * N_EXPERTS + e] + c, n).start())

    @pl.when(b == 0)
    def _():
        y_ref[...] = jnp.zeros_like(y_ref)
        fetch(0, 0)

    @pl.when(b + 1 < nb)
    def _():
        fetch(b + 1, 1 - slot)

    _wait_block_copies(nch_ref, b, lambda n: copy(slot, 0, 0, n))

    col = col_ref[...]
    loc1 = col[:, 0:1].astype(jnp.int32)
    loc2 = col[:, 1:2].astype(jnp.int32)
    r = lax.broadcasted_iota(jnp.int32, (col.shape[0], y_ref.shape[1]), 1)
    weights = jnp.where(r == loc1, col[:, 2:3], jnp.where(r == loc2, col[:, 3:4], 0.0)).astype(BF16)
    out = _rmsnorm_rows(x_ref[...] + _dot(weights, y_ref[slot]), gf_ref[...])
    first = 0
    for o_ref, n_blocks in zip(o_refs, blocks):
        @pl.when((b >= first) & (b < first + n_blocks))
        def _(o_ref=o_ref):
            o_ref[...] = out
        first += n_blocks


def _moe_combine(src, dst, nch, x, col, g_final, ys, tb, blocks):
    t, d = x.shape
    rows = _moe_block_rows(tb)
    return pl.pallas_call(
        functools.partial(_combine_body, blocks=blocks),
        grid_spec=pltpu.PrefetchScalarGridSpec(
            num_scalar_prefetch=3,
            grid=(t // tb,),
            in_specs=[
                pl.BlockSpec((tb, d), lambda i, *_: (i, 0)),
                pl.BlockSpec((tb, LANES), lambda i, *_: (i, 0)),
                pl.BlockSpec((1, d), lambda i, *_: (0, 0)),
                pl.BlockSpec(memory_space=pl.ANY),
            ],
            out_specs=_group_block_specs((tb, d), blocks),
            scratch_shapes=[pltpu.VMEM((2, rows, d), BF16), pltpu.SemaphoreType.DMA((2,))],
        ),
        out_shape=[jax.ShapeDtypeStruct((nb * tb, d), F32) for nb in blocks],
        compiler_params=_cparams(("arbitrary",)),
        name="moe_combine",
    )(src, dst, nch, x, col, g_final, ys)


def _moe(acts, w_out, xs, g, w, moe_weights, g_final):
    tb = MOE_TB
    assert all(x.shape[0] % tb == 0 for x in xs), [x.shape for x in xs]
    blocks = [x.shape[0] // tb for x in xs]
    nb = sum(blocks)
    t = nb * tb
    x, h, col, rowf, cnt = _route(acts, w_out, xs, g, w["w_router_hi"], w["w_router_lo"], w["b_router"], tb)
    n_tiles = -(-(2 * t + nb * N_EXPERTS * (MOE_ALIGN - 1)) // MOE_TILE) + N_EXPERTS
    plan = _moe_plan(cnt)
    xg = _moe_gather(plan, h, rowf, n_tiles, tb)
    yg = _expert_ffn(plan["first_tile"], plan["tile_count"], xg, *moe_weights)
    return _moe_combine(plan["src"], plan["dst"], plan["chunks"], x, col, g_final, yg, tb, blocks)


def _retention_tables(c_real, ct):
    h = np.arange(RET_HEADS, dtype=np.float64)
    log_gamma = np.log1p(-np.exp2(-5.0 - h))
    idx = np.arange(ct, dtype=np.float64)
    live = idx < c_real
    diff = idx[:, None] - idx[None, :]
    causal = (diff >= 0) & live[:, None] & live[None, :]
    decay = np.where(causal[None], np.exp(np.where(causal, diff, 0.0)[None] * log_gamma[:, None, None]), 0.0)
    q_dec = np.where(live[None], np.exp((idx + 1.0)[None] * log_gamma[:, None]), 0.0)
    k_dec = np.where(live[None], np.exp((c_real - 1.0 - idx)[None] * log_gamma[:, None]), 0.0)
    q_dec = np.broadcast_to(q_dec[..., None], (RET_HEADS, ct, LANES))
    k_dec = np.broadcast_to(k_dec[..., None], (RET_HEADS, ct, LANES))
    return (jnp.asarray(decay, F32), jnp.asarray(q_dec, F32), jnp.asarray(k_dec, F32))


def _rope_tables(pos0, length):
    half = HEAD_DIM // 2
    inv = ROPE_BASE ** (-np.arange(half, dtype=np.float64) / half)
    ang = (pos0 + np.arange(length, dtype=np.float64))[:, None] * inv[None, :]
    cos = np.concatenate([np.cos(ang), np.cos(ang)], axis=-1)
    sin = np.concatenate([-np.sin(ang), np.sin(ang)], axis=-1)
    return jnp.asarray(cos, F32), jnp.asarray(sin, F32)


def _retention_body(q_ref, k_ref, v_ref, g_ref, cos_ref, sin_ref, dec_ref, qd_ref, kd_ref, s0_ref, gn_ref,
                    o_ref, s_ref, *, c_real, ct, bb):
    @pl.when(pl.program_id(1) == 0)
    def _():
        s_ref[...] = s0_ref[...]

    cos = cos_ref[...]
    sin = sin_ref[...]
    for g, hd in [(g, hd) for g in range(bb) for hd in range(RET_HEADS)]:
        lo, hi = hd * HEAD_DIM, (hd + 1) * HEAD_DIM
        q = q_ref[g, :, lo:hi].astype(F32)
        k = k_ref[g, :, lo:hi].astype(F32)
        q = (q * cos + pltpu.roll(q, HEAD_DIM // 2, 1) * sin) * (HEAD_DIM ** -0.5)
        k = k * cos + pltpu.roll(k, HEAD_DIM // 2, 1) * sin
        v = v_ref[g, :, lo:hi].astype(F32)
        qb = q.astype(BF16)
        q_dec = qd_ref[hd, :c_real, :]
        state = s_ref[g, hd]

        s = _dot_nt(qb, _pad_rows(k, ct).astype(BF16)) * dec_ref[hd, :c_real, :]
        o = _dot(s.astype(BF16), _pad_rows(v, ct).astype(BF16)) + _dot(qb, state.astype(BF16)) * q_dec
        chunk_dec = q_dec[c_real - 1:c_real, :]
        s_ref[g, hd] = state * chunk_dec + _dot_tn((k * kd_ref[hd, :c_real, :]).astype(BF16), v.astype(BF16))

        o = o - jnp.mean(o, axis=-1, keepdims=True)
        o = o * lax.rsqrt(jnp.mean(o * o, axis=-1, keepdims=True) + EPS)
        o_ref[g, :, lo:hi] = _silu(g_ref[g, :, lo:hi].astype(F32)) * (o * gn_ref[:, lo:hi])


def _retention(proj, pos0, state, gn, bb):
    b, length, _ = proj.shape
    c_real = math.gcd(length, RET_CHUNK)
    ct = RET_CHUNK
    width = RET_HEADS * HEAD_DIM
    cos, sin = _rope_tables(pos0, length)
    decay, q_dec, k_dec = _retention_tables(c_real, ct)

    def col(i):
        return pl.BlockSpec((bb, c_real, width), lambda bi, c: (bi, c, i))

    full = lambda shape: pl.BlockSpec(shape, lambda bi, c: (0,) * len(shape))
    state_spec = pl.BlockSpec((bb, RET_HEADS, HEAD_DIM, HEAD_DIM), lambda bi, c: (bi, 0, 0, 0))
    return pl.pallas_call(
        functools.partial(_retention_body, c_real=c_real, ct=ct, bb=bb),
        grid=(b // bb, length // c_real),
        in_specs=[
            col(0), col(1), col(2), col(3),
            pl.BlockSpec((c_real, HEAD_DIM), lambda bi, c: (c, 0)),
            pl.BlockSpec((c_real, HEAD_DIM), lambda bi, c: (c, 0)),
            full((RET_HEADS, ct, ct)), full((RET_HEADS, ct, LANES)), full((RET_HEADS, ct, LANES)),
            state_spec,
            full((1, width)),
        ],
        out_specs=[
            pl.BlockSpec((bb, c_real, width), lambda bi, c: (bi, c, 0)),
            state_spec,
        ],
        out_shape=[
            jax.ShapeDtypeStruct((b, length, width), F32),
            jax.ShapeDtypeStruct(state.shape, F32),
        ],
        compiler_params=_cparams(("parallel", "arbitrary")),
        name="retention",
    )(proj, proj, proj, proj, cos, sin, decay, q_dec, k_dec, state, gn)


def _scan_rows(x, op, fill):
    rows = x.shape[0]
    row = lax.broadcasted_iota(jnp.int32, x.shape, 0)
    sh = 1
    while sh < rows:
        x = op(x, jnp.where(row >= sh, pltpu.roll(x, sh, 0), fill))
        sh *= 2
    return x


def _mlstm_body(q_ref, k_ref, v_ref, og_ref, gate_ref, bg_ref, cw_ref, cb_ref, cs_ref, c0_ref, n0_ref, m0_ref,
                gn_ref, o_ref, c_ref, n_ref, m_ref, conv_ref, xc_ref, *, c_real, ct, bb):
    tail = ML_CONV - 1

    @pl.when(pl.program_id(1) == 0)
    def _():
        c_ref[...] = c0_ref[...]
        n_ref[...] = n0_ref[...]
        m_ref[...] = m0_ref[...]
        xc_ref[:, SUBLANES - tail:SUBLANES, :] = cs_ref[...]

    for g in range(bb):
        _mlstm_sequence(q_ref.at[g], k_ref.at[g], v_ref.at[g], og_ref.at[g], gate_ref.at[g], bg_ref, cw_ref, cb_ref,
                        gn_ref, o_ref.at[g], c_ref.at[g], n_ref.at[g], m_ref.at[g], conv_ref.at[g], xc_ref.at[g],
                        c_real=c_real, ct=ct)


def _mlstm_sequence(q_ref, k_ref, v_ref, og_ref, gate_ref, bg_ref, cw_ref, cb_ref, gn_ref, o_ref, c_ref, n_ref, m_ref,
                    conv_ref, xc_ref, *, c_real, ct):
    width = ML_HEADS * HEAD_DIM
    tail = ML_CONV - 1

    xc_ref[SUBLANES:SUBLANES + c_real, :width] = q_ref[...].astype(F32)
    xc_ref[SUBLANES:SUBLANES + c_real, width:] = k_ref[...].astype(F32)
    y = cb_ref[...] + cw_ref[tail:tail + 1, :] * xc_ref[SUBLANES:SUBLANES + c_real, :]
    for j in range(tail):
        y = y + cw_ref[j:j + 1, :] * xc_ref[SUBLANES - tail + j:SUBLANES - tail + j + c_real, :]
    new_tail = xc_ref[c_real:c_real + SUBLANES, :]
    conv_ref[...] = new_tail[SUBLANES - tail:, :]
    xc_ref[0:SUBLANES, :] = new_tail
    qk = _silu(y)
    q_all = qk[:, :width]
    k_all = qk[:, width:] * (HEAD_DIM ** -0.5)
    v_all = v_ref[...].astype(F32)

    gates = gate_ref[...] + bg_ref[...]
    f_pre = pltpu.roll(gates, LANES - ML_HEADS, 1)
    log_f = jnp.minimum(f_pre, 0.0) - jnp.log(1.0 + jnp.exp(-jnp.abs(f_pre)))
    b_cum = _scan_rows(log_f, jnp.add, 0.0)
    src = gates - b_cum
    cmx = _scan_rows(src, jnp.maximum, NEG_INF)
    m_prev = m_ref[...]
    mx = jnp.maximum(m_prev, cmx)
    m_t = b_cum + mx
    w_prev = jnp.exp(m_prev - mx)
    e_neg_m = jnp.exp(-m_t)
    last = c_real - 1
    b_last = b_cum[last:last + 1, :]
    m_new = m_t[last:last + 1, :]
    w_end = jnp.exp(b_last + src - m_new)
    dec = jnp.exp(b_last + m_prev - m_new)
    m_ref[...] = m_new
    src_t = jnp.transpose(_pad_rows(src, ct))

    t_idx = lax.broadcasted_iota(jnp.int32, (c_real, ct), 0)
    s_idx = lax.broadcasted_iota(jnp.int32, (c_real, ct), 1)
    live = (s_idx <= t_idx) & (s_idx < c_real)

    for hd in range(ML_HEADS):
        lo, hi = hd * HEAD_DIM, (hd + 1) * HEAD_DIM
        q = q_all[:, lo:hi]
        k = k_all[:, lo:hi]
        v = v_all[:, lo:hi]
        qb = q.astype(BF16)
        c_state = c_ref[hd]
        n_state = n_ref[hd:hd + 1, :]
        wp = w_prev[:, hd:hd + 1]
        w = jnp.where(live, jnp.exp(src_t[hd:hd + 1, :] - mx[:, hd:hd + 1]), 0.0)
        s = _dot_nt(qb, _pad_rows(k, ct).astype(BF16)) * w
        num = _dot(s.astype(BF16), _pad_rows(v, ct).astype(BF16)) + wp * _dot(qb, c_state.astype(BF16))
        den = jnp.sum(s, axis=-1, keepdims=True) + wp * jnp.sum(q * n_state, axis=-1, keepdims=True)
        hh = num / jnp.maximum(jnp.abs(den), e_neg_m[:, hd:hd + 1])
        kw = k * w_end[:, hd:hd + 1]
        dec_h = dec[:, hd:hd + 1]
        c_ref[hd] = dec_h * c_state + _dot_tn(kw.astype(BF16), v.astype(BF16))
        n_ref[hd:hd + 1, :] = dec_h * n_state + jnp.sum(kw, axis=0, keepdims=True)

        hh = _sigmoid(og_ref[:, lo:hi].astype(F32)) * hh
        hh = hh - jnp.mean(hh, axis=-1, keepdims=True)
        hh = hh * lax.rsqrt(jnp.mean(hh * hh, axis=-1, keepdims=True) + EPS)
        o_ref[:, lo:hi] = hh * gn_ref[:, lo:hi]


def _mlstm(proj, gates, b_gates, conv_w, conv_b, conv_s, c0, n0, m0, gn, bb):
    b, length, _ = proj.shape
    c_real = math.gcd(length, ML_CHUNK)
    ct = ML_CHUNK
    width = ML_HEADS * HEAD_DIM
    base = (2 * RET_HEADS * HEAD_DIM + 2 * RET_HEADS * HEAD_DIM) // width

    def col(i):
        return pl.BlockSpec((bb, c_real, width), lambda bi, c: (bi, c, base + i))

    full = lambda shape: pl.BlockSpec(shape, lambda bi, c: (0,) * len(shape))
    per_b = lambda shape: pl.BlockSpec((bb,) + shape, lambda bi, c: (bi,) + (0,) * len(shape))
    bg = jnp.zeros((1, LANES), F32).at[0, :2 * ML_HEADS].set(b_gates)
    m0p = jnp.zeros((b, 1, LANES), F32).at[:, 0, :ML_HEADS].set(m0)
    outs = pl.pallas_call(
        functools.partial(_mlstm_body, c_real=c_real, ct=ct, bb=bb),
        grid=(b // bb, length // c_real),
        in_specs=[
            col(0), col(1), col(2), col(3),
            pl.BlockSpec((bb, c_real, LANES), lambda bi, c: (bi, c, 0)),
            full((1, LANES)),
            full((ML_CONV, 2 * width)),
            full((1, 2 * width)),
            per_b((ML_CONV - 1, 2 * width)),
            per_b((ML_HEADS, HEAD_DIM, HEAD_DIM)),
            per_b((ML_HEADS, HEAD_DIM)),
            per_b((1, LANES)),
            full((1, width)),
        ],
        out_specs=[
            pl.BlockSpec((bb, c_real, width), lambda bi, c: (bi, c, 0)),
            per_b((ML_HEADS, HEAD_DIM, HEAD_DIM)),
            per_b((ML_HEADS, HEAD_DIM)),
            per_b((1, LANES)),
            per_b((ML_CONV - 1, 2 * width)),
        ],
        out_shape=[
            jax.ShapeDtypeStruct((b, length, width), F32),
            jax.ShapeDtypeStruct(c0.shape, F32),
            jax.ShapeDtypeStruct(n0.shape, F32),
            jax.ShapeDtypeStruct((b, 1, LANES), F32),
            jax.ShapeDtypeStruct(conv_s.shape, F32),
        ],
        scratch_shapes=[pltpu.VMEM((bb, ct + SUBLANES, 2 * width), F32)],
        compiler_params=_cparams(("parallel", "arbitrary")),
        name="mlstm",
    )(proj, proj, proj, proj, gates, bg, conv_w, conv_b, conv_s, c0, n0, m0p, gn)
    h_m, c_new, n_new, m_new, conv_new = outs
    return h_m, c_new, n_new, m_new[:, 0, :ML_HEADS], conv_new


HG_UNROLL_BLOCK = 8
HG_UNROLL_STEP = 8


def _hgrn_head(cols, q_ref, f_ref, i_ref, g_ref, hl_ref, gn_ref, o_ref, st, *, blk, sub):
    hl = hl_ref[:, cols]
    hmax = jnp.max(hl, axis=0, keepdims=True)
    ex = jnp.exp(hl - hmax)
    p = ex / jnp.sum(ex, axis=0, keepdims=True)
    lower = (p[0:1] + p[1:2]) - p[0:1]
    gn = gn_ref[:, cols]

    f = lower + (1.0 - lower) * _sigmoid(f_ref[:, cols])
    log_k = jnp.log2(jnp.maximum(1.0 - f, 0.0))
    q = _silu(q_ref[:, cols].astype(F32))
    v = i_ref[:, cols].astype(F32)
    gate = _silu(g_ref[:, cols].astype(F32))

    row = lax.broadcasted_iota(jnp.int32, (blk, LANES), 0)
    rsub = row & (sub - 1)
    a = jnp.log2(f)
    sh = 1
    while sh < sub:
        a = a + jnp.where(rsub >= sh, pltpu.roll(a, sh, 0), 0.0)
        sh *= 2
    c = a - log_k
    q_in = q * jnp.exp2(a)

    t_idx = lax.broadcasted_iota(jnp.int32, (SUBLANES, LANES), 0)
    for j in range(blk // sub):
        r0 = j * sub
        a_end = a[r0 + sub - 1:r0 + sub, :]
        inter = _dot_nt(q_in[r0:r0 + sub].astype(BF16), st.astype(BF16))
        k_hat = jnp.exp2(a_end - c[r0:r0 + sub])
        v_j = v[r0:r0 + sub]
        st = st * jnp.exp2(a_end) + _dot_tn(v_j.astype(BF16), k_hat.astype(BF16))
        for part in range(sub // SUBLANES):
            t0 = r0 + part * SUBLANES
            a_t = a[t0:t0 + SUBLANES]
            q_t = q[t0:t0 + SUBLANES]
            o_t = inter[part * SUBLANES:(part + 1) * SUBLANES]
            for s in range((part + 1) * SUBLANES):
                arg = a_t - c[r0 + s:r0 + s + 1, :]
                if s >= part * SUBLANES:
                    arg = jnp.where(t_idx >= s - part * SUBLANES, arg, NEG_INF)
                z = q_t * jnp.exp2(arg)
                o_t = o_t + jnp.sum(z, axis=-1, keepdims=True) * v[r0 + s:r0 + s + 1, :]
            o_t = o_t * lax.rsqrt(jnp.mean(o_t * o_t, axis=-1, keepdims=True) + EPS)
            o_ref[t0:t0 + SUBLANES, cols] = o_t * gn * gate[t0:t0 + SUBLANES]
    return st


def _block_ref_rows(a, level, row8):
    rows = a.shape[0]
    half = level // 2
    if half >= SUBLANES:
        return jnp.concatenate([jnp.broadcast_to(a[b0 + half - 1:b0 + half, :], (level, LANES))
                                for b0 in range(0, rows, level)], axis=0)
    if level == 2:
        return jnp.where((row8 & 1) == 1, pltpu.roll(a, 1, 0), a)
    pieces = []
    for v0 in range(0, rows, SUBLANES):
        picks = [jnp.broadcast_to(a[v0 + b0 + half - 1:v0 + b0 + half, :], (SUBLANES, LANES))
                 for b0 in range(0, SUBLANES, level)]
        piece = picks[-1]
        for i in range(len(picks) - 2, -1, -1):
            piece = jnp.where(row8[:SUBLANES] < (i + 1) * level, picks[i], piece)
        pieces.append(piece)
    return jnp.concatenate(pieces, axis=0)


def _hgrn_head_block(cols, q_ref, f_ref, i_ref, g_ref, hl_ref, gn_ref, o_ref, st, *, blk):
    hl = hl_ref[:, cols]
    hmax = jnp.max(hl, axis=0, keepdims=True)
    ex = jnp.exp(hl - hmax)
    p = ex / jnp.sum(ex, axis=0, keepdims=True)
    lower = (p[0:1] + p[1:2]) - p[0:1]

    f = lower + (1.0 - lower) * _sigmoid(f_ref[:, cols])
    kk = 1.0 - f
    q = _silu(q_ref[:, cols].astype(F32))
    vb = i_ref[:, cols]
    a = _scan_rows(jnp.log2(f), jnp.add, 0.0)
    c = a - jnp.log2(jnp.maximum(kk, 0.0))
    row8 = lax.broadcasted_iota(jnp.int32, (blk, LANES), 0) & (SUBLANES - 1)
    group = (lax.broadcasted_iota(jnp.int32, (blk, blk), 0)
             ^ lax.broadcasted_iota(jnp.int32, (blk, blk), 1))

    scores = None
    level = blk
    while level >= 2:
        half = level // 2
        if half >= SUBLANES:
            zero = jnp.zeros((half, LANES), F32)
            qs, ks = [], []
            for b0 in range(0, blk, level):
                ref = a[b0 + half - 1:b0 + half, :]
                qs += [zero, q[b0 + half:b0 + level] * jnp.exp2(a[b0 + half:b0 + level] - ref)]
                ks += [jnp.exp2(ref - c[b0:b0 + half]), zero]
            q_l = jnp.concatenate(qs, axis=0)
            k_l = jnp.concatenate(ks, axis=0)
        else:
            ref = _block_ref_rows(a, level, row8)
            upper = (row8 & (level - 1)) >= half
            q_l = jnp.where(upper, q * jnp.exp2(a - ref), 0.0)
            k_l = jnp.where(upper, 0.0, jnp.exp2(ref - c))
        r = _dot_nt(q_l.astype(BF16), k_l.astype(BF16))
        scores = r if scores is None else jnp.where(group < level, r, scores)
        level = half
    scores = jnp.where(group < 1, _dot_nt(q.astype(BF16), kk.astype(BF16)), scores)

    a_end = a[blk - 1:blk, :]
    o = _dot_nt((q * jnp.exp2(a)).astype(BF16), st.astype(BF16)) + _dot(scores.astype(BF16), vb)
    o = o * lax.rsqrt(jnp.mean(o * o, axis=-1, keepdims=True) + EPS)
    o_ref[:, cols] = o * gn_ref[:, cols] * _silu(g_ref[:, cols].astype(F32))
    return st * jnp.exp2(a_end) + _dot_tn(vb, jnp.exp2(a_end - c).astype(BF16))


def _hgrn_body(q_ref, f_ref, i_ref, g_ref, hl_ref, s0_ref, gn_ref, o_ref, s_ref, st_ref, *, blk, sub, bb):
    @pl.when(pl.program_id(1) == 0)
    def _():
        for g, h in [(g, h) for g in range(bb) for h in range(HG_HEADS)]:
            st_ref[g, h] = jnp.transpose(s0_ref[g, h])

    if blk == HG_BLOCK:
        head, per_iter = functools.partial(_hgrn_head_block, blk=blk), HG_UNROLL_BLOCK // bb
    else:
        head, per_iter = functools.partial(_hgrn_head, blk=blk, sub=sub), HG_UNROLL_STEP // bb

    def heads(it, carry):
        for g, u in [(g, u) for g in range(bb) for u in range(per_iter)]:
            h = it * per_iter + u
            cols = pl.ds(pl.multiple_of(h * HEAD_DIM, HEAD_DIM), HEAD_DIM)
            st_ref[g, h] = head(cols, q_ref.at[g], f_ref.at[g], i_ref.at[g], g_ref.at[g], hl_ref, gn_ref,
                                o_ref.at[g], st_ref[g, h])
        return carry

    lax.fori_loop(0, HG_HEADS // per_iter, heads, 0)

    @pl.when(pl.program_id(1) == pl.num_programs(1) - 1)
    def _():
        for g, h in [(g, h) for g in range(bb) for h in range(HG_HEADS)]:
            s_ref[g, h] = jnp.transpose(st_ref[g, h])


def _hgrn(proj, f_pre, hg_lower, state, gn, bb):
    b, length, _ = proj.shape
    blk = min(length, HG_BLOCK)
    sub = math.gcd(length, HG_SUB)
    width = HG_HEADS * HEAD_DIM

    def col(i):
        return pl.BlockSpec((bb, blk, width), lambda bi, c: (bi, c, i))

    full = lambda shape: pl.BlockSpec(shape, lambda bi, c: (0,) * len(shape))
    state_spec = pl.BlockSpec((bb, HG_HEADS, HEAD_DIM, HEAD_DIM), lambda bi, c: (bi, 0, 0, 0))
    return pl.pallas_call(
        functools.partial(_hgrn_body, blk=blk, sub=sub, bb=bb),
        grid=(b // bb, length // blk),
        in_specs=[col(0), col(0), col(1), col(2), full(hg_lower.shape), state_spec, full((1, width))],
        out_specs=[pl.BlockSpec((bb, blk, width), lambda bi, c: (bi, c, 0)), state_spec],
        out_shape=[
            jax.ShapeDtypeStruct((b, length, width), F32),
            jax.ShapeDtypeStruct(state.shape, F32),
        ],
        scratch_shapes=[pltpu.VMEM((bb, HG_HEADS, HEAD_DIM, HEAD_DIM), F32)],
        compiler_params=_cparams(("parallel", "arbitrary")),
        name="hgrn2",
    )(proj, f_pre, proj, proj, hg_lower, state, gn)


def _prep_weights(p):
    w_in_ab = p["w_in_ab"][0]
    w_in_c = p["w_in_c"][0]
    width = HG_HEADS * HEAD_DIM
    w_gates = jnp.zeros((D_MODEL, LANES), BF16).at[:, :2 * ML_HEADS].set(w_in_ab[:, AB_MAIN:].astype(BF16))
    w_router = jnp.zeros((LANES, D_MODEL), F32).at[:N_EXPERTS, :].set(p["w_router"][0].T)
    w_router_hi = w_router.astype(BF16)
    w_router_lo = (w_router - w_router_hi.astype(F32)).astype(BF16)
    b_router = jnp.broadcast_to(p["b_router"][0][:, None], (N_EXPERTS, LANES))
    return {
        "w_ab": w_in_ab[:, :AB_MAIN].astype(BF16),
        "w_ab_gates": w_gates,
        "w_out_ab": p["w_out_ab"][0].astype(BF16),
        "w_c": jnp.concatenate([w_in_c[:, :width], w_in_c[:, 2 * width:]], axis=1).astype(BF16),
        "w_c_forget": w_in_c[:, width:2 * width].astype(BF16),
        "w_out_c": p["w_out_c"][0].astype(BF16),
        "w_ffn_gate": p["w_ffn_gate"][0].astype(BF16),
        "w_ffn_up": p["w_ffn_up"][0].astype(BF16),
        "w_ffn_down": p["w_ffn_down"][0].astype(BF16),
        "w_router_hi": w_router_hi,
        "w_router_lo": w_router_lo,
        "b_router": b_router,
    }


def _mixers(x, pos0, ret_s, mc_s, mn_s, mm_s, conv_s, hg_s, p, w):
    b, length, d = x.shape
    t = b * length
    tm = min(512, t)
    tm_mm = min(1024, t)
    prompt = length >= RET_CHUNK
    bb = 1 if prompt else 8
    assert t % tm_mm == 0 and t % tm == 0 and b % bb == 0 and b % 2 == 0, (b, length)
    row = lambda v: v.reshape(1, -1)
    x0 = x.reshape(t, d)

    proj, gates = _norm_matmul(x0, row(p["ln_mix"][0]), w["w_ab"], w["w_ab_gates"], tm_mm, 1024)
    proj = proj.reshape(b, length, AB_MAIN)
    o_ret, ret_new = _retention(proj, pos0, ret_s, row(p["ret_gn"][0]), 2 if prompt else bb)
    h_m, c_new, n_new, m_new, conv_new = _mlstm(
        proj, gates.reshape(b, length, LANES), p["b_gates_ab"][0], p["conv_w_ab"][0], row(p["conv_b_ab"][0]),
        conv_s, mc_s, mn_s, mm_s, row(p["ml_gn"][0]), bb)
    half = RET_HEADS * HEAD_DIM
    x2 = _ffn([o_ret.reshape(t, half), h_m.reshape(t, half)], [w["w_out_ab"][:half], w["w_out_ab"][half:]], x0,
              row(p["ln_ffn"][0]), w["w_ffn_gate"], w["w_ffn_up"], w["w_ffn_down"], tm, D_FF // 2)

    proj_c, f_pre = _norm_matmul(x2, row(p["ln_mix"][1]), w["w_c"], w["w_c_forget"], tm_mm, 1024)
    o_hg, hg_new = _hgrn(proj_c.reshape(b, length, 3 * d), f_pre.reshape(b, length, d), p["hg_lower"], hg_s,
                         row(p["hg_gn"][0]), bb)
    states = (ret_new[None], c_new[None], n_new[None], m_new[None], conv_new[None], hg_new[None])
    return x2, o_hg.reshape(t, d), states


def _experts(groups, shapes, p, w):
    row = lambda v: v.reshape(1, -1)
    ys = _moe([o for _, o in groups], w["w_out_c"], [x for x, _ in groups], row(p["ln_ffn"][1]), w,
              (p["w_moe_gate"][0], p["w_moe_up"][0], p["w_moe_down"][0]), row(p["ln_final"]))
    return [y.reshape(shape) for y, shape in zip(ys, shapes)]


def _trunk(x, pos0, ret_s, mc_s, mn_s, mm_s, conv_s, hg_s, p, w):
    x2, o_hg, states = _mixers(x, pos0, ret_s, mc_s, mn_s, mm_s, conv_s, hg_s, p, w)
    (y,) = _experts([(x2, o_hg)], [x.shape], p, w)
    return (y,) + states


def kernel(x_prompt, x_sample, state_ret, state_mlstm_c, state_mlstm_n, state_mlstm_m, state_conv, state_hgrn,
           ln_mix, ln_ffn, ln_final, w_in_ab, b_gates_ab, conv_w_ab, conv_b_ab, ret_gn, ml_gn, w_out_ab,
           w_in_c, hg_lower, hg_gn, w_out_c, w_ffn_gate, w_ffn_up, w_ffn_down,
           w_router, b_router, w_moe_gate, w_moe_up, w_moe_down):
    p = {"ln_mix": ln_mix, "ln_ffn": ln_ffn, "ln_final": ln_final,
         "w_in_ab": w_in_ab, "b_gates_ab": b_gates_ab, "conv_w_ab": conv_w_ab, "conv_b_ab": conv_b_ab,
         "ret_gn": ret_gn, "ml_gn": ml_gn, "w_out_ab": w_out_ab,
         "w_in_c": w_in_c, "hg_lower": hg_lower, "hg_gn": hg_gn, "w_out_c": w_out_c,
         "w_ffn_gate": w_ffn_gate, "w_ffn_up": w_ffn_up, "w_ffn_down": w_ffn_down,
         "w_router": w_router, "b_router": b_router,
         "w_moe_gate": w_moe_gate, "w_moe_up": w_moe_up, "w_moe_down": w_moe_down}
    w = _prep_weights(p)
    bp = x_prompt.shape[0]
    zeros = lambda *shape: jnp.zeros(shape, F32)
    x2_p, o_p, states_p = _mixers(
        x_prompt, 0,
        zeros(bp, RET_HEADS, HEAD_DIM, HEAD_DIM), zeros(bp, ML_HEADS, HEAD_DIM, HEAD_DIM),
        zeros(bp, ML_HEADS, HEAD_DIM), jnp.full((bp, ML_HEADS), NEG_INF, F32),
        zeros(bp, ML_CONV - 1, 2 * ML_HEADS * HEAD_DIM), zeros(bp, HG_HEADS, HEAD_DIM, HEAD_DIM), p, w)
    x2_s, o_s, states_s = _mixers(
        x_sample, 16384,
        state_ret[0], state_mlstm_c[0], state_mlstm_n[0], state_mlstm_m[0], state_conv[0], state_hgrn[0], p, w)
    y_p, y_s = _experts([(x2_p, o_p), (x2_s, o_s)], [x_prompt.shape, x_sample.shape], p, w)
    out = [y_p, y_s]
    for a, b in zip(states_p, states_s):
        out += [a, b]
    return tuple(out)
```

```python
import functools
import math

import numpy as np
import jax
import jax.numpy as jnp
from jax import lax
from jax.experimental import pallas as pl
from jax.experimental.pallas import tpu as pltpu

F32 = jnp.float32
BF16 = jnp.bfloat16

D_MODEL = 1024
HEAD_DIM = 128
RET_HEADS = 4
ML_HEADS = 4
ML_CONV = 4
HG_HEADS = 8
D_FF = 2816
N_EXPERTS = 8
ROPE_BASE = 10000.0
EPS = 1e-6
RET_CHUNK = 128
ML_CHUNK = 128
HG_BLOCK = 128
HG_SUB = 16

LANES = 128
SUBLANES = 8
AB_MAIN = 4096
VMEM_LIMIT = 56 * 1024 * 1024

NEG_INF = float("-inf")


def _cparams(sem):
    return pltpu.CompilerParams(dimension_semantics=sem, vmem_limit_bytes=VMEM_LIMIT)


def _sigmoid(x):
    return 1.0 / (1.0 + jnp.exp(-x))


def _silu(x):
    return x * _sigmoid(x)


def _rmsnorm_rows(x, g):
    ms = jnp.mean(x * x, axis=-1, keepdims=True)
    return x * lax.rsqrt(ms + EPS) * g


def _dot(a, b):
    return jnp.dot(a, b, preferred_element_type=F32)


def _dot_nt(a, b):
    return lax.dot_general(a, b, (((1,), (1,)), ((), ())), preferred_element_type=F32)


def _dot_tn(a, b):
    return lax.dot_general(a, b, (((0,), (0,)), ((), ())), preferred_element_type=F32)


def _pad_rows(x, rows):
    if x.shape[0] == rows:
        return x
    return jnp.concatenate([x, jnp.zeros((rows - x.shape[0], x.shape[1]), x.dtype)], axis=0)


def _norm_matmul_body(x_ref, g_ref, w_ref, ws_ref, o_ref, os_ref, h_ref):
    @pl.when(pl.program_id(1) == 0)
    def _():
        h = _rmsnorm_rows(x_ref[...], g_ref[...]).astype(BF16)
        h_ref[...] = h
        os_ref[...] = _dot(h, ws_ref[...])

    o_ref[...] = _dot(h_ref[...], w_ref[...]).astype(BF16)


def _norm_matmul(x, g, w, w_side, tm, tn):
    t, d = x.shape
    n, ns = w.shape[1], w_side.shape[1]
    return pl.pallas_call(
        _norm_matmul_body,
        grid=(t // tm, n // tn),
        in_specs=[
            pl.BlockSpec((tm, d), lambda i, j: (i, 0)),
            pl.BlockSpec((1, d), lambda i, j: (0, 0)),
            pl.BlockSpec((d, tn), lambda i, j: (0, j)),
            pl.BlockSpec((d, ns), lambda i, j: (0, 0)),
        ],
        out_specs=[pl.BlockSpec((tm, tn), lambda i, j: (i, j)), pl.BlockSpec((tm, ns), lambda i, j: (i, 0))],
        out_shape=[jax.ShapeDtypeStruct((t, n), BF16), jax.ShapeDtypeStruct((t, ns), F32)],
        scratch_shapes=[pltpu.VMEM((tm, d), BF16)],
        compiler_params=_cparams(("parallel", "arbitrary")),
        name="norm_matmul",
    )(x, g, w, w_side)


def _mixer_residual(x_ref, a_refs, w_refs):
    x = x_ref[...]
    for a_ref, w_ref in zip(a_refs, w_refs):
        x = x + _dot(a_ref[...].astype(BF16), w_ref[...])
    return x


def _ffn_body(*refs, n_in):
    a_refs, w_refs = refs[:n_in], refs[n_in:2 * n_in]
    x_ref, g_ref, wg_ref, wu_ref, wd_ref, o_ref, h_ref, x1_ref = refs[2 * n_in:]
    f = pl.program_id(1)

    @pl.when(f == 0)
    def _():
        x1 = _mixer_residual(x_ref, a_refs, w_refs)
        x1_ref[...] = x1
        h_ref[...] = _rmsnorm_rows(x1, g_ref[...]).astype(BF16)

    h = h_ref[...]
    a = _dot(h, wg_ref[...])
    u = _dot(h, wu_ref[...])
    act = (_silu(a) * u).astype(BF16)
    x1_ref[...] += _dot(act, wd_ref[...])

    @pl.when(f == pl.num_programs(1) - 1)
    def _():
        o_ref[...] = x1_ref[...]


def _ffn(acts, weights, x, g, wg, wu, wd, tm, tf):
    t, d = x.shape
    dff = wg.shape[1]
    n_in = len(acts)
    in_specs = [pl.BlockSpec((tm, a.shape[1]), lambda i, f: (i, 0)) for a in acts]
    in_specs += [pl.BlockSpec(w.shape, lambda i, f: (0, 0)) for w in weights]
    in_specs += [
        pl.BlockSpec((tm, d), lambda i, f: (i, 0)),
        pl.BlockSpec((1, d), lambda i, f: (0, 0)),
        pl.BlockSpec((d, tf), lambda i, f: (0, f)),
        pl.BlockSpec((d, tf), lambda i, f: (0, f)),
        pl.BlockSpec((tf, d), lambda i, f: (f, 0)),
    ]
    return pl.pallas_call(
        functools.partial(_ffn_body, n_in=n_in),
        grid=(t // tm, dff // tf),
        in_specs=in_specs,
        out_specs=pl.BlockSpec((tm, d), lambda i, f: (i, 0)),
        out_shape=jax.ShapeDtypeStruct((t, d), F32),
        scratch_shapes=[pltpu.VMEM((tm, d), BF16), pltpu.VMEM((tm, d), F32)],
        compiler_params=_cparams(("parallel", "arbitrary")),
        name="ffn",
    )(*acts, *weights, x, g, wg, wu, wd)


MOE_TB = 512
MOE_ALIGN = 16
MOE_BURST = 4
MOE_TILE = 256
MOE_CHUNKS_PER_TILE = MOE_TILE // MOE_ALIGN


def _moe_block_rows(tb):
    return 2 * tb + N_EXPERTS * MOE_ALIGN


def _group_block_specs(shape, blocks):
    specs, first = [], 0
    for nb in blocks:
        specs.append(pl.BlockSpec(shape, lambda i, *_, first=first, nb=nb: (jnp.clip(i - first, 0, nb - 1), 0)))
        first += nb
    return specs


def _route_body(*refs, blocks):
    n = len(blocks)
    a_refs, wo_ref, x_refs = refs[:n], refs[n], refs[n + 1:2 * n + 1]
    g_ref, whi_ref, wlo_ref, b_ref, tri_ref, x3_ref, h_ref, col_ref, rowf_ref, cnt_ref = refs[2 * n + 1:]
    i = pl.program_id(0)
    act, x_in = a_refs[0][...], x_refs[0][...]
    first = blocks[0]
    for a_ref, x_ref, nb in zip(a_refs[1:], x_refs[1:], blocks[1:]):
        act = jnp.where(i >= first, a_ref[...], act)
        x_in = jnp.where(i >= first, x_ref[...], x_in)
        first += nb
    x3 = x_in + _dot(act.astype(BF16), wo_ref[...])
    x3_ref[...] = x3
    h = _rmsnorm_rows(x3, g_ref[...])
    h_hi = h.astype(BF16)
    h_ref[...] = h_hi
    h_lo = (h - h_hi.astype(F32)).astype(BF16)

    logits = (_dot_nt(whi_ref[...], h_hi) + _dot_nt(wlo_ref[...], h_hi) + _dot_nt(whi_ref[...], h_lo))
    logits = logits[:N_EXPERTS, :] + b_ref[:, 0:1]
    row = lax.broadcasted_iota(jnp.int32, logits.shape, 0)
    m1 = jnp.max(logits, axis=0, keepdims=True)
    i1 = jnp.min(jnp.where(logits == m1, row, N_EXPERTS), axis=0, keepdims=True)
    lg2 = jnp.where(row == i1, NEG_INF, logits)
    m2 = jnp.max(lg2, axis=0, keepdims=True)
    i2 = jnp.min(jnp.where(lg2 == m2, row, N_EXPERTS), axis=0, keepdims=True)
    e = jnp.exp(m2 - m1)
    w1 = 1.0 / (1.0 + e)
    w2 = e / (1.0 + e)

    sel = jnp.where(row == i1, 1.0, jnp.where(row == i2, 1.0, 0.0))
    before = _dot(sel.astype(BF16), tri_ref[...])
    n = jnp.broadcast_to(jnp.sum(sel, axis=1, keepdims=True), (N_EXPERTS, LANES))
    padded = jnp.floor((n + (MOE_ALIGN - 1.0)) * (1.0 / MOE_ALIGN)) * MOE_ALIGN
    base = _scan_rows(padded, jnp.add, 0.0) - padded
    pos = base[:, 0:1] + before
    loc1 = jnp.sum(jnp.where(row == i1, pos, 0.0), axis=0, keepdims=True)
    loc2 = jnp.sum(jnp.where(row == i2, pos, 0.0), axis=0, keepdims=True)
    rowf = jnp.where(row == 0, loc1, jnp.where(row == 1, loc2, jnp.where(row == 2, w1,
                                                                          jnp.where(row == 3, w2, 0.0))))
    rowf_ref[0] = rowf
    for c in range(rowf.shape[1] // LANES):
        col_ref[c * LANES:(c + 1) * LANES, :] = jnp.transpose(_pad_rows(rowf[:, c * LANES:(c + 1) * LANES], LANES))
    cnt_ref[0] = n


def _route(acts, w_out, xs, g, w_hi, w_lo, b, tb):
    d = xs[0].shape[1]
    blocks = [x.shape[0] // tb for x in xs]
    nb = sum(blocks)
    t = nb * tb
    tri = jnp.asarray(np.triu(np.ones((tb, tb), np.float32), 1), BF16)
    full = lambda shape: pl.BlockSpec(shape, lambda i: (0,) * len(shape))
    return pl.pallas_call(
        functools.partial(_route_body, blocks=blocks),
        grid=(nb,),
        in_specs=(_group_block_specs((tb, acts[0].shape[1]), blocks) + [full(w_out.shape)]
                  + _group_block_specs((tb, d), blocks)
                  + [full((1, d)), full((LANES, d)), full((LANES, d)), full((SUBLANES, LANES)), full((tb, tb))]),
        out_specs=[
            pl.BlockSpec((tb, d), lambda i: (i, 0)),
            pl.BlockSpec((tb, d), lambda i: (i, 0)),
            pl.BlockSpec((tb, LANES), lambda i: (i, 0)),
            pl.BlockSpec((1, SUBLANES, tb), lambda i: (i, 0, 0)),
            pl.BlockSpec((1, SUBLANES, LANES), lambda i: (i, 0, 0)),
        ],
        out_shape=[
            jax.ShapeDtypeStruct((t, d), F32),
            jax.ShapeDtypeStruct((t, d), BF16),
            jax.ShapeDtypeStruct((t, LANES), F32),
            jax.ShapeDtypeStruct((nb, SUBLANES, tb), F32),
            jax.ShapeDtypeStruct((nb, SUBLANES, LANES), F32),
        ],
        compiler_params=_cparams(("parallel",)),
        name="route",
    )(*acts, w_out, *xs, g, w_hi, w_lo, b, tri)


def _moe_plan(cnt):
    n = cnt[:, :N_EXPERTS, 0].astype(jnp.int32)
    chunks = (n + (MOE_ALIGN - 1)) // MOE_ALIGN
    src = jnp.cumsum(chunks, axis=1) - chunks
    seg_chunks = jnp.sum(chunks, axis=0)
    seg_tiles = (seg_chunks + (MOE_CHUNKS_PER_TILE - 1)) // MOE_CHUNKS_PER_TILE
    seg_end = jnp.cumsum(seg_tiles)
    seg_first = (seg_end - seg_tiles) * MOE_CHUNKS_PER_TILE
    dst = seg_first[None, :] + jnp.cumsum(chunks, axis=0) - chunks
    flat = lambda a: a.reshape(-1).astype(jnp.int32)
    return {"src": flat(src), "dst": flat(dst), "chunks": flat(chunks),
            "first_tile": flat(seg_end - seg_tiles), "tile_count": flat(seg_tiles),
            "tail_start": flat(seg_first + seg_chunks), "tail_count": flat(seg_tiles * MOE_CHUNKS_PER_TILE - seg_chunks)}


def _chunk_rows(chunk, n=1):
    return pl.ds(pl.multiple_of(chunk * MOE_ALIGN, MOE_ALIGN), n * MOE_ALIGN)


def _for_block_copies(nch_ref, blk, fn):
    for e in range(N_EXPERTS):
        count = nch_ref[blk * N_EXPERTS + e]
        bursts = count // MOE_BURST
        lax.fori_loop(0, bursts, lambda i, carry, e=e: (fn(e, i * MOE_BURST, MOE_BURST), carry)[1], 0)
        lax.fori_loop(bursts * MOE_BURST, count, lambda c, carry, e=e: (fn(e, c, 1), carry)[1], 0)


def _block_copy_counts(nch_ref, blk):
    bursts = singles = 0
    for e in range(N_EXPERTS):
        count = nch_ref[blk * N_EXPERTS + e]
        bursts = bursts + count // MOE_BURST
        singles = singles + lax.rem(count, MOE_BURST)
    return bursts, singles


def _wait_block_copies(nch_ref, blk, copy):
    bursts, singles = _block_copy_counts(nch_ref, blk)
    lax.fori_loop(0, bursts, lambda i, carry: (copy(MOE_BURST).wait(), carry)[1], 0)
    lax.fori_loop(0, singles, lambda i, carry: (copy(1).wait(), carry)[1], 0)


def _gather_body(src_ref, dst_ref, nch_ref, tail_ref, ntail_ref, last_ref, h_ref, rowf_ref, hs_ref, z_ref, sem,
                 *, n_tiles):
    b = pl.program_id(0)
    slot = lax.rem(b, 2)
    tb = h_ref.shape[0]
    loc = rowf_ref[0].astype(jnp.int32)
    r = lax.broadcasted_iota(jnp.int32, (z_ref.shape[1], tb), 0)
    onehot = jnp.where(r == loc[0:1, :], 1.0, jnp.where(r == loc[1:2, :], 1.0, 0.0)).astype(BF16)
    z_ref[slot] = _dot(onehot, h_ref[...]).astype(BF16)

    def copy(buf, src_chunk, dst_chunk, n=1):
        return pltpu.make_async_copy(z_ref.at[buf, _chunk_rows(src_chunk, n)], hs_ref.at[_chunk_rows(dst_chunk, n)],
                                     sem.at[buf])

    def drain(blk, buf):
        _wait_block_copies(nch_ref, blk, lambda n: copy(buf, 0, 0, n))

    @pl.when(b > 0)
    def _():
        drain(b - 1, 1 - slot)

    _for_block_copies(nch_ref, b, lambda e, c, n: copy(slot, src_ref[b * N_EXPERTS + e] + c,
                                                       dst_ref[b * N_EXPERTS + e] + c, n).start())

    @pl.when(b == pl.num_programs(0) - 1)
    def _():
        drain(b, slot)
        z_ref[0, 0:MOE_ALIGN, :] = jnp.zeros((MOE_ALIGN, z_ref.shape[2]), BF16)
        past = last_ref[0]
        total = n_tiles * MOE_CHUNKS_PER_TILE
        for e in range(N_EXPERTS):
            lax.fori_loop(0, ntail_ref[e], lambda c, carry, e=e: (copy(0, 0, tail_ref[e] + c).start(), carry)[1], 0)
        lax.fori_loop(past, total, lambda c, carry: (copy(0, 0, c).start(), carry)[1], 0)
        for e in range(N_EXPERTS):
            lax.fori_loop(0, ntail_ref[e], lambda c, carry: (copy(0, 0, 0).wait(), carry)[1], 0)
        lax.fori_loop(past, total, lambda c, carry: (copy(0, 0, 0).wait(), carry)[1], 0)


def _moe_gather(plan, h, rowf, n_tiles, tb):
    t, d = h.shape
    rows = _moe_block_rows(tb)
    past = (plan["first_tile"][-1:] + plan["tile_count"][-1:]) * MOE_CHUNKS_PER_TILE
    return pl.pallas_call(
        functools.partial(_gather_body, n_tiles=n_tiles),
        grid_spec=pltpu.PrefetchScalarGridSpec(
            num_scalar_prefetch=6,
            grid=(t // tb,),
            in_specs=[
                pl.BlockSpec((tb, d), lambda i, *_: (i, 0)),
                pl.BlockSpec((1, SUBLANES, tb), lambda i, *_: (i, 0, 0)),
            ],
            out_specs=pl.BlockSpec(memory_space=pl.ANY),
            scratch_shapes=[pltpu.VMEM((2, rows, d), BF16), pltpu.SemaphoreType.DMA((2,))],
        ),
        out_shape=jax.ShapeDtypeStruct((n_tiles * MOE_TILE, d), BF16),
        compiler_params=_cparams(("arbitrary",)),
        name="moe_gather",
    )(plan["src"], plan["dst"], plan["chunks"], plan["tail_start"], plan["tail_count"], past, h, rowf)


W_CHUNK = 256


def _expert_ffn_body(first_ref, count_ref, xs_ref, wg_hbm, wu_hbm, wd_hbm, ys_ref,
                     wg_b, wu_b, wd_b, stage_c, stage_r, x_buf, y_buf, w_sem, in_sem, out_sem, *, n_tiles, halves):
    e = pl.program_id(0)
    last = pl.num_programs(0) - 1
    first = first_ref[e]
    count = count_ref[e]
    cur = lax.rem(e, 2)
    dff = wg_b.shape[2]
    tf = dff // halves
    per = dff // W_CHUNK
    n_chunks = 3 * per

    def span(j):
        return pl.ds(pl.multiple_of(j * W_CHUNK, W_CHUNK), W_CHUNK)

    def fetch(kind, expert, j, slot):
        if kind == 2:
            return pltpu.make_async_copy(wd_hbm.at[expert, span(j), :], stage_r.at[slot], w_sem.at[slot])
        return pltpu.make_async_copy((wg_hbm, wu_hbm)[kind].at[expert, :, span(j)], stage_c.at[slot], w_sem.at[slot])

    def for_kind(c, fn):
        kind = c // per
        for k in range(3):
            @pl.when(kind == k)
            def _(k=k):
                fn(k, c - k * per)

    n_pairs = -(-n_chunks // 2)

    def start_pair(expert, p):
        for s in range(2):
            c = 2 * p + s

            @pl.when(c < n_chunks)
            def _(c=c, s=s):
                for_kind(c, lambda k, j: fetch(k, expert, j, s).start())

    def convert_pair(expert, dst, p):
        for s in range(2):
            c = 2 * p + s

            def land(k, j, s=s):
                fetch(k, expert, j, s).wait()
                if k == 2:
                    wd_b[dst, span(j), :] = stage_r[s].astype(BF16)
                else:
                    (wg_b, wu_b)[k][dst, :, span(j)] = stage_c[s].astype(BF16)

            @pl.when(c < n_chunks)
            def _(c=c, land=land):
                for_kind(c, land)

        start_pair(expert, p + 1)

    def maybe_convert(p):
        @pl.when(p < n_pairs)
        def _():
            convert_pair(e + 1, 1 - cur, p)
        return jnp.minimum(p + 1, n_pairs)

    @pl.when(e == 0)
    def _():
        start_pair(0, 0)
        lax.fori_loop(0, n_pairs, lambda p, carry: (convert_pair(0, 0, p), carry)[1], 0)

    @pl.when(e < last)
    def _():
        start_pair(e + 1, 0)

    def tile_rows(tile):
        return pl.ds(pl.multiple_of(tile * MOE_TILE, MOE_TILE), MOE_TILE)

    def load(j, buf):
        return pltpu.make_async_copy(xs_ref.at[tile_rows(first + j)], x_buf.at[buf], in_sem.at[buf])

    def store(tile, buf):
        return pltpu.make_async_copy(y_buf.at[buf], ys_ref.at[tile_rows(tile)], out_sem.at[buf])

    @pl.when(count > 0)
    def _():
        load(0, 0).start()

    def tile_step(j, c):
        buf = lax.rem(j, 2)

        @pl.when(j + 1 < count)
        def _():
            load(j + 1, 1 - buf).start()

        load(j, buf).wait()
        xs = x_buf[buf]
        y = None
        for f in range(halves):
            a = _dot(xs, wg_b[cur, :, f * tf:(f + 1) * tf])
            u = _dot(xs, wu_b[cur, :, f * tf:(f + 1) * tf])
            part = _dot((_silu(a) * u).astype(BF16), wd_b[cur, f * tf:(f + 1) * tf, :])
            y = part if y is None else y + part

        @pl.when(j >= 2)
        def _():
            store(first, buf).wait()

        y_buf[buf] = y.astype(BF16)
        store(first + j, buf).start()
        return maybe_convert(c)

    p_done = lax.fori_loop(0, count, tile_step, jnp.where(e < last, 0, n_pairs))
    lax.fori_loop(p_done, n_pairs, lambda p, carry: (convert_pair(e + 1, 1 - cur, p), carry)[1], 0)

    for back in (2, 1):
        @pl.when(count >= back)
        def _(back=back):
            store(first, lax.rem(count - back, 2)).wait()

    @pl.when(e == last)
    def _():
        end = first + count
        y_buf[0] = jnp.zeros(y_buf.shape[1:], BF16)
        lax.fori_loop(end, n_tiles, lambda t, c: (store(t, 0).start(), c)[1], 0)
        lax.fori_loop(end, n_tiles, lambda t, c: (store(t, 0).wait(), c)[1], 0)


def _expert_ffn(first_tile, tile_count, xs, wg, wu, wd):
    rows, d = xs.shape
    dff = wg.shape[2]
    any_space = pl.BlockSpec(memory_space=pl.ANY)
    return pl.pallas_call(
        functools.partial(_expert_ffn_body, n_tiles=rows // MOE_TILE, halves=2),
        grid_spec=pltpu.PrefetchScalarGridSpec(
            num_scalar_prefetch=2,
            grid=(N_EXPERTS,),
            in_specs=[any_space, any_space, any_space, any_space],
            out_specs=any_space,
            scratch_shapes=[
                pltpu.VMEM((2, d, dff), BF16), pltpu.VMEM((2, d, dff), BF16), pltpu.VMEM((2, dff, d), BF16),
                pltpu.VMEM((2, d, W_CHUNK), F32), pltpu.VMEM((2, W_CHUNK, d), F32),
                pltpu.VMEM((2, MOE_TILE, d), BF16), pltpu.VMEM((2, MOE_TILE, d), BF16),
                pltpu.SemaphoreType.DMA((2,)), pltpu.SemaphoreType.DMA((2,)), pltpu.SemaphoreType.DMA((2,))],
        ),
        out_shape=jax.ShapeDtypeStruct((rows, d), BF16),
        compiler_params=_cparams(("arbitrary",)),
        name="expert_ffn",
    )(first_tile, tile_count, xs, wg, wu, wd)


def _combine_body(src_ref, dst_ref, nch_ref, x_ref, col_ref, gf_ref, ys_ref, *refs, blocks):
    o_refs, (y_ref, sem) = refs[:len(blocks)], refs[len(blocks):]
    b = pl.program_id(0)
    nb = pl.num_programs(0)
    slot = lax.rem(b, 2)

    def copy(buf, src_chunk, dst_chunk, n=1):
        return pltpu.make_async_copy(ys_ref.at[_chunk_rows(dst_chunk, n)], y_ref.at[buf, _chunk_rows(src_chunk, n)],
                                     sem.at[buf])

    def fetch(blk, buf):
        _for_block_copies(nch_ref, blk, lambda e, c, n: copy(buf, src_ref[blk * N_EXPERTS + e] + c,
                                                             dst_ref[blk * N_EXPERTS + e] + c, n).start())

    @pl.when(b == 0)
    def _():
        y_ref[...] = jnp.zeros_like(y_ref)
        fetch(0, 0)

    @pl.when(b + 1 < nb)
    def _():
        fetch(b + 1, 1 - slot)

    _wait_block_copies(nch_ref, b, lambda n: copy(slot, 0, 0, n))

    col = col_ref[...]
    loc1 = col[:, 0:1].astype(jnp.int32)
    loc2 = col[:, 1:2].astype(jnp.int32)
    r = lax.broadcasted_iota(jnp.int32, (col.shape[0], y_ref.shape[1]), 1)
    weights = jnp.where(r == loc1, col[:, 2:3], jnp.where(r == loc2, col[:, 3:4], 0.0)).astype(BF16)
    out = _rmsnorm_rows(x_ref[...] + _dot(weights, y_ref[slot]), gf_ref[...])
    first = 0
    for o_ref, n_blocks in zip(o_refs, blocks):
        @pl.when((b >= first) & (b < first + n_blocks))
        def _(o_ref=o_ref):
            o_ref[...] = out
        first += n_blocks


def _moe_combine(src, dst, nch, x, col, g_final, ys, tb, blocks):
    t, d = x.shape
    rows = _moe_block_rows(tb)
    return pl.pallas_call(
        functools.partial(_combine_body, blocks=blocks),
        grid_spec=pltpu.PrefetchScalarGridSpec(
            num_scalar_prefetch=3,
            grid=(t // tb,),
            in_specs=[
                pl.BlockSpec((tb, d), lambda i, *_: (i, 0)),
                pl.BlockSpec((tb, LANES), lambda i, *_: (i, 0)),
                pl.BlockSpec((1, d), lambda i, *_: (0, 0)),
                pl.BlockSpec(memory_space=pl.ANY),
            ],
            out_specs=_group_block_specs((tb, d), blocks),
            scratch_shapes=[pltpu.VMEM((2, rows, d), BF16), pltpu.SemaphoreType.DMA((2,))],
        ),
        out_shape=[jax.ShapeDtypeStruct((nb * tb, d), F32) for nb in blocks],
        compiler_params=_cparams(("arbitrary",)),
        name="moe_combine",
    )(src, dst, nch, x, col, g_final, ys)


def _moe(acts, w_out, xs, g, w, moe_weights, g_final):
    tb = MOE_TB
    assert all(x.shape[0] % tb == 0 for x in xs), [x.shape for x in xs]
    blocks = [x.shape[0] // tb for x in xs]
    nb = sum(blocks)
    t = nb * tb
    x, h, col, rowf, cnt = _route(acts, w_out, xs, g, w["w_router_hi"], w["w_router_lo"], w["b_router"], tb)
    n_tiles = -(-(2 * t + nb * N_EXPERTS * (MOE_ALIGN - 1)) // MOE_TILE) + N_EXPERTS
    plan = _moe_plan(cnt)
    xg = _moe_gather(plan, h, rowf, n_tiles, tb)
    yg = _expert_ffn(plan["first_tile"], plan["tile_count"], xg, *moe_weights)
    return _moe_combine(plan["src"], plan["dst"], plan["chunks"], x, col, g_final, yg, tb, blocks)


def _retention_tables(c_real, ct):
    h = np.arange(RET_HEADS, dtype=np.float64)
    log_gamma = np.log1p(-np.exp2(-5.0 - h))
    idx = np.arange(ct, dtype=np.float64)
    live = idx < c_real
    diff = idx[:, None] - idx[None, :]
    causal = (diff >= 0) & live[:, None] & live[None, :]
    decay = np.where(causal[None], np.exp(np.where(causal, diff, 0.0)[None] * log_gamma[:, None, None]), 0.0)
    q_dec = np.where(live[None], np.exp((idx + 1.0)[None] * log_gamma[:, None]), 0.0)
    k_dec = np.where(live[None], np.exp((c_real - 1.0 - idx)[None] * log_gamma[:, None]), 0.0)
    q_dec = np.broadcast_to(q_dec[..., None], (RET_HEADS, ct, LANES))
    k_dec = np.broadcast_to(k_dec[..., None], (RET_HEADS, ct, LANES))
    return (jnp.asarray(decay, F32), jnp.asarray(q_dec, F32), jnp.asarray(k_dec, F32))


def _rope_tables(pos0, length):
    half = HEAD_DIM // 2
    inv = ROPE_BASE ** (-np.arange(half, dtype=np.float64) / half)
    ang = (pos0 + np.arange(length, dtype=np.float64))[:, None] * inv[None, :]
    cos = np.concatenate([np.cos(ang), np.cos(ang)], axis=-1)
    sin = np.concatenate([-np.sin(ang), np.sin(ang)], axis=-1)
    return jnp.asarray(cos, F32), jnp.asarray(sin, F32)


def _retention_body(q_ref, k_ref, v_ref, g_ref, cos_ref, sin_ref, dec_ref, qd_ref, kd_ref, s0_ref, gn_ref,
                    o_ref, s_ref, *, c_real, ct, bb):
    @pl.when(pl.program_id(1) == 0)
    def _():
        s_ref[...] = s0_ref[...]

    cos = cos_ref[...]
    sin = sin_ref[...]
    for g, hd in [(g, hd) for g in range(bb) for hd in range(RET_HEADS)]:
        lo, hi = hd * HEAD_DIM, (hd + 1) * HEAD_DIM
        q = q_ref[g, :, lo:hi].astype(F32)
        k = k_ref[g, :, lo:hi].astype(F32)
        q = (q * cos + pltpu.roll(q, HEAD_DIM // 2, 1) * sin) * (HEAD_DIM ** -0.5)
        k = k * cos + pltpu.roll(k, HEAD_DIM // 2, 1) * sin
        v = v_ref[g, :, lo:hi].astype(F32)
        qb = q.astype(BF16)
        q_dec = qd_ref[hd, :c_real, :]
        state = s_ref[g, hd]

        s = _dot_nt(qb, _pad_rows(k, ct).astype(BF16)) * dec_ref[hd, :c_real, :]
        o = _dot(s.astype(BF16), _pad_rows(v, ct).astype(BF16)) + _dot(qb, state.astype(BF16)) * q_dec
        chunk_dec = q_dec[c_real - 1:c_real, :]
        s_ref[g, hd] = state * chunk_dec + _dot_tn((k * kd_ref[hd, :c_real, :]).astype(BF16), v.astype(BF16))

        o = o - jnp.mean(o, axis=-1, keepdims=True)
        o = o * lax.rsqrt(jnp.mean(o * o, axis=-1, keepdims=True) + EPS)
        o_ref[g, :, lo:hi] = _silu(g_ref[g, :, lo:hi].astype(F32)) * (o * gn_ref[:, lo:hi])


def _retention(proj, pos0, state, gn, bb):
    b, length, _ = proj.shape
    assert b % bb == 0, (b, bb)
    c_real = math.gcd(length, RET_CHUNK)
    ct = RET_CHUNK
    width = RET_HEADS * HEAD_DIM
    cos, sin = _rope_tables(pos0, length)
    decay, q_dec, k_dec = _retention_tables(c_real, ct)

    def col(i):
        return pl.BlockSpec((bb, c_real, width), lambda bi, c: (bi, c, i))

    full = lambda shape: pl.BlockSpec(shape, lambda bi, c: (0,) * len(shape))
    state_spec = pl.BlockSpec((bb, RET_HEADS, HEAD_DIM, HEAD_DIM), lambda bi, c: (bi, 0, 0, 0))
    return pl.pallas_call(
        functools.partial(_retention_body, c_real=c_real, ct=ct, bb=bb),
        grid=(b // bb, length // c_real),
        in_specs=[
            col(0), col(1), col(2), col(3),
            pl.BlockSpec((c_real, HEAD_DIM), lambda bi, c: (c, 0)),
            pl.BlockSpec((c_real, HEAD_DIM), lambda bi, c: (c, 0)),
            full((RET_HEADS, ct, ct)), full((RET_HEADS, ct, LANES)), full((RET_HEADS, ct, LANES)),
            state_spec,
            full((1, width)),
        ],
        out_specs=[
            pl.BlockSpec((bb, c_real, width), lambda bi, c: (bi, c, 0)),
            state_spec,
        ],
        out_shape=[
            jax.ShapeDtypeStruct((b, length, width), F32),
            jax.ShapeDtypeStruct(state.shape, F32),
        ],
        compiler_params=_cparams(("parallel", "arbitrary")),
        name="retention",
    )(proj, proj, proj, proj, cos, sin, decay, q_dec, k_dec, state, gn)


def _scan_rows(x, op, fill):
    rows = x.shape[0]
    row = lax.broadcasted_iota(jnp.int32, x.shape, 0)
    sh = 1
    while sh < rows:
        x = op(x, jnp.where(row >= sh, pltpu.roll(x, sh, 0), fill))
        sh *= 2
    return x


def _mlstm_body(q_ref, k_ref, v_ref, og_ref, gate_ref, bg_ref, cw_ref, cb_ref, cs_ref, c0_ref, n0_ref, m0_ref,
                gn_ref, o_ref, c_ref, n_ref, m_ref, conv_ref, xc_ref, *, c_real, ct, bb):
    tail = ML_CONV - 1

    @pl.when(pl.program_id(1) == 0)
    def _():
        c_ref[...] = c0_ref[...]
        n_ref[...] = n0_ref[...]
        m_ref[...] = m0_ref[...]
        xc_ref[:, SUBLANES - tail:SUBLANES, :] = cs_ref[...]

    for g in range(bb):
        _mlstm_sequence(q_ref.at[g], k_ref.at[g], v_ref.at[g], og_ref.at[g], gate_ref.at[g], bg_ref, cw_ref, cb_ref,
                        gn_ref, o_ref.at[g], c_ref.at[g], n_ref.at[g], m_ref.at[g], conv_ref.at[g], xc_ref.at[g],
                        c_real=c_real, ct=ct)


def _mlstm_sequence(q_ref, k_ref, v_ref, og_ref, gate_ref, bg_ref, cw_ref, cb_ref, gn_ref, o_ref, c_ref, n_ref, m_ref,
                    conv_ref, xc_ref, *, c_real, ct):
    width = ML_HEADS * HEAD_DIM
    tail = ML_CONV - 1

    xc_ref[SUBLANES:SUBLANES + c_real, :width] = q_ref[...].astype(F32)
    xc_ref[SUBLANES:SUBLANES + c_real, width:] = k_ref[...].astype(F32)
    y = cb_ref[...] + cw_ref[tail:tail + 1, :] * xc_ref[SUBLANES:SUBLANES + c_real, :]
    for j in range(tail):
        y = y + cw_ref[j:j + 1, :] * xc_ref[SUBLANES - tail + j:SUBLANES - tail + j + c_real, :]
    new_tail = xc_ref[c_real:c_real + SUBLANES, :]
    conv_ref[...] = new_tail[SUBLANES - tail:, :]
    xc_ref[0:SUBLANES, :] = new_tail
    qk = _silu(y)
    q_all = qk[:, :width]
    k_all = qk[:, width:] * (HEAD_DIM ** -0.5)
    v_all = v_ref[...].astype(F32)

    gates = gate_ref[...] + bg_ref[...]
    f_pre = pltpu.roll(gates, LANES - ML_HEADS, 1)
    log_f = jnp.minimum(f_pre, 0.0) - jnp.log(1.0 + jnp.exp(-jnp.abs(f_pre)))
    b_cum = _scan_rows(log_f, jnp.add, 0.0)
    src = gates - b_cum
    cmx = _scan_rows(src, jnp.maximum, NEG_INF)
    m_prev = m_ref[...]
    mx = jnp.maximum(m_prev, cmx)
    m_t = b_cum + mx
    w_prev = jnp.exp(m_prev - mx)
    e_neg_m = jnp.exp(-m_t)
    last = c_real - 1
    b_last = b_cum[last:last + 1, :]
    m_new = m_t[last:last + 1, :]
    w_end = jnp.exp(b_last + src - m_new)
    dec = jnp.exp(b_last + m_prev - m_new)
    m_ref[...] = m_new
    src_t = jnp.transpose(_pad_rows(src, ct))

    t_idx = lax.broadcasted_iota(jnp.int32, (c_real, ct), 0)
    s_idx = lax.broadcasted_iota(jnp.int32, (c_real, ct), 1)
    live = (s_idx <= t_idx) & (s_idx < c_real)

    for hd in range(ML_HEADS):
        lo, hi = hd * HEAD_DIM, (hd + 1) * HEAD_DIM
        q = q_all[:, lo:hi]
        k = k_all[:, lo:hi]
        v = v_all[:, lo:hi]
        qb = q.astype(BF16)
        c_state = c_ref[hd]
        n_state = n_ref[hd:hd + 1, :]
        wp = w_prev[:, hd:hd + 1]
        w = jnp.where(live, jnp.exp(src_t[hd:hd + 1, :] - mx[:, hd:hd + 1]), 0.0)
        s = _dot_nt(qb, _pad_rows(k, ct).astype(BF16)) * w
        num = _dot(s.astype(BF16), _pad_rows(v, ct).astype(BF16)) + wp * _dot(qb, c_state.astype(BF16))
        den = jnp.sum(s, axis=-1, keepdims=True) + wp * jnp.sum(q * n_state, axis=-1, keepdims=True)
        hh = num / jnp.maximum(jnp.abs(den), e_neg_m[:, hd:hd + 1])
        kw = k * w_end[:, hd:hd + 1]
        dec_h = dec[:, hd:hd + 1]
        c_ref[hd] = dec_h * c_state + _dot_tn(kw.astype(BF16), v.astype(BF16))
        n_ref[hd:hd + 1, :] = dec_h * n_state + jnp.sum(kw, axis=0, keepdims=True)

        hh = _sigmoid(og_ref[:, lo:hi].astype(F32)) * hh
        hh = hh - jnp.mean(hh, axis=-1, keepdims=True)
        hh = hh * lax.rsqrt(jnp.mean(hh * hh, axis=-1, keepdims=True) + EPS)
        o_ref[:, lo:hi] = hh * gn_ref[:, lo:hi]


def _mlstm(proj, gates, b_gates, conv_w, conv_b, conv_s, c0, n0, m0, gn, bb):
    b, length, _ = proj.shape
    assert b % bb == 0, (b, bb)
    c_real = math.gcd(length, ML_CHUNK)
    ct = ML_CHUNK
    width = ML_HEADS * HEAD_DIM
    base = (2 * RET_HEADS * HEAD_DIM + 2 * RET_HEADS * HEAD_DIM) // width

    def col(i):
        return pl.BlockSpec((bb, c_real, width), lambda bi, c: (bi, c, base + i))

    full = lambda shape: pl.BlockSpec(shape, lambda bi, c: (0,) * len(shape))
    per_b = lambda shape: pl.BlockSpec((bb,) + shape, lambda bi, c: (bi,) + (0,) * len(shape))
    bg = jnp.zeros((1, LANES), F32).at[0, :2 * ML_HEADS].set(b_gates)
    m0p = jnp.zeros((b, 1, LANES), F32).at[:, 0, :ML_HEADS].set(m0)
    outs = pl.pallas_call(
        functools.partial(_mlstm_body, c_real=c_real, ct=ct, bb=bb),
        grid=(b // bb, length // c_real),
        in_specs=[
            col(0), col(1), col(2), col(3),
            pl.BlockSpec((bb, c_real, LANES), lambda bi, c: (bi, c, 0)),
            full((1, LANES)),
            full((ML_CONV, 2 * width)),
            full((1, 2 * width)),
            per_b((ML_CONV - 1, 2 * width)),
            per_b((ML_HEADS, HEAD_DIM, HEAD_DIM)),
            per_b((ML_HEADS, HEAD_DIM)),
            per_b((1, LANES)),
            full((1, width)),
        ],
        out_specs=[
            pl.BlockSpec((bb, c_real, width), lambda bi, c: (bi, c, 0)),
            per_b((ML_HEADS, HEAD_DIM, HEAD_DIM)),
            per_b((ML_HEADS, HEAD_DIM)),
            per_b((1, LANES)),
            per_b((ML_CONV - 1, 2 * width)),
        ],
        out_shape=[
            jax.ShapeDtypeStruct((b, length, width), F32),
            jax.ShapeDtypeStruct(c0.shape, F32),
            jax.ShapeDtypeStruct(n0.shape, F32),
            jax.ShapeDtypeStruct((b, 1, LANES), F32),
            jax.ShapeDtypeStruct(conv_s.shape, F32),
        ],
        scratch_shapes=[pltpu.VMEM((bb, ct + SUBLANES, 2 * width), F32)],
        compiler_params=_cparams(("parallel", "arbitrary")),
        name="mlstm",
    )(proj, proj, proj, proj, gates, bg, conv_w, conv_b, conv_s, c0, n0, m0p, gn)
    h_m, c_new, n_new, m_new, conv_new = outs
    return h_m, c_new, n_new, m_new[:, 0, :ML_HEADS], conv_new


HG_UNROLL_BLOCK = 8
HG_UNROLL_STEP = 8


def _hgrn_head(cols, q_ref, f_ref, i_ref, g_ref, hl_ref, gn_ref, o_ref, st, *, blk, sub):
    hl = hl_ref[:, cols]
    hmax = jnp.max(hl, axis=0, keepdims=True)
    ex = jnp.exp(hl - hmax)
    p = ex / jnp.sum(ex, axis=0, keepdims=True)
    lower = (p[0:1] + p[1:2]) - p[0:1]
    gn = gn_ref[:, cols]

    f = lower + (1.0 - lower) * _sigmoid(f_ref[:, cols])
    log_k = jnp.log2(jnp.maximum(1.0 - f, 0.0))
    q = _silu(q_ref[:, cols].astype(F32))
    v = i_ref[:, cols].astype(F32)
    gate = _silu(g_ref[:, cols].astype(F32))

    row = lax.broadcasted_iota(jnp.int32, (blk, LANES), 0)
    rsub = row & (sub - 1)
    a = jnp.log2(f)
    sh = 1
    while sh < sub:
        a = a + jnp.where(rsub >= sh, pltpu.roll(a, sh, 0), 0.0)
        sh *= 2
    c = a - log_k
    q_in = q * jnp.exp2(a)

    t_idx = lax.broadcasted_iota(jnp.int32, (SUBLANES, LANES), 0)
    for j in range(blk // sub):
        r0 = j * sub
        a_end = a[r0 + sub - 1:r0 + sub, :]
        inter = _dot_nt(q_in[r0:r0 + sub].astype(BF16), st.astype(BF16))
        k_hat = jnp.exp2(a_end - c[r0:r0 + sub])
        v_j = v[r0:r0 + sub]
        st = st * jnp.exp2(a_end) + _dot_tn(v_j.astype(BF16), k_hat.astype(BF16))
        for part in range(sub // SUBLANES):
            t0 = r0 + part * SUBLANES
            a_t = a[t0:t0 + SUBLANES]
            q_t = q[t0:t0 + SUBLANES]
            o_t = inter[part * SUBLANES:(part + 1) * SUBLANES]
            for s in range((part + 1) * SUBLANES):
                arg = a_t - c[r0 + s:r0 + s + 1, :]
                if s >= part * SUBLANES:
                    arg = jnp.where(t_idx >= s - part * SUBLANES, arg, NEG_INF)
                z = q_t * jnp.exp2(arg)
                o_t = o_t + jnp.sum(z, axis=-1, keepdims=True) * v[r0 + s:r0 + s + 1, :]
            o_t = o_t * lax.rsqrt(jnp.mean(o_t * o_t, axis=-1, keepdims=True) + EPS)
            o_ref[t0:t0 + SUBLANES, cols] = o_t * gn * gate[t0:t0 + SUBLANES]
    return st


def _block_ref_rows(a, level, row8):
    rows = a.shape[0]
    half = level // 2
    if half >= SUBLANES:
        return jnp.concatenate([jnp.broadcast_to(a[b0 + half - 1:b0 + half, :], (level, LANES))
                                for b0 in range(0, rows, level)], axis=0)
    if level == 2:
        return jnp.where((row8 & 1) == 1, pltpu.roll(a, 1, 0), a)
    pieces = []
    for v0 in range(0, rows, SUBLANES):
        picks = [jnp.broadcast_to(a[v0 + b0 + half - 1:v0 + b0 + half, :], (SUBLANES, LANES))
                 for b0 in range(0, SUBLANES, level)]
        piece = picks[-1]
        for i in range(len(picks) - 2, -1, -1):
            piece = jnp.where(row8[:SUBLANES] < (i + 1) * level, picks[i], piece)
        pieces.append(piece)
    return jnp.concatenate(pieces, axis=0)


def _hgrn_head_block(cols, q_ref, f_ref, i_ref, g_ref, hl_ref, gn_ref, o_ref, st, *, blk):
    hl = hl_ref[:, cols]
    hmax = jnp.max(hl, axis=0, keepdims=True)
    ex = jnp.exp(hl - hmax)
    p = ex / jnp.sum(ex, axis=0, keepdims=True)
    lower = (p[0:1] + p[1:2]) - p[0:1]

    f = lower + (1.0 - lower) * _sigmoid(f_ref[:, cols])
    kk = 1.0 - f
    q = _silu(q_ref[:, cols].astype(F32))
    vb = i_ref[:, cols]
    a = _scan_rows(jnp.log2(f), jnp.add, 0.0)
    c = a - jnp.log2(jnp.maximum(kk, 0.0))
    row8 = lax.broadcasted_iota(jnp.int32, (blk, LANES), 0) & (SUBLANES - 1)
    group = (lax.broadcasted_iota(jnp.int32, (blk, blk), 0)
             ^ lax.broadcasted_iota(jnp.int32, (blk, blk), 1))

    scores = None
    level = blk
    while level >= 2:
        half = level // 2
        if half >= SUBLANES:
            zero = jnp.zeros((half, LANES), F32)
            qs, ks = [], []
            for b0 in range(0, blk, level):
                ref = a[b0 + half - 1:b0 + half, :]
                qs += [zero, q[b0 + half:b0 + level] * jnp.exp2(a[b0 + half:b0 + level] - ref)]
                ks += [jnp.exp2(ref - c[b0:b0 + half]), zero]
            q_l = jnp.concatenate(qs, axis=0)
            k_l = jnp.concatenate(ks, axis=0)
        else:
            ref = _block_ref_rows(a, level, row8)
            upper = (row8 & (level - 1)) >= half
            q_l = jnp.where(upper, q * jnp.exp2(a - ref), 0.0)
            k_l = jnp.where(upper, 0.0, jnp.exp2(ref - c))
        r = _dot_nt(q_l.astype(BF16), k_l.astype(BF16))
        scores = r if scores is None else jnp.where(group < level, r, scores)
        level = half
    scores = jnp.where(group < 1, _dot_nt(q.astype(BF16), kk.astype(BF16)), scores)

    a_end = a[blk - 1:blk, :]
    o = _dot_nt((q * jnp.exp2(a)).astype(BF16), st.astype(BF16)) + _dot(scores.astype(BF16), vb)
    o = o * lax.rsqrt(jnp.mean(o * o, axis=-1, keepdims=True) + EPS)
    o_ref[:, cols] = o * gn_ref[:, cols] * _silu(g_ref[:, cols].astype(F32))
    return st * jnp.exp2(a_end) + _dot_tn(vb, jnp.exp2(a_end - c).astype(BF16))


def _hgrn_body(q_ref, f_ref, i_ref, g_ref, hl_ref, s0_ref, gn_ref, o_ref, s_ref, st_ref, *, blk, sub, bb):
    @pl.when(pl.program_id(1) == 0)
    def _():
        for g, h in [(g, h) for g in range(bb) for h in range(HG_HEADS)]:
            st_ref[g, h] = jnp.transpose(s0_ref[g, h])

    if blk == HG_BLOCK:
        head, per_iter = functools.partial(_hgrn_head_block, blk=blk), HG_UNROLL_BLOCK // bb
    else:
        head, per_iter = functools.partial(_hgrn_head, blk=blk, sub=sub), HG_UNROLL_STEP // bb

    def heads(it, carry):
        for g, u in [(g, u) for g in range(bb) for u in range(per_iter)]:
            h = it * per_iter + u
            cols = pl.ds(pl.multiple_of(h * HEAD_DIM, HEAD_DIM), HEAD_DIM)
            st_ref[g, h] = head(cols, q_ref.at[g], f_ref.at[g], i_ref.at[g], g_ref.at[g], hl_ref, gn_ref,
                                o_ref.at[g], st_ref[g, h])
        return carry

    lax.fori_loop(0, HG_HEADS // per_iter, heads, 0)

    @pl.when(pl.program_id(1) == pl.num_programs(1) - 1)
    def _():
        for g, h in [(g, h) for g in range(bb) for h in range(HG_HEADS)]:
            s_ref[g, h] = jnp.transpose(st_ref[g, h])


def _hgrn(proj, f_pre, hg_lower, state, gn, bb):
    b, length, _ = proj.shape
    blk = min(length, HG_BLOCK)
    sub = math.gcd(length, HG_SUB)
    assert b % bb == 0, (b, bb)
    width = HG_HEADS * HEAD_DIM

    def col(i):
        return pl.BlockSpec((bb, blk, width), lambda bi, c: (bi, c, i))

    full = lambda shape: pl.BlockSpec(shape, lambda bi, c: (0,) * len(shape))
    state_spec = pl.BlockSpec((bb, HG_HEADS, HEAD_DIM, HEAD_DIM), lambda bi, c: (bi, 0, 0, 0))
    return pl.pallas_call(
        functools.partial(_hgrn_body, blk=blk, sub=sub, bb=bb),
        grid=(b // bb, length // blk),
        in_specs=[col(0), col(0), col(1), col(2), full(hg_lower.shape), state_spec, full((1, width))],
        out_specs=[pl.BlockSpec((bb, blk, width), lambda bi, c: (bi, c, 0)), state_spec],
        out_shape=[
            jax.ShapeDtypeStruct((b, length, width), F32),
            jax.ShapeDtypeStruct(state.shape, F32),
        ],
        scratch_shapes=[pltpu.VMEM((bb, HG_HEADS, HEAD_DIM, HEAD_DIM), F32)],
        compiler_params=_cparams(("parallel", "arbitrary")),
        name="hgrn2",
    )(proj, f_pre, proj, proj, hg_lower, state, gn)


def _prep_weights(p):
    w_in_ab = p["w_in_ab"][0]
    w_in_c = p["w_in_c"][0]
    width = HG_HEADS * HEAD_DIM
    w_gates = jnp.zeros((D_MODEL, LANES), BF16).at[:, :2 * ML_HEADS].set(w_in_ab[:, AB_MAIN:].astype(BF16))
    w_router = jnp.zeros((LANES, D_MODEL), F32).at[:N_EXPERTS, :].set(p["w_router"][0].T)
    w_router_hi = w_router.astype(BF16)
    w_router_lo = (w_router - w_router_hi.astype(F32)).astype(BF16)
    b_router = jnp.broadcast_to(p["b_router"][0][:, None], (N_EXPERTS, LANES))
    return {
        "w_ab": w_in_ab[:, :AB_MAIN].astype(BF16),
        "w_ab_gates": w_gates,
        "w_out_ab": p["w_out_ab"][0].astype(BF16),
        "w_c": jnp.concatenate([w_in_c[:, :width], w_in_c[:, 2 * width:]], axis=1).astype(BF16),
        "w_c_forget": w_in_c[:, width:2 * width].astype(BF16),
        "w_out_c": p["w_out_c"][0].astype(BF16),
        "w_ffn_gate": p["w_ffn_gate"][0].astype(BF16),
        "w_ffn_up": p["w_ffn_up"][0].astype(BF16),
        "w_ffn_down": p["w_ffn_down"][0].astype(BF16),
        "w_router_hi": w_router_hi,
        "w_router_lo": w_router_lo,
        "b_router": b_router,
    }


def _mixers(x, pos0, ret_s, mc_s, mn_s, mm_s, conv_s, hg_s, p, w):
    b, length, d = x.shape
    t = b * length
    tm = min(512, t)
    tm_mm = min(1024, t)
    prompt = length >= RET_CHUNK
    bb = 1 if prompt else 8
    assert t % tm_mm == 0 and t % tm == 0 and b % bb == 0 and b % 2 == 0, (b, length)
    row = lambda v: v.reshape(1, -1)
    x0 = x.reshape(t, d)

    proj, gates = _norm_matmul(x0, row(p["ln_mix"][0]), w["w_ab"], w["w_ab_gates"], tm_mm, 1024)
    proj = proj.reshape(b, length, AB_MAIN)
    o_ret, ret_new = _retention(proj, pos0, ret_s, row(p["ret_gn"][0]), 4 if prompt else bb)
    h_m, c_new, n_new, m_new, conv_new = _mlstm(
        proj, gates.reshape(b, length, LANES), p["b_gates_ab"][0], p["conv_w_ab"][0], row(p["conv_b_ab"][0]),
        conv_s, mc_s, mn_s, mm_s, row(p["ml_gn"][0]), bb)
    half = RET_HEADS * HEAD_DIM
    x2 = _ffn([o_ret.reshape(t, half), h_m.reshape(t, half)], [w["w_out_ab"][:half], w["w_out_ab"][half:]], x0,
              row(p["ln_ffn"][0]), w["w_ffn_gate"], w["w_ffn_up"], w["w_ffn_down"], tm, D_FF // 2)

    proj_c, f_pre = _norm_matmul(x2, row(p["ln_mix"][1]), w["w_c"], w["w_c_forget"], tm_mm, 1024)
    o_hg, hg_new = _hgrn(proj_c.reshape(b, length, 3 * d), f_pre.reshape(b, length, d), p["hg_lower"], hg_s,
                         row(p["hg_gn"][0]), bb)
    states = (ret_new[None], c_new[None], n_new[None], m_new[None], conv_new[None], hg_new[None])
    return x2, o_hg.reshape(t, d), states


def _experts(groups, shapes, p, w):
    row = lambda v: v.reshape(1, -1)
    ys = _moe([o for _, o in groups], w["w_out_c"], [x for x, _ in groups], row(p["ln_ffn"][1]), w,
              (p["w_moe_gate"][0], p["w_moe_up"][0], p["w_moe_down"][0]), row(p["ln_final"]))
    return [y.reshape(shape) for y, shape in zip(ys, shapes)]


def _trunk(x, pos0, ret_s, mc_s, mn_s, mm_s, conv_s, hg_s, p, w):
    x2, o_hg, states = _mixers(x, pos0, ret_s, mc_s, mn_s, mm_s, conv_s, hg_s, p, w)
    (y,) = _experts([(x2, o_hg)], [x.shape], p, w)
    return (y,) + states


def kernel(x_prompt, x_sample, state_ret, state_mlstm_c, state_mlstm_n, state_mlstm_m, state_conv, state_hgrn,
           ln_mix, ln_ffn, ln_final, w_in_ab, b_gates_ab, conv_w_ab, conv_b_ab, ret_gn, ml_gn, w_out_ab,
           w_in_c, hg_lower, hg_gn, w_out_c, w_ffn_gate, w_ffn_up, w_ffn_down,
           w_router, b_router, w_moe_gate, w_moe_up, w_moe_down):
    p = {"ln_mix": ln_mix, "ln_ffn": ln_ffn, "ln_final": ln_final,
         "w_in_ab": w_in_ab, "b_gates_ab": b_gates_ab, "conv_w_ab": conv_w_ab, "conv_b_ab": conv_b_ab,
         "ret_gn": ret_gn, "ml_gn": ml_gn, "w_out_ab": w_out_ab,
         "w_in_c": w_in_c, "hg_lower": hg_lower, "hg_gn": hg_gn, "w_out_c": w_out_c,
         "w_ffn_gate": w_ffn_gate, "w_ffn_up": w_ffn_up, "w_ffn_down": w_ffn_down,
         "w_router": w_router, "b_router": b_router,
         "w_moe_gate": w_moe_gate, "w_moe_up": w_moe_up, "w_moe_down": w_moe_down}
    w = _prep_weights(p)
    bp = x_prompt.shape[0]
    zeros = lambda *shape: jnp.zeros(shape, F32)
    x2_p, o_p, states_p = _mixers(
        x_prompt, 0,
        zeros(bp, RET_HEADS, HEAD_DIM, HEAD_DIM), zeros(bp, ML_HEADS, HEAD_DIM, HEAD_DIM),
        zeros(bp, ML_HEADS, HEAD_DIM), jnp.full((bp, ML_HEADS), NEG_INF, F32),
        zeros(bp, ML_CONV - 1, 2 * ML_HEADS * HEAD_DIM), zeros(bp, HG_HEADS, HEAD_DIM, HEAD_DIM), p, w)
    x2_s, o_s, states_s = _mixers(
        x_sample, 16384,
        state_ret[0], state_mlstm_c[0], state_mlstm_n[0], state_mlstm_m[0], state_conv[0], state_hgrn[0], p, w)
    y_p, y_s = _experts([(x2_p, o_p), (x2_s, o_s)], [x_prompt.shape, x_sample.shape], p, w)
    out = [y_p, y_s]
    for a, b in zip(states_p, states_s):
        out += [a, b]
    return tuple(out)
```

```python
import functools
import math

import numpy as np
import jax
import jax.numpy as jnp
from jax import lax
from jax.experimental import pallas as pl
from jax.experimental.pallas import tpu as pltpu

F32 = jnp.float32
BF16 = jnp.bfloat16

D_MODEL = 1024
HEAD_DIM = 128
RET_HEADS = 4
ML_HEADS = 4
ML_CONV = 4
HG_HEADS = 8
D_FF = 2816
N_EXPERTS = 8
ROPE_BASE = 10000.0
EPS = 1e-6
RET_CHUNK = 128
ML_CHUNK = 128
HG_BLOCK = 128
HG_SUB = 16

LANES = 128
SUBLANES = 8
AB_MAIN = 4096
VMEM_LIMIT = 56 * 1024 * 1024

NEG_INF = float("-inf")


def _cparams(sem):
    return pltpu.CompilerParams(dimension_semantics=sem, vmem_limit_bytes=VMEM_LIMIT)


def _sigmoid(x):
    return 1.0 / (1.0 + jnp.exp(-x))


def _silu(x):
    return x * _sigmoid(x)


def _rmsnorm_rows(x, g):
    ms = jnp.mean(x * x, axis=-1, keepdims=True)
    return x * lax.rsqrt(ms + EPS) * g


def _dot(a, b):
    return jnp.dot(a, b, preferred_element_type=F32)


def _dot_nt(a, b):
    return lax.dot_general(a, b, (((1,), (1,)), ((), ())), preferred_element_type=F32)


def _dot_tn(a, b):
    return lax.dot_general(a, b, (((0,), (0,)), ((), ())), preferred_element_type=F32)


def _pad_rows(x, rows):
    if x.shape[0] == rows:
        return x
    return jnp.concatenate([x, jnp.zeros((rows - x.shape[0], x.shape[1]), x.dtype)], axis=0)


def _norm_matmul_body(x_ref, g_ref, w_ref, ws_ref, o_ref, os_ref, h_ref):
    @pl.when(pl.program_id(1) == 0)
    def _():
        h = _rmsnorm_rows(x_ref[...], g_ref[...]).astype(BF16)
        h_ref[...] = h
        os_ref[...] = _dot(h, ws_ref[...])

    o_ref[...] = _dot(h_ref[...], w_ref[...]).astype(BF16)


def _norm_matmul(x, g, w, w_side, tm, tn):
    t, d = x.shape
    n, ns = w.shape[1], w_side.shape[1]
    return pl.pallas_call(
        _norm_matmul_body,
        grid=(t // tm, n // tn),
        in_specs=[
            pl.BlockSpec((tm, d), lambda i, j: (i, 0)),
            pl.BlockSpec((1, d), lambda i, j: (0, 0)),
            pl.BlockSpec((d, tn), lambda i, j: (0, j)),
            pl.BlockSpec((d, ns), lambda i, j: (0, 0)),
        ],
        out_specs=[pl.BlockSpec((tm, tn), lambda i, j: (i, j)), pl.BlockSpec((tm, ns), lambda i, j: (i, 0))],
        out_shape=[jax.ShapeDtypeStruct((t, n), BF16), jax.ShapeDtypeStruct((t, ns), F32)],
        scratch_shapes=[pltpu.VMEM((tm, d), BF16)],
        compiler_params=_cparams(("parallel", "arbitrary")),
        name="norm_matmul",
    )(x, g, w, w_side)


def _mixer_residual(x_ref, a_refs, w_refs):
    x = x_ref[...]
    for a_ref, w_ref in zip(a_refs, w_refs):
        x = x + _dot(a_ref[...].astype(BF16), w_ref[...])
    return x


def _ffn_body(*refs, n_in):
    a_refs, w_refs = refs[:n_in], refs[n_in:2 * n_in]
    x_ref, g_ref, wg_ref, wu_ref, wd_ref, o_ref, h_ref, x1_ref = refs[2 * n_in:]
    f = pl.program_id(1)

    @pl.when(f == 0)
    def _():
        x1 = _mixer_residual(x_ref, a_refs, w_refs)
        x1_ref[...] = x1
        h_ref[...] = _rmsnorm_rows(x1, g_ref[...]).astype(BF16)

    h = h_ref[...]
    a = _dot(h, wg_ref[...])
    u = _dot(h, wu_ref[...])
    act = (_silu(a) * u).astype(BF16)
    x1_ref[...] += _dot(act, wd_ref[...])

    @pl.when(f == pl.num_programs(1) - 1)
    def _():
        o_ref[...] = x1_ref[...]


def _ffn(acts, weights, x, g, wg, wu, wd, tm, tf):
    t, d = x.shape
    dff = wg.shape[1]
    n_in = len(acts)
    in_specs = [pl.BlockSpec((tm, a.shape[1]), lambda i, f: (i, 0)) for a in acts]
    in_specs += [pl.BlockSpec(w.shape, lambda i, f: (0, 0)) for w in weights]
    in_specs += [
        pl.BlockSpec((tm, d), lambda i, f: (i, 0)),
        pl.BlockSpec((1, d), lambda i, f: (0, 0)),
        pl.BlockSpec((d, tf), lambda i, f: (0, f)),
        pl.BlockSpec((d, tf), lambda i, f: (0, f)),
        pl.BlockSpec((tf, d), lambda i, f: (f, 0)),
    ]
    return pl.pallas_call(
        functools.partial(_ffn_body, n_in=n_in),
        grid=(t // tm, dff // tf),
        in_specs=in_specs,
        out_specs=pl.BlockSpec((tm, d), lambda i, f: (i, 0)),
        out_shape=jax.ShapeDtypeStruct((t, d), F32),
        scratch_shapes=[pltpu.VMEM((tm, d), BF16), pltpu.VMEM((tm, d), F32)],
        compiler_params=_cparams(("parallel", "arbitrary")),
        name="ffn",
    )(*acts, *weights, x, g, wg, wu, wd)


MOE_TB = 512
MOE_ALIGN = 16
MOE_BURST = 4
MOE_TILE = 256
MOE_CHUNKS_PER_TILE = MOE_TILE // MOE_ALIGN


def _moe_block_rows(tb):
    return 2 * tb + N_EXPERTS * MOE_ALIGN


def _group_block_specs(shape, blocks):
    specs, first = [], 0
    for nb in blocks:
        specs.append(pl.BlockSpec(shape, lambda i, *_, first=first, nb=nb: (jnp.clip(i - first, 0, nb - 1), 0)))
        first += nb
    return specs


def _route_body(*refs, blocks):
    n = len(blocks)
    a_refs, wo_ref, x_refs = refs[:n], refs[n], refs[n + 1:2 * n + 1]
    g_ref, whi_ref, wlo_ref, b_ref, tri_ref, x3_ref, h_ref, col_ref, rowf_ref, cnt_ref = refs[2 * n + 1:]
    i = pl.program_id(0)
    act, x_in = a_refs[0][...], x_refs[0][...]
    first = blocks[0]
    for a_ref, x_ref, nb in zip(a_refs[1:], x_refs[1:], blocks[1:]):
        act = jnp.where(i >= first, a_ref[...], act)
        x_in = jnp.where(i >= first, x_ref[...], x_in)
        first += nb
    x3 = x_in + _dot(act.astype(BF16), wo_ref[...])
    x3_ref[...] = x3
    h = _rmsnorm_rows(x3, g_ref[...])
    h_hi = h.astype(BF16)
    h_ref[...] = h_hi
    h_lo = (h - h_hi.astype(F32)).astype(BF16)

    both = _dot_nt(jnp.concatenate([whi_ref[...], wlo_ref[...]], axis=0), h_hi)
    logits = (both[:N_EXPERTS, :] + both[LANES:LANES + N_EXPERTS, :]
              + _dot_nt(whi_ref[...], h_lo)[:N_EXPERTS, :] + b_ref[:, 0:1])
    row = lax.broadcasted_iota(jnp.int32, logits.shape, 0)
    m1 = jnp.max(logits, axis=0, keepdims=True)
    i1 = jnp.min(jnp.where(logits == m1, row, N_EXPERTS), axis=0, keepdims=True)
    lg2 = jnp.where(row == i1, NEG_INF, logits)
    m2 = jnp.max(lg2, axis=0, keepdims=True)
    i2 = jnp.min(jnp.where(lg2 == m2, row, N_EXPERTS), axis=0, keepdims=True)
    e = jnp.exp(m2 - m1)
    w1 = 1.0 / (1.0 + e)
    w2 = e / (1.0 + e)

    sel = jnp.where(row == i1, 1.0, jnp.where(row == i2, 1.0, 0.0))
    before = _dot(sel.astype(BF16), tri_ref[...])
    n = jnp.broadcast_to(jnp.sum(sel, axis=1, keepdims=True), (N_EXPERTS, LANES))
    padded = jnp.floor((n + (MOE_ALIGN - 1.0)) * (1.0 / MOE_ALIGN)) * MOE_ALIGN
    base = _scan_rows(padded, jnp.add, 0.0) - padded
    pos = base[:, 0:1] + before
    loc1 = jnp.sum(jnp.where(row == i1, pos, 0.0), axis=0, keepdims=True)
    loc2 = jnp.sum(jnp.where(row == i2, pos, 0.0), axis=0, keepdims=True)
    rowf = jnp.where(row == 0, loc1, jnp.where(row == 1, loc2, jnp.where(row == 2, w1,
                                                                          jnp.where(row == 3, w2, 0.0))))
    rowf_ref[0] = rowf
    for c in range(rowf.shape[1] // LANES):
        col_ref[c * LANES:(c + 1) * LANES, :] = jnp.transpose(_pad_rows(rowf[:, c * LANES:(c + 1) * LANES], LANES))
    cnt_ref[0] = n


def _route(acts, w_out, xs, g, w_hi, w_lo, b, tb):
    d = xs[0].shape[1]
    blocks = [x.shape[0] // tb for x in xs]
    nb = sum(blocks)
    t = nb * tb
    tri = jnp.asarray(np.triu(np.ones((tb, tb), np.float32), 1), BF16)
    full = lambda shape: pl.BlockSpec(shape, lambda i: (0,) * len(shape))
    return pl.pallas_call(
        functools.partial(_route_body, blocks=blocks),
        grid=(nb,),
        in_specs=(_group_block_specs((tb, acts[0].shape[1]), blocks) + [full(w_out.shape)]
                  + _group_block_specs((tb, d), blocks)
                  + [full((1, d)), full((LANES, d)), full((LANES, d)), full((SUBLANES, LANES)), full((tb, tb))]),
        out_specs=[
            pl.BlockSpec((tb, d), lambda i: (i, 0)),
            pl.BlockSpec((tb, d), lambda i: (i, 0)),
            pl.BlockSpec((tb, LANES), lambda i: (i, 0)),
            pl.BlockSpec((1, SUBLANES, tb), lambda i: (i, 0, 0)),
            pl.BlockSpec((1, SUBLANES, LANES), lambda i: (i, 0, 0)),
        ],
        out_shape=[
            jax.ShapeDtypeStruct((t, d), F32),
            jax.ShapeDtypeStruct((t, d), BF16),
            jax.ShapeDtypeStruct((t, LANES), F32),
            jax.ShapeDtypeStruct((nb, SUBLANES, tb), F32),
            jax.ShapeDtypeStruct((nb, SUBLANES, LANES), F32),
        ],
        compiler_params=_cparams(("parallel",)),
        name="route",
    )(*acts, w_out, *xs, g, w_hi, w_lo, b, tri)


def _moe_plan(cnt):
    n = cnt[:, :N_EXPERTS, 0].astype(jnp.int32)
    chunks = (n + (MOE_ALIGN - 1)) // MOE_ALIGN
    src = jnp.cumsum(chunks, axis=1) - chunks
    seg_chunks = jnp.sum(chunks, axis=0)
    seg_tiles = (seg_chunks + (MOE_CHUNKS_PER_TILE - 1)) // MOE_CHUNKS_PER_TILE
    seg_end = jnp.cumsum(seg_tiles)
    seg_first = (seg_end - seg_tiles) * MOE_CHUNKS_PER_TILE
    dst = seg_first[None, :] + jnp.cumsum(chunks, axis=0) - chunks
    flat = lambda a: a.reshape(-1).astype(jnp.int32)
    return {"src": flat(src), "dst": flat(dst), "chunks": flat(chunks),
            "first_tile": flat(seg_end - seg_tiles), "tile_count": flat(seg_tiles),
            "tail_start": flat(seg_first + seg_chunks), "tail_count": flat(seg_tiles * MOE_CHUNKS_PER_TILE - seg_chunks)}


def _chunk_rows(chunk, n=1):
    return pl.ds(pl.multiple_of(chunk * MOE_ALIGN, MOE_ALIGN), n * MOE_ALIGN)


def _for_block_copies(nch_ref, blk, fn):
    for e in range(N_EXPERTS):
        count = nch_ref[blk * N_EXPERTS + e]
        bursts = count // MOE_BURST
        lax.fori_loop(0, bursts, lambda i, carry, e=e: (fn(e, i * MOE_BURST, MOE_BURST), carry)[1], 0)
        lax.fori_loop(bursts * MOE_BURST, count, lambda c, carry, e=e: (fn(e, c, 1), carry)[1], 0)


def _block_copy_counts(nch_ref, blk):
    bursts = singles = 0
    for e in range(N_EXPERTS):
        count = nch_ref[blk * N_EXPERTS + e]
        bursts = bursts + count // MOE_BURST
        singles = singles + lax.rem(count, MOE_BURST)
    return bursts, singles


def _wait_block_copies(nch_ref, blk, copy):
    bursts, singles = _block_copy_counts(nch_ref, blk)
    lax.fori_loop(0, bursts, lambda i, carry: (copy(MOE_BURST).wait(), carry)[1], 0)
    lax.fori_loop(0, singles, lambda i, carry: (copy(1).wait(), carry)[1], 0)


def _gather_body(src_ref, dst_ref, nch_ref, tail_ref, ntail_ref, last_ref, h_ref, rowf_ref, hs_ref, z_ref, sem,
                 *, n_tiles):
    b = pl.program_id(0)
    slot = lax.rem(b, 2)
    tb = h_ref.shape[0]
    loc = rowf_ref[0].astype(jnp.int32)
    r = lax.broadcasted_iota(jnp.int32, (z_ref.shape[1], tb), 0)
    onehot = jnp.where(r == loc[0:1, :], 1.0, jnp.where(r == loc[1:2, :], 1.0, 0.0)).astype(BF16)
    z_ref[slot] = _dot(onehot, h_ref[...]).astype(BF16)

    def copy(buf, src_chunk, dst_chunk, n=1):
        return pltpu.make_async_copy(z_ref.at[buf, _chunk_rows(src_chunk, n)], hs_ref.at[_chunk_rows(dst_chunk, n)],
                                     sem.at[buf])

    def drain(blk, buf):
        _wait_block_copies(nch_ref, blk, lambda n: copy(buf, 0, 0, n))

    @pl.when(b > 0)
    def _():
        drain(b - 1, 1 - slot)

    _for_block_copies(nch_ref, b, lambda e, c, n: copy(slot, src_ref[b * N_EXPERTS + e] + c,
                                                       dst_ref[b * N_EXPERTS + e] + c, n).start())

    @pl.when(b == pl.num_programs(0) - 1)
    def _():
        drain(b, slot)
        z_ref[0, 0:MOE_ALIGN, :] = jnp.zeros((MOE_ALIGN, z_ref.shape[2]), BF16)
        past = last_ref[0]
        total = n_tiles * MOE_CHUNKS_PER_TILE
        for e in range(N_EXPERTS):
            lax.fori_loop(0, ntail_ref[e], lambda c, carry, e=e: (copy(0, 0, tail_ref[e] + c).start(), carry)[1], 0)
        lax.fori_loop(past, total, lambda c, carry: (copy(0, 0, c).start(), carry)[1], 0)
        for e in range(N_EXPERTS):
            lax.fori_loop(0, ntail_ref[e], lambda c, carry: (copy(0, 0, 0).wait(), carry)[1], 0)
        lax.fori_loop(past, total, lambda c, carry: (copy(0, 0, 0).wait(), carry)[1], 0)


def _moe_gather(plan, h, rowf, n_tiles, tb):
    t, d = h.shape
    rows = _moe_block_rows(tb)
    past = (plan["first_tile"][-1:] + plan["tile_count"][-1:]) * MOE_CHUNKS_PER_TILE
    return pl.pallas_call(
        functools.partial(_gather_body, n_tiles=n_tiles),
        grid_spec=pltpu.PrefetchScalarGridSpec(
            num_scalar_prefetch=6,
            grid=(t // tb,),
            in_specs=[
                pl.BlockSpec((tb, d), lambda i, *_: (i, 0)),
                pl.BlockSpec((1, SUBLANES, tb), lambda i, *_: (i, 0, 0)),
            ],
            out_specs=pl.BlockSpec(memory_space=pl.ANY),
            scratch_shapes=[pltpu.VMEM((2, rows, d), BF16), pltpu.SemaphoreType.DMA((2,))],
        ),
        out_shape=jax.ShapeDtypeStruct((n_tiles * MOE_TILE, d), BF16),
        compiler_params=_cparams(("arbitrary",)),
        name="moe_gather",
    )(plan["src"], plan["dst"], plan["chunks"], plan["tail_start"], plan["tail_count"], past, h, rowf)


W_CHUNK = 256


def _expert_ffn_body(first_ref, count_ref, xs_ref, wg_hbm, wu_hbm, wd_hbm, ys_ref,
                     wg_b, wu_b, wd_b, stage_c, stage_r, x_buf, y_buf, w_sem, in_sem, out_sem, *, n_tiles, halves):
    e = pl.program_id(0)
    last = pl.num_programs(0) - 1
    first = first_ref[e]
    count = count_ref[e]
    cur = lax.rem(e, 2)
    dff = wg_b.shape[2]
    tf = dff // halves
    per = dff // W_CHUNK
    n_chunks = 3 * per

    def span(j):
        return pl.ds(pl.multiple_of(j * W_CHUNK, W_CHUNK), W_CHUNK)

    def fetch(kind, expert, j, slot):
        if kind == 2:
            return pltpu.make_async_copy(wd_hbm.at[expert, span(j), :], stage_r.at[slot], w_sem.at[slot])
        return pltpu.make_async_copy((wg_hbm, wu_hbm)[kind].at[expert, :, span(j)], stage_c.at[slot], w_sem.at[slot])

    def for_kind(c, fn):
        kind = c // per
        for k in range(3):
            @pl.when(kind == k)
            def _(k=k):
                fn(k, c - k * per)

    n_pairs = -(-n_chunks // 2)

    def start_pair(expert, p):
        for s in range(2):
            c = 2 * p + s

            @pl.when(c < n_chunks)
            def _(c=c, s=s):
                for_kind(c, lambda k, j: fetch(k, expert, j, s).start())

    def convert_pair(expert, dst, p):
        for s in range(2):
            c = 2 * p + s

            def land(k, j, s=s):
                fetch(k, expert, j, s).wait()
                if k == 2:
                    wd_b[dst, span(j), :] = stage_r[s].astype(BF16)
                else:
                    (wg_b, wu_b)[k][dst, :, span(j)] = stage_c[s].astype(BF16)

            @pl.when(c < n_chunks)
            def _(c=c, land=land):
                for_kind(c, land)

        start_pair(expert, p + 1)

    def maybe_convert(p):
        @pl.when(p < n_pairs)
        def _():
            convert_pair(e + 1, 1 - cur, p)
        return jnp.minimum(p + 1, n_pairs)

    @pl.when(e == 0)
    def _():
        start_pair(0, 0)
        lax.fori_loop(0, n_pairs, lambda p, carry: (convert_pair(0, 0, p), carry)[1], 0)

    @pl.when(e < last)
    def _():
        start_pair(e + 1, 0)

    def tile_rows(tile):
        return pl.ds(pl.multiple_of(tile * MOE_TILE, MOE_TILE), MOE_TILE)

    def load(j, buf):
        return pltpu.make_async_copy(xs_ref.at[tile_rows(first + j)], x_buf.at[buf], in_sem.at[buf])

    def store(tile, buf):
        return pltpu.make_async_copy(y_buf.at[buf], ys_ref.at[tile_rows(tile)], out_sem.at[buf])

    @pl.when(count > 0)
    def _():
        load(0, 0).start()

    def tile_step(j, c):
        buf = lax.rem(j, 2)

        @pl.when(j + 1 < count)
        def _():
            load(j + 1, 1 - buf).start()

        load(j, buf).wait()
        xs = x_buf[buf]
        y = None
        for f in range(halves):
            a = _dot(xs, wg_b[cur, :, f * tf:(f + 1) * tf])
            u = _dot(xs, wu_b[cur, :, f * tf:(f + 1) * tf])
            part = _dot((_silu(a) * u).astype(BF16), wd_b[cur, f * tf:(f + 1) * tf, :])
            y = part if y is None else y + part

        @pl.when(j >= 2)
        def _():
            store(first, buf).wait()

        y_buf[buf] = y.astype(BF16)
        store(first + j, buf).start()
        return maybe_convert(c)

    p_done = lax.fori_loop(0, count, tile_step, jnp.where(e < last, 0, n_pairs))
    lax.fori_loop(p_done, n_pairs, lambda p, carry: (convert_pair(e + 1, 1 - cur, p), carry)[1], 0)

    for back in (2, 1):
        @pl.when(count >= back)
        def _(back=back):
            store(first, lax.rem(count - back, 2)).wait()

    @pl.when(e == last)
    def _():
        end = first + count
        y_buf[0] = jnp.zeros(y_buf.shape[1:], BF16)
        lax.fori_loop(end, n_tiles, lambda t, c: (store(t, 0).start(), c)[1], 0)
        lax.fori_loop(end, n_tiles, lambda t, c: (store(t, 0).wait(), c)[1], 0)


def _expert_ffn(first_tile, tile_count, xs, wg, wu, wd):
    rows, d = xs.shape
    dff = wg.shape[2]
    any_space = pl.BlockSpec(memory_space=pl.ANY)
    return pl.pallas_call(
        functools.partial(_expert_ffn_body, n_tiles=rows // MOE_TILE, halves=2),
        grid_spec=pltpu.PrefetchScalarGridSpec(
            num_scalar_prefetch=2,
            grid=(N_EXPERTS,),
            in_specs=[any_space, any_space, any_space, any_space],
            out_specs=any_space,
            scratch_shapes=[
                pltpu.VMEM((2, d, dff), BF16), pltpu.VMEM((2, d, dff), BF16), pltpu.VMEM((2, dff, d), BF16),
                pltpu.VMEM((2, d, W_CHUNK), F32), pltpu.VMEM((2, W_CHUNK, d), F32),
                pltpu.VMEM((2, MOE_TILE, d), BF16), pltpu.VMEM((2, MOE_TILE, d), BF16),
                pltpu.SemaphoreType.DMA((2,)), pltpu.SemaphoreType.DMA((2,)), pltpu.SemaphoreType.DMA((2,))],
        ),
        out_shape=jax.ShapeDtypeStruct((rows, d), BF16),
        compiler_params=_cparams(("arbitrary",)),
        name="expert_ffn",
    )(first_tile, tile_count, xs, wg, wu, wd)


def _combine_body(src_ref, dst_ref, nch_ref, x_ref, col_ref, gf_ref, ys_ref, *refs, blocks):
    o_refs, (y_ref, sem) = refs[:len(blocks)], refs[len(blocks):]
    b = pl.program_id(0)
    nb = pl.num_programs(0)
    slot = lax.rem(b, 2)

    def copy(buf, src_chunk, dst_chunk, n=1):
        return pltpu.make_async_copy(ys_ref.at[_chunk_rows(dst_chunk, n)], y_ref.at[buf, _chunk_rows(src_chunk, n)],
                                     sem.at[buf])

    def fetch(blk, buf):
        _for_block_copies(nch_ref, blk, lambda e, c, n: copy(buf, src_ref[blk * N_EXPERTS + e] + c,
                                                             dst_ref[blk * N_EXPERTS + e] + c, n).start())

    @pl.when(b == 0)
    def _():
        y_ref[...] = jnp.zeros_like(y_ref)
        fetch(0, 0)

    @pl.when(b + 1 < nb)
    def _():
        fetch(b + 1, 1 - slot)

    _wait_block_copies(nch_ref, b, lambda n: copy(slot, 0, 0, n))

    col = col_ref[...]
    loc1 = col[:, 0:1].astype(jnp.int32)
    loc2 = col[:, 1:2].astype(jnp.int32)
    r = lax.broadcasted_iota(jnp.int32, (col.shape[0], y_ref.shape[1]), 1)
    weights = jnp.where(r == loc1, col[:, 2:3], jnp.where(r == loc2, col[:, 3:4], 0.0)).astype(BF16)
    out = _rmsnorm_rows(x_ref[...] + _dot(weights, y_ref[slot]), gf_ref[...])
    first = 0
    for o_ref, n_blocks in zip(o_refs, blocks):
        @pl.when((b >= first) & (b < first + n_blocks))
        def _(o_ref=o_ref):
            o_ref[...] = out
        first += n_blocks


def _moe_combine(src, dst, nch, x, col, g_final, ys, tb, blocks):
    t, d = x.shape
    rows = _moe_block_rows(tb)
    return pl.pallas_call(
        functools.partial(_combine_body, blocks=blocks),
        grid_spec=pltpu.PrefetchScalarGridSpec(
            num_scalar_prefetch=3,
            grid=(t // tb,),
            in_specs=[
                pl.BlockSpec((tb, d), lambda i, *_: (i, 0)),
                pl.BlockSpec((tb, LANES), lambda i, *_: (i, 0)),
                pl.BlockSpec((1, d), lambda i, *_: (0, 0)),
                pl.BlockSpec(memory_space=pl.ANY),
            ],
            out_specs=_group_block_specs((tb, d), blocks),
            scratch_shapes=[pltpu.VMEM((2, rows, d), BF16), pltpu.SemaphoreType.DMA((2,))],
        ),
        out_shape=[jax.ShapeDtypeStruct((nb * tb, d), F32) for nb in blocks],
        compiler_params=_cparams(("arbitrary",)),
        name="moe_combine",
    )(src, dst, nch, x, col, g_final, ys)


def _moe(acts, w_out, xs, g, w, moe_weights, g_final):
    tb = MOE_TB
    assert all(x.shape[0] % tb == 0 for x in xs), [x.shape for x in xs]
    blocks = [x.shape[0] // tb for x in xs]
    nb = sum(blocks)
    t = nb * tb
    x, h, col, rowf, cnt = _route(acts, w_out, xs, g, w["w_router_hi"], w["w_router_lo"], w["b_router"], tb)
    n_tiles = -(-(2 * t + nb * N_EXPERTS * (MOE_ALIGN - 1)) // MOE_TILE) + N_EXPERTS
    plan = _moe_plan(cnt)
    xg = _moe_gather(plan, h, rowf, n_tiles, tb)
    yg = _expert_ffn(plan["first_tile"], plan["tile_count"], xg, *moe_weights)
    return _moe_combine(plan["src"], plan["dst"], plan["chunks"], x, col, g_final, yg, tb, blocks)


def _retention_tables(c_real, ct):
    h = np.arange(RET_HEADS, dtype=np.float64)
    log_gamma = np.log1p(-np.exp2(-5.0 - h))
    idx = np.arange(ct, dtype=np.float64)
    live = idx < c_real
    diff = idx[:, None] - idx[None, :]
    causal = (diff >= 0) & live[:, None] & live[None, :]
    decay = np.where(causal[None], np.exp(np.where(causal, diff, 0.0)[None] * log_gamma[:, None, None]), 0.0)
    q_dec = np.where(live[None], np.exp((idx + 1.0)[None] * log_gamma[:, None]), 0.0)
    k_dec = np.where(live[None], np.exp((c_real - 1.0 - idx)[None] * log_gamma[:, None]), 0.0)
    q_dec = np.broadcast_to(q_dec[..., None], (RET_HEADS, ct, LANES))
    k_dec = np.broadcast_to(k_dec[..., None], (RET_HEADS, ct, LANES))
    return (jnp.asarray(decay, F32), jnp.asarray(q_dec, F32), jnp.asarray(k_dec, F32))


def _rope_tables(pos0, length):
    half = HEAD_DIM // 2
    inv = ROPE_BASE ** (-np.arange(half, dtype=np.float64) / half)
    ang = (pos0 + np.arange(length, dtype=np.float64))[:, None] * inv[None, :]
    cos = np.concatenate([np.cos(ang), np.cos(ang)], axis=-1)
    sin = np.concatenate([-np.sin(ang), np.sin(ang)], axis=-1)
    return jnp.asarray(cos, F32), jnp.asarray(sin, F32)


def _retention_body(q_ref, k_ref, v_ref, g_ref, cos_ref, sin_ref, dec_ref, qd_ref, kd_ref, s0_ref, gn_ref,
                    o_ref, s_ref, *, c_real, ct, bb):
    @pl.when(pl.program_id(1) == 0)
    def _():
        s_ref[...] = s0_ref[...]

    cos = cos_ref[...]
    sin = sin_ref[...]
    for g, hd in [(g, hd) for g in range(bb) for hd in range(RET_HEADS)]:
        lo, hi = hd * HEAD_DIM, (hd + 1) * HEAD_DIM
        q = q_ref[g, :, lo:hi].astype(F32)
        k = k_ref[g, :, lo:hi].astype(F32)
        q = (q * cos + pltpu.roll(q, HEAD_DIM // 2, 1) * sin) * (HEAD_DIM ** -0.5)
        k = k * cos + pltpu.roll(k, HEAD_DIM // 2, 1) * sin
        v = v_ref[g, :, lo:hi].astype(F32)
        qb = q.astype(BF16)
        q_dec = qd_ref[hd, :c_real, :]
        state = s_ref[g, hd]

        s = _dot_nt(qb, _pad_rows(k, ct).astype(BF16)) * dec_ref[hd, :c_real, :]
        o = _dot(s.astype(BF16), _pad_rows(v, ct).astype(BF16)) + _dot(qb, state.astype(BF16)) * q_dec
        chunk_dec = q_dec[c_real - 1:c_real, :]
        s_ref[g, hd] = state * chunk_dec + _dot_tn((k * kd_ref[hd, :c_real, :]).astype(BF16), v.astype(BF16))

        o = o - jnp.mean(o, axis=-1, keepdims=True)
        o = o * lax.rsqrt(jnp.mean(o * o, axis=-1, keepdims=True) + EPS)
        o_ref[g, :, lo:hi] = _silu(g_ref[g, :, lo:hi].astype(F32)) * (o * gn_ref[:, lo:hi])


def _retention(proj, pos0, state, gn, bb):
    b, length, _ = proj.shape
    assert b % bb == 0, (b, bb)
    c_real = math.gcd(length, RET_CHUNK)
    ct = RET_CHUNK
    width = RET_HEADS * HEAD_DIM
    cos, sin = _rope_tables(pos0, length)
    decay, q_dec, k_dec = _retention_tables(c_real, ct)

    def col(i):
        return pl.BlockSpec((bb, c_real, width), lambda bi, c: (bi, c, i))

    full = lambda shape: pl.BlockSpec(shape, lambda bi, c: (0,) * len(shape))
    state_spec = pl.BlockSpec((bb, RET_HEADS, HEAD_DIM, HEAD_DIM), lambda bi, c: (bi, 0, 0, 0))
    return pl.pallas_call(
        functools.partial(_retention_body, c_real=c_real, ct=ct, bb=bb),
        grid=(b // bb, length // c_real),
        in_specs=[
            col(0), col(1), col(2), col(3),
            pl.BlockSpec((c_real, HEAD_DIM), lambda bi, c: (c, 0)),
            pl.BlockSpec((c_real, HEAD_DIM), lambda bi, c: (c, 0)),
            full((RET_HEADS, ct, ct)), full((RET_HEADS, ct, LANES)), full((RET_HEADS, ct, LANES)),
            state_spec,
            full((1, width)),
        ],
        out_specs=[
            pl.BlockSpec((bb, c_real, width), lambda bi, c: (bi, c, 0)),
            state_spec,
        ],
        out_shape=[
            jax.ShapeDtypeStruct((b, length, width), F32),
            jax.ShapeDtypeStruct(state.shape, F32),
        ],
        compiler_params=_cparams(("parallel", "arbitrary")),
        name="retention",
    )(proj, proj, proj, proj, cos, sin, decay, q_dec, k_dec, state, gn)


def _scan_rows(x, op, fill):
    rows = x.shape[0]
    row = lax.broadcasted_iota(jnp.int32, x.shape, 0)
    sh = 1
    while sh < rows:
        x = op(x, jnp.where(row >= sh, pltpu.roll(x, sh, 0), fill))
        sh *= 2
    return x


def _mlstm_body(q_ref, k_ref, v_ref, og_ref, gate_ref, bg_ref, cw_ref, cb_ref, cs_ref, c0_ref, n0_ref, m0_ref,
                gn_ref, o_ref, c_ref, n_ref, m_ref, conv_ref, xc_ref, *, c_real, ct, bb):
    tail = ML_CONV - 1

    @pl.when(pl.program_id(1) == 0)
    def _():
        c_ref[...] = c0_ref[...]
        n_ref[...] = n0_ref[...]
        m_ref[...] = m0_ref[...]
        xc_ref[:, SUBLANES - tail:SUBLANES, :] = cs_ref[...]

    for g in range(bb):
        _mlstm_sequence(q_ref.at[g], k_ref.at[g], v_ref.at[g], og_ref.at[g], gate_ref.at[g], bg_ref, cw_ref, cb_ref,
                        gn_ref, o_ref.at[g], c_ref.at[g], n_ref.at[g], m_ref.at[g], conv_ref.at[g], xc_ref.at[g],
                        c_real=c_real, ct=ct)


def _mlstm_sequence(q_ref, k_ref, v_ref, og_ref, gate_ref, bg_ref, cw_ref, cb_ref, gn_ref, o_ref, c_ref, n_ref, m_ref,
                    conv_ref, xc_ref, *, c_real, ct):
    width = ML_HEADS * HEAD_DIM
    tail = ML_CONV - 1

    xc_ref[SUBLANES:SUBLANES + c_real, :width] = q_ref[...].astype(F32)
    xc_ref[SUBLANES:SUBLANES + c_real, width:] = k_ref[...].astype(F32)
    y = cb_ref[...] + cw_ref[tail:tail + 1, :] * xc_ref[SUBLANES:SUBLANES + c_real, :]
    for j in range(tail):
        y = y + cw_ref[j:j + 1, :] * xc_ref[SUBLANES - tail + j:SUBLANES - tail + j + c_real, :]
    new_tail = xc_ref[c_real:c_real + SUBLANES, :]
    conv_ref[...] = new_tail[SUBLANES - tail:, :]
    xc_ref[0:SUBLANES, :] = new_tail
    qk = _silu(y)
    q_all = qk[:, :width]
    k_all = qk[:, width:] * (HEAD_DIM ** -0.5)
    v_all = v_ref[...].astype(F32)

    gates = gate_ref[...] + bg_ref[...]
    f_pre = pltpu.roll(gates, LANES - ML_HEADS, 1)
    log_f = jnp.minimum(f_pre, 0.0) - jnp.log(1.0 + jnp.exp(-jnp.abs(f_pre)))
    b_cum = _scan_rows(log_f, jnp.add, 0.0)
    src = gates - b_cum
    cmx = _scan_rows(src, jnp.maximum, NEG_INF)
    m_prev = m_ref[...]
    mx = jnp.maximum(m_prev, cmx)
    m_t = b_cum + mx
    w_prev = jnp.exp(m_prev - mx)
    e_neg_m = jnp.exp(-m_t)
    last = c_real - 1
    b_last = b_cum[last:last + 1, :]
    m_new = m_t[last:last + 1, :]
    w_end = jnp.exp(b_last + src - m_new)
    dec = jnp.exp(b_last + m_prev - m_new)
    m_ref[...] = m_new
    src_t = jnp.transpose(_pad_rows(src, ct))

    t_idx = lax.broadcasted_iota(jnp.int32, (c_real, ct), 0)
    s_idx = lax.broadcasted_iota(jnp.int32, (c_real, ct), 1)
    live = (s_idx <= t_idx) & (s_idx < c_real)

    for hd in range(ML_HEADS):
        lo, hi = hd * HEAD_DIM, (hd + 1) * HEAD_DIM
        q = q_all[:, lo:hi]
        k = k_all[:, lo:hi]
        v = v_all[:, lo:hi]
        qb = q.astype(BF16)
        c_state = c_ref[hd]
        n_state = n_ref[hd:hd + 1, :]
        wp = w_prev[:, hd:hd + 1]
        w = jnp.where(live, jnp.exp(src_t[hd:hd + 1, :] - mx[:, hd:hd + 1]), 0.0)
        s = _dot_nt(qb, _pad_rows(k, ct).astype(BF16)) * w
        num = _dot(s.astype(BF16), _pad_rows(v, ct).astype(BF16)) + wp * _dot(qb, c_state.astype(BF16))
        den = jnp.sum(s, axis=-1, keepdims=True) + wp * jnp.sum(q * n_state, axis=-1, keepdims=True)
        hh = num / jnp.maximum(jnp.abs(den), e_neg_m[:, hd:hd + 1])
        kw = k * w_end[:, hd:hd + 1]
        dec_h = dec[:, hd:hd + 1]
        c_ref[hd] = dec_h * c_state + _dot_tn(kw.astype(BF16), v.astype(BF16))
        n_ref[hd:hd + 1, :] = dec_h * n_state + jnp.sum(kw, axis=0, keepdims=True)

        hh = _sigmoid(og_ref[:, lo:hi].astype(F32)) * hh
        hh = hh - jnp.mean(hh, axis=-1, keepdims=True)
        hh = hh * lax.rsqrt(jnp.mean(hh * hh, axis=-1, keepdims=True) + EPS)
        o_ref[:, lo:hi] = hh * gn_ref[:, lo:hi]


def _mlstm(proj, gates, b_gates, conv_w, conv_b, conv_s, c0, n0, m0, gn, bb):
    b, length, _ = proj.shape
    assert b % bb == 0, (b, bb)
    c_real = math.gcd(length, ML_CHUNK)
    ct = ML_CHUNK
    width = ML_HEADS * HEAD_DIM
    base = (2 * RET_HEADS * HEAD_DIM + 2 * RET_HEADS * HEAD_DIM) // width

    def col(i):
        return pl.BlockSpec((bb, c_real, width), lambda bi, c: (bi, c, base + i))

    full = lambda shape: pl.BlockSpec(shape, lambda bi, c: (0,) * len(shape))
    per_b = lambda shape: pl.BlockSpec((bb,) + shape, lambda bi, c: (bi,) + (0,) * len(shape))
    bg = jnp.zeros((1, LANES), F32).at[0, :2 * ML_HEADS].set(b_gates)
    m0p = jnp.zeros((b, 1, LANES), F32).at[:, 0, :ML_HEADS].set(m0)
    outs = pl.pallas_call(
        functools.partial(_mlstm_body, c_real=c_real, ct=ct, bb=bb),
        grid=(b // bb, length // c_real),
        in_specs=[
            col(0), col(1), col(2), col(3),
            pl.BlockSpec((bb, c_real, LANES), lambda bi, c: (bi, c, 0)),
            full((1, LANES)),
            full((ML_CONV, 2 * width)),
            full((1, 2 * width)),
            per_b((ML_CONV - 1, 2 * width)),
            per_b((ML_HEADS, HEAD_DIM, HEAD_DIM)),
            per_b((ML_HEADS, HEAD_DIM)),
            per_b((1, LANES)),
            full((1, width)),
        ],
        out_specs=[
            pl.BlockSpec((bb, c_real, width), lambda bi, c: (bi, c, 0)),
            per_b((ML_HEADS, HEAD_DIM, HEAD_DIM)),
            per_b((ML_HEADS, HEAD_DIM)),
            per_b((1, LANES)),
            per_b((ML_CONV - 1, 2 * width)),
        ],
        out_shape=[
            jax.ShapeDtypeStruct((b, length, width), F32),
            jax.ShapeDtypeStruct(c0.shape, F32),
            jax.ShapeDtypeStruct(n0.shape, F32),
            jax.ShapeDtypeStruct((b, 1, LANES), F32),
            jax.ShapeDtypeStruct(conv_s.shape, F32),
        ],
        scratch_shapes=[pltpu.VMEM((bb, ct + SUBLANES, 2 * width), F32)],
        compiler_params=_cparams(("parallel", "arbitrary")),
        name="mlstm",
    )(proj, proj, proj, proj, gates, bg, conv_w, conv_b, conv_s, c0, n0, m0p, gn)
    h_m, c_new, n_new, m_new, conv_new = outs
    return h_m, c_new, n_new, m_new[:, 0, :ML_HEADS], conv_new


HG_UNROLL_BLOCK = 8
HG_UNROLL_STEP = 8


def _hgrn_head(cols, q_ref, f_ref, i_ref, g_ref, hl_ref, gn_ref, o_ref, st, *, blk, sub):
    hl = hl_ref[:, cols]
    hmax = jnp.max(hl, axis=0, keepdims=True)
    ex = jnp.exp(hl - hmax)
    p = ex / jnp.sum(ex, axis=0, keepdims=True)
    lower = (p[0:1] + p[1:2]) - p[0:1]
    gn = gn_ref[:, cols]

    f = lower + (1.0 - lower) * _sigmoid(f_ref[:, cols])
    log_k = jnp.log2(jnp.maximum(1.0 - f, 0.0))
    q = _silu(q_ref[:, cols].astype(F32))
    v = i_ref[:, cols].astype(F32)
    gate = _silu(g_ref[:, cols].astype(F32))

    row = lax.broadcasted_iota(jnp.int32, (blk, LANES), 0)
    rsub = row & (sub - 1)
    a = jnp.log2(f)
    sh = 1
    while sh < sub:
        a = a + jnp.where(rsub >= sh, pltpu.roll(a, sh, 0), 0.0)
        sh *= 2
    c = a - log_k
    q_in = q * jnp.exp2(a)

    t_idx = lax.broadcasted_iota(jnp.int32, (SUBLANES, LANES), 0)
    for j in range(blk // sub):
        r0 = j * sub
        a_end = a[r0 + sub - 1:r0 + sub, :]
        inter = _dot_nt(q_in[r0:r0 + sub].astype(BF16), st.astype(BF16))
        k_hat = jnp.exp2(a_end - c[r0:r0 + sub])
        v_j = v[r0:r0 + sub]
        st = st * jnp.exp2(a_end) + _dot_tn(v_j.astype(BF16), k_hat.astype(BF16))
        for part in range(sub // SUBLANES):
            t0 = r0 + part * SUBLANES
            a_t = a[t0:t0 + SUBLANES]
            q_t = q[t0:t0 + SUBLANES]
            o_t = inter[part * SUBLANES:(part + 1) * SUBLANES]
            for s in range((part + 1) * SUBLANES):
                arg = a_t - c[r0 + s:r0 + s + 1, :]
                if s >= part * SUBLANES:
                    arg = jnp.where(t_idx >= s - part * SUBLANES, arg, NEG_INF)
                z = q_t * jnp.exp2(arg)
                o_t = o_t + jnp.sum(z, axis=-1, keepdims=True) * v[r0 + s:r0 + s + 1, :]
            o_t = o_t * lax.rsqrt(jnp.mean(o_t * o_t, axis=-1, keepdims=True) + EPS)
            o_ref[t0:t0 + SUBLANES, cols] = o_t * gn * gate[t0:t0 + SUBLANES]
    return st


def _block_ref_rows(a, level, row8):
    rows = a.shape[0]
    half = level // 2
    if half >= SUBLANES:
        return jnp.concatenate([jnp.broadcast_to(a[b0 + half - 1:b0 + half, :], (level, LANES))
                                for b0 in range(0, rows, level)], axis=0)
    if level == 2:
        return jnp.where((row8 & 1) == 1, pltpu.roll(a, 1, 0), a)
    pieces = []
    for v0 in range(0, rows, SUBLANES):
        picks = [jnp.broadcast_to(a[v0 + b0 + half - 1:v0 + b0 + half, :], (SUBLANES, LANES))
                 for b0 in range(0, SUBLANES, level)]
        piece = picks[-1]
        for i in range(len(picks) - 2, -1, -1):
            piece = jnp.where(row8[:SUBLANES] < (i + 1) * level, picks[i], piece)
        pieces.append(piece)
    return jnp.concatenate(pieces, axis=0)


def _hgrn_head_block(cols, q_ref, f_ref, i_ref, g_ref, hl_ref, gn_ref, o_ref, st, *, blk):
    hl = hl_ref[:, cols]
    hmax = jnp.max(hl, axis=0, keepdims=True)
    ex = jnp.exp(hl - hmax)
    p = ex / jnp.sum(ex, axis=0, keepdims=True)
    lower = (p[0:1] + p[1:2]) - p[0:1]

    f = lower + (1.0 - lower) * _sigmoid(f_ref[:, cols])
    kk = 1.0 - f
    q = _silu(q_ref[:, cols].astype(F32))
    vb = i_ref[:, cols]
    a = _scan_rows(jnp.log2(f), jnp.add, 0.0)
    c = a - jnp.log2(jnp.maximum(kk, 0.0))
    row8 = lax.broadcasted_iota(jnp.int32, (blk, LANES), 0) & (SUBLANES - 1)
    group = (lax.broadcasted_iota(jnp.int32, (blk, blk), 0)
             ^ lax.broadcasted_iota(jnp.int32, (blk, blk), 1))

    scores = None
    level = blk
    while level >= 2:
        half = level // 2
        if half >= SUBLANES:
            zero = jnp.zeros((half, LANES), F32)
            qs, ks = [], []
            for b0 in range(0, blk, level):
                ref = a[b0 + half - 1:b0 + half, :]
                qs += [zero, q[b0 + half:b0 + level] * jnp.exp2(a[b0 + half:b0 + level] - ref)]
                ks += [jnp.exp2(ref - c[b0:b0 + half]), zero]
            q_l = jnp.concatenate(qs, axis=0)
            k_l = jnp.concatenate(ks, axis=0)
        else:
            ref = _block_ref_rows(a, level, row8)
            upper = (row8 & (level - 1)) >= half
            q_l = jnp.where(upper, q * jnp.exp2(a - ref), 0.0)
            k_l = jnp.where(upper, 0.0, jnp.exp2(ref - c))
        r = _dot_nt(q_l.astype(BF16), k_l.astype(BF16))
        scores = r if scores is None else jnp.where(group < level, r, scores)
        level = half
    scores = jnp.where(group < 1, _dot_nt(q.astype(BF16), kk.astype(BF16)), scores)

    a_end = a[blk - 1:blk, :]
    o = _dot_nt((q * jnp.exp2(a)).astype(BF16), st.astype(BF16)) + _dot(scores.astype(BF16), vb)
    o = o * lax.rsqrt(jnp.mean(o * o, axis=-1, keepdims=True) + EPS)
    o_ref[:, cols] = o * gn_ref[:, cols] * _silu(g_ref[:, cols].astype(F32))
    return st * jnp.exp2(a_end) + _dot_tn(vb, jnp.exp2(a_end - c).astype(BF16))


def _hgrn_body(q_ref, f_ref, i_ref, g_ref, hl_ref, s0_ref, gn_ref, o_ref, s_ref, st_ref, *, blk, sub, bb):
    @pl.when(pl.program_id(1) == 0)
    def _():
        for g, h in [(g, h) for g in range(bb) for h in range(HG_HEADS)]:
            st_ref[g, h] = jnp.transpose(s0_ref[g, h])

    if blk == HG_BLOCK:
        head, per_iter = functools.partial(_hgrn_head_block, blk=blk), HG_UNROLL_BLOCK // bb
    else:
        head, per_iter = functools.partial(_hgrn_head, blk=blk, sub=sub), HG_UNROLL_STEP // bb

    def heads(it, carry):
        for g, u in [(g, u) for g in range(bb) for u in range(per_iter)]:
            h = it * per_iter + u
            cols = pl.ds(pl.multiple_of(h * HEAD_DIM, HEAD_DIM), HEAD_DIM)
            st_ref[g, h] = head(cols, q_ref.at[g], f_ref.at[g], i_ref.at[g], g_ref.at[g], hl_ref, gn_ref,
                                o_ref.at[g], st_ref[g, h])
        return carry

    lax.fori_loop(0, HG_HEADS // per_iter, heads, 0)

    @pl.when(pl.program_id(1) == pl.num_programs(1) - 1)
    def _():
        for g, h in [(g, h) for g in range(bb) for h in range(HG_HEADS)]:
            s_ref[g, h] = jnp.transpose(st_ref[g, h])


def _hgrn(proj, f_pre, hg_lower, state, gn, bb):
    b, length, _ = proj.shape
    blk = min(length, HG_BLOCK)
    sub = math.gcd(length, HG_SUB)
    assert b % bb == 0, (b, bb)
    width = HG_HEADS * HEAD_DIM

    def col(i):
        return pl.BlockSpec((bb, blk, width), lambda bi, c: (bi, c, i))

    full = lambda shape: pl.BlockSpec(shape, lambda bi, c: (0,) * len(shape))
    state_spec = pl.BlockSpec((bb, HG_HEADS, HEAD_DIM, HEAD_DIM), lambda bi, c: (bi, 0, 0, 0))
    return pl.pallas_call(
        functools.partial(_hgrn_body, blk=blk, sub=sub, bb=bb),
        grid=(b // bb, length // blk),
        in_specs=[col(0), col(0), col(1), col(2), full(hg_lower.shape), state_spec, full((1, width))],
        out_specs=[pl.BlockSpec((bb, blk, width), lambda bi, c: (bi, c, 0)), state_spec],
        out_shape=[
            jax.ShapeDtypeStruct((b, length, width), F32),
            jax.ShapeDtypeStruct(state.shape, F32),
        ],
        scratch_shapes=[pltpu.VMEM((bb, HG_HEADS, HEAD_DIM, HEAD_DIM), F32)],
        compiler_params=_cparams(("parallel", "arbitrary")),
        name="hgrn2",
    )(proj, f_pre, proj, proj, hg_lower, state, gn)


def _prep_weights(p):
    w_in_ab = p["w_in_ab"][0]
    w_in_c = p["w_in_c"][0]
    width = HG_HEADS * HEAD_DIM
    w_gates = jnp.zeros((D_MODEL, LANES), BF16).at[:, :2 * ML_HEADS].set(w_in_ab[:, AB_MAIN:].astype(BF16))
    w_router = jnp.zeros((LANES, D_MODEL), F32).at[:N_EXPERTS, :].set(p["w_router"][0].T)
    w_router_hi = w_router.astype(BF16)
    w_router_lo = (w_router - w_router_hi.astype(F32)).astype(BF16)
    b_router = jnp.broadcast_to(p["b_router"][0][:, None], (N_EXPERTS, LANES))
    return {
        "w_ab": w_in_ab[:, :AB_MAIN].astype(BF16),
        "w_ab_gates": w_gates,
        "w_out_ab": p["w_out_ab"][0].astype(BF16),
        "w_c": jnp.concatenate([w_in_c[:, :width], w_in_c[:, 2 * width:]], axis=1).astype(BF16),
        "w_c_forget": w_in_c[:, width:2 * width].astype(BF16),
        "w_out_c": p["w_out_c"][0].astype(BF16),
        "w_ffn_gate": p["w_ffn_gate"][0].astype(BF16),
        "w_ffn_up": p["w_ffn_up"][0].astype(BF16),
        "w_ffn_down": p["w_ffn_down"][0].astype(BF16),
        "w_router_hi": w_router_hi,
        "w_router_lo": w_router_lo,
        "b_router": b_router,
    }


def _mixers(x, pos0, ret_s, mc_s, mn_s, mm_s, conv_s, hg_s, p, w):
    b, length, d = x.shape
    t = b * length
    tm = min(512, t)
    tm_mm = min(1024, t)
    prompt = length >= RET_CHUNK
    bb = 1 if prompt else 8
    assert t % tm_mm == 0 and t % tm == 0 and b % bb == 0 and b % 2 == 0, (b, length)
    row = lambda v: v.reshape(1, -1)
    x0 = x.reshape(t, d)

    proj, gates = _norm_matmul(x0, row(p["ln_mix"][0]), w["w_ab"], w["w_ab_gates"], tm_mm, 1024)
    proj = proj.reshape(b, length, AB_MAIN)
    o_ret, ret_new = _retention(proj, pos0, ret_s, row(p["ret_gn"][0]), 4 if prompt else bb)
    h_m, c_new, n_new, m_new, conv_new = _mlstm(
        proj, gates.reshape(b, length, LANES), p["b_gates_ab"][0], p["conv_w_ab"][0], row(p["conv_b_ab"][0]),
        conv_s, mc_s, mn_s, mm_s, row(p["ml_gn"][0]), bb)
    half = RET_HEADS * HEAD_DIM
    x2 = _ffn([o_ret.reshape(t, half), h_m.reshape(t, half)], [w["w_out_ab"][:half], w["w_out_ab"][half:]], x0,
              row(p["ln_ffn"][0]), w["w_ffn_gate"], w["w_ffn_up"], w["w_ffn_down"], tm, D_FF // 2)

    proj_c, f_pre = _norm_matmul(x2, row(p["ln_mix"][1]), w["w_c"], w["w_c_forget"], tm_mm, 1024)
    o_hg, hg_new = _hgrn(proj_c.reshape(b, length, 3 * d), f_pre.reshape(b, length, d), p["hg_lower"], hg_s,
                         row(p["hg_gn"][0]), bb)
    states = (ret_new[None], c_new[None], n_new[None], m_new[None], conv_new[None], hg_new[None])
    return x2, o_hg.reshape(t, d), states


def _experts(groups, shapes, p, w):
    row = lambda v: v.reshape(1, -1)
    ys = _moe([o for _, o in groups], w["w_out_c"], [x for x, _ in groups], row(p["ln_ffn"][1]), w,
              (p["w_moe_gate"][0], p["w_moe_up"][0], p["w_moe_down"][0]), row(p["ln_final"]))
    return [y.reshape(shape) for y, shape in zip(ys, shapes)]


def kernel(x_prompt, x_sample, state_ret, state_mlstm_c, state_mlstm_n, state_mlstm_m, state_conv, state_hgrn,
           ln_mix, ln_ffn, ln_final, w_in_ab, b_gates_ab, conv_w_ab, conv_b_ab, ret_gn, ml_gn, w_out_ab,
           w_in_c, hg_lower, hg_gn, w_out_c, w_ffn_gate, w_ffn_up, w_ffn_down,
           w_router, b_router, w_moe_gate, w_moe_up, w_moe_down):
    p = {"ln_mix": ln_mix, "ln_ffn": ln_ffn, "ln_final": ln_final,
         "w_in_ab": w_in_ab, "b_gates_ab": b_gates_ab, "conv_w_ab": conv_w_ab, "conv_b_ab": conv_b_ab,
         "ret_gn": ret_gn, "ml_gn": ml_gn, "w_out_ab": w_out_ab,
         "w_in_c": w_in_c, "hg_lower": hg_lower, "hg_gn": hg_gn, "w_out_c": w_out_c,
         "w_ffn_gate": w_ffn_gate, "w_ffn_up": w_ffn_up, "w_ffn_down": w_ffn_down,
         "w_router": w_router, "b_router": b_router,
         "w_moe_gate": w_moe_gate, "w_moe_up": w_moe_up, "w_moe_down": w_moe_down}
    w = _prep_weights(p)
    bp = x_prompt.shape[0]
    zeros = lambda *shape: jnp.zeros(shape, F32)
    x2_p, o_p, states_p = _mixers(
        x_prompt, 0,
        zeros(bp, RET_HEADS, HEAD_DIM, HEAD_DIM), zeros(bp, ML_HEADS, HEAD_DIM, HEAD_DIM),
        zeros(bp, ML_HEADS, HEAD_DIM), jnp.full((bp, ML_HEADS), NEG_INF, F32),
        zeros(bp, ML_CONV - 1, 2 * ML_HEADS * HEAD_DIM), zeros(bp, HG_HEADS, HEAD_DIM, HEAD_DIM), p, w)
    x2_s, o_s, states_s = _mixers(
        x_sample, 16384,
        state_ret[0], state_mlstm_c[0], state_mlstm_n[0], state_mlstm_m[0], state_conv[0], state_hgrn[0], p, w)
    y_p, y_s = _experts([(x2_p, o_p), (x2_s, o_s)], [x_prompt.shape, x_sample.shape], p, w)
    out = [y_p, y_s]
    for a, b in zip(states_p, states_s):
        out += [a, b]
    return tuple(out)
```

```python
import functools
import math

import numpy as np
import jax
import jax.numpy as jnp
from jax import lax
from jax.experimental import pallas as pl
from jax.experimental.pallas import tpu as pltpu

F32 = jnp.float32
BF16 = jnp.bfloat16

D_MODEL = 1024
HEAD_DIM = 128
RET_HEADS = 4
ML_HEADS = 4
ML_CONV = 4
HG_HEADS = 8
D_FF = 2816
N_EXPERTS = 8
ROPE_BASE = 10000.0
EPS = 1e-6
RET_CHUNK = 128
ML_CHUNK = 128
HG_BLOCK = 128
HG_SUB = 16

LANES = 128
SUBLANES = 8
AB_MAIN = 4096
VMEM_LIMIT = 56 * 1024 * 1024

NEG_INF = float("-inf")


def _cparams(sem):
    return pltpu.CompilerParams(dimension_semantics=sem, vmem_limit_bytes=VMEM_LIMIT)


def _sigmoid(x):
    return 1.0 / (1.0 + jnp.exp(-x))


def _silu(x):
    return x * _sigmoid(x)


def _rmsnorm_rows(x, g):
    ms = jnp.mean(x * x, axis=-1, keepdims=True)
    return x * lax.rsqrt(ms + EPS) * g


def _dot(a, b):
    return jnp.dot(a, b, preferred_element_type=F32)


def _dot_nt(a, b):
    return lax.dot_general(a, b, (((1,), (1,)), ((), ())), preferred_element_type=F32)


def _dot_tn(a, b):
    return lax.dot_general(a, b, (((0,), (0,)), ((), ())), preferred_element_type=F32)


def _pad_rows(x, rows):
    if x.shape[0] == rows:
        return x
    return jnp.concatenate([x, jnp.zeros((rows - x.shape[0], x.shape[1]), x.dtype)], axis=0)


def _norm_matmul_body(x_ref, g_ref, w_ref, ws_ref, o_ref, os_ref, h_ref):
    @pl.when(pl.program_id(1) == 0)
    def _():
        h = _rmsnorm_rows(x_ref[...], g_ref[...]).astype(BF16)
        h_ref[...] = h
        os_ref[...] = _dot(h, ws_ref[...])

    o_ref[...] = _dot(h_ref[...], w_ref[...]).astype(BF16)


def _norm_matmul(x, g, w, w_side, tm, tn):
    t, d = x.shape
    n, ns = w.shape[1], w_side.shape[1]
    return pl.pallas_call(
        _norm_matmul_body,
        grid=(t // tm, n // tn),
        in_specs=[
            pl.BlockSpec((tm, d), lambda i, j: (i, 0)),
            pl.BlockSpec((1, d), lambda i, j: (0, 0)),
            pl.BlockSpec((d, tn), lambda i, j: (0, j)),
            pl.BlockSpec((d, ns), lambda i, j: (0, 0)),
        ],
        out_specs=[pl.BlockSpec((tm, tn), lambda i, j: (i, j)), pl.BlockSpec((tm, ns), lambda i, j: (i, 0))],
        out_shape=[jax.ShapeDtypeStruct((t, n), BF16), jax.ShapeDtypeStruct((t, ns), F32)],
        scratch_shapes=[pltpu.VMEM((tm, d), BF16)],
        compiler_params=_cparams(("parallel", "arbitrary")),
        name="norm_matmul",
    )(x, g, w, w_side)


def _mixer_residual(x_ref, a_refs, w_refs):
    x = x_ref[...]
    for a_ref, w_ref in zip(a_refs, w_refs):
        x = x + _dot(a_ref[...].astype(BF16), w_ref[...])
    return x


def _ffn_body(*refs, n_in):
    a_refs, w_refs = refs[:n_in], refs[n_in:2 * n_in]
    x_ref, g_ref, wg_ref, wu_ref, wd_ref, o_ref, h_ref, x1_ref = refs[2 * n_in:]
    f = pl.program_id(1)

    @pl.when(f == 0)
    def _():
        x1 = _mixer_residual(x_ref, a_refs, w_refs)
        x1_ref[...] = x1
        h_ref[...] = _rmsnorm_rows(x1, g_ref[...]).astype(BF16)

    h = h_ref[...]
    a = _dot(h, wg_ref[...])
    u = _dot(h, wu_ref[...])
    act = (_silu(a) * u).astype(BF16)
    x1_ref[...] += _dot(act, wd_ref[...])

    @pl.when(f == pl.num_programs(1) - 1)
    def _():
        o_ref[...] = x1_ref[...]


def _ffn(acts, weights, x, g, wg, wu, wd, tm, tf):
    t, d = x.shape
    dff = wg.shape[1]
    n_in = len(acts)
    in_specs = [pl.BlockSpec((tm, a.shape[1]), lambda i, f: (i, 0)) for a in acts]
    in_specs += [pl.BlockSpec(w.shape, lambda i, f: (0, 0)) for w in weights]
    in_specs += [
        pl.BlockSpec((tm, d), lambda i, f: (i, 0)),
        pl.BlockSpec((1, d), lambda i, f: (0, 0)),
        pl.BlockSpec((d, tf), lambda i, f: (0, f)),
        pl.BlockSpec((d, tf), lambda i, f: (0, f)),
        pl.BlockSpec((tf, d), lambda i, f: (f, 0)),
    ]
    return pl.pallas_call(
        functools.partial(_ffn_body, n_in=n_in),
        grid=(t // tm, dff // tf),
        in_specs=in_specs,
        out_specs=pl.BlockSpec((tm, d), lambda i, f: (i, 0)),
        out_shape=jax.ShapeDtypeStruct((t, d), F32),
        scratch_shapes=[pltpu.VMEM((tm, d), BF16), pltpu.VMEM((tm, d), F32)],
        compiler_params=_cparams(("parallel", "arbitrary")),
        name="ffn",
    )(*acts, *weights, x, g, wg, wu, wd)


MOE_TB = 512
MOE_ALIGN = 16
MOE_BURST = 4
MOE_TILE = 256
MOE_CHUNKS_PER_TILE = MOE_TILE // MOE_ALIGN


def _moe_block_rows(tb):
    return 2 * tb + N_EXPERTS * MOE_ALIGN


def _group_block_specs(shape, blocks):
    specs, first = [], 0
    for nb in blocks:
        specs.append(pl.BlockSpec(shape, lambda i, *_, first=first, nb=nb: (jnp.clip(i - first, 0, nb - 1), 0)))
        first += nb
    return specs


def _route_body(*refs, blocks):
    n = len(blocks)
    a_refs, wo_ref, x_refs = refs[:n], refs[n], refs[n + 1:2 * n + 1]
    g_ref, whi_ref, wlo_ref, b_ref, tri_ref, x3_ref, h_ref, col_ref, rowf_ref, cnt_ref = refs[2 * n + 1:]
    i = pl.program_id(0)
    act, x_in = a_refs[0][...], x_refs[0][...]
    first = blocks[0]
    for a_ref, x_ref, nb in zip(a_refs[1:], x_refs[1:], blocks[1:]):
        act = jnp.where(i >= first, a_ref[...], act)
        x_in = jnp.where(i >= first, x_ref[...], x_in)
        first += nb
    x3 = x_in + _dot(act.astype(BF16), wo_ref[...])
    x3_ref[...] = x3
    h = _rmsnorm_rows(x3, g_ref[...])
    h_hi = h.astype(BF16)
    h_ref[...] = h_hi
    h_lo = (h - h_hi.astype(F32)).astype(BF16)

    both = _dot_nt(jnp.concatenate([whi_ref[...], wlo_ref[...]], axis=0), h_hi)
    logits = (both[:N_EXPERTS, :] + both[LANES:LANES + N_EXPERTS, :]
              + _dot_nt(whi_ref[...], h_lo)[:N_EXPERTS, :] + b_ref[:, 0:1])
    row = lax.broadcasted_iota(jnp.int32, logits.shape, 0)
    m1 = jnp.max(logits, axis=0, keepdims=True)
    i1 = jnp.min(jnp.where(logits == m1, row, N_EXPERTS), axis=0, keepdims=True)
    lg2 = jnp.where(row == i1, NEG_INF, logits)
    m2 = jnp.max(lg2, axis=0, keepdims=True)
    i2 = jnp.min(jnp.where(lg2 == m2, row, N_EXPERTS), axis=0, keepdims=True)
    e = jnp.exp(m2 - m1)
    w1 = 1.0 / (1.0 + e)
    w2 = e / (1.0 + e)

    sel = jnp.where(row == i1, 1.0, jnp.where(row == i2, 1.0, 0.0))
    before = _dot(sel.astype(BF16), tri_ref[...])
    n = jnp.broadcast_to(jnp.sum(sel, axis=1, keepdims=True), (N_EXPERTS, LANES))
    padded = jnp.floor((n + (MOE_ALIGN - 1.0)) * (1.0 / MOE_ALIGN)) * MOE_ALIGN
    base = _scan_rows(padded, jnp.add, 0.0) - padded
    pos = base[:, 0:1] + before
    loc1 = jnp.sum(jnp.where(row == i1, pos, 0.0), axis=0, keepdims=True)
    loc2 = jnp.sum(jnp.where(row == i2, pos, 0.0), axis=0, keepdims=True)
    rowf = jnp.where(row == 0, loc1, jnp.where(row == 1, loc2, jnp.where(row == 2, w1,
                                                                          jnp.where(row == 3, w2, 0.0))))
    rowf_ref[0] = rowf
    for c in range(rowf.shape[1] // LANES):
        col_ref[c * LANES:(c + 1) * LANES, :] = jnp.transpose(_pad_rows(rowf[:, c * LANES:(c + 1) * LANES], LANES))
    cnt_ref[0] = n


def _route(acts, w_out, xs, g, w_hi, w_lo, b, tb):
    d = xs[0].shape[1]
    blocks = [x.shape[0] // tb for x in xs]
    nb = sum(blocks)
    t = nb * tb
    tri = jnp.asarray(np.triu(np.ones((tb, tb), np.float32), 1), BF16)
    full = lambda shape: pl.BlockSpec(shape, lambda i: (0,) * len(shape))
    return pl.pallas_call(
        functools.partial(_route_body, blocks=blocks),
        grid=(nb,),
        in_specs=(_group_block_specs((tb, acts[0].shape[1]), blocks) + [full(w_out.shape)]
                  + _group_block_specs((tb, d), blocks)
                  + [full((1, d)), full((LANES, d)), full((LANES, d)), full((SUBLANES, LANES)), full((tb, tb))]),
        out_specs=[
            pl.BlockSpec((tb, d), lambda i: (i, 0)),
            pl.BlockSpec((tb, d), lambda i: (i, 0)),
            pl.BlockSpec((tb, LANES), lambda i: (i, 0)),
            pl.BlockSpec((1, SUBLANES, tb), lambda i: (i, 0, 0)),
            pl.BlockSpec((1, SUBLANES, LANES), lambda i: (i, 0, 0)),
        ],
        out_shape=[
            jax.ShapeDtypeStruct((t, d), F32),
            jax.ShapeDtypeStruct((t, d), BF16),
            jax.ShapeDtypeStruct((t, LANES), F32),
            jax.ShapeDtypeStruct((nb, SUBLANES, tb), F32),
            jax.ShapeDtypeStruct((nb, SUBLANES, LANES), F32),
        ],
        compiler_params=_cparams(("parallel",)),
        name="route",
    )(*acts, w_out, *xs, g, w_hi, w_lo, b, tri)


def _moe_plan(cnt):
    n = cnt[:, :N_EXPERTS, 0].astype(jnp.int32)
    chunks = (n + (MOE_ALIGN - 1)) // MOE_ALIGN
    src = jnp.cumsum(chunks, axis=1) - chunks
    seg_chunks = jnp.sum(chunks, axis=0)
    seg_tiles = (seg_chunks + (MOE_CHUNKS_PER_TILE - 1)) // MOE_CHUNKS_PER_TILE
    seg_end = jnp.cumsum(seg_tiles)
    seg_first = (seg_end - seg_tiles) * MOE_CHUNKS_PER_TILE
    dst = seg_first[None, :] + jnp.cumsum(chunks, axis=0) - chunks
    flat = lambda a: a.reshape(-1).astype(jnp.int32)
    return {"src": flat(src), "dst": flat(dst), "chunks": flat(chunks),
            "first_tile": flat(seg_end - seg_tiles), "tile_count": flat(seg_tiles),
            "tail_start": flat(seg_first + seg_chunks), "tail_count": flat(seg_tiles * MOE_CHUNKS_PER_TILE - seg_chunks)}


def _chunk_rows(chunk, n=1):
    return pl.ds(pl.multiple_of(chunk * MOE_ALIGN, MOE_ALIGN), n * MOE_ALIGN)


def _for_block_copies(nch_ref, blk, fn):
    for e in range(N_EXPERTS):
        count = nch_ref[blk * N_EXPERTS + e]
        bursts = count // MOE_BURST
        lax.fori_loop(0, bursts, lambda i, carry, e=e: (fn(e, i * MOE_BURST, MOE_BURST), carry)[1], 0)
        lax.fori_loop(bursts * MOE_BURST, count, lambda c, carry, e=e: (fn(e, c, 1), carry)[1], 0)


def _block_copy_counts(nch_ref, blk):
    bursts = singles = 0
    for e in range(N_EXPERTS):
        count = nch_ref[blk * N_EXPERTS + e]
        bursts = bursts + count // MOE_BURST
        singles = singles + lax.rem(count, MOE_BURST)
    return bursts, singles


def _wait_block_copies(nch_ref, blk, copy):
    bursts, singles = _block_copy_counts(nch_ref, blk)
    lax.fori_loop(0, bursts, lambda i, carry: (copy(MOE_BURST).wait(), carry)[1], 0)
    lax.fori_loop(0, singles, lambda i, carry: (copy(1).wait(), carry)[1], 0)


def _gather_body(src_ref, dst_ref, nch_ref, tail_ref, ntail_ref, last_ref, h_ref, rowf_ref, hs_ref, z_ref, sem,
                 *, n_tiles):
    b = pl.program_id(0)
    slot = lax.rem(b, 2)
    tb = h_ref.shape[0]
    loc = rowf_ref[0].astype(jnp.int32)
    r = lax.broadcasted_iota(jnp.int32, (z_ref.shape[1], tb), 0)
    onehot = jnp.where(r == loc[0:1, :], 1.0, jnp.where(r == loc[1:2, :], 1.0, 0.0)).astype(BF16)
    z_ref[slot] = _dot(onehot, h_ref[...]).astype(BF16)

    def copy(buf, src_chunk, dst_chunk, n=1):
        return pltpu.make_async_copy(z_ref.at[buf, _chunk_rows(src_chunk, n)], hs_ref.at[_chunk_rows(dst_chunk, n)],
                                     sem.at[buf])

    def drain(blk, buf):
        _wait_block_copies(nch_ref, blk, lambda n: copy(buf, 0, 0, n))

    @pl.when(b > 0)
    def _():
        drain(b - 1, 1 - slot)

    _for_block_copies(nch_ref, b, lambda e, c, n: copy(slot, src_ref[b * N_EXPERTS + e] + c,
                                                       dst_ref[b * N_EXPERTS + e] + c, n).start())

    @pl.when(b == pl.num_programs(0) - 1)
    def _():
        drain(b, slot)
        z_ref[0, 0:MOE_ALIGN, :] = jnp.zeros((MOE_ALIGN, z_ref.shape[2]), BF16)
        past = last_ref[0]
        total = n_tiles * MOE_CHUNKS_PER_TILE
        for e in range(N_EXPERTS):
            lax.fori_loop(0, ntail_ref[e], lambda c, carry, e=e: (copy(0, 0, tail_ref[e] + c).start(), carry)[1], 0)
        lax.fori_loop(past, total, lambda c, carry: (copy(0, 0, c).start(), carry)[1], 0)
        for e in range(N_EXPERTS):
            lax.fori_loop(0, ntail_ref[e], lambda c, carry: (copy(0, 0, 0).wait(), carry)[1], 0)
        lax.fori_loop(past, total, lambda c, carry: (copy(0, 0, 0).wait(), carry)[1], 0)


def _moe_gather(plan, h, rowf, n_tiles, tb):
    t, d = h.shape
    rows = _moe_block_rows(tb)
    past = (plan["first_tile"][-1:] + plan["tile_count"][-1:]) * MOE_CHUNKS_PER_TILE
    return pl.pallas_call(
        functools.partial(_gather_body, n_tiles=n_tiles),
        grid_spec=pltpu.PrefetchScalarGridSpec(
            num_scalar_prefetch=6,
            grid=(t // tb,),
            in_specs=[
                pl.BlockSpec((tb, d), lambda i, *_: (i, 0)),
                pl.BlockSpec((1, SUBLANES, tb), lambda i, *_: (i, 0, 0)),
            ],
            out_specs=pl.BlockSpec(memory_space=pl.ANY),
            scratch_shapes=[pltpu.VMEM((2, rows, d), BF16), pltpu.SemaphoreType.DMA((2,))],
        ),
        out_shape=jax.ShapeDtypeStruct((n_tiles * MOE_TILE, d), BF16),
        compiler_params=_cparams(("arbitrary",)),
        name="moe_gather",
    )(plan["src"], plan["dst"], plan["chunks"], plan["tail_start"], plan["tail_count"], past, h, rowf)


W_CHUNK = 256


def _expert_ffn_body(first_ref, count_ref, xs_ref, wg_hbm, wu_hbm, wd_hbm, ys_ref,
                     wg_b, wu_b, wd_b, stage_c, stage_r, x_buf, y_buf, w_sem, in_sem, out_sem, *, n_tiles, halves):
    e = pl.program_id(0)
    last = pl.num_programs(0) - 1
    first = first_ref[e]
    count = count_ref[e]
    cur = lax.rem(e, 2)
    dff = wg_b.shape[2]
    tf = dff // halves
    per = dff // W_CHUNK
    n_chunks = 3 * per

    def span(j):
        return pl.ds(pl.multiple_of(j * W_CHUNK, W_CHUNK), W_CHUNK)

    def fetch(kind, expert, j, slot):
        if kind == 2:
            return pltpu.make_async_copy(wd_hbm.at[expert, span(j), :], stage_r.at[slot], w_sem.at[slot])
        return pltpu.make_async_copy((wg_hbm, wu_hbm)[kind].at[expert, :, span(j)], stage_c.at[slot], w_sem.at[slot])

    def for_kind(c, fn):
        kind = c // per
        for k in range(3):
            @pl.when(kind == k)
            def _(k=k):
                fn(k, c - k * per)

    n_pairs = -(-n_chunks // 2)

    def start_pair(expert, p):
        for s in range(2):
            c = 2 * p + s

            @pl.when(c < n_chunks)
            def _(c=c, s=s):
                for_kind(c, lambda k, j: fetch(k, expert, j, s).start())

    def convert_pair(expert, dst, p):
        for s in range(2):
            c = 2 * p + s

            def land(k, j, s=s):
                fetch(k, expert, j, s).wait()
                if k == 2:
                    wd_b[dst, span(j), :] = stage_r[s].astype(BF16)
                else:
                    (wg_b, wu_b)[k][dst, :, span(j)] = stage_c[s].astype(BF16)

            @pl.when(c < n_chunks)
            def _(c=c, land=land):
                for_kind(c, land)

        start_pair(expert, p + 1)

    def maybe_convert(p):
        @pl.when(p < n_pairs)
        def _():
            convert_pair(e + 1, 1 - cur, p)
        return jnp.minimum(p + 1, n_pairs)

    @pl.when(e == 0)
    def _():
        start_pair(0, 0)
        lax.fori_loop(0, n_pairs, lambda p, carry: (convert_pair(0, 0, p), carry)[1], 0)

    @pl.when(e < last)
    def _():
        start_pair(e + 1, 0)

    def tile_rows(tile):
        return pl.ds(pl.multiple_of(tile * MOE_TILE, MOE_TILE), MOE_TILE)

    def load(j, buf):
        return pltpu.make_async_copy(xs_ref.at[tile_rows(first + j)], x_buf.at[buf], in_sem.at[buf])

    def store(tile, buf):
        return pltpu.make_async_copy(y_buf.at[buf], ys_ref.at[tile_rows(tile)], out_sem.at[buf])

    @pl.when(count > 0)
    def _():
        load(0, 0).start()

    def tile_step(j, c):
        buf = lax.rem(j, 2)

        @pl.when(j + 1 < count)
        def _():
            load(j + 1, 1 - buf).start()

        load(j, buf).wait()
        xs = x_buf[buf]
        y = None
        for f in range(halves):
            a = _dot(xs, wg_b[cur, :, f * tf:(f + 1) * tf])
            u = _dot(xs, wu_b[cur, :, f * tf:(f + 1) * tf])
            part = _dot((_silu(a) * u).astype(BF16), wd_b[cur, f * tf:(f + 1) * tf, :])
            y = part if y is None else y + part

        @pl.when(j >= 2)
        def _():
            store(first, buf).wait()

        y_buf[buf] = y.astype(BF16)
        store(first + j, buf).start()
        return maybe_convert(c)

    p_done = lax.fori_loop(0, count, tile_step, jnp.where(e < last, 0, n_pairs))
    lax.fori_loop(p_done, n_pairs, lambda p, carry: (convert_pair(e + 1, 1 - cur, p), carry)[1], 0)

    for back in (2, 1):
        @pl.when(count >= back)
        def _(back=back):
            store(first, lax.rem(count - back, 2)).wait()

    @pl.when(e == last)
    def _():
        end = first + count
        y_buf[0] = jnp.zeros(y_buf.shape[1:], BF16)
        lax.fori_loop(end, n_tiles, lambda t, c: (store(t, 0).start(), c)[1], 0)
        lax.fori_loop(end, n_tiles, lambda t, c: (store(t, 0).wait(), c)[1], 0)


def _expert_ffn(first_tile, tile_count, xs, wg, wu, wd):
    rows, d = xs.shape
    dff = wg.shape[2]
    any_space = pl.BlockSpec(memory_space=pl.ANY)
    return pl.pallas_call(
        functools.partial(_expert_ffn_body, n_tiles=rows // MOE_TILE, halves=2),
        grid_spec=pltpu.PrefetchScalarGridSpec(
            num_scalar_prefetch=2,
            grid=(N_EXPERTS,),
            in_specs=[any_space, any_space, any_space, any_space],
            out_specs=any_space,
            scratch_shapes=[
                pltpu.VMEM((2, d, dff), BF16), pltpu.VMEM((2, d, dff), BF16), pltpu.VMEM((2, dff, d), BF16),
                pltpu.VMEM((2, d, W_CHUNK), F32), pltpu.VMEM((2, W_CHUNK, d), F32),
                pltpu.VMEM((2, MOE_TILE, d), BF16), pltpu.VMEM((2, MOE_TILE, d), BF16),
                pltpu.SemaphoreType.DMA((2,)), pltpu.SemaphoreType.DMA((2,)), pltpu.SemaphoreType.DMA((2,))],
        ),
        out_shape=jax.ShapeDtypeStruct((rows, d), BF16),
        compiler_params=_cparams(("arbitrary",)),
        name="expert_ffn",
    )(first_tile, tile_count, xs, wg, wu, wd)


def _combine_body(src_ref, dst_ref, nch_ref, x_ref, col_ref, gf_ref, ys_ref, *refs, blocks):
    o_refs, (y_ref, sem) = refs[:len(blocks)], refs[len(blocks):]
    b = pl.program_id(0)
    nb = pl.num_programs(0)
    slot = lax.rem(b, 2)

    def copy(buf, src_chunk, dst_chunk, n=1):
        return pltpu.make_async_copy(ys_ref.at[_chunk_rows(dst_chunk, n)], y_ref.at[buf, _chunk_rows(src_chunk, n)],
                                     sem.at[buf])

    def fetch(blk, buf):
        _for_block_copies(nch_ref, blk, lambda e, c, n: copy(buf, src_ref[blk * N_EXPERTS + e] + c,
                                                             dst_ref[blk * N_EXPERTS + e] + c, n).start())

    @pl.when(b == 0)
    def _():
        y_ref[...] = jnp.zeros_like(y_ref)
        fetch(0, 0)

    @pl.when(b + 1 < nb)
    def _():
        fetch(b + 1, 1 - slot)

    _wait_block_copies(nch_ref, b, lambda n: copy(slot, 0, 0, n))

    col = col_ref[...]
    loc1 = col[:, 0:1].astype(jnp.int32)
    loc2 = col[:, 1:2].astype(jnp.int32)
    r = lax.broadcasted_iota(jnp.int32, (col.shape[0], y_ref.shape[1]), 1)
    weights = jnp.where(r == loc1, col[:, 2:3], jnp.where(r == loc2, col[:, 3:4], 0.0)).astype(BF16)
    out = _rmsnorm_rows(x_ref[...] + _dot(weights, y_ref[slot]), gf_ref[...])
    first = 0
    for o_ref, n_blocks in zip(o_refs, blocks):
        @pl.when((b >= first) & (b < first + n_blocks))
        def _(o_ref=o_ref):
            o_ref[...] = out
        first += n_blocks


def _moe_combine(src, dst, nch, x, col, g_final, ys, tb, blocks):
    t, d = x.shape
    rows = _moe_block_rows(tb)
    return pl.pallas_call(
        functools.partial(_combine_body, blocks=blocks),
        grid_spec=pltpu.PrefetchScalarGridSpec(
            num_scalar_prefetch=3,
            grid=(t // tb,),
            in_specs=[
                pl.BlockSpec((tb, d), lambda i, *_: (i, 0)),
                pl.BlockSpec((tb, LANES), lambda i, *_: (i, 0)),
                pl.BlockSpec((1, d), lambda i, *_: (0, 0)),
                pl.BlockSpec(memory_space=pl.ANY),
            ],
            out_specs=_group_block_specs((tb, d), blocks),
            scratch_shapes=[pltpu.VMEM((2, rows, d), BF16), pltpu.SemaphoreType.DMA((2,))],
        ),
        out_shape=[jax.ShapeDtypeStruct((nb * tb, d), F32) for nb in blocks],
        compiler_params=_cparams(("arbitrary",)),
        name="moe_combine",
    )(src, dst, nch, x, col, g_final, ys)


def _moe(acts, w_out, xs, g, w, moe_weights, g_final):
    tb = MOE_TB
    assert all(x.shape[0] % tb == 0 for x in xs), [x.shape for x in xs]
    blocks = [x.shape[0] // tb for x in xs]
    nb = sum(blocks)
    t = nb * tb
    x, h, col, rowf, cnt = _route(acts, w_out, xs, g, w["w_router_hi"], w["w_router_lo"], w["b_router"], tb)
    n_tiles = -(-(2 * t + nb * N_EXPERTS * (MOE_ALIGN - 1)) // MOE_TILE) + N_EXPERTS
    plan = _moe_plan(cnt)
    xg = _moe_gather(plan, h, rowf, n_tiles, tb)
    yg = _expert_ffn(plan["first_tile"], plan["tile_count"], xg, *moe_weights)
    return _moe_combine(plan["src"], plan["dst"], plan["chunks"], x, col, g_final, yg, tb, blocks)


def _retention_tables(c_real, ct):
    h = np.arange(RET_HEADS, dtype=np.float64)
    log_gamma = np.log1p(-np.exp2(-5.0 - h))
    idx = np.arange(ct, dtype=np.float64)
    live = idx < c_real
    diff = idx[:, None] - idx[None, :]
    causal = (diff >= 0) & live[:, None] & live[None, :]
    decay = np.where(causal[None], np.exp(np.where(causal, diff, 0.0)[None] * log_gamma[:, None, None]), 0.0)
    q_dec = np.where(live[None], np.exp((idx + 1.0)[None] * log_gamma[:, None]), 0.0)
    k_dec = np.where(live[None], np.exp((c_real - 1.0 - idx)[None] * log_gamma[:, None]), 0.0)
    q_dec = np.broadcast_to(q_dec[..., None], (RET_HEADS, ct, LANES))
    k_dec = np.broadcast_to(k_dec[..., None], (RET_HEADS, ct, LANES))
    return (jnp.asarray(decay, F32), jnp.asarray(q_dec, F32), jnp.asarray(k_dec, F32))


def _rope_tables(pos0, length):
    half = HEAD_DIM // 2
    inv = ROPE_BASE ** (-np.arange(half, dtype=np.float64) / half)
    ang = (pos0 + np.arange(length, dtype=np.float64))[:, None] * inv[None, :]
    cos = np.concatenate([np.cos(ang), np.cos(ang)], axis=-1)
    sin = np.concatenate([-np.sin(ang), np.sin(ang)], axis=-1)
    return jnp.asarray(cos, F32), jnp.asarray(sin, F32)


def _retention_body(q_ref, k_ref, v_ref, g_ref, cos_ref, sin_ref, dec_ref, qd_ref, kd_ref, s0_ref, gn_ref,
                    o_ref, s_ref, *, c_real, ct, bb):
    @pl.when(pl.program_id(1) == 0)
    def _():
        s_ref[...] = s0_ref[...]

    cos = cos_ref[...]
    sin = sin_ref[...]
    for g, hd in [(g, hd) for g in range(bb) for hd in range(RET_HEADS)]:
        lo, hi = hd * HEAD_DIM, (hd + 1) * HEAD_DIM
        q = q_ref[g, :, lo:hi].astype(F32)
        k = k_ref[g, :, lo:hi].astype(F32)
        q = (q * cos + pltpu.roll(q, HEAD_DIM // 2, 1) * sin) * (HEAD_DIM ** -0.5)
        k = k * cos + pltpu.roll(k, HEAD_DIM // 2, 1) * sin
        v = v_ref[g, :, lo:hi].astype(F32)
        qb = q.astype(BF16)
        q_dec = qd_ref[hd, :c_real, :]
        state = s_ref[g, hd]

        s = _dot_nt(qb, _pad_rows(k, ct).astype(BF16)) * dec_ref[hd, :c_real, :]
        o = _dot(s.astype(BF16), _pad_rows(v, ct).astype(BF16)) + _dot(qb, state.astype(BF16)) * q_dec
        chunk_dec = q_dec[c_real - 1:c_real, :]
        s_ref[g, hd] = state * chunk_dec + _dot_tn((k * kd_ref[hd, :c_real, :]).astype(BF16), v.astype(BF16))

        o = o - jnp.mean(o, axis=-1, keepdims=True)
        o = o * lax.rsqrt(jnp.mean(o * o, axis=-1, keepdims=True) + EPS)
        o_ref[g, :, lo:hi] = _silu(g_ref[g, :, lo:hi].astype(F32)) * (o * gn_ref[:, lo:hi])


def _retention(proj, pos0, state, gn, bb):
    b, length, _ = proj.shape
    assert b % bb == 0, (b, bb)
    c_real = math.gcd(length, RET_CHUNK)
    ct = RET_CHUNK
    width = RET_HEADS * HEAD_DIM
    cos, sin = _rope_tables(pos0, length)
    decay, q_dec, k_dec = _retention_tables(c_real, ct)

    def col(i):
        return pl.BlockSpec((bb, c_real, width), lambda bi, c: (bi, c, i))

    full = lambda shape: pl.BlockSpec(shape, lambda bi, c: (0,) * len(shape))
    state_spec = pl.BlockSpec((bb, RET_HEADS, HEAD_DIM, HEAD_DIM), lambda bi, c: (bi, 0, 0, 0))
    return pl.pallas_call(
        functools.partial(_retention_body, c_real=c_real, ct=ct, bb=bb),
        grid=(b // bb, length // c_real),
        in_specs=[
            col(0), col(1), col(2), col(3),
            pl.BlockSpec((c_real, HEAD_DIM), lambda bi, c: (c, 0)),
            pl.BlockSpec((c_real, HEAD_DIM), lambda bi, c: (c, 0)),
            full((RET_HEADS, ct, ct)), full((RET_HEADS, ct, LANES)), full((RET_HEADS, ct, LANES)),
            state_spec,
            full((1, width)),
        ],
        out_specs=[
            pl.BlockSpec((bb, c_real, width), lambda bi, c: (bi, c, 0)),
            state_spec,
        ],
        out_shape=[
            jax.ShapeDtypeStruct((b, length, width), F32),
            jax.ShapeDtypeStruct(state.shape, F32),
        ],
        compiler_params=_cparams(("parallel", "arbitrary")),
        name="retention",
    )(proj, proj, proj, proj, cos, sin, decay, q_dec, k_dec, state, gn)


def _scan_rows(x, op, fill):
    rows = x.shape[0]
    row = lax.broadcasted_iota(jnp.int32, x.shape, 0)
    sh = 1
    while sh < rows:
        x = op(x, jnp.where(row >= sh, pltpu.roll(x, sh, 0), fill))
        sh *= 2
    return x


def _mlstm_body(q_ref, k_ref, v_ref, og_ref, gate_ref, bg_ref, cw_ref, cb_ref, cs_ref, c0_ref, n0_ref, m0_ref,
                gn_ref, o_ref, c_ref, n_ref, m_ref, conv_ref, xc_ref, *, c_real, ct, bb):
    tail = ML_CONV - 1

    @pl.when(pl.program_id(1) == 0)
    def _():
        c_ref[...] = c0_ref[...]
        n_ref[...] = n0_ref[...]
        m_ref[...] = m0_ref[...]
        xc_ref[:, SUBLANES - tail:SUBLANES, :] = cs_ref[...]

    for g in range(bb):
        _mlstm_sequence(q_ref.at[g], k_ref.at[g], v_ref.at[g], og_ref.at[g], gate_ref.at[g], bg_ref, cw_ref, cb_ref,
                        gn_ref, o_ref.at[g], c_ref.at[g], n_ref.at[g], m_ref.at[g], conv_ref.at[g], xc_ref.at[g],
                        c_real=c_real, ct=ct)


def _mlstm_sequence(q_ref, k_ref, v_ref, og_ref, gate_ref, bg_ref, cw_ref, cb_ref, gn_ref, o_ref, c_ref, n_ref, m_ref,
                    conv_ref, xc_ref, *, c_real, ct):
    width = ML_HEADS * HEAD_DIM
    tail = ML_CONV - 1

    xc_ref[SUBLANES:SUBLANES + c_real, :width] = q_ref[...].astype(F32)
    xc_ref[SUBLANES:SUBLANES + c_real, width:] = k_ref[...].astype(F32)
    y = cb_ref[...] + cw_ref[tail:tail + 1, :] * xc_ref[SUBLANES:SUBLANES + c_real, :]
    for j in range(tail):
        y = y + cw_ref[j:j + 1, :] * xc_ref[SUBLANES - tail + j:SUBLANES - tail + j + c_real, :]
    new_tail = xc_ref[c_real:c_real + SUBLANES, :]
    conv_ref[...] = new_tail[SUBLANES - tail:, :]
    xc_ref[0:SUBLANES, :] = new_tail
    qk = _silu(y)
    q_all = qk[:, :width]
    k_all = qk[:, width:] * (HEAD_DIM ** -0.5)
    v_all = v_ref[...].astype(F32)

    gates = gate_ref[...] + bg_ref[...]
    f_pre = pltpu.roll(gates, LANES - ML_HEADS, 1)
    log_f = jnp.minimum(f_pre, 0.0) - jnp.log(1.0 + jnp.exp(-jnp.abs(f_pre)))
    b_cum = _scan_rows(log_f, jnp.add, 0.0)
    src = gates - b_cum
    cmx = _scan_rows(src, jnp.maximum, NEG_INF)
    m_prev = m_ref[...]
    mx = jnp.maximum(m_prev, cmx)
    m_t = b_cum + mx
    w_prev = jnp.exp(m_prev - mx)
    e_neg_m = jnp.exp(-m_t)
    last = c_real - 1
    b_last = b_cum[last:last + 1, :]
    m_new = m_t[last:last + 1, :]
    w_end = jnp.exp(b_last + src - m_new)
    dec = jnp.exp(b_last + m_prev - m_new)
    m_ref[...] = m_new
    src_t = jnp.transpose(_pad_rows(src, ct))

    t_idx = lax.broadcasted_iota(jnp.int32, (c_real, ct), 0)
    s_idx = lax.broadcasted_iota(jnp.int32, (c_real, ct), 1)
    live = (s_idx <= t_idx) & (s_idx < c_real)

    for hd in range(ML_HEADS):
        lo, hi = hd * HEAD_DIM, (hd + 1) * HEAD_DIM
        q = q_all[:, lo:hi]
        k = k_all[:, lo:hi]
        v = v_all[:, lo:hi]
        qb = q.astype(BF16)
        c_state = c_ref[hd]
        n_state = n_ref[hd:hd + 1, :]
        wp = w_prev[:, hd:hd + 1]
        w = jnp.where(live, jnp.exp(src_t[hd:hd + 1, :] - mx[:, hd:hd + 1]), 0.0)
        s = _dot_nt(qb, _pad_rows(k, ct).astype(BF16)) * w
        num = _dot(s.astype(BF16), _pad_rows(v, ct).astype(BF16)) + wp * _dot(qb, c_state.astype(BF16))
        den = jnp.sum(s, axis=-1, keepdims=True) + wp * jnp.sum(q * n_state, axis=-1, keepdims=True)
        hh = num / jnp.maximum(jnp.abs(den), e_neg_m[:, hd:hd + 1])
        kw = k * w_end[:, hd:hd + 1]
        dec_h = dec[:, hd:hd + 1]
        c_ref[hd] = dec_h * c_state + _dot_tn(kw.astype(BF16), v.astype(BF16))
        n_ref[hd:hd + 1, :] = dec_h * n_state + jnp.sum(kw, axis=0, keepdims=True)

        hh = _sigmoid(og_ref[:, lo:hi].astype(F32)) * hh
        hh = hh - jnp.mean(hh, axis=-1, keepdims=True)
        hh = hh * lax.rsqrt(jnp.mean(hh * hh, axis=-1, keepdims=True) + EPS)
        o_ref[:, lo:hi] = hh * gn_ref[:, lo:hi]


def _mlstm(proj, gates, b_gates, conv_w, conv_b, conv_s, c0, n0, m0, gn, bb):
    b, length, _ = proj.shape
    assert b % bb == 0, (b, bb)
    c_real = math.gcd(length, ML_CHUNK)
    ct = ML_CHUNK
    width = ML_HEADS * HEAD_DIM
    base = (2 * RET_HEADS * HEAD_DIM + 2 * RET_HEADS * HEAD_DIM) // width

    def col(i):
        return pl.BlockSpec((bb, c_real, width), lambda bi, c: (bi, c, base + i))

    full = lambda shape: pl.BlockSpec(shape, lambda bi, c: (0,) * len(shape))
    per_b = lambda shape: pl.BlockSpec((bb,) + shape, lambda bi, c: (bi,) + (0,) * len(shape))
    bg = jnp.zeros((1, LANES), F32).at[0, :2 * ML_HEADS].set(b_gates)
    m0p = jnp.zeros((b, 1, LANES), F32).at[:, 0, :ML_HEADS].set(m0)
    outs = pl.pallas_call(
        functools.partial(_mlstm_body, c_real=c_real, ct=ct, bb=bb),
        grid=(b // bb, length // c_real),
        in_specs=[
            col(0), col(1), col(2), col(3),
            pl.BlockSpec((bb, c_real, LANES), lambda bi, c: (bi, c, 0)),
            full((1, LANES)),
            full((ML_CONV, 2 * width)),
            full((1, 2 * width)),
            per_b((ML_CONV - 1, 2 * width)),
            per_b((ML_HEADS, HEAD_DIM, HEAD_DIM)),
            per_b((ML_HEADS, HEAD_DIM)),
            per_b((1, LANES)),
            full((1, width)),
        ],
        out_specs=[
            pl.BlockSpec((bb, c_real, width), lambda bi, c: (bi, c, 0)),
            per_b((ML_HEADS, HEAD_DIM, HEAD_DIM)),
            per_b((ML_HEADS, HEAD_DIM)),
            per_b((1, LANES)),
            per_b((ML_CONV - 1, 2 * width)),
        ],
        out_shape=[
            jax.ShapeDtypeStruct((b, length, width), F32),
            jax.ShapeDtypeStruct(c0.shape, F32),
            jax.ShapeDtypeStruct(n0.shape, F32),
            jax.ShapeDtypeStruct((b, 1, LANES), F32),
            jax.ShapeDtypeStruct(conv_s.shape, F32),
        ],
        scratch_shapes=[pltpu.VMEM((bb, ct + SUBLANES, 2 * width), F32)],
        compiler_params=_cparams(("parallel", "arbitrary")),
        name="mlstm",
    )(proj, proj, proj, proj, gates, bg, conv_w, conv_b, conv_s, c0, n0, m0p, gn)
    h_m, c_new, n_new, m_new, conv_new = outs
    return h_m, c_new, n_new, m_new[:, 0, :ML_HEADS], conv_new


HG_UNROLL_BLOCK = 8
HG_UNROLL_STEP = 8


def _hgrn_head(cols, q_ref, f_ref, i_ref, g_ref, hl_ref, gn_ref, o_ref, st, *, blk, sub):
    hl = hl_ref[:, cols]
    hmax = jnp.max(hl, axis=0, keepdims=True)
    ex = jnp.exp(hl - hmax)
    p = ex / jnp.sum(ex, axis=0, keepdims=True)
    lower = (p[0:1] + p[1:2]) - p[0:1]
    gn = gn_ref[:, cols]

    f = lower + (1.0 - lower) * _sigmoid(f_ref[:, cols])
    log_k = jnp.log2(jnp.maximum(1.0 - f, 0.0))
    q = _silu(q_ref[:, cols].astype(F32))
    v = i_ref[:, cols].astype(F32)
    gate = _silu(g_ref[:, cols].astype(F32))

    row = lax.broadcasted_iota(jnp.int32, (blk, LANES), 0)
    rsub = row & (sub - 1)
    a = jnp.log2(f)
    sh = 1
    while sh < sub:
        a = a + jnp.where(rsub >= sh, pltpu.roll(a, sh, 0), 0.0)
        sh *= 2
    c = a - log_k
    q_in = q * jnp.exp2(a)

    t_idx = lax.broadcasted_iota(jnp.int32, (SUBLANES, LANES), 0)
    for j in range(blk // sub):
        r0 = j * sub
        a_end = a[r0 + sub - 1:r0 + sub, :]
        inter = _dot_nt(q_in[r0:r0 + sub].astype(BF16), st.astype(BF16))
        k_hat = jnp.exp2(a_end - c[r0:r0 + sub])
        v_j = v[r0:r0 + sub]
        st = st * jnp.exp2(a_end) + _dot_tn(v_j.astype(BF16), k_hat.astype(BF16))
        for part in range(sub // SUBLANES):
            t0 = r0 + part * SUBLANES
            a_t = a[t0:t0 + SUBLANES]
            q_t = q[t0:t0 + SUBLANES]
            o_t = inter[part * SUBLANES:(part + 1) * SUBLANES]
            for s in range((part + 1) * SUBLANES):
                arg = a_t - c[r0 + s:r0 + s + 1, :]
                if s >= part * SUBLANES:
                    arg = jnp.where(t_idx >= s - part * SUBLANES, arg, NEG_INF)
                z = q_t * jnp.exp2(arg)
                o_t = o_t + jnp.sum(z, axis=-1, keepdims=True) * v[r0 + s:r0 + s + 1, :]
            o_t = o_t * lax.rsqrt(jnp.mean(o_t * o_t, axis=-1, keepdims=True) + EPS)
            o_ref[t0:t0 + SUBLANES, cols] = o_t * gn * gate[t0:t0 + SUBLANES]
    return st


def _block_ref_rows(a, level, row8):
    rows = a.shape[0]
    half = level // 2
    if half >= SUBLANES:
        return jnp.concatenate([jnp.broadcast_to(a[b0 + half - 1:b0 + half, :], (level, LANES))
                                for b0 in range(0, rows, level)], axis=0)
    if level == 2:
        return jnp.where((row8 & 1) == 1, pltpu.roll(a, 1, 0), a)
    pieces = []
    for v0 in range(0, rows, SUBLANES):
        picks = [jnp.broadcast_to(a[v0 + b0 + half - 1:v0 + b0 + half, :], (SUBLANES, LANES))
                 for b0 in range(0, SUBLANES, level)]
        piece = picks[-1]
        for i in range(len(picks) - 2, -1, -1):
            piece = jnp.where(row8[:SUBLANES] < (i + 1) * level, picks[i], piece)
        pieces.append(piece)
    return jnp.concatenate(pieces, axis=0)


def _hgrn_head_block(cols, q_ref, f_ref, i_ref, g_ref, hl_ref, gn_ref, o_ref, st, *, blk):
    hl = hl_ref[:, cols]
    hmax = jnp.max(hl, axis=0, keepdims=True)
    ex = jnp.exp(hl - hmax)
    p = ex / jnp.sum(ex, axis=0, keepdims=True)
    lower = (p[0:1] + p[1:2]) - p[0:1]

    f = lower + (1.0 - lower) * _sigmoid(f_ref[:, cols])
    kk = 1.0 - f
    q = _silu(q_ref[:, cols].astype(F32))
    vb = i_ref[:, cols]
    a = _scan_rows(jnp.log2(f), jnp.add, 0.0)
    c = a - jnp.log2(jnp.maximum(kk, 0.0))
    row8 = lax.broadcasted_iota(jnp.int32, (blk, LANES), 0) & (SUBLANES - 1)
    group = (lax.broadcasted_iota(jnp.int32, (blk, blk), 0)
             ^ lax.broadcasted_iota(jnp.int32, (blk, blk), 1))

    scores = None
    level = blk
    while level >= 2:
        half = level // 2
        if half >= SUBLANES:
            zero = jnp.zeros((half, LANES), F32)
            qs, ks = [], []
            for b0 in range(0, blk, level):
                ref = a[b0 + half - 1:b0 + half, :]
                qs += [zero, q[b0 + half:b0 + level] * jnp.exp2(a[b0 + half:b0 + level] - ref)]
                ks += [jnp.exp2(ref - c[b0:b0 + half]), zero]
            q_l = jnp.concatenate(qs, axis=0)
            k_l = jnp.concatenate(ks, axis=0)
        else:
            ref = _block_ref_rows(a, level, row8)
            upper = (row8 & (level - 1)) >= half
            q_l = jnp.where(upper, q * jnp.exp2(a - ref), 0.0)
            k_l = jnp.where(upper, 0.0, jnp.exp2(ref - c))
        r = _dot_nt(q_l.astype(BF16), k_l.astype(BF16))
        scores = r if scores is None else jnp.where(group < level, r, scores)
        level = half
    scores = jnp.where(group < 1, _dot_nt(q.astype(BF16), kk.astype(BF16)), scores)

    a_end = a[blk - 1:blk, :]
    o = _dot_nt((q * jnp.exp2(a)).astype(BF16), st.astype(BF16)) + _dot(scores.astype(BF16), vb)
    o = o * lax.rsqrt(jnp.mean(o * o, axis=-1, keepdims=True) + EPS)
    o_ref[:, cols] = o * gn_ref[:, cols] * _silu(g_ref[:, cols].astype(F32))
    return st * jnp.exp2(a_end) + _dot_tn(vb, jnp.exp2(a_end - c).astype(BF16))


def _hgrn_body(q_ref, f_ref, i_ref, g_ref, hl_ref, s0_ref, gn_ref, o_ref, s_ref, st_ref, *, blk, sub, bb):
    @pl.when(pl.program_id(1) == 0)
    def _():
        for g, h in [(g, h) for g in range(bb) for h in range(HG_HEADS)]:
            st_ref[g, h] = jnp.transpose(s0_ref[g, h])

    if blk == HG_BLOCK:
        head, per_iter = functools.partial(_hgrn_head_block, blk=blk), HG_UNROLL_BLOCK // bb
    else:
        head, per_iter = functools.partial(_hgrn_head, blk=blk, sub=sub), HG_UNROLL_STEP // bb

    def heads(it, carry):
        for g, u in [(g, u) for g in range(bb) for u in range(per_iter)]:
            h = it * per_iter + u
            cols = pl.ds(pl.multiple_of(h * HEAD_DIM, HEAD_DIM), HEAD_DIM)
            st_ref[g, h] = head(cols, q_ref.at[g], f_ref.at[g], i_ref.at[g], g_ref.at[g], hl_ref, gn_ref,
                                o_ref.at[g], st_ref[g, h])
        return carry

    lax.fori_loop(0, HG_HEADS // per_iter, heads, 0)

    @pl.when(pl.program_id(1) == pl.num_programs(1) - 1)
    def _():
        for g, h in [(g, h) for g in range(bb) for h in range(HG_HEADS)]:
            s_ref[g, h] = jnp.transpose(st_ref[g, h])


def _hgrn(proj, f_pre, hg_lower, state, gn, bb):
    b, length, _ = proj.shape
    blk = min(length, HG_BLOCK)
    sub = math.gcd(length, HG_SUB)
    assert b % bb == 0, (b, bb)
    width = HG_HEADS * HEAD_DIM

    def col(i):
        return pl.BlockSpec((bb, blk, width), lambda bi, c: (bi, c, i))

    full = lambda shape: pl.BlockSpec(shape, lambda bi, c: (0,) * len(shape))
    state_spec = pl.BlockSpec((bb, HG_HEADS, HEAD_DIM, HEAD_DIM), lambda bi, c: (bi, 0, 0, 0))
    return pl.pallas_call(
        functools.partial(_hgrn_body, blk=blk, sub=sub, bb=bb),
        grid=(b // bb, length // blk),
        in_specs=[col(0), col(0), col(1), col(2), full(hg_lower.shape), state_spec, full((1, width))],
        out_specs=[pl.BlockSpec((bb, blk, width), lambda bi, c: (bi, c, 0)), state_spec],
        out_shape=[
            jax.ShapeDtypeStruct((b, length, width), F32),
            jax.ShapeDtypeStruct(state.shape, F32),
        ],
        scratch_shapes=[pltpu.VMEM((bb, HG_HEADS, HEAD_DIM, HEAD_DIM), F32)],
        compiler_params=_cparams(("parallel", "arbitrary")),
        name="hgrn2",
    )(proj, f_pre, proj, proj, hg_lower, state, gn)


def _prep_weights(p):
    w_in_ab = p["w_in_ab"][0]
    w_in_c = p["w_in_c"][0]
    width = HG_HEADS * HEAD_DIM
    w_gates = jnp.zeros((D_MODEL, LANES), BF16).at[:, :2 * ML_HEADS].set(w_in_ab[:, AB_MAIN:].astype(BF16))
    w_router = jnp.zeros((LANES, D_MODEL), F32).at[:N_EXPERTS, :].set(p["w_router"][0].T)
    w_router_hi = w_router.astype(BF16)
    w_router_lo = (w_router - w_router_hi.astype(F32)).astype(BF16)
    b_router = jnp.broadcast_to(p["b_router"][0][:, None], (N_EXPERTS, LANES))
    return {
        "w_ab": w_in_ab[:, :AB_MAIN].astype(BF16),
        "w_ab_gates": w_gates,
        "w_out_ab": p["w_out_ab"][0].astype(BF16),
        "w_c": jnp.concatenate([w_in_c[:, :width], w_in_c[:, 2 * width:]], axis=1).astype(BF16),
        "w_c_forget": w_in_c[:, width:2 * width].astype(BF16),
        "w_out_c": p["w_out_c"][0].astype(BF16),
        "w_ffn_gate": p["w_ffn_gate"][0].astype(BF16),
        "w_ffn_up": p["w_ffn_up"][0].astype(BF16),
        "w_ffn_down": p["w_ffn_down"][0].astype(BF16),
        "w_router_hi": w_router_hi,
        "w_router_lo": w_router_lo,
        "b_router": b_router,
    }


def _mixers(x, pos0, ret_s, mc_s, mn_s, mm_s, conv_s, hg_s, p, w):
    b, length, d = x.shape
    t = b * length
    tm = min(512, t)
    tm_mm = min(1024, t)
    prompt = length >= RET_CHUNK
    bb = 1 if prompt else 8
    assert t % tm_mm == 0 and t % tm == 0 and b % bb == 0 and b % 2 == 0, (b, length)
    row = lambda v: v.reshape(1, -1)
    x0 = x.reshape(t, d)

    proj, gates = _norm_matmul(x0, row(p["ln_mix"][0]), w["w_ab"], w["w_ab_gates"], tm_mm, AB_MAIN // 2)
    proj = proj.reshape(b, length, AB_MAIN)
    o_ret, ret_new = _retention(proj, pos0, ret_s, row(p["ret_gn"][0]), 4 if prompt else bb)
    h_m, c_new, n_new, m_new, conv_new = _mlstm(
        proj, gates.reshape(b, length, LANES), p["b_gates_ab"][0], p["conv_w_ab"][0], row(p["conv_b_ab"][0]),
        conv_s, mc_s, mn_s, mm_s, row(p["ml_gn"][0]), bb)
    half = RET_HEADS * HEAD_DIM
    x2 = _ffn([o_ret.reshape(t, half), h_m.reshape(t, half)], [w["w_out_ab"][:half], w["w_out_ab"][half:]], x0,
              row(p["ln_ffn"][0]), w["w_ffn_gate"], w["w_ffn_up"], w["w_ffn_down"], tm, D_FF // 2)

    proj_c, f_pre = _norm_matmul(x2, row(p["ln_mix"][1]), w["w_c"], w["w_c_forget"], tm_mm, 3 * d // 2)
    o_hg, hg_new = _hgrn(proj_c.reshape(b, length, 3 * d), f_pre.reshape(b, length, d), p["hg_lower"], hg_s,
                         row(p["hg_gn"][0]), bb)
    states = (ret_new[None], c_new[None], n_new[None], m_new[None], conv_new[None], hg_new[None])
    return x2, o_hg.reshape(t, d), states


def _experts(groups, shapes, p, w):
    row = lambda v: v.reshape(1, -1)
    ys = _moe([o for _, o in groups], w["w_out_c"], [x for x, _ in groups], row(p["ln_ffn"][1]), w,
              (p["w_moe_gate"][0], p["w_moe_up"][0], p["w_moe_down"][0]), row(p["ln_final"]))
    return [y.reshape(shape) for y, shape in zip(ys, shapes)]


def kernel(x_prompt, x_sample, state_ret, state_mlstm_c, state_mlstm_n, state_mlstm_m, state_conv, state_hgrn,
           ln_mix, ln_ffn, ln_final, w_in_ab, b_gates_ab, conv_w_ab, conv_b_ab, ret_gn, ml_gn, w_out_ab,
           w_in_c, hg_lower, hg_gn, w_out_c, w_ffn_gate, w_ffn_up, w_ffn_down,
           w_router, b_router, w_moe_gate, w_moe_up, w_moe_down):
    p = {"ln_mix": ln_mix, "ln_ffn": ln_ffn, "ln_final": ln_final,
         "w_in_ab": w_in_ab, "b_gates_ab": b_gates_ab, "conv_w_ab": conv_w_ab, "conv_b_ab": conv_b_ab,
         "ret_gn": ret_gn, "ml_gn": ml_gn, "w_out_ab": w_out_ab,
         "w_in_c": w_in_c, "hg_lower": hg_lower, "hg_gn": hg_gn, "w_out_c": w_out_c,
         "w_ffn_gate": w_ffn_gate, "w_ffn_up": w_ffn_up, "w_ffn_down": w_ffn_down,
         "w_router": w_router, "b_router": b_router,
         "w_moe_gate": w_moe_gate, "w_moe_up": w_moe_up, "w_moe_down": w_moe_down}
    w = _prep_weights(p)
    bp = x_prompt.shape[0]
    zeros = lambda *shape: jnp.zeros(shape, F32)
    x2_p, o_p, states_p = _mixers(
        x_prompt, 0,
        zeros(bp, RET_HEADS, HEAD_DIM, HEAD_DIM), zeros(bp, ML_HEADS, HEAD_DIM, HEAD_DIM),
        zeros(bp, ML_HEADS, HEAD_DIM), jnp.full((bp, ML_HEADS), NEG_INF, F32),
        zeros(bp, ML_CONV - 1, 2 * ML_HEADS * HEAD_DIM), zeros(bp, HG_HEADS, HEAD_DIM, HEAD_DIM), p, w)
    x2_s, o_s, states_s = _mixers(
        x_sample, 16384,
        state_ret[0], state_mlstm_c[0], state_mlstm_n[0], state_mlstm_m[0], state_conv[0], state_hgrn[0], p, w)
    y_p, y_s = _experts([(x2_p, o_p), (x2_s, o_s)], [x_prompt.shape, x_sample.shape], p, w)
    out = [y_p, y_s]
    for a, b in zip(states_p, states_s):
        out += [a, b]
    return tuple(out)
```

```python
import functools
import math

import numpy as np
import jax
import jax.numpy as jnp
from jax import lax
from jax.experimental import pallas as pl
from jax.experimental.pallas import tpu as pltpu

F32 = jnp.float32
BF16 = jnp.bfloat16

D_MODEL = 1024
HEAD_DIM = 128
RET_HEADS = 4
ML_HEADS = 4
ML_CONV = 4
HG_HEADS = 8
D_FF = 2816
N_EXPERTS = 8
ROPE_BASE = 10000.0
EPS = 1e-6
RET_CHUNK = 128
ML_CHUNK = 128
HG_BLOCK = 128
HG_SUB = 16

LANES = 128
SUBLANES = 8
AB_MAIN = 4096
VMEM_LIMIT = 56 * 1024 * 1024

NEG_INF = float("-inf")


def _cparams(sem):
    return pltpu.CompilerParams(dimension_semantics=sem, vmem_limit_bytes=VMEM_LIMIT)


def _sigmoid(x):
    return 1.0 / (1.0 + jnp.exp(-x))


def _silu(x):
    return x * _sigmoid(x)


def _rmsnorm_rows(x, g):
    ms = jnp.mean(x * x, axis=-1, keepdims=True)
    return x * lax.rsqrt(ms + EPS) * g


def _dot(a, b):
    return jnp.dot(a, b, preferred_element_type=F32)


def _dot_nt(a, b):
    return lax.dot_general(a, b, (((1,), (1,)), ((), ())), preferred_element_type=F32)


def _dot_tn(a, b):
    return lax.dot_general(a, b, (((0,), (0,)), ((), ())), preferred_element_type=F32)


def _pad_rows(x, rows):
    if x.shape[0] == rows:
        return x
    return jnp.concatenate([x, jnp.zeros((rows - x.shape[0], x.shape[1]), x.dtype)], axis=0)


def _norm_matmul_body(x_ref, g_ref, w_ref, ws_ref, o_ref, os_ref, h_ref):
    @pl.when(pl.program_id(1) == 0)
    def _():
        h = _rmsnorm_rows(x_ref[...], g_ref[...]).astype(BF16)
        h_ref[...] = h
        os_ref[...] = _dot(h, ws_ref[...])

    o_ref[...] = _dot(h_ref[...], w_ref[...]).astype(BF16)


def _norm_matmul(x, g, w, w_side, tm, tn):
    t, d = x.shape
    n, ns = w.shape[1], w_side.shape[1]
    return pl.pallas_call(
        _norm_matmul_body,
        grid=(t // tm, n // tn),
        in_specs=[
            pl.BlockSpec((tm, d), lambda i, j: (i, 0)),
            pl.BlockSpec((1, d), lambda i, j: (0, 0)),
            pl.BlockSpec((d, tn), lambda i, j: (0, j)),
            pl.BlockSpec((d, ns), lambda i, j: (0, 0)),
        ],
        out_specs=[pl.BlockSpec((tm, tn), lambda i, j: (i, j)), pl.BlockSpec((tm, ns), lambda i, j: (i, 0))],
        out_shape=[jax.ShapeDtypeStruct((t, n), BF16), jax.ShapeDtypeStruct((t, ns), F32)],
        scratch_shapes=[pltpu.VMEM((tm, d), BF16)],
        compiler_params=_cparams(("parallel", "arbitrary")),
        name="norm_matmul",
    )(x, g, w, w_side)


def _mixer_residual(x_ref, a_refs, w_refs):
    x = x_ref[...]
    for a_ref, w_ref in zip(a_refs, w_refs):
        x = x + _dot(a_ref[...].astype(BF16), w_ref[...])
    return x


def _ffn_body(*refs, n_in):
    a_refs, w_refs = refs[:n_in], refs[n_in:2 * n_in]
    x_ref, g_ref, wg_ref, wu_ref, wd_ref, o_ref, h_ref, x1_ref = refs[2 * n_in:]
    f = pl.program_id(1)

    @pl.when(f == 0)
    def _():
        x1 = _mixer_residual(x_ref, a_refs, w_refs)
        x1_ref[...] = x1
        h_ref[...] = _rmsnorm_rows(x1, g_ref[...]).astype(BF16)

    h = h_ref[...]
    a = _dot(h, wg_ref[...])
    u = _dot(h, wu_ref[...])
    act = (_silu(a) * u).astype(BF16)
    x1_ref[...] += _dot(act, wd_ref[...])

    @pl.when(f == pl.num_programs(1) - 1)
    def _():
        o_ref[...] = x1_ref[...]


def _ffn(acts, weights, x, g, wg, wu, wd, tm, tf):
    t, d = x.shape
    dff = wg.shape[1]
    n_in = len(acts)
    in_specs = [pl.BlockSpec((tm, a.shape[1]), lambda i, f: (i, 0)) for a in acts]
    in_specs += [pl.BlockSpec(w.shape, lambda i, f: (0, 0)) for w in weights]
    in_specs += [
        pl.BlockSpec((tm, d), lambda i, f: (i, 0)),
        pl.BlockSpec((1, d), lambda i, f: (0, 0)),
        pl.BlockSpec((d, tf), lambda i, f: (0, f)),
        pl.BlockSpec((d, tf), lambda i, f: (0, f)),
        pl.BlockSpec((tf, d), lambda i, f: (f, 0)),
    ]
    return pl.pallas_call(
        functools.partial(_ffn_body, n_in=n_in),
        grid=(t // tm, dff // tf),
        in_specs=in_specs,
        out_specs=pl.BlockSpec((tm, d), lambda i, f: (i, 0)),
        out_shape=jax.ShapeDtypeStruct((t, d), F32),
        scratch_shapes=[pltpu.VMEM((tm, d), BF16), pltpu.VMEM((tm, d), F32)],
        compiler_params=_cparams(("parallel", "arbitrary")),
        name="ffn",
    )(*acts, *weights, x, g, wg, wu, wd)


MOE_TB = 512
MOE_ALIGN = 16
MOE_BURST = 4
MOE_TILE = 256
MOE_CHUNKS_PER_TILE = MOE_TILE // MOE_ALIGN


def _moe_block_rows(tb):
    return 2 * tb + N_EXPERTS * MOE_ALIGN


def _group_block_specs(shape, blocks):
    specs, first = [], 0
    for nb in blocks:
        specs.append(pl.BlockSpec(shape, lambda i, *_, first=first, nb=nb: (jnp.clip(i - first, 0, nb - 1), 0)))
        first += nb
    return specs


def _route_body(*refs, blocks):
    n = len(blocks)
    a_refs, wo_ref, x_refs = refs[:n], refs[n], refs[n + 1:2 * n + 1]
    g_ref, whi_ref, wlo_ref, b_ref, tri_ref, x3_ref, h_ref, col_ref, rowf_ref, cnt_ref = refs[2 * n + 1:]
    i = pl.program_id(0)
    act, x_in = a_refs[0][...], x_refs[0][...]
    first = blocks[0]
    for a_ref, x_ref, nb in zip(a_refs[1:], x_refs[1:], blocks[1:]):
        act = jnp.where(i >= first, a_ref[...], act)
        x_in = jnp.where(i >= first, x_ref[...], x_in)
        first += nb
    x3 = x_in + _dot(act.astype(BF16), wo_ref[...])
    x3_ref[...] = x3
    h = _rmsnorm_rows(x3, g_ref[...])
    h_hi = h.astype(BF16)
    h_ref[...] = h_hi
    h_lo = (h - h_hi.astype(F32)).astype(BF16)

    both = _dot_nt(jnp.concatenate([whi_ref[...], wlo_ref[...]], axis=0), h_hi)
    logits = (both[:N_EXPERTS, :] + both[LANES:LANES + N_EXPERTS, :]
              + _dot_nt(whi_ref[...], h_lo)[:N_EXPERTS, :] + b_ref[:, 0:1])
    row = lax.broadcasted_iota(jnp.int32, logits.shape, 0)
    m1 = jnp.max(logits, axis=0, keepdims=True)
    i1 = jnp.min(jnp.where(logits == m1, row, N_EXPERTS), axis=0, keepdims=True)
    lg2 = jnp.where(row == i1, NEG_INF, logits)
    m2 = jnp.max(lg2, axis=0, keepdims=True)
    i2 = jnp.min(jnp.where(lg2 == m2, row, N_EXPERTS), axis=0, keepdims=True)
    e = jnp.exp(m2 - m1)
    w1 = 1.0 / (1.0 + e)
    w2 = e / (1.0 + e)

    sel = jnp.where(row == i1, 1.0, jnp.where(row == i2, 1.0, 0.0))
    before = _dot(sel.astype(BF16), tri_ref[...])
    n = jnp.broadcast_to(jnp.sum(sel, axis=1, keepdims=True), (N_EXPERTS, LANES))
    padded = jnp.floor((n + (MOE_ALIGN - 1.0)) * (1.0 / MOE_ALIGN)) * MOE_ALIGN
    base = _scan_rows(padded, jnp.add, 0.0) - padded
    pos = base[:, 0:1] + before
    loc1 = jnp.sum(jnp.where(row == i1, pos, 0.0), axis=0, keepdims=True)
    loc2 = jnp.sum(jnp.where(row == i2, pos, 0.0), axis=0, keepdims=True)
    rowf = jnp.where(row == 0, loc1, jnp.where(row == 1, loc2, jnp.where(row == 2, w1,
                                                                          jnp.where(row == 3, w2, 0.0))))
    rowf_ref[0] = rowf
    for c in range(rowf.shape[1] // LANES):
        col_ref[c * LANES:(c + 1) * LANES, :] = jnp.transpose(_pad_rows(rowf[:, c * LANES:(c + 1) * LANES], LANES))
    cnt_ref[0] = n


def _route(acts, w_out, xs, g, w_hi, w_lo, b, tb):
    d = xs[0].shape[1]
    blocks = [x.shape[0] // tb for x in xs]
    nb = sum(blocks)
    t = nb * tb
    tri = jnp.asarray(np.triu(np.ones((tb, tb), np.float32), 1), BF16)
    full = lambda shape: pl.BlockSpec(shape, lambda i: (0,) * len(shape))
    return pl.pallas_call(
        functools.partial(_route_body, blocks=blocks),
        grid=(nb,),
        in_specs=(_group_block_specs((tb, acts[0].shape[1]), blocks) + [full(w_out.shape)]
                  + _group_block_specs((tb, d), blocks)
                  + [full((1, d)), full((LANES, d)), full((LANES, d)), full((SUBLANES, LANES)), full((tb, tb))]),
        out_specs=[
            pl.BlockSpec((tb, d), lambda i: (i, 0)),
            pl.BlockSpec((tb, d), lambda i: (i, 0)),
            pl.BlockSpec((tb, LANES), lambda i: (i, 0)),
            pl.BlockSpec((1, SUBLANES, tb), lambda i: (i, 0, 0)),
            pl.BlockSpec((1, SUBLANES, LANES), lambda i: (i, 0, 0)),
        ],
        out_shape=[
            jax.ShapeDtypeStruct((t, d), F32),
            jax.ShapeDtypeStruct((t, d), BF16),
            jax.ShapeDtypeStruct((t, LANES), F32),
            jax.ShapeDtypeStruct((nb, SUBLANES, tb), F32),
            jax.ShapeDtypeStruct((nb, SUBLANES, LANES), F32),
        ],
        compiler_params=_cparams(("parallel",)),
        name="route",
    )(*acts, w_out, *xs, g, w_hi, w_lo, b, tri)


def _moe_plan(cnt):
    n = cnt[:, :N_EXPERTS, 0].astype(jnp.int32)
    chunks = (n + (MOE_ALIGN - 1)) // MOE_ALIGN
    src = jnp.cumsum(chunks, axis=1) - chunks
    seg_chunks = jnp.sum(chunks, axis=0)
    seg_tiles = (seg_chunks + (MOE_CHUNKS_PER_TILE - 1)) // MOE_CHUNKS_PER_TILE
    seg_end = jnp.cumsum(seg_tiles)
    seg_first = (seg_end - seg_tiles) * MOE_CHUNKS_PER_TILE
    dst = seg_first[None, :] + jnp.cumsum(chunks, axis=0) - chunks
    flat = lambda a: a.reshape(-1).astype(jnp.int32)
    return {"src": flat(src), "dst": flat(dst), "chunks": flat(chunks),
            "first_tile": flat(seg_end - seg_tiles), "tile_count": flat(seg_tiles),
            "tail_start": flat(seg_first + seg_chunks), "tail_count": flat(seg_tiles * MOE_CHUNKS_PER_TILE - seg_chunks)}


def _chunk_rows(chunk, n=1):
    return pl.ds(pl.multiple_of(chunk * MOE_ALIGN, MOE_ALIGN), n * MOE_ALIGN)


def _for_block_copies(nch_ref, blk, fn):
    for e in range(N_EXPERTS):
        count = nch_ref[blk * N_EXPERTS + e]
        bursts = count // MOE_BURST
        lax.fori_loop(0, bursts, lambda i, carry, e=e: (fn(e, i * MOE_BURST, MOE_BURST), carry)[1], 0)
        lax.fori_loop(bursts * MOE_BURST, count, lambda c, carry, e=e: (fn(e, c, 1), carry)[1], 0)


def _block_copy_counts(nch_ref, blk):
    bursts = singles = 0
    for e in range(N_EXPERTS):
        count = nch_ref[blk * N_EXPERTS + e]
        bursts = bursts + count // MOE_BURST
        singles = singles + lax.rem(count, MOE_BURST)
    return bursts, singles


def _wait_block_copies(nch_ref, blk, copy):
    bursts, singles = _block_copy_counts(nch_ref, blk)
    lax.fori_loop(0, bursts, lambda i, carry: (copy(MOE_BURST).wait(), carry)[1], 0)
    lax.fori_loop(0, singles, lambda i, carry: (copy(1).wait(), carry)[1], 0)


def _gather_body(src_ref, dst_ref, nch_ref, tail_ref, ntail_ref, last_ref, h_ref, rowf_ref, hs_ref, z_ref, sem,
                 *, n_tiles):
    b = pl.program_id(0)
    slot = lax.rem(b, 2)
    tb = h_ref.shape[0]
    loc = rowf_ref[0].astype(jnp.int32)
    r = lax.broadcasted_iota(jnp.int32, (z_ref.shape[1], tb), 0)
    onehot = jnp.where(r == loc[0:1, :], 1.0, jnp.where(r == loc[1:2, :], 1.0, 0.0)).astype(BF16)
    z_ref[slot] = _dot(onehot, h_ref[...]).astype(BF16)

    def copy(buf, src_chunk, dst_chunk, n=1):
        return pltpu.make_async_copy(z_ref.at[buf, _chunk_rows(src_chunk, n)], hs_ref.at[_chunk_rows(dst_chunk, n)],
                                     sem.at[buf])

    def drain(blk, buf):
        _wait_block_copies(nch_ref, blk, lambda n: copy(buf, 0, 0, n))

    @pl.when(b > 0)
    def _():
        drain(b - 1, 1 - slot)

    _for_block_copies(nch_ref, b, lambda e, c, n: copy(slot, src_ref[b * N_EXPERTS + e] + c,
                                                       dst_ref[b * N_EXPERTS + e] + c, n).start(priority=e % 2))

    @pl.when(b == pl.num_programs(0) - 1)
    def _():
        drain(b, slot)
        z_ref[0, 0:MOE_ALIGN, :] = jnp.zeros((MOE_ALIGN, z_ref.shape[2]), BF16)
        past = last_ref[0]
        total = n_tiles * MOE_CHUNKS_PER_TILE
        for e in range(N_EXPERTS):
            lax.fori_loop(0, ntail_ref[e], lambda c, carry, e=e: (copy(0, 0, tail_ref[e] + c).start(), carry)[1], 0)
        lax.fori_loop(past, total, lambda c, carry: (copy(0, 0, c).start(), carry)[1], 0)
        for e in range(N_EXPERTS):
            lax.fori_loop(0, ntail_ref[e], lambda c, carry: (copy(0, 0, 0).wait(), carry)[1], 0)
        lax.fori_loop(past, total, lambda c, carry: (copy(0, 0, 0).wait(), carry)[1], 0)


def _moe_gather(plan, h, rowf, n_tiles, tb):
    t, d = h.shape
    rows = _moe_block_rows(tb)
    past = (plan["first_tile"][-1:] + plan["tile_count"][-1:]) * MOE_CHUNKS_PER_TILE
    return pl.pallas_call(
        functools.partial(_gather_body, n_tiles=n_tiles),
        grid_spec=pltpu.PrefetchScalarGridSpec(
            num_scalar_prefetch=6,
            grid=(t // tb,),
            in_specs=[
                pl.BlockSpec((tb, d), lambda i, *_: (i, 0)),
                pl.BlockSpec((1, SUBLANES, tb), lambda i, *_: (i, 0, 0)),
            ],
            out_specs=pl.BlockSpec(memory_space=pl.ANY),
            scratch_shapes=[pltpu.VMEM((2, rows, d), BF16), pltpu.SemaphoreType.DMA((2,))],
        ),
        out_shape=jax.ShapeDtypeStruct((n_tiles * MOE_TILE, d), BF16),
        compiler_params=_cparams(("arbitrary",)),
        name="moe_gather",
    )(plan["src"], plan["dst"], plan["chunks"], plan["tail_start"], plan["tail_count"], past, h, rowf)


W_CHUNK = 256


def _expert_ffn_body(first_ref, count_ref, xs_ref, wg_hbm, wu_hbm, wd_hbm, ys_ref,
                     wg_b, wu_b, wd_b, stage_c, stage_r, x_buf, y_buf, w_sem, in_sem, out_sem, *, n_tiles, halves):
    e = pl.program_id(0)
    last = pl.num_programs(0) - 1
    first = first_ref[e]
    count = count_ref[e]
    cur = lax.rem(e, 2)
    dff = wg_b.shape[2]
    tf = dff // halves
    per = dff // W_CHUNK
    n_chunks = 3 * per

    def span(j):
        return pl.ds(pl.multiple_of(j * W_CHUNK, W_CHUNK), W_CHUNK)

    def fetch(kind, expert, j, slot):
        if kind == 2:
            return pltpu.make_async_copy(wd_hbm.at[expert, span(j), :], stage_r.at[slot], w_sem.at[slot])
        return pltpu.make_async_copy((wg_hbm, wu_hbm)[kind].at[expert, :, span(j)], stage_c.at[slot], w_sem.at[slot])

    def for_kind(c, fn):
        kind = c // per
        for k in range(3):
            @pl.when(kind == k)
            def _(k=k):
                fn(k, c - k * per)

    n_pairs = -(-n_chunks // 2)

    def start_pair(expert, p):
        for s in range(2):
            c = 2 * p + s

            @pl.when(c < n_chunks)
            def _(c=c, s=s):
                for_kind(c, lambda k, j: fetch(k, expert, j, s).start())

    def convert_pair(expert, dst, p):
        for s in range(2):
            c = 2 * p + s

            def land(k, j, s=s):
                fetch(k, expert, j, s).wait()
                if k == 2:
                    wd_b[dst, span(j), :] = stage_r[s].astype(BF16)
                else:
                    (wg_b, wu_b)[k][dst, :, span(j)] = stage_c[s].astype(BF16)

            @pl.when(c < n_chunks)
            def _(c=c, land=land):
                for_kind(c, land)

        start_pair(expert, p + 1)

    def maybe_convert(p):
        @pl.when(p < n_pairs)
        def _():
            convert_pair(e + 1, 1 - cur, p)
        return jnp.minimum(p + 1, n_pairs)

    @pl.when(e == 0)
    def _():
        start_pair(0, 0)
        lax.fori_loop(0, n_pairs, lambda p, carry: (convert_pair(0, 0, p), carry)[1], 0)

    @pl.when(e < last)
    def _():
        start_pair(e + 1, 0)

    def tile_rows(tile):
        return pl.ds(pl.multiple_of(tile * MOE_TILE, MOE_TILE), MOE_TILE)

    def load(j, buf):
        return pltpu.make_async_copy(xs_ref.at[tile_rows(first + j)], x_buf.at[buf], in_sem.at[buf])

    def store(tile, buf):
        return pltpu.make_async_copy(y_buf.at[buf], ys_ref.at[tile_rows(tile)], out_sem.at[buf])

    @pl.when(count > 0)
    def _():
        load(0, 0).start()

    def tile_step(j, c):
        buf = lax.rem(j, 2)

        @pl.when(j + 1 < count)
        def _():
            load(j + 1, 1 - buf).start()

        load(j, buf).wait()
        xs = x_buf[buf]
        y = None
        for f in range(halves):
            a = _dot(xs, wg_b[cur, :, f * tf:(f + 1) * tf])
            u = _dot(xs, wu_b[cur, :, f * tf:(f + 1) * tf])
            part = _dot((_silu(a) * u).astype(BF16), wd_b[cur, f * tf:(f + 1) * tf, :])
            y = part if y is None else y + part

        @pl.when(j >= 2)
        def _():
            store(first, buf).wait()

        y_buf[buf] = y.astype(BF16)
        store(first + j, buf).start()
        return maybe_convert(c)

    p_done = lax.fori_loop(0, count, tile_step, jnp.where(e < last, 0, n_pairs))
    lax.fori_loop(p_done, n_pairs, lambda p, carry: (convert_pair(e + 1, 1 - cur, p), carry)[1], 0)

    for back in (2, 1):
        @pl.when(count >= back)
        def _(back=back):
            store(first, lax.rem(count - back, 2)).wait()

    @pl.when(e == last)
    def _():
        end = first + count
        y_buf[0] = jnp.zeros(y_buf.shape[1:], BF16)
        lax.fori_loop(end, n_tiles, lambda t, c: (store(t, 0).start(), c)[1], 0)
        lax.fori_loop(end, n_tiles, lambda t, c: (store(t, 0).wait(), c)[1], 0)


def _expert_ffn(first_tile, tile_count, xs, wg, wu, wd):
    rows, d = xs.shape
    dff = wg.shape[2]
    any_space = pl.BlockSpec(memory_space=pl.ANY)
    return pl.pallas_call(
        functools.partial(_expert_ffn_body, n_tiles=rows // MOE_TILE, halves=2),
        grid_spec=pltpu.PrefetchScalarGridSpec(
            num_scalar_prefetch=2,
            grid=(N_EXPERTS,),
            in_specs=[any_space, any_space, any_space, any_space],
            out_specs=any_space,
            scratch_shapes=[
                pltpu.VMEM((2, d, dff), BF16), pltpu.VMEM((2, d, dff), BF16), pltpu.VMEM((2, dff, d), BF16),
                pltpu.VMEM((2, d, W_CHUNK), F32), pltpu.VMEM((2, W_CHUNK, d), F32),
                pltpu.VMEM((2, MOE_TILE, d), BF16), pltpu.VMEM((2, MOE_TILE, d), BF16),
                pltpu.SemaphoreType.DMA((2,)), pltpu.SemaphoreType.DMA((2,)), pltpu.SemaphoreType.DMA((2,))],
        ),
        out_shape=jax.ShapeDtypeStruct((rows, d), BF16),
        compiler_params=_cparams(("arbitrary",)),
        name="expert_ffn",
    )(first_tile, tile_count, xs, wg, wu, wd)


def _combine_body(src_ref, dst_ref, nch_ref, x_ref, col_ref, gf_ref, ys_ref, *refs, blocks):
    o_refs, (y_ref, sem) = refs[:len(blocks)], refs[len(blocks):]
    b = pl.program_id(0)
    nb = pl.num_programs(0)
    slot = lax.rem(b, 2)

    def copy(buf, src_chunk, dst_chunk, n=1):
        return pltpu.make_async_copy(ys_ref.at[_chunk_rows(dst_chunk, n)], y_ref.at[buf, _chunk_rows(src_chunk, n)],
                                     sem.at[buf])

    def fetch(blk, buf):
        _for_block_copies(nch_ref, blk, lambda e, c, n: copy(buf, src_ref[blk * N_EXPERTS + e] + c,
                                                             dst_ref[blk * N_EXPERTS + e] + c, n).start(priority=e % 2))

    @pl.when(b == 0)
    def _():
        y_ref[...] = jnp.zeros_like(y_ref)
        fetch(0, 0)

    @pl.when(b + 1 < nb)
    def _():
        fetch(b + 1, 1 - slot)

    _wait_block_copies(nch_ref, b, lambda n: copy(slot, 0, 0, n))

    col = col_ref[...]
    loc1 = col[:, 0:1].astype(jnp.int32)
    loc2 = col[:, 1:2].astype(jnp.int32)
    r = lax.broadcasted_iota(jnp.int32, (col.shape[0], y_ref.shape[1]), 1)
    weights = jnp.where(r == loc1, col[:, 2:3], jnp.where(r == loc2, col[:, 3:4], 0.0)).astype(BF16)
    out = _rmsnorm_rows(x_ref[...] + _dot(weights, y_ref[slot]), gf_ref[...])
    first = 0
    for o_ref, n_blocks in zip(o_refs, blocks):
        @pl.when((b >= first) & (b < first + n_blocks))
        def _(o_ref=o_ref):
            o_ref[...] = out
        first += n_blocks


def _moe_combine(src, dst, nch, x, col, g_final, ys, tb, blocks):
    t, d = x.shape
    rows = _moe_block_rows(tb)
    return pl.pallas_call(
        functools.partial(_combine_body, blocks=blocks),
        grid_spec=pltpu.PrefetchScalarGridSpec(
            num_scalar_prefetch=3,
            grid=(t // tb,),
            in_specs=[
                pl.BlockSpec((tb, d), lambda i, *_: (i, 0)),
                pl.BlockSpec((tb, LANES), lambda i, *_: (i, 0)),
                pl.BlockSpec((1, d), lambda i, *_: (0, 0)),
                pl.BlockSpec(memory_space=pl.ANY),
            ],
            out_specs=_group_block_specs((tb, d), blocks),
            scratch_shapes=[pltpu.VMEM((2, rows, d), BF16), pltpu.SemaphoreType.DMA((2,))],
        ),
        out_shape=[jax.ShapeDtypeStruct((nb * tb, d), F32) for nb in blocks],
        compiler_params=_cparams(("arbitrary",)),
        name="moe_combine",
    )(src, dst, nch, x, col, g_final, ys)


def _moe(acts, w_out, xs, g, w, moe_weights, g_final):
    tb = MOE_TB
    assert all(x.shape[0] % tb == 0 for x in xs), [x.shape for x in xs]
    blocks = [x.shape[0] // tb for x in xs]
    nb = sum(blocks)
    t = nb * tb
    x, h, col, rowf, cnt = _route(acts, w_out, xs, g, w["w_router_hi"], w["w_router_lo"], w["b_router"], tb)
    n_tiles = -(-(2 * t + nb * N_EXPERTS * (MOE_ALIGN - 1)) // MOE_TILE) + N_EXPERTS
    plan = _moe_plan(cnt)
    xg = _moe_gather(plan, h, rowf, n_tiles, tb)
    yg = _expert_ffn(plan["first_tile"], plan["tile_count"], xg, *moe_weights)
    return _moe_combine(plan["src"], plan["dst"], plan["chunks"], x, col, g_final, yg, tb, blocks)


def _retention_tables(c_real, ct):
    h = np.arange(RET_HEADS, dtype=np.float64)
    log_gamma = np.log1p(-np.exp2(-5.0 - h))
    idx = np.arange(ct, dtype=np.float64)
    live = idx < c_real
    diff = idx[:, None] - idx[None, :]
    causal = (diff >= 0) & live[:, None] & live[None, :]
    decay = np.where(causal[None], np.exp(np.where(causal, diff, 0.0)[None] * log_gamma[:, None, None]), 0.0)
    q_dec = np.where(live[None], np.exp((idx + 1.0)[None] * log_gamma[:, None]), 0.0)
    k_dec = np.where(live[None], np.exp((c_real - 1.0 - idx)[None] * log_gamma[:, None]), 0.0)
    q_dec = np.broadcast_to(q_dec[..., None], (RET_HEADS, ct, LANES))
    k_dec = np.broadcast_to(k_dec[..., None], (RET_HEADS, ct, LANES))
    return (jnp.asarray(decay, F32), jnp.asarray(q_dec, F32), jnp.asarray(k_dec, F32))


def _rope_tables(pos0, length):
    half = HEAD_DIM // 2
    inv = ROPE_BASE ** (-np.arange(half, dtype=np.float64) / half)
    ang = (pos0 + np.arange(length, dtype=np.float64))[:, None] * inv[None, :]
    cos = np.concatenate([np.cos(ang), np.cos(ang)], axis=-1)
    sin = np.concatenate([-np.sin(ang), np.sin(ang)], axis=-1)
    return jnp.asarray(cos, F32), jnp.asarray(sin, F32)


def _retention_body(q_ref, k_ref, v_ref, g_ref, cos_ref, sin_ref, dec_ref, qd_ref, kd_ref, s0_ref, gn_ref,
                    o_ref, s_ref, *, c_real, ct, bb):
    @pl.when(pl.program_id(1) == 0)
    def _():
        s_ref[...] = s0_ref[...]

    cos = cos_ref[...]
    sin = sin_ref[...]
    for g, hd in [(g, hd) for g in range(bb) for hd in range(RET_HEADS)]:
        lo, hi = hd * HEAD_DIM, (hd + 1) * HEAD_DIM
        q = q_ref[g, :, lo:hi].astype(F32)
        k = k_ref[g, :, lo:hi].astype(F32)
        q = (q * cos + pltpu.roll(q, HEAD_DIM // 2, 1) * sin) * (HEAD_DIM ** -0.5)
        k = k * cos + pltpu.roll(k, HEAD_DIM // 2, 1) * sin
        v = v_ref[g, :, lo:hi].astype(F32)
        qb = q.astype(BF16)
        q_dec = qd_ref[hd, :c_real, :]
        state = s_ref[g, hd]

        s = _dot_nt(qb, _pad_rows(k, ct).astype(BF16)) * dec_ref[hd, :c_real, :]
        o = _dot(s.astype(BF16), _pad_rows(v, ct).astype(BF16)) + _dot(qb, state.astype(BF16)) * q_dec
        chunk_dec = q_dec[c_real - 1:c_real, :]
        s_ref[g, hd] = state * chunk_dec + _dot_tn((k * kd_ref[hd, :c_real, :]).astype(BF16), v.astype(BF16))

        o = o - jnp.mean(o, axis=-1, keepdims=True)
        o = o * lax.rsqrt(jnp.mean(o * o, axis=-1, keepdims=True) + EPS)
        o_ref[g, :, lo:hi] = _silu(g_ref[g, :, lo:hi].astype(F32)) * (o * gn_ref[:, lo:hi])


def _retention(proj, pos0, state, gn, bb):
    b, length, _ = proj.shape
    assert b % bb == 0, (b, bb)
    c_real = math.gcd(length, RET_CHUNK)
    ct = RET_CHUNK
    width = RET_HEADS * HEAD_DIM
    cos, sin = _rope_tables(pos0, length)
    decay, q_dec, k_dec = _retention_tables(c_real, ct)

    def col(i):
        return pl.BlockSpec((bb, c_real, width), lambda bi, c: (bi, c, i))

    full = lambda shape: pl.BlockSpec(shape, lambda bi, c: (0,) * len(shape))
    state_spec = pl.BlockSpec((bb, RET_HEADS, HEAD_DIM, HEAD_DIM), lambda bi, c: (bi, 0, 0, 0))
    return pl.pallas_call(
        functools.partial(_retention_body, c_real=c_real, ct=ct, bb=bb),
        grid=(b // bb, length // c_real),
        in_specs=[
            col(0), col(1), col(2), col(3),
            pl.BlockSpec((c_real, HEAD_DIM), lambda bi, c: (c, 0)),
            pl.BlockSpec((c_real, HEAD_DIM), lambda bi, c: (c, 0)),
            full((RET_HEADS, ct, ct)), full((RET_HEADS, ct, LANES)), full((RET_HEADS, ct, LANES)),
            state_spec,
            full((1, width)),
        ],
        out_specs=[
            pl.BlockSpec((bb, c_real, width), lambda bi, c: (bi, c, 0)),
            state_spec,
        ],
        out_shape=[
            jax.ShapeDtypeStruct((b, length, width), F32),
            jax.ShapeDtypeStruct(state.shape, F32),
        ],
        compiler_params=_cparams(("parallel", "arbitrary")),
        name="retention",
    )(proj, proj, proj, proj, cos, sin, decay, q_dec, k_dec, state, gn)


def _scan_rows(x, op, fill):
    rows = x.shape[0]
    row = lax.broadcasted_iota(jnp.int32, x.shape, 0)
    sh = 1
    while sh < rows:
        x = op(x, jnp.where(row >= sh, pltpu.roll(x, sh, 0), fill))
        sh *= 2
    return x


def _mlstm_body(q_ref, k_ref, v_ref, og_ref, gate_ref, bg_ref, cw_ref, cb_ref, cs_ref, c0_ref, n0_ref, m0_ref,
                gn_ref, o_ref, c_ref, n_ref, m_ref, conv_ref, xc_ref, *, c_real, ct, bb):
    tail = ML_CONV - 1

    @pl.when(pl.program_id(1) == 0)
    def _():
        c_ref[...] = c0_ref[...]
        n_ref[...] = n0_ref[...]
        m_ref[...] = m0_ref[...]
        xc_ref[:, SUBLANES - tail:SUBLANES, :] = cs_ref[...]

    for g in range(bb):
        _mlstm_sequence(q_ref.at[g], k_ref.at[g], v_ref.at[g], og_ref.at[g], gate_ref.at[g], bg_ref, cw_ref, cb_ref,
                        gn_ref, o_ref.at[g], c_ref.at[g], n_ref.at[g], m_ref.at[g], conv_ref.at[g], xc_ref.at[g],
                        c_real=c_real, ct=ct)


def _mlstm_sequence(q_ref, k_ref, v_ref, og_ref, gate_ref, bg_ref, cw_ref, cb_ref, gn_ref, o_ref, c_ref, n_ref, m_ref,
                    conv_ref, xc_ref, *, c_real, ct):
    width = ML_HEADS * HEAD_DIM
    tail = ML_CONV - 1

    xc_ref[SUBLANES:SUBLANES + c_real, :width] = q_ref[...].astype(F32)
    xc_ref[SUBLANES:SUBLANES + c_real, width:] = k_ref[...].astype(F32)
    y = cb_ref[...] + cw_ref[tail:tail + 1, :] * xc_ref[SUBLANES:SUBLANES + c_real, :]
    for j in range(tail):
        y = y + cw_ref[j:j + 1, :] * xc_ref[SUBLANES - tail + j:SUBLANES - tail + j + c_real, :]
    new_tail = xc_ref[c_real:c_real + SUBLANES, :]
    conv_ref[...] = new_tail[SUBLANES - tail:, :]
    xc_ref[0:SUBLANES, :] = new_tail
    qk = _silu(y)
    q_all = qk[:, :width]
    k_all = qk[:, width:] * (HEAD_DIM ** -0.5)
    v_all = v_ref[...].astype(F32)

    gates = gate_ref[...] + bg_ref[...]
    f_pre = pltpu.roll(gates, LANES - ML_HEADS, 1)
    log_f = jnp.minimum(f_pre, 0.0) - jnp.log(1.0 + jnp.exp(-jnp.abs(f_pre)))
    b_cum = _scan_rows(log_f, jnp.add, 0.0)
    src = gates - b_cum
    cmx = _scan_rows(src, jnp.maximum, NEG_INF)
    m_prev = m_ref[...]
    mx = jnp.maximum(m_prev, cmx)
    m_t = b_cum + mx
    w_prev = jnp.exp(m_prev - mx)
    e_neg_m = jnp.exp(-m_t)
    last = c_real - 1
    b_last = b_cum[last:last + 1, :]
    m_new = m_t[last:last + 1, :]
    w_end = jnp.exp(b_last + src - m_new)
    dec = jnp.exp(b_last + m_prev - m_new)
    m_ref[...] = m_new
    src_t = jnp.transpose(_pad_rows(src, ct))

    t_idx = lax.broadcasted_iota(jnp.int32, (c_real, ct), 0)
    s_idx = lax.broadcasted_iota(jnp.int32, (c_real, ct), 1)
    live = (s_idx <= t_idx) & (s_idx < c_real)

    for hd in range(ML_HEADS):
        lo, hi = hd * HEAD_DIM, (hd + 1) * HEAD_DIM
        q = q_all[:, lo:hi]
        k = k_all[:, lo:hi]
        v = v_all[:, lo:hi]
        qb = q.astype(BF16)
        c_state = c_ref[hd]
        n_state = n_ref[hd:hd + 1, :]
        wp = w_prev[:, hd:hd + 1]
        w = jnp.where(live, jnp.exp(src_t[hd:hd + 1, :] - mx[:, hd:hd + 1]), 0.0)
        s = _dot_nt(qb, _pad_rows(k, ct).astype(BF16)) * w
        num = _dot(s.astype(BF16), _pad_rows(v, ct).astype(BF16)) + wp * _dot(qb, c_state.astype(BF16))
        den = jnp.sum(s, axis=-1, keepdims=True) + wp * jnp.sum(q * n_state, axis=-1, keepdims=True)
        hh = num / jnp.maximum(jnp.abs(den), e_neg_m[:, hd:hd + 1])
        kw = k * w_end[:, hd:hd + 1]
        dec_h = dec[:, hd:hd + 1]
        c_ref[hd] = dec_h * c_state + _dot_tn(kw.astype(BF16), v.astype(BF16))
        n_ref[hd:hd + 1, :] = dec_h * n_state + jnp.sum(kw, axis=0, keepdims=True)

        hh = _sigmoid(og_ref[:, lo:hi].astype(F32)) * hh
        hh = hh - jnp.mean(hh, axis=-1, keepdims=True)
        hh = hh * lax.rsqrt(jnp.mean(hh * hh, axis=-1, keepdims=True) + EPS)
        o_ref[:, lo:hi] = hh * gn_ref[:, lo:hi]


def _mlstm(proj, gates, b_gates, conv_w, conv_b, conv_s, c0, n0, m0, gn, bb):
    b, length, _ = proj.shape
    assert b % bb == 0, (b, bb)
    c_real = math.gcd(length, ML_CHUNK)
    ct = ML_CHUNK
    width = ML_HEADS * HEAD_DIM
    base = (2 * RET_HEADS * HEAD_DIM + 2 * RET_HEADS * HEAD_DIM) // width

    def col(i):
        return pl.BlockSpec((bb, c_real, width), lambda bi, c: (bi, c, base + i))

    full = lambda shape: pl.BlockSpec(shape, lambda bi, c: (0,) * len(shape))
    per_b = lambda shape: pl.BlockSpec((bb,) + shape, lambda bi, c: (bi,) + (0,) * len(shape))
    bg = jnp.zeros((1, LANES), F32).at[0, :2 * ML_HEADS].set(b_gates)
    m0p = jnp.zeros((b, 1, LANES), F32).at[:, 0, :ML_HEADS].set(m0)
    outs = pl.pallas_call(
        functools.partial(_mlstm_body, c_real=c_real, ct=ct, bb=bb),
        grid=(b // bb, length // c_real),
        in_specs=[
            col(0), col(1), col(2), col(3),
            pl.BlockSpec((bb, c_real, LANES), lambda bi, c: (bi, c, 0)),
            full((1, LANES)),
            full((ML_CONV, 2 * width)),
            full((1, 2 * width)),
            per_b((ML_CONV - 1, 2 * width)),
            per_b((ML_HEADS, HEAD_DIM, HEAD_DIM)),
            per_b((ML_HEADS, HEAD_DIM)),
            per_b((1, LANES)),
            full((1, width)),
        ],
        out_specs=[
            pl.BlockSpec((bb, c_real, width), lambda bi, c: (bi, c, 0)),
            per_b((ML_HEADS, HEAD_DIM, HEAD_DIM)),
            per_b((ML_HEADS, HEAD_DIM)),
            per_b((1, LANES)),
            per_b((ML_CONV - 1, 2 * width)),
        ],
        out_shape=[
            jax.ShapeDtypeStruct((b, length, width), F32),
            jax.ShapeDtypeStruct(c0.shape, F32),
            jax.ShapeDtypeStruct(n0.shape, F32),
            jax.ShapeDtypeStruct((b, 1, LANES), F32),
            jax.ShapeDtypeStruct(conv_s.shape, F32),
        ],
        scratch_shapes=[pltpu.VMEM((bb, ct + SUBLANES, 2 * width), F32)],
        compiler_params=_cparams(("parallel", "arbitrary")),
        name="mlstm",
    )(proj, proj, proj, proj, gates, bg, conv_w, conv_b, conv_s, c0, n0, m0p, gn)
    h_m, c_new, n_new, m_new, conv_new = outs
    return h_m, c_new, n_new, m_new[:, 0, :ML_HEADS], conv_new


HG_UNROLL_BLOCK = 8
HG_UNROLL_STEP = 8


def _hgrn_head(cols, q_ref, f_ref, i_ref, g_ref, hl_ref, gn_ref, o_ref, st, *, blk, sub):
    hl = hl_ref[:, cols]
    hmax = jnp.max(hl, axis=0, keepdims=True)
    ex = jnp.exp(hl - hmax)
    p = ex / jnp.sum(ex, axis=0, keepdims=True)
    lower = (p[0:1] + p[1:2]) - p[0:1]
    gn = gn_ref[:, cols]

    f = lower + (1.0 - lower) * _sigmoid(f_ref[:, cols])
    log_k = jnp.log2(jnp.maximum(1.0 - f, 0.0))
    q = _silu(q_ref[:, cols].astype(F32))
    v = i_ref[:, cols].astype(F32)
    gate = _silu(g_ref[:, cols].astype(F32))

    row = lax.broadcasted_iota(jnp.int32, (blk, LANES), 0)
    rsub = row & (sub - 1)
    a = jnp.log2(f)
    sh = 1
    while sh < sub:
        a = a + jnp.where(rsub >= sh, pltpu.roll(a, sh, 0), 0.0)
        sh *= 2
    c = a - log_k
    q_in = q * jnp.exp2(a)

    t_idx = lax.broadcasted_iota(jnp.int32, (SUBLANES, LANES), 0)
    for j in range(blk // sub):
        r0 = j * sub
        a_end = a[r0 + sub - 1:r0 + sub, :]
        inter = _dot_nt(q_in[r0:r0 + sub].astype(BF16), st.astype(BF16))
        k_hat = jnp.exp2(a_end - c[r0:r0 + sub])
        v_j = v[r0:r0 + sub]
        st = st * jnp.exp2(a_end) + _dot_tn(v_j.astype(BF16), k_hat.astype(BF16))
        for part in range(sub // SUBLANES):
            t0 = r0 + part * SUBLANES
            a_t = a[t0:t0 + SUBLANES]
            q_t = q[t0:t0 + SUBLANES]
            o_t = inter[part * SUBLANES:(part + 1) * SUBLANES]
            for s in range((part + 1) * SUBLANES):
                arg = a_t - c[r0 + s:r0 + s + 1, :]
                if s >= part * SUBLANES:
                    arg = jnp.where(t_idx >= s - part * SUBLANES, arg, NEG_INF)
                z = q_t * jnp.exp2(arg)
                o_t = o_t + jnp.sum(z, axis=-1, keepdims=True) * v[r0 + s:r0 + s + 1, :]
            o_t = o_t * lax.rsqrt(jnp.mean(o_t * o_t, axis=-1, keepdims=True) + EPS)
            o_ref[t0:t0 + SUBLANES, cols] = o_t * gn * gate[t0:t0 + SUBLANES]
    return st


def _block_ref_rows(a, level, row8):
    rows = a.shape[0]
    half = level // 2
    if half >= SUBLANES:
        return jnp.concatenate([jnp.broadcast_to(a[b0 + half - 1:b0 + half, :], (level, LANES))
                                for b0 in range(0, rows, level)], axis=0)
    if level == 2:
        return jnp.where((row8 & 1) == 1, pltpu.roll(a, 1, 0), a)
    pieces = []
    for v0 in range(0, rows, SUBLANES):
        picks = [jnp.broadcast_to(a[v0 + b0 + half - 1:v0 + b0 + half, :], (SUBLANES, LANES))
                 for b0 in range(0, SUBLANES, level)]
        piece = picks[-1]
        for i in range(len(picks) - 2, -1, -1):
            piece = jnp.where(row8[:SUBLANES] < (i + 1) * level, picks[i], piece)
        pieces.append(piece)
    return jnp.concatenate(pieces, axis=0)


def _hgrn_head_block(cols, q_ref, f_ref, i_ref, g_ref, hl_ref, gn_ref, o_ref, st, *, blk):
    hl = hl_ref[:, cols]
    hmax = jnp.max(hl, axis=0, keepdims=True)
    ex = jnp.exp(hl - hmax)
    p = ex / jnp.sum(ex, axis=0, keepdims=True)
    lower = (p[0:1] + p[1:2]) - p[0:1]

    f = lower + (1.0 - lower) * _sigmoid(f_ref[:, cols])
    kk = 1.0 - f
    q = _silu(q_ref[:, cols].astype(F32))
    vb = i_ref[:, cols]
    a = _scan_rows(jnp.log2(f), jnp.add, 0.0)
    c = a - jnp.log2(jnp.maximum(kk, 0.0))
    row8 = lax.broadcasted_iota(jnp.int32, (blk, LANES), 0) & (SUBLANES - 1)
    group = (lax.broadcasted_iota(jnp.int32, (blk, blk), 0)
             ^ lax.broadcasted_iota(jnp.int32, (blk, blk), 1))

    scores = None
    level = blk
    while level >= 2:
        half = level // 2
        if half >= SUBLANES:
            zero = jnp.zeros((half, LANES), F32)
            qs, ks = [], []
            for b0 in range(0, blk, level):
                ref = a[b0 + half - 1:b0 + half, :]
                qs += [zero, q[b0 + half:b0 + level] * jnp.exp2(a[b0 + half:b0 + level] - ref)]
                ks += [jnp.exp2(ref - c[b0:b0 + half]), zero]
            q_l = jnp.concatenate(qs, axis=0)
            k_l = jnp.concatenate(ks, axis=0)
        else:
            ref = _block_ref_rows(a, level, row8)
            upper = (row8 & (level - 1)) >= half
            q_l = jnp.where(upper, q * jnp.exp2(a - ref), 0.0)
            k_l = jnp.where(upper, 0.0, jnp.exp2(ref - c))
        r = _dot_nt(q_l.astype(BF16), k_l.astype(BF16))
        scores = r if scores is None else jnp.where(group < level, r, scores)
        level = half
    scores = jnp.where(group < 1, _dot_nt(q.astype(BF16), kk.astype(BF16)), scores)

    a_end = a[blk - 1:blk, :]
    o = _dot_nt((q * jnp.exp2(a)).astype(BF16), st.astype(BF16)) + _dot(scores.astype(BF16), vb)
    o = o * lax.rsqrt(jnp.mean(o * o, axis=-1, keepdims=True) + EPS)
    o_ref[:, cols] = o * gn_ref[:, cols] * _silu(g_ref[:, cols].astype(F32))
    return st * jnp.exp2(a_end) + _dot_tn(vb, jnp.exp2(a_end - c).astype(BF16))


def _hgrn_body(q_ref, f_ref, i_ref, g_ref, hl_ref, s0_ref, gn_ref, o_ref, s_ref, st_ref, *, blk, sub, bb):
    @pl.when(pl.program_id(1) == 0)
    def _():
        for g, h in [(g, h) for g in range(bb) for h in range(HG_HEADS)]:
            st_ref[g, h] = jnp.transpose(s0_ref[g, h])

    if blk == HG_BLOCK:
        head, per_iter = functools.partial(_hgrn_head_block, blk=blk), HG_UNROLL_BLOCK // bb
    else:
        head, per_iter = functools.partial(_hgrn_head, blk=blk, sub=sub), HG_UNROLL_STEP // bb

    def heads(it, carry):
        for g, u in [(g, u) for g in range(bb) for u in range(per_iter)]:
            h = it * per_iter + u
            cols = pl.ds(pl.multiple_of(h * HEAD_DIM, HEAD_DIM), HEAD_DIM)
            st_ref[g, h] = head(cols, q_ref.at[g], f_ref.at[g], i_ref.at[g], g_ref.at[g], hl_ref, gn_ref,
                                o_ref.at[g], st_ref[g, h])
        return carry

    lax.fori_loop(0, HG_HEADS // per_iter, heads, 0)

    @pl.when(pl.program_id(1) == pl.num_programs(1) - 1)
    def _():
        for g, h in [(g, h) for g in range(bb) for h in range(HG_HEADS)]:
            s_ref[g, h] = jnp.transpose(st_ref[g, h])


def _hgrn(proj, f_pre, hg_lower, state, gn, bb):
    b, length, _ = proj.shape
    blk = min(length, HG_BLOCK)
    sub = math.gcd(length, HG_SUB)
    assert b % bb == 0, (b, bb)
    width = HG_HEADS * HEAD_DIM

    def col(i):
        return pl.BlockSpec((bb, blk, width), lambda bi, c: (bi, c, i))

    full = lambda shape: pl.BlockSpec(shape, lambda bi, c: (0,) * len(shape))
    state_spec = pl.BlockSpec((bb, HG_HEADS, HEAD_DIM, HEAD_DIM), lambda bi, c: (bi, 0, 0, 0))
    return pl.pallas_call(
        functools.partial(_hgrn_body, blk=blk, sub=sub, bb=bb),
        grid=(b // bb, length // blk),
        in_specs=[col(0), col(0), col(1), col(2), full(hg_lower.shape), state_spec, full((1, width))],
        out_specs=[pl.BlockSpec((bb, blk, width), lambda bi, c: (bi, c, 0)), state_spec],
        out_shape=[
            jax.ShapeDtypeStruct((b, length, width), F32),
            jax.ShapeDtypeStruct(state.shape, F32),
        ],
        scratch_shapes=[pltpu.VMEM((bb, HG_HEADS, HEAD_DIM, HEAD_DIM), F32)],
        compiler_params=_cparams(("parallel", "arbitrary")),
        name="hgrn2",
    )(proj, f_pre, proj, proj, hg_lower, state, gn)


def _prep_weights(p):
    w_in_ab = p["w_in_ab"][0]
    w_in_c = p["w_in_c"][0]
    width = HG_HEADS * HEAD_DIM
    w_gates = jnp.zeros((D_MODEL, LANES), BF16).at[:, :2 * ML_HEADS].set(w_in_ab[:, AB_MAIN:].astype(BF16))
    w_router = jnp.zeros((LANES, D_MODEL), F32).at[:N_EXPERTS, :].set(p["w_router"][0].T)
    w_router_hi = w_router.astype(BF16)
    w_router_lo = (w_router - w_router_hi.astype(F32)).astype(BF16)
    b_router = jnp.broadcast_to(p["b_router"][0][:, None], (N_EXPERTS, LANES))
    return {
        "w_ab": w_in_ab[:, :AB_MAIN].astype(BF16),
        "w_ab_gates": w_gates,
        "w_out_ab": p["w_out_ab"][0].astype(BF16),
        "w_c": jnp.concatenate([w_in_c[:, :width], w_in_c[:, 2 * width:]], axis=1).astype(BF16),
        "w_c_forget": w_in_c[:, width:2 * width].astype(BF16),
        "w_out_c": p["w_out_c"][0].astype(BF16),
        "w_ffn_gate": p["w_ffn_gate"][0].astype(BF16),
        "w_ffn_up": p["w_ffn_up"][0].astype(BF16),
        "w_ffn_down": p["w_ffn_down"][0].astype(BF16),
        "w_router_hi": w_router_hi,
        "w_router_lo": w_router_lo,
        "b_router": b_router,
    }


def _mixers(x, pos0, ret_s, mc_s, mn_s, mm_s, conv_s, hg_s, p, w):
    b, length, d = x.shape
    t = b * length
    tm = min(512, t)
    tm_mm = min(1024, t)
    prompt = length >= RET_CHUNK
    bb = 1 if prompt else 8
    assert t % tm_mm == 0 and t % tm == 0 and b % bb == 0 and b % 2 == 0, (b, length)
    row = lambda v: v.reshape(1, -1)
    x0 = x.reshape(t, d)

    proj, gates = _norm_matmul(x0, row(p["ln_mix"][0]), w["w_ab"], w["w_ab_gates"], tm_mm, AB_MAIN // 2)
    proj = proj.reshape(b, length, AB_MAIN)
    o_ret, ret_new = _retention(proj, pos0, ret_s, row(p["ret_gn"][0]), 4 if prompt else bb)
    h_m, c_new, n_new, m_new, conv_new = _mlstm(
        proj, gates.reshape(b, length, LANES), p["b_gates_ab"][0], p["conv_w_ab"][0], row(p["conv_b_ab"][0]),
        conv_s, mc_s, mn_s, mm_s, row(p["ml_gn"][0]), bb)
    half = RET_HEADS * HEAD_DIM
    x2 = _ffn([o_ret.reshape(t, half), h_m.reshape(t, half)], [w["w_out_ab"][:half], w["w_out_ab"][half:]], x0,
              row(p["ln_ffn"][0]), w["w_ffn_gate"], w["w_ffn_up"], w["w_ffn_down"], tm, D_FF // 2)

    proj_c, f_pre = _norm_matmul(x2, row(p["ln_mix"][1]), w["w_c"], w["w_c_forget"], tm_mm, 3 * d // 2)
    o_hg, hg_new = _hgrn(proj_c.reshape(b, length, 3 * d), f_pre.reshape(b, length, d), p["hg_lower"], hg_s,
                         row(p["hg_gn"][0]), bb)
    states = (ret_new[None], c_new[None], n_new[None], m_new[None], conv_new[None], hg_new[None])
    return x2, o_hg.reshape(t, d), states


def _experts(groups, shapes, p, w):
    row = lambda v: v.reshape(1, -1)
    ys = _moe([o for _, o in groups], w["w_out_c"], [x for x, _ in groups], row(p["ln_ffn"][1]), w,
              (p["w_moe_gate"][0], p["w_moe_up"][0], p["w_moe_down"][0]), row(p["ln_final"]))
    return [y.reshape(shape) for y, shape in zip(ys, shapes)]


def kernel(x_prompt, x_sample, state_ret, state_mlstm_c, state_mlstm_n, state_mlstm_m, state_conv, state_hgrn,
           ln_mix, ln_ffn, ln_final, w_in_ab, b_gates_ab, conv_w_ab, conv_b_ab, ret_gn, ml_gn, w_out_ab,
           w_in_c, hg_lower, hg_gn, w_out_c, w_ffn_gate, w_ffn_up, w_ffn_down,
           w_router, b_router, w_moe_gate, w_moe_up, w_moe_down):
    p = {"ln_mix": ln_mix, "ln_ffn": ln_ffn, "ln_final": ln_final,
         "w_in_ab": w_in_ab, "b_gates_ab": b_gates_ab, "conv_w_ab": conv_w_ab, "conv_b_ab": conv_b_ab,
         "ret_gn": ret_gn, "ml_gn": ml_gn, "w_out_ab": w_out_ab,
         "w_in_c": w_in_c, "hg_lower": hg_lower, "hg_gn": hg_gn, "w_out_c": w_out_c,
         "w_ffn_gate": w_ffn_gate, "w_ffn_up": w_ffn_up, "w_ffn_down": w_ffn_down,
         "w_router": w_router, "b_router": b_router,
         "w_moe_gate": w_moe_gate, "w_moe_up": w_moe_up, "w_moe_down": w_moe_down}
    w = _prep_weights(p)
    bp = x_prompt.shape[0]
    zeros = lambda *shape: jnp.zeros(shape, F32)
    x2_p, o_p, states_p = _mixers(
        x_prompt, 0,
        zeros(bp, RET_HEADS, HEAD_DIM, HEAD_DIM), zeros(bp, ML_HEADS, HEAD_DIM, HEAD_DIM),
        zeros(bp, ML_HEADS, HEAD_DIM), jnp.full((bp, ML_HEADS), NEG_INF, F32),
        zeros(bp, ML_CONV - 1, 2 * ML_HEADS * HEAD_DIM), zeros(bp, HG_HEADS, HEAD_DIM, HEAD_DIM), p, w)
    x2_s, o_s, states_s = _mixers(
        x_sample, 16384,
        state_ret[0], state_mlstm_c[0], state_mlstm_n[0], state_mlstm_m[0], state_conv[0], state_hgrn[0], p, w)
    y_p, y_s = _experts([(x2_p, o_p), (x2_s, o_s)], [x_prompt.shape, x_sample.shape], p, w)
    out = [y_p, y_s]
    for a, b in zip(states_p, states_s):
        out += [a, b]
    return tuple(out)
```
